```python
import jax, jax.numpy as jnp
from jax import lax
import numpy as np

D_MODEL = 2048
BATCH = 8
SEQ = 8192
DEPTH = 4

HEAD_DIM = 128
N_MIX_HEADS = D_MODEL // HEAD_DIM
N_ATTN_HEADS = 3 * N_MIX_HEADS // 4
N_SGU_GROUPS = N_MIX_HEADS - N_ATTN_HEADS
SGU_GROUP = HEAD_DIM
ATTN_WIDTH = N_ATTN_HEADS * HEAD_DIM
SGU_WIDTH = N_SGU_GROUPS * SGU_GROUP
MIX_IN = 3 * ATTN_WIDTH + 2 * SGU_WIDTH
CHUNK = 128
DILATED_BRANCHES = ((128, 1), (512, 4), (2048, 16))
BRANCH_BLOCK = 128
ROPE_THETA = 500000.0
ROPE_DIM = HEAD_DIM // 4
CONV_WIDTH = 3
D_FF = -(-8 * D_MODEL // (3 * 256)) * 256
N_EVEN = (DEPTH + 1) // 2
N_ODD = DEPTH // 2
EPS = 1e-6

kernel_name = "hybrid_dilated_attn_sgu_shortconv_adaln"


def rmsnorm(x, g):
    xf = x.astype(jnp.float32)
    y = xf * lax.rsqrt(jnp.mean(xf * xf, axis=-1, keepdims=True) + EPS)
    return (y * g.astype(jnp.float32)).astype(x.dtype)


def rope_tables(positions):
    inv_freq = ROPE_THETA ** (-jnp.arange(0, ROPE_DIM, 2, dtype=jnp.float32) / ROPE_DIM)
    ang = positions.astype(jnp.float32)[..., None] * inv_freq
    return jnp.cos(ang)[:, :, None, :], jnp.sin(ang)[:, :, None, :]


def apply_partial_rope(t, cos, sin):
    half = ROPE_DIM // 2
    t1 = t[..., :half].astype(jnp.float32)
    t2 = t[..., half:ROPE_DIM].astype(jnp.float32)
    rot = jnp.concatenate([t1 * cos - t2 * sin, t2 * cos + t1 * sin], axis=-1).astype(t.dtype)
    return jnp.concatenate([rot, t[..., ROPE_DIM:]], axis=-1)


def _dilated_branch(q, k, v, span, dilation):
    b, s, h, d = q.shape
    n = s // dilation
    nb = -(-n // BRANCH_BLOCK)
    pad = nb * BRANCH_BLOCK - n

    def to_sub(t):
        t = t.reshape(b, n, dilation, h, d).transpose(0, 2, 1, 3, 4)
        t = jnp.pad(t, ((0, 0), (0, 0), (0, pad), (0, 0), (0, 0)))
        return t.reshape(b, dilation, nb, BRANCH_BLOCK, h, d)

    def with_prev(t):
        prev = jnp.pad(t, ((0, 0), (0, 0), (1, 0), (0, 0), (0, 0), (0, 0)))[:, :, :nb]
        return jnp.concatenate([prev, t], axis=3)

    qs = to_sub(q)
    kb = with_prev(to_sub(k))
    vb = with_prev(to_sub(v))
    scores = jnp.einsum("brnqhd,brnkhd->brnqhk", qs, kb).astype(jnp.float32) * (d ** -0.5)

    qi = jnp.arange(BRANCH_BLOCK)[:, None]
    kj = jnp.arange(2 * BRANCH_BLOCK)[None, :]
    dist = qi + BRANCH_BLOCK - kj
    band = (dist >= 0) & (dist <= span)
    not_first = jnp.arange(nb)[:, None, None] > 0
    valid = band[None] & (not_first | (kj >= BRANCH_BLOCK)[None])
    scores = jnp.where(valid[None, None, :, :, None, :], scores, -jnp.inf)

    m = jnp.max(scores, axis=-1, keepdims=True)
    p = jnp.exp(scores - m)
    l = jnp.sum(p, axis=-1)
    o = jnp.einsum("brnqhk,brnkhd->brnqhd", p, vb.astype(jnp.float32)) / l[..., None]
    lse = m[..., 0] + jnp.log(l)

    def from_sub(t):
        t = t.reshape(b, dilation, nb * BRANCH_BLOCK, *t.shape[4:])[:, :, :n]
        t = jnp.moveaxis(t, 1, 2)
        return t.reshape(b, s, *t.shape[3:])

    return from_sub(o), from_sub(lse)


def dilated_attention(q, k, v):
    outs, lses = zip(*[_dilated_branch(q, k, v, w // dil, dil) for w, dil in DILATED_BRANCHES])
    alpha = jax.nn.softmax(jnp.stack(lses, axis=-1), axis=-1)
    o = sum(alpha[..., i, None] * outs[i] for i in range(len(outs)))
    return o.astype(q.dtype)


def spatial_gating(u, v, w_s, b_s):
    b, s, _ = u.shape
    shp = (b, s // CHUNK, CHUNK, N_SGU_GROUPS, SGU_GROUP)
    u = jax.nn.gelu(u).reshape(shp)
    v = jax.nn.gelu(v).reshape(shp)
    w = w_s * jnp.tril(jnp.ones((CHUNK, CHUNK), w_s.dtype))
    mixed = jnp.einsum("gts,bnsgc->bntgc", w, v) + b_s.T[:, :, None]
    return (u * mixed).reshape(b, s, SGU_WIDTH)


def attn_sgu_mixer(h, w_in, w_s, b_s, w_out, cos, sin):
    b, s, _ = h.shape
    z = h @ w_in
    q, k, v, u, vg = jnp.split(
        z, [ATTN_WIDTH, 2 * ATTN_WIDTH, 3 * ATTN_WIDTH, 3 * ATTN_WIDTH + SGU_WIDTH], axis=-1)
    hs = (b, s, N_ATTN_HEADS, HEAD_DIM)
    q = apply_partial_rope(q.reshape(hs), cos, sin)
    k = apply_partial_rope(k.reshape(hs), cos, sin)
    attn = dilated_attention(q, k, v.reshape(hs)).reshape(b, s, ATTN_WIDTH)
    sgu = spatial_gating(u, vg, w_s, b_s)
    return jnp.concatenate([attn, sgu], axis=-1) @ w_out


def short_conv_mixer(h, w_in, conv_w, w_out):
    gb, gc, hx = jnp.split(h @ w_in, 3, axis=-1)
    y = gc * hx
    s = y.shape[1]
    y_pad = jnp.pad(y, ((0, 0), (CONV_WIDTH - 1, 0), (0, 0)))
    conv = sum(conv_w[j] * y_pad[:, j:j + s] for j in range(CONV_WIDTH))
    return (gb * conv) @ w_out


def swiglu(h, w_gate, w_up, w_down):
    return (jax.nn.silu(h @ w_gate) * (h @ w_up)) @ w_down


def _fwd_setup_inputs(seed: int = 0) -> dict:
    key = jax.random.key(seed)
    ks = jax.random.split(key, 20)
    nrm = jax.random.normal
    f32 = jnp.float32
    d = D_MODEL
    x = nrm(ks[0], (BATCH, SEQ, d), f32)
    c = nrm(ks[1], (BATCH, d), f32)
    positions = (jnp.arange(SEQ, dtype=jnp.int32)[None, :]
                 + jax.random.randint(ks[2], (BATCH, 1), 0, 4096, dtype=jnp.int32))
    ada_w = nrm(ks[3], (DEPTH, d, 6 * d), f32) * (0.5 * d ** -0.5)
    ada_b = nrm(ks[4], (DEPTH, 6 * d), f32) * 0.02
    norm_mix = 1.0 + 0.02 * nrm(ks[5], (DEPTH, d), f32)
    norm_ffn = 1.0 + 0.02 * nrm(ks[6], (DEPTH, d), f32)
    ab_w_in = nrm(ks[7], (N_EVEN, d, MIX_IN), f32) * d ** -0.5
    sgu_w = nrm(ks[8], (N_EVEN, N_SGU_GROUPS, CHUNK, CHUNK), f32) * CHUNK ** -0.5
    sgu_b = 1.0 + 0.1 * nrm(ks[9], (N_EVEN, N_SGU_GROUPS, CHUNK), f32)
    ab_w_out = nrm(ks[10], (N_EVEN, ATTN_WIDTH + SGU_WIDTH, d), f32) * (ATTN_WIDTH + SGU_WIDTH) ** -0.5
    conv_w_in = nrm(ks[11], (N_ODD, d, 3 * d), f32) * d ** -0.5
    conv_w = nrm(ks[12], (N_ODD, CONV_WIDTH, d), f32) * CONV_WIDTH ** -0.5
    conv_w_out = nrm(ks[13], (N_ODD, d, d), f32) * d ** -0.5
    ffn_w_gate = nrm(ks[14], (DEPTH, d, D_FF), f32) * d ** -0.5
    ffn_w_up = nrm(ks[15], (DEPTH, d, D_FF), f32) * d ** -0.5
    ffn_w_down = nrm(ks[16], (DEPTH, D_FF, d), f32) * D_FF ** -0.5
    final_norm = 1.0 + 0.02 * nrm(ks[17], (d,), f32)
    return {"x": x, "c": c, "positions": positions, "ada_w": ada_w, "ada_b": ada_b,
            "norm_mix": norm_mix, "norm_ffn": norm_ffn, "ab_w_in": ab_w_in,
            "sgu_w": sgu_w, "sgu_b": sgu_b, "ab_w_out": ab_w_out,
            "conv_w_in": conv_w_in, "conv_w": conv_w, "conv_w_out": conv_w_out,
            "ffn_w_gate": ffn_w_gate, "ffn_w_up": ffn_w_up, "ffn_w_down": ffn_w_down,
            "final_norm": final_norm}


def _fwd_reference(x, c, positions, ada_w, ada_b, norm_mix, norm_ffn, ab_w_in, sgu_w, sgu_b,
              ab_w_out, conv_w_in, conv_w, conv_w_out, ffn_w_gate, ffn_w_up, ffn_w_down,
              final_norm):
    cos, sin = rope_tables(positions)
    c_act = jax.nn.silu(c)
    for layer in range(DEPTH):
        mod = (c_act @ ada_w[layer] + ada_b[layer])[:, None, :]
        sh_m, sc_m, g_m, sh_f, sc_f, g_f = jnp.split(mod, 6, axis=-1)
        h = rmsnorm(x, norm_mix[layer]) * (1 + sc_m) + sh_m
        i = layer // 2
        if layer % 2 == 0:
            mix = attn_sgu_mixer(h, ab_w_in[i], sgu_w[i], sgu_b[i], ab_w_out[i], cos, sin)
        else:
            mix = short_conv_mixer(h, conv_w_in[i], conv_w[i], conv_w_out[i])
        x = x + g_m * mix
        h = rmsnorm(x, norm_ffn[layer]) * (1 + sc_f) + sh_f
        x = x + g_f * swiglu(h, ffn_w_gate[layer], ffn_w_up[layer], ffn_w_down[layer])
    return rmsnorm(x, final_norm)


import jax as _jax
import jax.numpy as _jnp

TWIN_FORMAT = 'train_step'
FWD_PARAMS = ['x', 'c', 'positions', 'ada_w', 'ada_b', 'norm_mix', 'norm_ffn', 'ab_w_in', 'sgu_w', 'sgu_b', 'ab_w_out', 'conv_w_in', 'conv_w', 'conv_w_out', 'ffn_w_gate', 'ffn_w_up', 'ffn_w_down', 'final_norm']
TWIN_WEIGHTS = ['ada_w', 'ada_b', 'norm_mix', 'norm_ffn', 'ab_w_in', 'sgu_w', 'sgu_b', 'ab_w_out', 'conv_w_in', 'conv_w', 'conv_w_out', 'ffn_w_gate', 'ffn_w_up', 'ffn_w_down', 'final_norm']
TWIN_DIFF_INPUT = 'x'
TWIN_INPUTS = ['x', 'c', 'positions', 'ada_w', 'ada_b', 'norm_mix', 'norm_ffn', 'ab_w_in', 'sgu_w', 'sgu_b', 'ab_w_out', 'conv_w_in', 'conv_w', 'conv_w_out', 'ffn_w_gate', 'ffn_w_up', 'ffn_w_down', 'final_norm', 'loss_target', 'm_ada_w', 'm_ada_b', 'm_norm_mix', 'm_norm_ffn', 'm_ab_w_in', 'm_sgu_w', 'm_sgu_b', 'm_ab_w_out', 'm_conv_w_in', 'm_conv_w', 'm_conv_w_out', 'm_ffn_w_gate', 'm_ffn_w_up', 'm_ffn_w_down', 'm_final_norm', 'v_ada_w', 'v_ada_b', 'v_norm_mix', 'v_norm_ffn', 'v_ab_w_in', 'v_sgu_w', 'v_sgu_b', 'v_ab_w_out', 'v_conv_w_in', 'v_conv_w', 'v_conv_w_out', 'v_ffn_w_gate', 'v_ffn_w_up', 'v_ffn_w_down', 'v_final_norm']
TWIN_OUTPUTS = ['loss', 'grad_x', 'grad_ada_w', 'grad_ada_b', 'grad_norm_mix', 'grad_norm_ffn', 'grad_ab_w_in', 'grad_sgu_w', 'grad_sgu_b', 'grad_ab_w_out', 'grad_conv_w_in', 'grad_conv_w', 'grad_conv_w_out', 'grad_ffn_w_gate', 'grad_ffn_w_up', 'grad_ffn_w_down', 'grad_final_norm', 'delta_ada_w', 'delta_ada_b', 'delta_norm_mix', 'delta_norm_ffn', 'delta_ab_w_in', 'delta_sgu_w', 'delta_sgu_b', 'delta_ab_w_out', 'delta_conv_w_in', 'delta_conv_w', 'delta_conv_w_out', 'delta_ffn_w_gate', 'delta_ffn_w_up', 'delta_ffn_w_down', 'delta_final_norm', 'new_m_ada_w', 'new_m_ada_b', 'new_m_norm_mix', 'new_m_norm_ffn', 'new_m_ab_w_in', 'new_m_sgu_w', 'new_m_sgu_b', 'new_m_ab_w_out', 'new_m_conv_w_in', 'new_m_conv_w', 'new_m_conv_w_out', 'new_m_ffn_w_gate', 'new_m_ffn_w_up', 'new_m_ffn_w_down', 'new_m_final_norm', 'new_v_ada_w', 'new_v_ada_b', 'new_v_norm_mix', 'new_v_norm_ffn', 'new_v_ab_w_in', 'new_v_sgu_w', 'new_v_sgu_b', 'new_v_ab_w_out', 'new_v_conv_w_in', 'new_v_conv_w', 'new_v_conv_w_out', 'new_v_ffn_w_gate', 'new_v_ffn_w_up', 'new_v_ffn_w_down', 'new_v_final_norm']
TWIN_LEAF_KINDS = {'loss': 'loss', 'grad_x': 'grad_x', 'grad_ada_w': 'grad_w', 'grad_ada_b': 'grad_w', 'grad_norm_mix': 'grad_w', 'grad_norm_ffn': 'grad_w', 'grad_ab_w_in': 'grad_w', 'grad_sgu_w': 'grad_w', 'grad_sgu_b': 'grad_w', 'grad_ab_w_out': 'grad_w', 'grad_conv_w_in': 'grad_w', 'grad_conv_w': 'grad_w', 'grad_conv_w_out': 'grad_w', 'grad_ffn_w_gate': 'grad_w', 'grad_ffn_w_up': 'grad_w', 'grad_ffn_w_down': 'grad_w', 'grad_final_norm': 'grad_w', 'delta_ada_w': 'delta_w', 'delta_ada_b': 'delta_w', 'delta_norm_mix': 'delta_w', 'delta_norm_ffn': 'delta_w', 'delta_ab_w_in': 'delta_w', 'delta_sgu_w': 'delta_w', 'delta_sgu_b': 'delta_w', 'delta_ab_w_out': 'delta_w', 'delta_conv_w_in': 'delta_w', 'delta_conv_w': 'delta_w', 'delta_conv_w_out': 'delta_w', 'delta_ffn_w_gate': 'delta_w', 'delta_ffn_w_up': 'delta_w', 'delta_ffn_w_down': 'delta_w', 'delta_final_norm': 'delta_w', 'new_m_ada_w': 'new_m', 'new_m_ada_b': 'new_m', 'new_m_norm_mix': 'new_m', 'new_m_norm_ffn': 'new_m', 'new_m_ab_w_in': 'new_m', 'new_m_sgu_w': 'new_m', 'new_m_sgu_b': 'new_m', 'new_m_ab_w_out': 'new_m', 'new_m_conv_w_in': 'new_m', 'new_m_conv_w': 'new_m', 'new_m_conv_w_out': 'new_m', 'new_m_ffn_w_gate': 'new_m', 'new_m_ffn_w_up': 'new_m', 'new_m_ffn_w_down': 'new_m', 'new_m_final_norm': 'new_m', 'new_v_ada_w': 'new_v', 'new_v_ada_b': 'new_v', 'new_v_norm_mix': 'new_v', 'new_v_norm_ffn': 'new_v', 'new_v_ab_w_in': 'new_v', 'new_v_sgu_w': 'new_v', 'new_v_sgu_b': 'new_v', 'new_v_ab_w_out': 'new_v', 'new_v_conv_w_in': 'new_v', 'new_v_conv_w': 'new_v', 'new_v_conv_w_out': 'new_v', 'new_v_ffn_w_gate': 'new_v', 'new_v_ffn_w_up': 'new_v', 'new_v_ffn_w_down': 'new_v', 'new_v_final_norm': 'new_v'}


def _forward(args):
    return _fwd_reference(*[args[k] for k in FWD_PARAMS])


def _output_shape():
    def fwd():
        inp = _fwd_setup_inputs(0)
        return _fwd_reference(*[inp[k] for k in FWD_PARAMS])
    out = _jax.eval_shape(fwd)
    return out.shape, out.dtype

N_MICROBATCH = 1
ADAM_LR = 0.001
ADAM_B1 = 0.9
ADAM_B2 = 0.999
ADAM_EPS = 1e-08
ADAM_WD = 0.01
ADAM_STEP = 10
PER_EXAMPLE_BATCH_AXIS = {'x': 0, 'c': 0, 'positions': 0, 'loss_target': 0}
SHARED_INPUTS = []
_WEIGHT_DTYPES = {'ada_w': _jnp.float32, 'ada_b': _jnp.float32, 'norm_mix': _jnp.float32, 'norm_ffn': _jnp.float32, 'ab_w_in': _jnp.float32, 'sgu_w': _jnp.float32, 'sgu_b': _jnp.float32, 'ab_w_out': _jnp.float32, 'conv_w_in': _jnp.float32, 'conv_w': _jnp.float32, 'conv_w_out': _jnp.float32, 'ffn_w_gate': _jnp.float32, 'ffn_w_up': _jnp.float32, 'ffn_w_down': _jnp.float32, 'final_norm': _jnp.float32}
MOMENT_SCALE = {'ada_w': 4.522140e-02, 'ada_b': 7.878035e-02, 'norm_mix': 5.633946e-02, 'norm_ffn': 3.775151e-02, 'ab_w_in': 1.649746e-02, 'sgu_w': 1.567956e-02, 'sgu_b': 3.173182e-02, 'ab_w_out': 2.275753e-02, 'conv_w_in': 4.535173e-02, 'conv_w': 4.639095e-02, 'conv_w_out': 4.517466e-02, 'ffn_w_gate': 1.683309e-02, 'ffn_w_up': 1.630642e-02, 'ffn_w_down': 2.705515e-02, 'final_norm': 3.207721e+01}


def _to_microbatches(a, axis):
    t = _jnp.moveaxis(a, axis, 0)
    t = t.reshape((N_MICROBATCH, t.shape[0] // N_MICROBATCH) + t.shape[1:])
    return _jnp.moveaxis(t, 1, axis + 1)


def setup_inputs(seed: int = 0) -> dict:
    inp = _fwd_setup_inputs(seed)
    key = _jax.random.fold_in(_jax.random.key(seed), 7919)
    shape, _ = _output_shape()
    out = dict(inp)
    out["loss_target"] = _jax.random.normal(_jax.random.fold_in(key, 0), shape, _jnp.float32)
    for i, name in enumerate(TWIN_WEIGHTS):
        w = inp[name].astype(_jnp.float32)
        if MOMENT_SCALE is None:
            s = _jnp.sqrt(_jnp.mean(_jnp.square(w)) + 1e-30)
        else:
            s = MOMENT_SCALE[name]
        km, kv = _jax.random.split(_jax.random.fold_in(key, i + 1))
        out[name] = w
        out["m_" + name] = s * _jax.random.normal(km, w.shape, _jnp.float32)
        out["v_" + name] = (s * s) * _jax.random.uniform(kv, w.shape, _jnp.float32, 0.5, 1.5)
    if N_MICROBATCH > 1:
        for name, axis in PER_EXAMPLE_BATCH_AXIS.items():
            out[name] = _to_microbatches(out[name], axis)
    return {'x': out['x'], 'c': out['c'], 'positions': out['positions'], 'ada_w': out['ada_w'], 'ada_b': out['ada_b'], 'norm_mix': out['norm_mix'], 'norm_ffn': out['norm_ffn'], 'ab_w_in': out['ab_w_in'], 'sgu_w': out['sgu_w'], 'sgu_b': out['sgu_b'], 'ab_w_out': out['ab_w_out'], 'conv_w_in': out['conv_w_in'], 'conv_w': out['conv_w'], 'conv_w_out': out['conv_w_out'], 'ffn_w_gate': out['ffn_w_gate'], 'ffn_w_up': out['ffn_w_up'], 'ffn_w_down': out['ffn_w_down'], 'final_norm': out['final_norm'], 'loss_target': out['loss_target'], 'm_ada_w': out['m_ada_w'], 'm_ada_b': out['m_ada_b'], 'm_norm_mix': out['m_norm_mix'], 'm_norm_ffn': out['m_norm_ffn'], 'm_ab_w_in': out['m_ab_w_in'], 'm_sgu_w': out['m_sgu_w'], 'm_sgu_b': out['m_sgu_b'], 'm_ab_w_out': out['m_ab_w_out'], 'm_conv_w_in': out['m_conv_w_in'], 'm_conv_w': out['m_conv_w'], 'm_conv_w_out': out['m_conv_w_out'], 'm_ffn_w_gate': out['m_ffn_w_gate'], 'm_ffn_w_up': out['m_ffn_w_up'], 'm_ffn_w_down': out['m_ffn_w_down'], 'm_final_norm': out['m_final_norm'], 'v_ada_w': out['v_ada_w'], 'v_ada_b': out['v_ada_b'], 'v_norm_mix': out['v_norm_mix'], 'v_norm_ffn': out['v_norm_ffn'], 'v_ab_w_in': out['v_ab_w_in'], 'v_sgu_w': out['v_sgu_w'], 'v_sgu_b': out['v_sgu_b'], 'v_ab_w_out': out['v_ab_w_out'], 'v_conv_w_in': out['v_conv_w_in'], 'v_conv_w': out['v_conv_w'], 'v_conv_w_out': out['v_conv_w_out'], 'v_ffn_w_gate': out['v_ffn_w_gate'], 'v_ffn_w_up': out['v_ffn_w_up'], 'v_ffn_w_down': out['v_ffn_w_down'], 'v_final_norm': out['v_final_norm']}


def _loss(weights, diff, rest, loss_target):
    with _jax.named_scope("forward"):
        args = {**rest, TWIN_DIFF_INPUT: diff, **{k: w.astype(_WEIGHT_DTYPES[k]) for k, w in weights.items()}}
        y = _forward(args)
    with _jax.named_scope("loss_head"):
        err = _jnp.square(y.astype(_jnp.float32) - loss_target)
        return 0.5 * _jnp.sum(_jnp.mean(err, axis=-1)) if err.ndim else 0.5 * err


def _adamw(w, g, m, v):
    m = ADAM_B1 * m + (1.0 - ADAM_B1) * g
    v = ADAM_B2 * v + (1.0 - ADAM_B2) * _jnp.square(g)
    m_hat = m / (1.0 - ADAM_B1 ** ADAM_STEP)
    v_hat = v / (1.0 - ADAM_B2 ** ADAM_STEP)
    delta = -ADAM_LR * (m_hat / (_jnp.sqrt(v_hat) + ADAM_EPS) + ADAM_WD * w)
    return delta, m, v


def reference(x, c, positions, ada_w, ada_b, norm_mix, norm_ffn, ab_w_in, sgu_w, sgu_b, ab_w_out, conv_w_in, conv_w, conv_w_out, ffn_w_gate, ffn_w_up, ffn_w_down, final_norm, loss_target, m_ada_w, m_ada_b, m_norm_mix, m_norm_ffn, m_ab_w_in, m_sgu_w, m_sgu_b, m_ab_w_out, m_conv_w_in, m_conv_w, m_conv_w_out, m_ffn_w_gate, m_ffn_w_up, m_ffn_w_down, m_final_norm, v_ada_w, v_ada_b, v_norm_mix, v_norm_ffn, v_ab_w_in, v_sgu_w, v_sgu_b, v_ab_w_out, v_conv_w_in, v_conv_w, v_conv_w_out, v_ffn_w_gate, v_ffn_w_up, v_ffn_w_down, v_final_norm):
    given = dict(x=x, c=c, positions=positions, ada_w=ada_w, ada_b=ada_b, norm_mix=norm_mix, norm_ffn=norm_ffn, ab_w_in=ab_w_in, sgu_w=sgu_w, sgu_b=sgu_b, ab_w_out=ab_w_out, conv_w_in=conv_w_in, conv_w=conv_w, conv_w_out=conv_w_out, ffn_w_gate=ffn_w_gate, ffn_w_up=ffn_w_up, ffn_w_down=ffn_w_down, final_norm=final_norm, loss_target=loss_target, m_ada_w=m_ada_w, m_ada_b=m_ada_b, m_norm_mix=m_norm_mix, m_norm_ffn=m_norm_ffn, m_ab_w_in=m_ab_w_in, m_sgu_w=m_sgu_w, m_sgu_b=m_sgu_b, m_ab_w_out=m_ab_w_out, m_conv_w_in=m_conv_w_in, m_conv_w=m_conv_w, m_conv_w_out=m_conv_w_out, m_ffn_w_gate=m_ffn_w_gate, m_ffn_w_up=m_ffn_w_up, m_ffn_w_down=m_ffn_w_down, m_final_norm=m_final_norm, v_ada_w=v_ada_w, v_ada_b=v_ada_b, v_norm_mix=v_norm_mix, v_norm_ffn=v_norm_ffn, v_ab_w_in=v_ab_w_in, v_sgu_w=v_sgu_w, v_sgu_b=v_sgu_b, v_ab_w_out=v_ab_w_out, v_conv_w_in=v_conv_w_in, v_conv_w=v_conv_w, v_conv_w_out=v_conv_w_out, v_ffn_w_gate=v_ffn_w_gate, v_ffn_w_up=v_ffn_w_up, v_ffn_w_down=v_ffn_w_down, v_final_norm=v_final_norm)
    weights = {n: given[n] for n in TWIN_WEIGHTS}
    shared = {n: given[n] for n in SHARED_INPUTS}
    per_example = {n: given[n] for n in ['x', 'c', 'positions']}
    grad_fn = _jax.value_and_grad(_loss, argnums=(0, 1))

    def one_microbatch(ex, loss_target):
        ex = dict(ex)
        diff = ex.pop(TWIN_DIFF_INPUT)
        return grad_fn(weights, diff, {**shared, **ex}, loss_target)

    if N_MICROBATCH == 1:
        loss, (grad_w, grad_x) = one_microbatch(per_example, given["loss_target"])
    else:
        def body(carry, xs):
            loss_sum, grad_sum = carry
            l_k, (gw_k, gx_k) = one_microbatch(xs[0], xs[1])
            with _jax.named_scope("update"):
                return (loss_sum + l_k, _jax.tree.map(_jnp.add, grad_sum, gw_k)), gx_k

        init = (_jnp.zeros((), _jnp.float32), _jax.tree.map(_jnp.zeros_like, weights))
        (loss, grad_w), grad_x = _jax.lax.scan(body, init, (per_example, given["loss_target"]))
    with _jax.named_scope("update"):
        delta_w, new_m, new_v = {}, {}, {}
        for n in TWIN_WEIGHTS:
            delta_w[n], new_m[n], new_v[n] = _adamw(weights[n], grad_w[n], given["m_" + n], given["v_" + n])
    return (loss, grad_x, *[grad_w[n] for n in TWIN_WEIGHTS], *[delta_w[n] for n in TWIN_WEIGHTS],
            *[new_m[n] for n in TWIN_WEIGHTS], *[new_v[n] for n in TWIN_WEIGHTS])
```

```python
import functools
import math

import jax
import jax.numpy as jnp
from jax import lax
from jax.experimental import pallas as pl
from jax.experimental.pallas import tpu as pltpu

F32 = jnp.float32
BF16 = jnp.bfloat16
HEAD_DIM = 128
CHUNK = 128
ATTN_BLOCK = 128
DILATIONS = (1, 4, 16)
ROPE_DIM = HEAD_DIM // 4
ROPE_THETA = 500000.0
EPS = 1e-6
MASKED = -1e30
ADAM_LR, ADAM_B1, ADAM_B2, ADAM_EPS, ADAM_WD, ADAM_STEP = 0.001, 0.9, 0.999, 1e-08, 0.01, 10
VMEM_LIMIT_BYTES = 56 * 1024 * 1024
MESH = pl.DeviceIdType.MESH
ANY = pl.BlockSpec(memory_space=pl.ANY)
N_CHIPS = 4
N_DEV = 8


def _pick(n, cands):
    for t in cands:
        if n % t == 0:
            return t
    return n


def _params(sem):
    return pltpu.CompilerParams(dimension_semantics=sem, vmem_limit_bytes=VMEM_LIMIT_BYTES)


def _mm(a, b, mode, out_dtype, name, layer=None):
    bshape = b.shape[1:] if layer is not None else b.shape
    if mode == "nn":
        (M, K), (K2, N) = a.shape, bshape
    elif mode == "nt":
        (M, K), (N, K2) = a.shape, bshape
    else:
        (K, M), (K2, N) = a.shape, bshape
    assert K == K2, (a.shape, b.shape, mode)
    if mode == "tn":
        tm = _pick(M, (2048, 1408, 1024, 512, 256, 128))
        tn = _pick(N, (1408, 1536, 1024, 512, 256, 128))
        tk = _pick(K, (512, 256, 128))
    else:
        tm = _pick(M, (1024, 512, 256, 128))
        tk = K if K <= 2048 else _pick(K, (1408, 1536, 1024, 512, 256, 128))
        tn = _pick(N, (512, 256, 128)) if tk == K else _pick(N, (1024, 512, 256, 128))
    nk = K // tk
    dims = {"nn": (((1,), (0,)), ((), ())), "nt": (((1,), (1,)), ((), ())), "tn": (((0,), (0,)), ((), ()))}[mode]

    in_place = out_dtype == F32

    def body(a_ref, b_ref, o_ref, *acc):
        k = pl.program_id(2)
        part = lax.dot_general(a_ref[...].astype(BF16), b_ref[...].astype(BF16), dims, preferred_element_type=F32)
        if nk == 1:
            o_ref[...] = part.astype(o_ref.dtype)
        else:
            acc_ref = o_ref if in_place else acc[0]

            @pl.when(k == 0)
            def _():
                acc_ref[...] = part

            @pl.when(k > 0)
            def _():
                acc_ref[...] += part

            if not in_place:
                @pl.when(k == nk - 1)
                def _():
                    o_ref[...] = acc_ref[...].astype(o_ref.dtype)

    if mode == "tn":
        a_spec = pl.BlockSpec((tk, tm), lambda i, j, k: (k, i))
    else:
        a_spec = pl.BlockSpec((tm, tk), lambda i, j, k: (i, k))
    if mode == "nt":
        bblk, bidx = (tn, tk), (lambda i, j, k: (j, k))
    else:
        bblk, bidx = (tk, tn), (lambda i, j, k: (k, j))
    if layer is not None:
        b_spec = pl.BlockSpec((None,) + bblk, lambda i, j, k: (layer,) + bidx(i, j, k))
    else:
        b_spec = pl.BlockSpec(bblk, bidx)
    return pl.pallas_call(
        body, name=name, grid=(M // tm, N // tn, nk),
        in_specs=[a_spec, b_spec],
        out_specs=pl.BlockSpec((tm, tn), lambda i, j, k: (i, j)),
        out_shape=jax.ShapeDtypeStruct((M, N), out_dtype),
        scratch_shapes=[pltpu.VMEM((tm, tn), F32)] if nk > 1 and not in_place else [],
        compiler_params=_params(("parallel", "parallel", "arbitrary")),
    )(a, b)


def _rows(S):
    return _pick(S, (256, 128, 64, 32, 16, 8))


def _row_spec(tr, width, col=0):
    return pl.BlockSpec((tr, width), lambda i: (i, col))


def _vec_spec(rows, width):
    return pl.BlockSpec((rows, width), lambda i: (0, 0))


def _rms(xv):
    return lax.rsqrt(jnp.mean(xv * xv, axis=-1, keepdims=True) + EPS)


def _norm_mod(x, y, g, w_eff, sh, name):
    S, D = x.shape
    tr = _rows(S)
    fused = y is not None

    def body(*refs):
        if fused:
            x_ref, y_ref, g_ref, w_ref, s_ref, x1_ref, h_ref = refs
            xv = x_ref[...] + g_ref[...] * y_ref[...]
            x1_ref[...] = xv
        else:
            x_ref, w_ref, s_ref, h_ref = refs
            xv = x_ref[...]
        h_ref[...] = (xv * _rms(xv) * w_ref[...] + s_ref[...]).astype(BF16)

    big, vec = _row_spec(tr, D), _vec_spec(1, D)
    if fused:
        ins, in_specs = (x, y, g, w_eff, sh), [big, big, vec, vec, vec]
        out_shape = (jax.ShapeDtypeStruct((S, D), F32), jax.ShapeDtypeStruct((S, D), BF16))
        out_specs = (big, big)
    else:
        ins, in_specs = (x, w_eff, sh), [big, vec, vec]
        out_shape = jax.ShapeDtypeStruct((S, D), BF16)
        out_specs = big
    out = pl.pallas_call(body, name=name, grid=(S // tr,), in_specs=in_specs, out_specs=out_specs,
                         out_shape=out_shape, compiler_params=_params(("parallel",)))(*ins)
    return out if fused else (None, out)


def _norm_mod_bwd(dh, x, w_eff, dres, name):
    S, D = x.shape
    tr = _rows(S)

    def body(dh_ref, x_ref, w_ref, r_ref, dx_ref, dsh_ref, dw_ref):
        xv = x_ref[...]
        dhv = dh_ref[...].astype(F32)
        r = _rms(xv)
        xn = xv * r
        dxn = dhv * w_ref[...]
        dx_ref[...] = r_ref[...] + r * (dxn - xn * jnp.mean(dxn * xn, axis=-1, keepdims=True))

        @pl.when(pl.program_id(0) == 0)
        def _():
            dsh_ref[...] = jnp.zeros_like(dsh_ref)
            dw_ref[...] = jnp.zeros_like(dw_ref)

        dsh_ref[...] += jnp.sum(dhv, axis=0, keepdims=True)
        dw_ref[...] += jnp.sum(dhv * xn, axis=0, keepdims=True)

    big, vec = _row_spec(tr, D), _vec_spec(1, D)
    return pl.pallas_call(
        body, name=name, grid=(S // tr,), in_specs=[big, big, vec, big], out_specs=(big, vec, vec),
        out_shape=(jax.ShapeDtypeStruct((S, D), F32), jax.ShapeDtypeStruct((1, D), F32), jax.ShapeDtypeStruct((1, D), F32)),
        compiler_params=_params(("arbitrary",)))(dh, x, w_eff, dres)


def _gate_bwd(dx, y, g, name):
    S, D = dx.shape
    tr = _rows(S)

    def body(dx_ref, y_ref, g_ref, dy_ref, dg_ref):
        dxv = dx_ref[...]
        dy_ref[...] = (dxv * g_ref[...]).astype(BF16)

        @pl.when(pl.program_id(0) == 0)
        def _():
            dg_ref[...] = jnp.zeros_like(dg_ref)

        dg_ref[...] += jnp.sum(dxv * y_ref[...], axis=0, keepdims=True)

    big, vec = _row_spec(tr, D), _vec_spec(1, D)
    return pl.pallas_call(
        body, name=name, grid=(S // tr,), in_specs=[big, big, vec], out_specs=(big, vec),
        out_shape=(jax.ShapeDtypeStruct((S, D), BF16), jax.ShapeDtypeStruct((1, D), F32)),
        compiler_params=_params(("arbitrary",)))(dx, y, g)


def _loss_head(x, y, g, gamma, target, name):
    S, D = x.shape
    tr = _rows(S)

    def body(x_ref, y_ref, g_ref, gm_ref, t_ref, dx_ref, loss_ref, dgm_ref):
        xv = x_ref[...] + g_ref[...] * y_ref[...]
        r = _rms(xv)
        xn = xv * r
        err = xn * gm_ref[...] - t_ref[...]
        dout = err * (1.0 / D)
        dxn = dout * gm_ref[...]
        dx_ref[...] = r * (dxn - xn * jnp.mean(dxn * xn, axis=-1, keepdims=True))

        @pl.when(pl.program_id(0) == 0)
        def _():
            loss_ref[...] = jnp.zeros_like(loss_ref)
            dgm_ref[...] = jnp.zeros_like(dgm_ref)

        loss_ref[...] += 0.5 * jnp.sum(jnp.mean(err * err, axis=-1, keepdims=True), axis=0, keepdims=True)
        dgm_ref[...] += jnp.sum(dout * xn, axis=0, keepdims=True)

    big, vec = _row_spec(tr, D), _vec_spec(1, D)
    return pl.pallas_call(
        body, name=name, grid=(S // tr,), in_specs=[big, big, vec, vec, big],
        out_specs=(big, _vec_spec(1, 1), vec),
        out_shape=(jax.ShapeDtypeStruct((S, D), F32), jax.ShapeDtypeStruct((1, 1), F32), jax.ShapeDtypeStruct((1, D), F32)),
        compiler_params=_params(("arbitrary",)))(x, y, g, gamma, target)


def _silu(a):
    return a * jax.nn.sigmoid(a)


def _swiglu(ab, name):
    S, F2 = ab.shape
    F = F2 // 2
    tr = _pick(S, (128, 64, 32, 16, 8))

    def body(a_ref, b_ref, f_ref):
        f_ref[...] = (_silu(a_ref[...].astype(F32)) * b_ref[...].astype(F32)).astype(BF16)

    return pl.pallas_call(
        body, name=name, grid=(S // tr,), in_specs=[_row_spec(tr, F, 0), _row_spec(tr, F, 1)],
        out_specs=_row_spec(tr, F), out_shape=jax.ShapeDtypeStruct((S, F), BF16),
        compiler_params=_params(("parallel",)))(ab, ab)


def _swiglu_bwd(ab, df, name):
    S, F2 = ab.shape
    F = F2 // 2
    tr = _pick(S, (128, 64, 32, 16, 8))

    def body(a_ref, b_ref, df_ref, da_ref, db_ref):
        a = a_ref[...].astype(F32)
        sg = jax.nn.sigmoid(a)
        dfv = df_ref[...].astype(F32)
        da_ref[...] = (dfv * b_ref[...].astype(F32) * (sg * (1.0 + a * (1.0 - sg)))).astype(BF16)
        db_ref[...] = (dfv * a * sg).astype(BF16)

    def body2(a_ref, b_ref, df_ref, o_ref):
        body(a_ref, b_ref, df_ref, o_ref.at[:, pl.ds(0, F)], o_ref.at[:, pl.ds(F, F)])

    return pl.pallas_call(
        body2, name=name, grid=(S // tr,),
        in_specs=[_row_spec(tr, F, 0), _row_spec(tr, F, 1), _row_spec(tr, F)],
        out_specs=_row_spec(tr, F2), out_shape=jax.ShapeDtypeStruct((S, F2), BF16),
        compiler_params=_params(("parallel",)))(ab, ab, df)


def _shift_rows(v, n):
    return pltpu.roll(v, n, 0)


def _conv_fwd(p, w8, name):
    S, D3 = p.shape
    D = D3 // 3
    tr = _rows(S)
    nb8 = tr // 8

    def body(gb_ref, gc_ref, hx_ref, gcp_ref, hxp_ref, w_ref, o_ref):
        i = pl.program_id(0)
        y = gc_ref[...].astype(F32) * hx_ref[...].astype(F32)
        yp = jnp.where(i > 0, gcp_ref[...].astype(F32) * hxp_ref[...].astype(F32), 0.0)
        w0, w1, w2 = w_ref[0:1, :], w_ref[1:2, :], w_ref[2:3, :]
        conv = w0 * _shift_rows(y, 2) + w1 * _shift_rows(y, 1) + w2 * y
        o_ref[...] = (gb_ref[...].astype(F32) * conv).astype(BF16)
        rid = lax.broadcasted_iota(jnp.int32, (8, D), 0)
        y8 = y[0:8, :]
        y1 = jnp.where(rid < 1, _shift_rows(yp, 1), _shift_rows(y8, 1))
        y2 = jnp.where(rid < 2, _shift_rows(yp, 2), _shift_rows(y8, 2))
        conv8 = w0 * y2 + w1 * y1 + w2 * y8
        o_ref[0:8, :] = (gb_ref[0:8, :].astype(F32) * conv8).astype(BF16)

    def col(c):
        return pl.BlockSpec((tr, D), lambda i: (i, c))

    def prev8(c):
        return pl.BlockSpec((8, D), lambda i: (jnp.maximum(i * nb8 - 1, 0), c))

    return pl.pallas_call(
        body, name=name, grid=(S // tr,),
        in_specs=[col(0), col(1), col(2), prev8(1), prev8(2), _vec_spec(8, D)],
        out_specs=_row_spec(tr, D), out_shape=jax.ShapeDtypeStruct((S, D), BF16),
        compiler_params=_params(("parallel",)))(p, p, p, p, p, w8)


def _conv_bwd(p, do, w8, name):
    S, D3 = p.shape
    D = D3 // 3
    tr = _rows(S)
    nb8 = tr // 8
    nt = S // tr

    def body(gb_ref, gc_ref, hx_ref, gcp_ref, hxp_ref, do_ref, gbn_ref, don_ref, w_ref, dp_ref, dw_ref):
        i = pl.program_id(0)
        gb = gb_ref[...].astype(F32)
        gc = gc_ref[...].astype(F32)
        hx = hx_ref[...].astype(F32)
        dov = do_ref[...].astype(F32)
        y = gc * hx
        yp = jnp.where(i > 0, gcp_ref[...].astype(F32) * hxp_ref[...].astype(F32), 0.0)
        dconv = dov * gb
        dcn = jnp.where(i < nt - 1, don_ref[...].astype(F32) * gbn_ref[...].astype(F32), 0.0)
        w0, w1, w2 = w_ref[0:1, :], w_ref[1:2, :], w_ref[2:3, :]
        rid = lax.broadcasted_iota(jnp.int32, (tr, D), 0)
        rid8 = lax.broadcasted_iota(jnp.int32, (8, D), 0)
        yp1 = jnp.concatenate([_shift_rows(yp, 1), jnp.zeros((tr - 8, D), F32)], axis=0)
        yp2 = jnp.concatenate([_shift_rows(yp, 2), jnp.zeros((tr - 8, D), F32)], axis=0)
        y1 = jnp.where(rid < 1, yp1, _shift_rows(y, 1))
        y2 = jnp.where(rid < 2, yp2, _shift_rows(y, 2))
        conv = w0 * y2 + w1 * y1 + w2 * y
        dn1 = jnp.concatenate([jnp.zeros((tr - 8, D), F32), _shift_rows(dcn, 7)], axis=0)
        dn2 = jnp.concatenate([jnp.zeros((tr - 8, D), F32), _shift_rows(dcn, 6)], axis=0)
        d1 = jnp.where(rid >= tr - 1, dn1, _shift_rows(dconv, tr - 1))
        d2 = jnp.where(rid >= tr - 2, dn2, _shift_rows(dconv, tr - 2))
        dy = w2 * dconv + w1 * d1 + w0 * d2
        dp_ref[:, pl.ds(0, D)] = (dov * conv).astype(BF16)
        dp_ref[:, pl.ds(D, D)] = (dy * hx).astype(BF16)
        dp_ref[:, pl.ds(2 * D, D)] = (dy * gc).astype(BF16)

        @pl.when(i == 0)
        def _():
            dw_ref[...] = jnp.zeros_like(dw_ref)

        upd = jnp.where(rid8 == 0, jnp.sum(dconv * y2, axis=0, keepdims=True),
                        jnp.where(rid8 == 1, jnp.sum(dconv * y1, axis=0, keepdims=True),
                                  jnp.where(rid8 == 2, jnp.sum(dconv * y, axis=0, keepdims=True), 0.0)))
        dw_ref[...] += upd

    def col(c):
        return pl.BlockSpec((tr, D), lambda i: (i, c))

    def prev8(c):
        return pl.BlockSpec((8, D), lambda i: (jnp.maximum(i * nb8 - 1, 0), c))

    def next8(c):
        return pl.BlockSpec((8, D), lambda i: (jnp.minimum((i + 1) * nb8, S // 8 - 1), c))

    return pl.pallas_call(
        body, name=name, grid=(nt,),
        in_specs=[col(0), col(1), col(2), prev8(1), prev8(2), col(0), next8(0), next8(0), _vec_spec(8, D)],
        out_specs=(_row_spec(tr, D3), _vec_spec(8, D)),
        out_shape=(jax.ShapeDtypeStruct((S, D3), BF16), jax.ShapeDtypeStruct((8, D), F32)),
        compiler_params=_params(("arbitrary",)))(p, p, p, p, p, do, p, do, w8)


_GELU_C = math.sqrt(2.0 / math.pi)


def _gelu(v):
    return 0.5 * v * (1.0 + jnp.tanh(_GELU_C * (v + 0.044715 * v * v * v)))


def _gelu_grad(v):
    t = jnp.tanh(_GELU_C * (v + 0.044715 * v * v * v))
    return 0.5 * (1.0 + t) + 0.5 * v * (1.0 - t * t) * _GELU_C * (1.0 + 3.0 * 0.044715 * v * v)


def _tril(w):
    r = lax.broadcasted_iota(jnp.int32, (CHUNK, CHUNK), 0)
    c = lax.broadcasted_iota(jnp.int32, (CHUNK, CHUNK), 1)
    return jnp.where(r >= c, w, 0.0)


def _sgu_fwd(z, w, bT, n_attn, name):
    S = z.shape[0]
    G = w.shape[0]
    W = G * CHUNK
    tr = _pick(S, (512, 256, 128))
    ucol = 3 * n_attn * HEAD_DIM // W

    def body(u_ref, v_ref, w_ref, b_ref, o_ref):
        for g in range(G):
            wt = _tril(w_ref[g]).astype(BF16)
            for ci in range(tr // CHUNK):
                rows, cols = pl.ds(ci * CHUNK, CHUNK), pl.ds(g * CHUNK, CHUNK)
                gv = _gelu(v_ref[rows, cols].astype(F32)).astype(BF16)
                mixed = jnp.dot(wt, gv, preferred_element_type=F32) + b_ref[:, g:g + 1]
                o_ref[rows, cols] = (_gelu(u_ref[rows, cols].astype(F32)) * mixed).astype(BF16)

    return pl.pallas_call(
        body, name=name, grid=(S // tr,),
        in_specs=[_row_spec(tr, W, ucol), _row_spec(tr, W, ucol + 1),
                  pl.BlockSpec((G, CHUNK, CHUNK), lambda i: (0, 0, 0)), _vec_spec(CHUNK, G)],
        out_specs=_row_spec(tr, W), out_shape=jax.ShapeDtypeStruct((S, W), BF16),
        compiler_params=_params(("parallel",)))(z, z, w, bT)


def _sgu_bwd(z, dcat, w, bT, n_attn, name):
    S = z.shape[0]
    G = w.shape[0]
    W = G * CHUNK
    tr = _pick(S, (512, 256, 128))
    ucol = 3 * n_attn * HEAD_DIM // W
    dcol = n_attn * HEAD_DIM // W

    def body(u_ref, v_ref, d_ref, w_ref, b_ref, o_ref, dw_ref, db_ref):
        @pl.when(pl.program_id(0) == 0)
        def _():
            dw_ref[...] = jnp.zeros_like(dw_ref)
            db_ref[...] = jnp.zeros_like(db_ref)

        lane = lax.broadcasted_iota(jnp.int32, (CHUNK, G), 1)
        for g in range(G):
            wtf = _tril(w_ref[g])
            wt = wtf.astype(BF16)
            dw_acc = jnp.zeros((CHUNK, CHUNK), F32)
            db_acc = jnp.zeros((CHUNK, 1), F32)
            for ci in range(tr // CHUNK):
                rows, cols = pl.ds(ci * CHUNK, CHUNK), pl.ds(g * CHUNK, CHUNK)
                uv = u_ref[rows, cols].astype(F32)
                vv = v_ref[rows, cols].astype(F32)
                dov = d_ref[rows, cols]
                gv = _gelu(vv).astype(BF16)
                mixed = jnp.dot(wt, gv, preferred_element_type=F32) + b_ref[:, g:g + 1]
                dmixed = dov * _gelu(uv)
                dmb = dmixed.astype(BF16)
                dgv = lax.dot_general(wt, dmb, (((0,), (0,)), ((), ())), preferred_element_type=F32)
                o_ref[rows, cols] = (dov * mixed * _gelu_grad(uv)).astype(BF16)
                o_ref[rows, pl.ds(W + g * CHUNK, CHUNK)] = (dgv * _gelu_grad(vv)).astype(BF16)
                dw_acc += lax.dot_general(dmb, gv, (((1,), (1,)), ((), ())), preferred_element_type=F32)
                db_acc += jnp.sum(dmixed, axis=1, keepdims=True)
            dw_ref[g] += _tril(dw_acc)
            db_ref[...] += jnp.where(lane == g, db_acc, 0.0)

    return pl.pallas_call(
        body, name=name, grid=(S // tr,),
        in_specs=[_row_spec(tr, W, ucol), _row_spec(tr, W, ucol + 1), _row_spec(tr, W, dcol),
                  pl.BlockSpec((G, CHUNK, CHUNK), lambda i: (0, 0, 0)), _vec_spec(CHUNK, G)],
        out_specs=(_row_spec(tr, 2 * W), pl.BlockSpec((G, CHUNK, CHUNK), lambda i: (0, 0, 0)), _vec_spec(CHUNK, G)),
        out_shape=(jax.ShapeDtypeStruct((S, 2 * W), BF16), jax.ShapeDtypeStruct((G, CHUNK, CHUNK), F32),
                   jax.ShapeDtypeStruct((CHUNK, G), F32)),
        compiler_params=_params(("arbitrary",)))(z, z, dcat, w, bT)


def _rope(v, cs, sa, sb):
    return v * cs + pltpu.roll(v, HEAD_DIM - ROPE_DIM // 2, 1) * sa + pltpu.roll(v, ROPE_DIM // 2, 1) * sb


def _rope_t(d, cs, sa, sb):
    return d * cs + pltpu.roll(d * sa, ROPE_DIM // 2, 1) + pltpu.roll(d * sb, HEAD_DIM - ROPE_DIM // 2, 1)


def _qkv_prep(z, cs, sa, sb, n_attn, name):
    S = z.shape[0]
    A = n_attn * HEAD_DIM
    tr = _rows(S)

    def body(q_ref, k_ref, v_ref, c_ref, a_ref, b_ref, qo_ref, ko_ref, vo_ref):
        cv, av, bv = c_ref[...], a_ref[...], b_ref[...]
        for h in range(n_attn):
            cols = pl.ds(h * HEAD_DIM, HEAD_DIM)
            qo_ref[:, cols] = _rope(q_ref[:, cols].astype(F32), cv, av, bv)
            ko_ref[:, cols] = _rope(k_ref[:, cols].astype(F32), cv, av, bv)
        vo_ref[...] = v_ref[...].astype(F32)

    tab = _row_spec(tr, HEAD_DIM)
    out = jax.ShapeDtypeStruct((S, A), F32)
    return pl.pallas_call(
        body, name=name, grid=(S // tr,),
        in_specs=[_row_spec(tr, A, 0), _row_spec(tr, A, 1), _row_spec(tr, A, 2), tab, tab, tab],
        out_specs=(_row_spec(tr, A),) * 3, out_shape=(out,) * 3,
        compiler_params=_params(("parallel",)))(z, z, z, cs, sa, sb)


def _dqkv_post(dq, dk, dv, cs, sa, sb, n_attn, name):
    S, A = dq.shape
    tr = _rows(S)

    def body(q_ref, k_ref, v_ref, c_ref, a_ref, b_ref, o_ref):
        cv, av, bv = c_ref[...], a_ref[...], b_ref[...]
        for h in range(n_attn):
            cols = pl.ds(h * HEAD_DIM, HEAD_DIM)
            o_ref[:, pl.ds(h * HEAD_DIM, HEAD_DIM)] = _rope_t(q_ref[:, cols], cv, av, bv).astype(BF16)
            o_ref[:, pl.ds(A + h * HEAD_DIM, HEAD_DIM)] = _rope_t(k_ref[:, cols], cv, av, bv).astype(BF16)
        o_ref[:, pl.ds(2 * A, A)] = v_ref[...].astype(BF16)

    tab = _row_spec(tr, HEAD_DIM)
    return pl.pallas_call(
        body, name=name, grid=(S // tr,),
        in_specs=[_row_spec(tr, A)] * 3 + [tab, tab, tab],
        out_specs=_row_spec(tr, 3 * A), out_shape=jax.ShapeDtypeStruct((S, 3 * A), BF16),
        compiler_params=_params(("parallel",)))(dq, dk, dv, cs, sa, sb)


def _block_rows(d, S, it):
    nblk = S // (d * ATTN_BLOCK)
    r = it // nblk
    jb = it % nblk
    q0 = r + d * ATTN_BLOCK * jb
    k0 = r + d * ATTN_BLOCK * jnp.maximum(jb - 1, 0)
    off = jnp.where(jb > 0, ATTN_BLOCK, 0)
    return q0, k0, off


def _band(off):
    a = lax.broadcasted_iota(jnp.int32, (ATTN_BLOCK, 2 * ATTN_BLOCK), 0) + off
    kj = lax.broadcasted_iota(jnp.int32, (ATTN_BLOCK, 2 * ATTN_BLOCK), 1)
    return (kj <= a) & (kj >= a - ATTN_BLOCK)


_NT = (((1,), (1,)), ((), ()))
_TN = (((0,), (0,)), ((), ()))


def _attn_fwd(q, k, v, n_attn, name):
    S, A = q.shape
    scale = HEAD_DIM ** -0.5

    def body(q_hbm, k_hbm, v_hbm, o_hbm, lse_hbm, qs, ks, vs, acc, ms, ls, ob, sem):
        h = pl.program_id(0)
        cols = pl.ds(pl.multiple_of(h * HEAD_DIM, HEAD_DIM), HEAD_DIM)
        cps = [pltpu.make_async_copy(src.at[:, cols], dst, sem.at[i])
               for i, (src, dst) in enumerate(((q_hbm, qs), (k_hbm, ks), (v_hbm, vs)))]
        for cp in cps:
            cp.start()
        acc[...] = jnp.zeros_like(acc)
        ms[...] = jnp.full_like(ms, MASKED)
        ls[...] = jnp.zeros_like(ls)
        for cp in cps:
            cp.wait()
        for d in DILATIONS:
            def step(it, carry, d=d):
                q0, k0, off = _block_rows(d, S, it)
                qrows = pl.ds(q0, ATTN_BLOCK, stride=d)
                krows = pl.ds(k0, 2 * ATTN_BLOCK, stride=d)
                qb = qs[qrows, :].astype(BF16)
                kb = ks[krows, :].astype(BF16)
                vb = vs[krows, :].astype(BF16)
                s = lax.dot_general(qb, kb, _NT, preferred_element_type=F32) * scale
                s = jnp.where(_band(off), s, MASKED)
                m_old = ms[qrows, :]
                m_new = jnp.maximum(m_old, jnp.max(s, axis=-1, keepdims=True))
                alpha = jnp.exp(m_old - m_new)
                p = jnp.exp(s - m_new)
                ls[qrows, :] = alpha * ls[qrows, :] + jnp.sum(p, axis=-1, keepdims=True)
                acc[qrows, :] = alpha * acc[qrows, :] + jnp.dot(p.astype(BF16), vb, preferred_element_type=F32)
                ms[qrows, :] = m_new
                return carry
            lax.fori_loop(0, S // ATTN_BLOCK, step, 0)
        ob[...] = (acc[...] / ls[...]).astype(BF16)
        ms[...] = ms[...] + jnp.log(ls[...])
        out = [pltpu.make_async_copy(ob, o_hbm.at[:, cols], sem.at[0]),
               pltpu.make_async_copy(ms, lse_hbm.at[h], sem.at[1])]
        for cp in out:
            cp.start()
        for cp in out:
            cp.wait()

    return pl.pallas_call(
        body, name=name, grid=(n_attn,), in_specs=[ANY, ANY, ANY], out_specs=(ANY, ANY),
        out_shape=(jax.ShapeDtypeStruct((S, A), BF16), jax.ShapeDtypeStruct((n_attn, S, 1), F32)),
        scratch_shapes=[pltpu.VMEM((S, HEAD_DIM), F32)] * 4 + [pltpu.VMEM((S, 1), F32)] * 2
        + [pltpu.VMEM((S, HEAD_DIM), BF16), pltpu.SemaphoreType.DMA((3,))],
        compiler_params=_params(("arbitrary",)))(q, k, v)


def _attn_bwd(q, k, v, o, lse, dcat, n_attn, name):
    S, A = q.shape
    scale = HEAD_DIM ** -0.5

    def body(q_hbm, k_hbm, v_hbm, o_hbm, lse_hbm, do_hbm, dq_hbm, dk_hbm, dv_hbm,
             qs, ks, vs, dos, dqs, dks, dvs, lses, dls, ob, sem):
        h = pl.program_id(0)
        cols = pl.ds(pl.multiple_of(h * HEAD_DIM, HEAD_DIM), HEAD_DIM)
        cps = [pltpu.make_async_copy(src.at[:, cols], dst, sem.at[i])
               for i, (src, dst) in enumerate(((q_hbm, qs), (k_hbm, ks), (v_hbm, vs), (do_hbm, dos), (o_hbm, ob)))]
        cps.append(pltpu.make_async_copy(lse_hbm.at[h], lses, sem.at[5]))
        for cp in cps:
            cp.start()
        dqs[...] = jnp.zeros_like(dqs)
        dks[...] = jnp.zeros_like(dks)
        dvs[...] = jnp.zeros_like(dvs)
        for cp in cps:
            cp.wait()
        dls[...] = jnp.sum(dos[...] * ob[...].astype(F32), axis=-1, keepdims=True)
        for d in DILATIONS:
            def step(it, carry, d=d):
                q0, k0, off = _block_rows(d, S, it)
                qrows = pl.ds(q0, ATTN_BLOCK, stride=d)
                krows = pl.ds(k0, 2 * ATTN_BLOCK, stride=d)
                qb = qs[qrows, :].astype(BF16)
                kb = ks[krows, :].astype(BF16)
                vb = vs[krows, :].astype(BF16)
                dob = dos[qrows, :].astype(BF16)
                s = lax.dot_general(qb, kb, _NT, preferred_element_type=F32) * scale
                p = jnp.where(_band(off), jnp.exp(s - lses[qrows, :]), 0.0)
                pb = p.astype(BF16)
                dp = lax.dot_general(dob, vb, _NT, preferred_element_type=F32)
                ds = (p * (dp - dls[qrows, :]) * scale).astype(BF16)
                dvs[krows, :] = dvs[krows, :] + lax.dot_general(pb, dob, _TN, preferred_element_type=F32)
                dks[krows, :] = dks[krows, :] + lax.dot_general(ds, qb, _TN, preferred_element_type=F32)
                dqs[qrows, :] = dqs[qrows, :] + jnp.dot(ds, kb, preferred_element_type=F32)
                return carry
            lax.fori_loop(0, S // ATTN_BLOCK, step, 0)
        out = [pltpu.make_async_copy(src, dst.at[:, cols], sem.at[i])
               for i, (src, dst) in enumerate(((dqs, dq_hbm), (dks, dk_hbm), (dvs, dv_hbm)))]
        for cp in out:
            cp.start()
        for cp in out:
            cp.wait()

    grad = jax.ShapeDtypeStruct((S, A), F32)
    return pl.pallas_call(
        body, name=name, grid=(n_attn,), in_specs=[ANY] * 6, out_specs=(ANY, ANY, ANY), out_shape=(grad,) * 3,
        scratch_shapes=[pltpu.VMEM((S, HEAD_DIM), F32)] * 7 + [pltpu.VMEM((S, 1), F32)] * 2
        + [pltpu.VMEM((S, HEAD_DIM), BF16), pltpu.SemaphoreType.DMA((6,))],
        compiler_params=_params(("arbitrary",)))(q, k, v, o, lse, dcat)


def _ada_fwd(c_act, ada_w, name):
    L, D, n = ada_w.shape
    tn = _pick(n, (512, 256, 128))

    def body(c_ref, w_ref, o_ref):
        o_ref[...] = jnp.dot(c_ref[...], w_ref[...], preferred_element_type=F32)

    return pl.pallas_call(
        body, name=name, grid=(L, n // tn),
        in_specs=[pl.BlockSpec((N_DEV, D), lambda l, j: (0, 0)), pl.BlockSpec((None, D, tn), lambda l, j: (l, 0, j))],
        out_specs=pl.BlockSpec((None, N_DEV, tn), lambda l, j: (l, 0, j)),
        out_shape=jax.ShapeDtypeStruct((L, N_DEV, n), F32),
        compiler_params=_params(("parallel", "parallel")))(c_act, ada_w)


def _ada_bwd(c_act, dmod, name):
    L, _, n = dmod.shape
    D = c_act.shape[1]
    tn = _pick(n, (512, 256, 128))

    def body(c_ref, d_ref, o_ref):
        o_ref[...] = lax.dot_general(c_ref[...], d_ref[...], _TN, preferred_element_type=F32)

    return pl.pallas_call(
        body, name=name, grid=(L, n // tn),
        in_specs=[pl.BlockSpec((N_DEV, D), lambda l, j: (0, 0)), pl.BlockSpec((None, N_DEV, tn), lambda l, j: (l, 0, j))],
        out_specs=pl.BlockSpec((None, D, tn), lambda l, j: (l, 0, j)),
        out_shape=jax.ShapeDtypeStruct((L, D, n), F32),
        compiler_params=_params(("parallel", "parallel")))(c_act, dmod)


def _adamw(w, g, m, v, name):
    shape = w.shape
    C = shape[-1]
    R = w.size // C
    w2, g2, m2, v2 = (t.reshape(R, C) for t in (w, g, m, v))
    tr = _pick(R, (256, 128, 64, 32, 16, 8))
    tc = _pick(C, (2048, 1536, 1408, 1024, 512, 256, 128))

    def body(w_ref, g_ref, m_ref, v_ref, d_ref, mo_ref, vo_ref):
        gv = g_ref[...]
        mn = ADAM_B1 * m_ref[...] + (1.0 - ADAM_B1) * gv
        vn = ADAM_B2 * v_ref[...] + (1.0 - ADAM_B2) * (gv * gv)
        m_hat = mn / (1.0 - ADAM_B1 ** ADAM_STEP)
        v_hat = vn / (1.0 - ADAM_B2 ** ADAM_STEP)
        d_ref[...] = -ADAM_LR * (m_hat / (jnp.sqrt(v_hat) + ADAM_EPS) + ADAM_WD * w_ref[...])
        mo_ref[...] = mn
        vo_ref[...] = vn

    spec = pl.BlockSpec((tr, tc), lambda i, j: (i, j))
    out = jax.ShapeDtypeStruct((R, C), F32)
    d, mn, vn = pl.pallas_call(
        body, name=name, grid=(R // tr, C // tc), in_specs=[spec] * 4, out_specs=(spec,) * 3, out_shape=(out,) * 3,
        compiler_params=_params(("parallel", "parallel")))(w2, g2, m2, v2)
    return d.reshape(shape), mn.reshape(shape), vn.reshape(shape)


def _sum_leading(t, name):
    n, R, C = t.shape
    tr = _pick(R, (256, 128, 64, 32, 16, 8))

    def body(t_ref, o_ref):
        acc = t_ref[0]
        for i in range(1, n):
            acc = acc + t_ref[i]
        o_ref[...] = acc

    return pl.pallas_call(
        body, name=name, grid=(R // tr,), in_specs=[pl.BlockSpec((n, tr, C), lambda i: (0, i, 0))],
        out_specs=pl.BlockSpec((tr, C), lambda i: (i, 0)), out_shape=jax.ShapeDtypeStruct((R, C), F32),
        compiler_params=_params(("parallel",)))(t)


def _coords():
    return lax.axis_index("x"), lax.axis_index("y"), lax.axis_index("c")


def _other_chips(x, y):
    return [(1 - x, y), (x, 1 - y), (1 - x, 1 - y)]


def _all_gather8(t, name):
    R, C = t.shape

    def body(x_ref, out_ref, send_sems, recv_sems, local_sem):
        x, y, c = _coords()
        me, sibling = (x, y, c), (x, y, 1 - c)
        chips = _other_chips(x, y)

        def slot(px, py, pc):
            return out_ref.at[4 * px + 2 * py + pc]

        def copy(k, block, to, src=None):
            return pltpu.make_async_remote_copy(
                src_ref=slot(*block) if src is None else src, dst_ref=slot(*block),
                send_sem=send_sems.at[k], recv_sem=recv_sems.at[k], device_id=to, device_id_type=MESH)

        mine = pltpu.make_async_copy(x_ref, slot(*me), local_sem)
        mine.start()
        first = [copy(0, me, sibling, src=x_ref)]
        first += [copy(1 + j, me, (*chip, c), src=x_ref) for j, chip in enumerate(chips)]
        for cp in first:
            cp.start()
        passed = [copy(4 + j, (*chip, c), sibling) for j, chip in enumerate(chips)]
        for j, chip in enumerate(chips):
            copy(1 + j, (*chip, c), me).wait_recv()
            passed[j].start()
        copy(0, sibling, me).wait_recv()
        for j, chip in enumerate(chips):
            copy(4 + j, (*chip, 1 - c), me).wait_recv()
        for cp in first + passed:
            cp.wait_send()
        mine.wait()

    return pl.pallas_call(
        body, name=name, out_shape=jax.ShapeDtypeStruct((N_DEV, R, C), t.dtype),
        in_specs=[pl.BlockSpec(memory_space=pltpu.VMEM)], out_specs=pl.BlockSpec(memory_space=pltpu.VMEM),
        scratch_shapes=[pltpu.SemaphoreType.DMA((7,)), pltpu.SemaphoreType.DMA((7,)), pltpu.SemaphoreType.DMA],
        compiler_params=pltpu.CompilerParams(vmem_limit_bytes=VMEM_LIMIT_BYTES))(t)


def _window(ref, r0, nr, c0, nc):
    return ref.at[pl.ds(r0, nr), pl.ds(c0, nc)]


def _gather_weights(shards, plan, out_shapes, name):
    n_in = len(shards)
    n_out = len(out_shapes)
    n_cp = 3 * n_in

    def body(*refs):
        ins, outs = refs[:n_in], refs[n_in:n_in + n_out]
        send_sems, recv_sems, fsend_sems, frecv_sems, local_sems = refs[n_in + n_out:]
        x, y, c = _coords()
        sibling = (x, y, 1 - c)
        chips = _other_chips(x, y)
        me_chip = 2 * x + y

        def place(i, chip_idx, half):
            o, kind, base, _ = plan[i]
            r, cs = ins[i].shape[1], ins[i].shape[2]
            rh = r // 2
            if kind == "row":
                return _window(outs[o], base + chip_idx * r + half * rh, rh, 0, cs)
            return _window(outs[o], half * rh, rh, base + chip_idx * cs, cs)

        local = []
        for i in range(n_in):
            for half in range(2):
                rh = ins[i].shape[1] // 2
                cp = pltpu.make_async_copy(ins[i].at[plan[i][3], pl.ds(half * rh, rh), :], place(i, me_chip, half),
                                           local_sems.at[2 * i + half])
                cp.start()
                local.append(cp)
        sends = []
        for i in range(n_in):
            rh = ins[i].shape[1] // 2
            for j, chip in enumerate(chips):
                cp = pltpu.make_async_remote_copy(
                    src_ref=ins[i].at[plan[i][3], pl.ds(c * rh, rh), :], dst_ref=place(i, me_chip, c),
                    send_sem=send_sems.at[3 * i + j], recv_sem=recv_sems.at[3 * i + j],
                    device_id=(*chip, c), device_id_type=MESH)
                cp.start()
                sends.append(cp)
        passed = []
        for i in range(n_in):
            for j, chip in enumerate(chips):
                their = 2 * chip[0] + chip[1]
                landed = place(i, their, c)
                pltpu.make_async_remote_copy(
                    src_ref=landed, dst_ref=landed, send_sem=send_sems.at[3 * i + j], recv_sem=recv_sems.at[3 * i + j],
                    device_id=(*chip, c), device_id_type=MESH).wait_recv()
                cp = pltpu.make_async_remote_copy(
                    src_ref=landed, dst_ref=landed, send_sem=fsend_sems.at[3 * i + j], recv_sem=frecv_sems.at[3 * i + j],
                    device_id=sibling, device_id_type=MESH)
                cp.start()
                passed.append(cp)
        for i in range(n_in):
            for j, chip in enumerate(chips):
                their = 2 * chip[0] + chip[1]
                landed = place(i, their, 1 - c)
                pltpu.make_async_remote_copy(
                    src_ref=landed, dst_ref=landed, send_sem=fsend_sems.at[3 * i + j], recv_sem=frecv_sems.at[3 * i + j],
                    device_id=sibling, device_id_type=MESH).wait_recv()
        for cp in sends + passed:
            cp.wait_send()
        for cp in local:
            cp.wait()

    return pl.pallas_call(
        body, name=name, in_specs=[ANY] * n_in, out_specs=tuple([ANY] * n_out),
        out_shape=tuple(jax.ShapeDtypeStruct(s, BF16) for s in out_shapes),
        scratch_shapes=[pltpu.SemaphoreType.DMA((n_cp,))] * 4 + [pltpu.SemaphoreType.DMA((2 * n_in,))],
        compiler_params=pltpu.CompilerParams(vmem_limit_bytes=VMEM_LIMIT_BYTES))(*shards)


def _sibling_exchange(srcs, name):
    n = len(srcs)

    def body(*refs):
        ins, outs = refs[:n], refs[n:2 * n]
        send_sems, recv_sems = refs[2 * n:]
        x, y, c = _coords()
        cps = []
        for i in range(n):
            cp = pltpu.make_async_remote_copy(
                src_ref=ins[i].at[1 - c], dst_ref=outs[i], send_sem=send_sems.at[i], recv_sem=recv_sems.at[i],
                device_id=(x, y, 1 - c), device_id_type=MESH)
            cp.start()
            cps.append(cp)
        for cp in cps:
            cp.wait()

    return pl.pallas_call(
        body, name=name, in_specs=[ANY] * n, out_specs=tuple([ANY] * n),
        out_shape=tuple(jax.ShapeDtypeStruct(s.shape[1:], s.dtype) for s in srcs),
        scratch_shapes=[pltpu.SemaphoreType.DMA((n,))] * 2,
        compiler_params=pltpu.CompilerParams(vmem_limit_bytes=VMEM_LIMIT_BYTES))(*srcs)


def _chip_scatter(parts, plan, name):
    n = len(parts)

    def shard_shape(i):
        kind, _, size = plan[i]
        R, C = parts[i].shape
        return (size, C) if kind == "row" else (R, size)

    def body(*refs):
        ins, outs = refs[:n], refs[n:2 * n]
        send_sems, recv_sems = refs[2 * n:]
        x, y, c = _coords()
        chips = _other_chips(x, y)
        cps = []
        for i in range(n):
            kind, base, size = plan[i]
            R, C = ins[i].shape
            for j, chip in enumerate(chips):
                their = 2 * chip[0] + chip[1]
                if kind == "row":
                    src = _window(ins[i], base + their * size, size, 0, C)
                else:
                    src = _window(ins[i], 0, R, base + their * size, size)
                cp = pltpu.make_async_remote_copy(
                    src_ref=src, dst_ref=outs[i].at[j], send_sem=send_sems.at[3 * i + j],
                    recv_sem=recv_sems.at[3 * i + j], device_id=(*chip, c), device_id_type=MESH)
                cp.start()
                cps.append(cp)
        for cp in cps:
            cp.wait()

    return pl.pallas_call(
        body, name=name, in_specs=[ANY] * n, out_specs=tuple([ANY] * n),
        out_shape=tuple(jax.ShapeDtypeStruct((3,) + shard_shape(i), F32) for i in range(n)),
        scratch_shapes=[pltpu.SemaphoreType.DMA((3 * n,))] * 2,
        compiler_params=pltpu.CompilerParams(vmem_limit_bytes=VMEM_LIMIT_BYTES))(*parts)


def _sibling_share(halves, name):
    n = len(halves)

    def body(*refs):
        ins, outs = refs[:n], refs[n:2 * n]
        send_sems, recv_sems, local_sems = refs[2 * n:]
        x, y, c = _coords()
        cps, loc = [], []
        for i in range(n):
            lc = pltpu.make_async_copy(ins[i], outs[i].at[c], local_sems.at[i])
            lc.start()
            loc.append(lc)
            cp = pltpu.make_async_remote_copy(
                src_ref=ins[i], dst_ref=outs[i].at[c], send_sem=send_sems.at[i], recv_sem=recv_sems.at[i],
                device_id=(x, y, 1 - c), device_id_type=MESH)
            cp.start()
            cps.append(cp)
        for cp in cps:
            cp.wait()
        for lc in loc:
            lc.wait()

    return pl.pallas_call(
        body, name=name, in_specs=[ANY] * n, out_specs=tuple([ANY] * n),
        out_shape=tuple(jax.ShapeDtypeStruct((2,) + h.shape, h.dtype) for h in halves),
        scratch_shapes=[pltpu.SemaphoreType.DMA((n,))] * 3,
        compiler_params=pltpu.CompilerParams(vmem_limit_bytes=VMEM_LIMIT_BYTES))(*halves)


def _add_half(full3, recv, core, name):
    _, Rh, C = full3.shape
    tr = _pick(Rh, (256, 176, 128, 64, 32, 16, 8))
    tc = _pick(C, (2048, 1536, 1408, 1024, 512, 256, 128))

    def body(c_ref, a_ref, b_ref, o_ref):
        o_ref[...] = a_ref[...] + b_ref[...]

    return pl.pallas_call(
        body, name=name,
        grid_spec=pltpu.PrefetchScalarGridSpec(
            num_scalar_prefetch=1, grid=(Rh // tr, C // tc),
            in_specs=[pl.BlockSpec((None, tr, tc), lambda i, j, cr: (cr[0], i, j)),
                      pl.BlockSpec((tr, tc), lambda i, j, cr: (i, j))],
            out_specs=pl.BlockSpec((tr, tc), lambda i, j, cr: (i, j))),
        out_shape=jax.ShapeDtypeStruct((Rh, C), F32),
        compiler_params=_params(("parallel", "parallel")))(core, full3, recv)


def _add_scattered(part, recv, kind, base, size, chip, name):
    _, rs, cs = recv.shape
    tr = _pick(rs, (256, 176, 128, 64, 32, 16, 8))
    tc = _pick(cs, (2048, 1536, 1408, 1024, 512, 256, 128))
    assert base % size == 0
    if kind == "row":
        pidx = lambda i, j, cr: ((base // size + cr[0]) * (rs // tr) + i, j)
    else:
        pidx = lambda i, j, cr: (i, (base // size + cr[0]) * (cs // tc) + j)

    def body(c_ref, a_ref, r_ref, o_ref):
        o_ref[...] = ((a_ref[...] + r_ref[0]) + r_ref[1]) + r_ref[2]

    return pl.pallas_call(
        body, name=name,
        grid_spec=pltpu.PrefetchScalarGridSpec(
            num_scalar_prefetch=1, grid=(rs // tr, cs // tc),
            in_specs=[pl.BlockSpec((tr, tc), pidx), pl.BlockSpec((3, tr, tc), lambda i, j, cr: (0, i, j))],
            out_specs=pl.BlockSpec((tr, tc), lambda i, j, cr: (i, j))),
        out_shape=jax.ShapeDtypeStruct((rs, cs), F32),
        compiler_params=_params(("parallel", "parallel")))(chip, part, recv)


def _reduce_scatter(grads, plan, core, chip, tag):
    threes = []
    for g, windows in zip(grads, plan):
        R, C = g.shape
        if windows[0][0] == "row":
            size = windows[0][2]
            t = g.reshape(N_CHIPS, 2, size // 2, C).transpose(1, 0, 2, 3).reshape(2, R // 2, C)
        else:
            t = g.reshape(2, R // 2, C)
        threes.append(t)
    recv = _sibling_exchange(threes, f"rs_swap_{tag}")
    parts = [_add_half(t, r, core, f"rs_add_half_{tag}_{i}") for i, (t, r) in enumerate(zip(threes, recv))]
    flat_parts, flat_plan, owner = [], [], []
    for i, (p, windows) in enumerate(zip(parts, plan)):
        for kind, base, size in windows:
            flat_parts.append(p)
            flat_plan.append((kind, base // 2, size // 2) if kind == "row" else (kind, base, size))
            owner.append(i)
    got = _chip_scatter(flat_parts, flat_plan, f"rs_scatter_{tag}")
    halves = [_add_scattered(p, r, k, b, s, chip, f"rs_add_{tag}_{n}")
              for n, (p, r, (k, b, s)) in enumerate(zip(flat_parts, got, flat_plan))]
    both = _sibling_share(halves, f"rs_share_{tag}")
    out = []
    for t in both:
        _, rh, cs = t.shape
        out.append(t.reshape(2 * rh, cs))
    return out


def _rope_tables(positions, S):
    half = ROPE_DIM // 2
    inv_freq = ROPE_THETA ** (-jnp.arange(0, ROPE_DIM, 2, dtype=F32) / ROPE_DIM)
    ang = positions.reshape(S, 1).astype(F32) * inv_freq[None, :]
    cos, sin = jnp.cos(ang), jnp.sin(ang)
    zeros = jnp.zeros((S, half), F32)
    rest0 = jnp.zeros((S, HEAD_DIM - ROPE_DIM), F32)
    cs = jnp.concatenate([cos, cos, jnp.ones((S, HEAD_DIM - ROPE_DIM), F32)], axis=1)
    sa = jnp.concatenate([-sin, zeros, rest0], axis=1)
    sb = jnp.concatenate([zeros, sin, rest0], axis=1)
    return cs, sa, sb


def kernel(x, c, positions, ada_w, ada_b, norm_mix, norm_ffn, ab_w_in, sgu_w, sgu_b, ab_w_out, conv_w_in, conv_w, conv_w_out, ffn_w_gate, ffn_w_up, ffn_w_down, final_norm, loss_target, m_ada_w, m_ada_b, m_norm_mix, m_norm_ffn, m_ab_w_in, m_sgu_w, m_sgu_b, m_ab_w_out, m_conv_w_in, m_conv_w, m_conv_w_out, m_ffn_w_gate, m_ffn_w_up, m_ffn_w_down, m_final_norm, v_ada_w, v_ada_b, v_norm_mix, v_norm_ffn, v_ab_w_in, v_sgu_w, v_sgu_b, v_ab_w_out, v_conv_w_in, v_conv_w, v_conv_w_out, v_ffn_w_gate, v_ffn_w_up, v_ffn_w_down, v_final_norm):
    S, D = x.shape[1], x.shape[2]
    L = ada_w.shape[0]
    n_mix_heads = D // HEAD_DIM
    n_attn = 3 * n_mix_heads // 4
    A = n_attn * HEAD_DIM
    G = n_mix_heads - n_attn
    F = ffn_w_gate.shape[2] * N_CHIPS
    mix_in = ab_w_in.shape[2] * N_CHIPS
    xi, yi, ci = _coords()
    chip = 2 * xi + yi
    dev = 4 * xi + 2 * yi + ci
    core1 = jnp.reshape(ci, (1,)).astype(jnp.int32)
    chip1 = jnp.reshape(chip, (1,)).astype(jnp.int32)
    x2 = x.reshape(S, D)
    target = loss_target.reshape(S, D)

    n_conv = conv_w.size
    w0 = D + n_conv
    w0p = -(-w0 // 128) * 128
    pack = jnp.zeros((8, w0p), F32).at[0, :D].set(c[0]).at[0, D:w0].set(conv_w.reshape(-1))
    g0 = _all_gather8(pack, "gather_cond")
    c_all = g0[:, 0, :D]
    c_act = c_all * jax.nn.sigmoid(c_all)
    conv_full = jnp.concatenate(
        [g0[2 * j, 0, D:w0].reshape(conv_w.shape) for j in range(N_CHIPS)], axis=2)
    mod_part = _ada_fwd(c_act, ada_w, "ada_fwd")
    n_ada = ada_w.shape[2]
    g1 = _all_gather8(mod_part.reshape(L * N_DEV, n_ada), "gather_mod")
    mod_all = jnp.concatenate([g1[2 * j].reshape(L, N_DEV, n_ada) for j in range(N_CHIPS)], axis=2)
    mod = lax.dynamic_index_in_dim(mod_all, dev, axis=1, keepdims=False) + ada_b
    mods = mod.reshape(L, 6, 1, D)
    cs, sa, sb = _rope_tables(positions, S)

    bf = lambda t: t.astype(BF16)
    w_in_e, w_out_e = bf(ab_w_in), bf(ab_w_out)
    w_in_o, w_out_o = bf(conv_w_in), bf(conv_w_out)
    w_gate, w_up, w_down = bf(ffn_w_gate), bf(ffn_w_up), bf(ffn_w_down)
    layer_w = []
    for l in range(L):
        i = l // 2
        if l % 2 == 0:
            first, n_in_cols = (w_in_e, w_out_e), mix_in
        else:
            first, n_in_cols = (w_in_o, w_out_o), 3 * D
        shards = [first[0], first[1], w_gate, w_up, w_down]
        plan = [(0, "col", 0, i), (1, "row", 0, i), (2, "col", 0, l), (2, "col", F, l), (3, "row", 0, l)]
        shapes = [(D, n_in_cols), (D, D), (D, 2 * F), (F, D)]
        layer_w.append(_gather_weights(shards, plan, shapes, f"gather_w{l}"))

    saved = []
    xc = x2
    pending = None
    for l in range(L):
        i = l // 2
        w_in, w_out, w_gu, w_dn = layer_w[l]
        sh_m, sc_m, g_m, sh_f, sc_f, g_f = (mods[l, t] for t in range(6))
        weff_m = norm_mix[l][None, :] * (1.0 + sc_m)
        weff_f = norm_ffn[l][None, :] * (1.0 + sc_f)
        if pending is None:
            _, h = _norm_mod(xc, None, None, weff_m, sh_m, f"norm_mix{l}")
        else:
            xc, h = _norm_mod(xc, pending[0], pending[1], weff_m, sh_m, f"norm_mix{l}")
        z = _mm(h, w_in, "nn", BF16, f"mm_in{l}")
        st = dict(x=xc, h=h, z=z, weff_m=weff_m, weff_f=weff_f, g_m=g_m, g_f=g_f, sc_m=sc_m, sc_f=sc_f)
        if l % 2 == 0:
            q, k, v = _qkv_prep(z, cs, sa, sb, n_attn, f"qkv_prep{l}")
            o, lse = _attn_fwd(q, k, v, n_attn, f"attn_fwd{l}")
            bT = sgu_b[i].T
            so = _sgu_fwd(z, sgu_w[i], bT, n_attn, f"sgu_fwd{l}")
            cat = jnp.concatenate([o, so], axis=1)
            st.update(q=q, k=k, v=v, o=o, lse=lse, bT=bT)
        else:
            w8 = jnp.zeros((8, D), F32).at[:3].set(conv_full[i])
            cat = _conv_fwd(z, w8, f"conv_fwd{l}")
            st.update(w8=w8)
        mix = _mm(cat, w_out, "nn", F32, f"mm_out{l}")
        x1, h2 = _norm_mod(xc, mix, g_m, weff_f, sh_f, f"norm_ffn{l}")
        ab = _mm(h2, w_gu, "nn", BF16, f"mm_gu{l}")
        f = _swiglu(ab, f"swiglu{l}")
        yv = _mm(f, w_dn, "nn", F32, f"mm_down{l}")
        st.update(cat=cat, mix=mix, x1=x1, h2=h2, ab=ab, f=f, y=yv)
        saved.append(st)
        xc = x1
        pending = (yv, g_f)

    dx, loss11, dfinal = _loss_head(xc, pending[0], pending[1], final_norm[None, :], target, "loss_head")
    loss = lax.psum(loss11[0, 0], ("x", "y", "c"))

    dmods = [None] * L
    dnorm_mix, dnorm_ffn = [None] * L, [None] * L
    big = {}
    dsgu_w, dsgu_b, dconv = [None] * (L - L // 2), [None] * (L - L // 2), [None] * (L // 2)
    for l in reversed(range(L)):
        i = l // 2
        st = saved[l]
        w_in, w_out, w_gu, w_dn = layer_w[l]
        dy, dg_f = _gate_bwd(dx, st["y"], st["g_f"], f"gate_f_bwd{l}")
        df = _mm(dy, w_dn, "nt", BF16, f"mm_down_dx{l}")
        dw_dn = _mm(st["f"], dy, "tn", F32, f"mm_down_dw{l}")
        dab = _swiglu_bwd(st["ab"], df, f"swiglu_bwd{l}")
        dh2 = _mm(dab, w_gu, "nt", F32, f"mm_gu_dx{l}")
        dw_gu = _mm(st["h2"], dab, "tn", F32, f"mm_gu_dw{l}")
        dx1, dsh_f, dweff_f = _norm_mod_bwd(dh2, st["x1"], st["weff_f"], dx, f"norm_ffn_bwd{l}")
        dmix, dg_m = _gate_bwd(dx1, st["mix"], st["g_m"], f"gate_m_bwd{l}")
        dcat = _mm(dmix, w_out, "nt", F32, f"mm_out_dx{l}")
        dw_out = _mm(st["cat"], dmix, "tn", F32, f"mm_out_dw{l}")
        if l % 2 == 0:
            dq, dk, dv = _attn_bwd(st["q"], st["k"], st["v"], st["o"], st["lse"], dcat, n_attn, f"attn_bwd{l}")
            dqkv = _dqkv_post(dq, dk, dv, cs, sa, sb, n_attn, f"dqkv_post{l}")
            duv, dsgu_w[i], dbT = _sgu_bwd(st["z"], dcat, sgu_w[i], st["bT"], n_attn, f"sgu_bwd{l}")
            dsgu_b[i] = dbT.T
            dz = jnp.concatenate([dqkv, duv], axis=1)
        else:
            dz, dw8 = _conv_bwd(st["z"], dcat, st["w8"], f"conv_bwd{l}")
            dconv[i] = dw8[:3]
        dh = _mm(dz, w_in, "nt", F32, f"mm_in_dx{l}")
        dw_in = _mm(st["h"], dz, "tn", F32, f"mm_in_dw{l}")
        dx, dsh_m, dweff_m = _norm_mod_bwd(dh, st["x"], st["weff_m"], dx1, f"norm_mix_bwd{l}")
        dmods[l] = jnp.concatenate(
            [dsh_m, dweff_m * norm_mix[l][None, :], dg_m, dsh_f, dweff_f * norm_ffn[l][None, :], dg_f], axis=1)
        dnorm_mix[l] = dweff_m * (1.0 + st["sc_m"])
        dnorm_ffn[l] = dweff_f * (1.0 + st["sc_f"])
        n_in_cols = w_in.shape[1]
        plan = [[("col", 0, n_in_cols // N_CHIPS)], [("row", 0, D // N_CHIPS)],
                [("col", 0, F // N_CHIPS), ("col", F, F // N_CHIPS)], [("row", 0, F // N_CHIPS)]]
        big[l] = _reduce_scatter([dw_in, dw_out, dw_gu, dw_dn], plan, core1, chip1, f"l{l}")
    grad_x = dx.reshape(1, S, D)

    dmod = jnp.concatenate(dmods, axis=0)
    small = [dmod.reshape(-1), jnp.concatenate(dnorm_mix, 0).reshape(-1), jnp.concatenate(dnorm_ffn, 0).reshape(-1),
             jnp.stack(dsgu_w).reshape(-1), jnp.stack(dsgu_b).reshape(-1), jnp.stack(dconv).reshape(-1), dfinal.reshape(-1)]
    sizes = [t.size for t in small]
    flat = jnp.concatenate(small)
    n_flat = flat.size
    rows = -(-n_flat // (128 * 8)) * 8
    flat = jnp.concatenate([flat, jnp.zeros((rows * 128 - n_flat,), F32)]).reshape(rows, 128)
    g2 = _all_gather8(flat, "gather_small")
    tot = _sum_leading(g2, "sum_small").reshape(-1)
    offs = [0]
    for s in sizes:
        offs.append(offs[-1] + s)
    take = lambda n, shape: tot[offs[n]:offs[n + 1]].reshape(shape)
    g_ada_b = take(0, ada_b.shape)
    g_norm_mix = take(1, norm_mix.shape)
    g_norm_ffn = take(2, norm_ffn.shape)
    g_sgu_w = take(3, sgu_w.shape)
    g_sgu_b = take(4, sgu_b.shape)
    g_conv_full = take(5, conv_full.shape)
    n_cw = conv_w.shape[2]
    g_conv_w = lax.dynamic_slice_in_dim(g_conv_full, chip * n_cw, n_cw, axis=2)
    g_final = take(6, final_norm.shape)
    dmod_all = g2[:, :, :].reshape(N_DEV, -1)[:, :offs[1]].reshape(N_DEV, L, 6 * D)
    dmod_mine = lax.dynamic_slice_in_dim(dmod_all, chip * n_ada, n_ada, axis=2).transpose(1, 0, 2)
    g_ada_w = _ada_bwd(c_act, dmod_mine, "ada_bwd")

    def stack(idx, layers):
        return jnp.stack([big[l][idx] for l in layers])

    even, odd, every = list(range(0, L, 2)), list(range(1, L, 2)), list(range(L))
    grads = dict(
        ada_w=g_ada_w, ada_b=g_ada_b, norm_mix=g_norm_mix, norm_ffn=g_norm_ffn,
        ab_w_in=stack(0, even), sgu_w=g_sgu_w, sgu_b=g_sgu_b, ab_w_out=stack(1, even),
        conv_w_in=stack(0, odd), conv_w=g_conv_w, conv_w_out=stack(1, odd),
        ffn_w_gate=stack(2, every), ffn_w_up=stack(3, every), ffn_w_down=stack(4, every), final_norm=g_final)
    weights = dict(ada_w=ada_w, ada_b=ada_b, norm_mix=norm_mix, norm_ffn=norm_ffn, ab_w_in=ab_w_in, sgu_w=sgu_w,
                   sgu_b=sgu_b, ab_w_out=ab_w_out, conv_w_in=conv_w_in, conv_w=conv_w, conv_w_out=conv_w_out,
                   ffn_w_gate=ffn_w_gate, ffn_w_up=ffn_w_up, ffn_w_down=ffn_w_down, final_norm=final_norm)
    ms = dict(ada_w=m_ada_w, ada_b=m_ada_b, norm_mix=m_norm_mix, norm_ffn=m_norm_ffn, ab_w_in=m_ab_w_in, sgu_w=m_sgu_w,
              sgu_b=m_sgu_b, ab_w_out=m_ab_w_out, conv_w_in=m_conv_w_in, conv_w=m_conv_w, conv_w_out=m_conv_w_out,
              ffn_w_gate=m_ffn_w_gate, ffn_w_up=m_ffn_w_up, ffn_w_down=m_ffn_w_down, final_norm=m_final_norm)
    vs = dict(ada_w=v_ada_w, ada_b=v_ada_b, norm_mix=v_norm_mix, norm_ffn=v_norm_ffn, ab_w_in=v_ab_w_in, sgu_w=v_sgu_w,
              sgu_b=v_sgu_b, ab_w_out=v_ab_w_out, conv_w_in=v_conv_w_in, conv_w=v_conv_w, conv_w_out=v_conv_w_out,
              ffn_w_gate=v_ffn_w_gate, ffn_w_up=v_ffn_w_up, ffn_w_down=v_ffn_w_down, final_norm=v_final_norm)
    names = list(weights)
    deltas, new_m, new_v = {}, {}, {}
    for n in names:
        w, g = weights[n], grads[n]
        if w.ndim == 1:
            d_, m_, v_ = _adamw(w[None, :], g[None, :], ms[n][None, :], vs[n][None, :], f"adamw_{n}")
            deltas[n], new_m[n], new_v[n] = d_[0], m_[0], v_[0]
        else:
            deltas[n], new_m[n], new_v[n] = _adamw(w, g, ms[n], vs[n], f"adamw_{n}")
    return (loss, grad_x, *[grads[n] for n in names], *[deltas[n] for n in names],
            *[new_m[n] for n in names], *[new_v[n] for n in names])
```

```python
import functools
import math

import jax
import jax.numpy as jnp
from jax import lax
from jax.experimental import pallas as pl
from jax.experimental.pallas import tpu as pltpu

F32 = jnp.float32
BF16 = jnp.bfloat16
HEAD_DIM = 128
CHUNK = 128
ATTN_BLOCK = 128
ATTN_LANES_FWD = 4
ATTN_LANES_BWD = 2
DILATIONS = (1, 4, 16)
ROPE_DIM = HEAD_DIM // 4
ROPE_THETA = 500000.0
EPS = 1e-6
MASKED = -1e30
ADAM_LR, ADAM_B1, ADAM_B2, ADAM_EPS, ADAM_WD, ADAM_STEP = 0.001, 0.9, 0.999, 1e-08, 0.01, 10
VMEM_LIMIT_BYTES = 56 * 1024 * 1024
MESH = pl.DeviceIdType.MESH
ANY = pl.BlockSpec(memory_space=pl.ANY)
N_CHIPS = 4
N_DEV = 8


def _pick(n, cands):
    for t in cands:
        if n % t == 0:
            return t
    return n


def _params(sem):
    return pltpu.CompilerParams(dimension_semantics=sem, vmem_limit_bytes=VMEM_LIMIT_BYTES)


def _mm(a, b, mode, out_dtype, name, layer=None, job=None):
    bshape = b.shape[1:] if layer is not None else b.shape
    if mode == "nn":
        (M, K), (K2, N) = a.shape, bshape
    elif mode == "nt":
        (M, K), (N, K2) = a.shape, bshape
    else:
        (K, M), (K2, N) = a.shape, bshape
    assert K == K2, (a.shape, b.shape, mode)
    if mode == "tn":
        tm = _pick(M, (2048, 1408, 1024, 512, 256, 128))
        tn = _pick(N, (1408, 1536, 1024, 512, 256, 128))
        tk = _pick(K, (512, 256, 128))
    else:
        tm = _pick(M, (1024, 512, 256, 128))
        tk = K if K <= 2048 else _pick(K, (1408, 1536, 1024, 512, 256, 128))
        tn = _pick(N, (512, 256, 128)) if tk == K else _pick(N, (1024, 512, 256, 128))
    nk = K // tk
    dims = {"nn": (((1,), (0,)), ((), ())), "nt": (((1,), (1,)), ((), ())), "tn": (((0,), (0,)), ((), ()))}[mode]

    in_place = out_dtype == F32

    n_ji = len(job["ins"]) if job else 0
    n_jo = len(job["outs"]) if job else 0
    n_acc = 1 if nk > 1 and not in_place else 0
    grid = (M // tm, N // tn, nk)

    def body(a_ref, b_ref, *rest):
        jin, o_ref, jout = rest[:n_ji], rest[n_ji], rest[n_ji + 1:n_ji + 1 + n_jo]
        scratch = rest[n_ji + 1 + n_jo:]
        acc, sems = scratch[:n_acc], scratch[n_acc:]
        i, j, k = pl.program_id(0), pl.program_id(1), pl.program_id(2)
        if job:
            @pl.when((i == 0) & (j == 0) & (k == 0))
            def _():
                job["start"](jin, jout, sems)

        part = lax.dot_general(a_ref[...].astype(BF16), b_ref[...].astype(BF16), dims, preferred_element_type=F32)
        if nk == 1:
            o_ref[...] = part.astype(o_ref.dtype)
        else:
            acc_ref = o_ref if in_place else acc[0]

            @pl.when(k == 0)
            def _():
                acc_ref[...] = part

            @pl.when(k > 0)
            def _():
                acc_ref[...] += part

            if not in_place:
                @pl.when(k == nk - 1)
                def _():
                    o_ref[...] = acc_ref[...].astype(o_ref.dtype)

        if job:
            @pl.when((i == grid[0] - 1) & (j == grid[1] - 1) & (k == grid[2] - 1))
            def _():
                job["finish"](jin, jout, sems)

    if mode == "tn":
        a_spec = pl.BlockSpec((tk, tm), lambda i, j, k: (k, i))
    else:
        a_spec = pl.BlockSpec((tm, tk), lambda i, j, k: (i, k))
    if mode == "nt":
        bblk, bidx = (tn, tk), (lambda i, j, k: (j, k))
    else:
        bblk, bidx = (tk, tn), (lambda i, j, k: (k, j))
    if layer is not None:
        b_spec = pl.BlockSpec((None,) + bblk, lambda i, j, k: (layer,) + bidx(i, j, k))
    else:
        b_spec = pl.BlockSpec(bblk, bidx)
    out_spec = pl.BlockSpec((tm, tn), lambda i, j, k: (i, j))
    out_shape = jax.ShapeDtypeStruct((M, N), out_dtype)
    acc_scratch = [pltpu.VMEM((tm, tn), F32)] * n_acc
    if not job:
        return pl.pallas_call(
            body, name=name, grid=grid, in_specs=[a_spec, b_spec], out_specs=out_spec, out_shape=out_shape,
            scratch_shapes=acc_scratch, compiler_params=_params(("parallel", "parallel", "arbitrary")),
        )(a, b)
    return pl.pallas_call(
        body, name=name, grid=grid, in_specs=[a_spec, b_spec] + [ANY] * n_ji,
        out_specs=(out_spec,) + (ANY,) * n_jo, out_shape=(out_shape,) + tuple(job["outs"]),
        scratch_shapes=acc_scratch + list(job["sems"]),
        compiler_params=_params(("arbitrary", "arbitrary", "arbitrary")),
    )(a, b, *job["ins"])


def _rows(S):
    return _pick(S, (256, 128, 64, 32, 16, 8))


def _row_spec(tr, width, col=0):
    return pl.BlockSpec((tr, width), lambda i: (i, col))


def _vec_spec(rows, width):
    return pl.BlockSpec((rows, width), lambda i: (0, 0))


def _rms(xv):
    return lax.rsqrt(jnp.mean(xv * xv, axis=-1, keepdims=True) + EPS)


def _norm_mod(x, y, g, w_eff, sh, name):
    S, D = x.shape
    tr = _rows(S)
    fused = y is not None

    def body(*refs):
        if fused:
            x_ref, y_ref, g_ref, w_ref, s_ref, x1_ref, h_ref = refs
            xv = x_ref[...] + g_ref[...] * y_ref[...]
            x1_ref[...] = xv
        else:
            x_ref, w_ref, s_ref, h_ref = refs
            xv = x_ref[...]
        h_ref[...] = (xv * _rms(xv) * w_ref[...] + s_ref[...]).astype(BF16)

    big, vec = _row_spec(tr, D), _vec_spec(1, D)
    if fused:
        ins, in_specs = (x, y, g, w_eff, sh), [big, big, vec, vec, vec]
        out_shape = (jax.ShapeDtypeStruct((S, D), F32), jax.ShapeDtypeStruct((S, D), BF16))
        out_specs = (big, big)
    else:
        ins, in_specs = (x, w_eff, sh), [big, vec, vec]
        out_shape = jax.ShapeDtypeStruct((S, D), BF16)
        out_specs = big
    out = pl.pallas_call(body, name=name, grid=(S // tr,), in_specs=in_specs, out_specs=out_specs,
                         out_shape=out_shape, compiler_params=_params(("parallel",)))(*ins)
    return out if fused else (None, out)


def _norm_mod_bwd(dh, x, w_eff, dres, name):
    S, D = x.shape
    tr = _rows(S)

    def body(dh_ref, x_ref, w_ref, r_ref, dx_ref, dsh_ref, dw_ref):
        xv = x_ref[...]
        dhv = dh_ref[...].astype(F32)
        r = _rms(xv)
        xn = xv * r
        dxn = dhv * w_ref[...]
        dx_ref[...] = r_ref[...] + r * (dxn - xn * jnp.mean(dxn * xn, axis=-1, keepdims=True))

        @pl.when(pl.program_id(0) == 0)
        def _():
            dsh_ref[...] = jnp.zeros_like(dsh_ref)
            dw_ref[...] = jnp.zeros_like(dw_ref)

        dsh_ref[...] += jnp.sum(dhv, axis=0, keepdims=True)
        dw_ref[...] += jnp.sum(dhv * xn, axis=0, keepdims=True)

    big, vec = _row_spec(tr, D), _vec_spec(1, D)
    return pl.pallas_call(
        body, name=name, grid=(S // tr,), in_specs=[big, big, vec, big], out_specs=(big, vec, vec),
        out_shape=(jax.ShapeDtypeStruct((S, D), F32), jax.ShapeDtypeStruct((1, D), F32), jax.ShapeDtypeStruct((1, D), F32)),
        compiler_params=_params(("arbitrary",)))(dh, x, w_eff, dres)


def _gate_bwd(dx, y, g, name):
    S, D = dx.shape
    tr = _rows(S)

    def body(dx_ref, y_ref, g_ref, dy_ref, dg_ref):
        dxv = dx_ref[...]
        dy_ref[...] = (dxv * g_ref[...]).astype(BF16)

        @pl.when(pl.program_id(0) == 0)
        def _():
            dg_ref[...] = jnp.zeros_like(dg_ref)

        dg_ref[...] += jnp.sum(dxv * y_ref[...], axis=0, keepdims=True)

    big, vec = _row_spec(tr, D), _vec_spec(1, D)
    return pl.pallas_call(
        body, name=name, grid=(S // tr,), in_specs=[big, big, vec], out_specs=(big, vec),
        out_shape=(jax.ShapeDtypeStruct((S, D), BF16), jax.ShapeDtypeStruct((1, D), F32)),
        compiler_params=_params(("arbitrary",)))(dx, y, g)


def _loss_head(x, y, g, gamma, target, name):
    S, D = x.shape
    tr = _rows(S)

    def body(x_ref, y_ref, g_ref, gm_ref, t_ref, dx_ref, loss_ref, dgm_ref):
        xv = x_ref[...] + g_ref[...] * y_ref[...]
        r = _rms(xv)
        xn = xv * r
        err = xn * gm_ref[...] - t_ref[...]
        dout = err * (1.0 / D)
        dxn = dout * gm_ref[...]
        dx_ref[...] = r * (dxn - xn * jnp.mean(dxn * xn, axis=-1, keepdims=True))

        @pl.when(pl.program_id(0) == 0)
        def _():
            loss_ref[...] = jnp.zeros_like(loss_ref)
            dgm_ref[...] = jnp.zeros_like(dgm_ref)

        loss_ref[...] += 0.5 * jnp.sum(jnp.mean(err * err, axis=-1, keepdims=True), axis=0, keepdims=True)
        dgm_ref[...] += jnp.sum(dout * xn, axis=0, keepdims=True)

    big, vec = _row_spec(tr, D), _vec_spec(1, D)
    return pl.pallas_call(
        body, name=name, grid=(S // tr,), in_specs=[big, big, vec, vec, big],
        out_specs=(big, _vec_spec(1, 1), vec),
        out_shape=(jax.ShapeDtypeStruct((S, D), F32), jax.ShapeDtypeStruct((1, 1), F32), jax.ShapeDtypeStruct((1, D), F32)),
        compiler_params=_params(("arbitrary",)))(x, y, g, gamma, target)


def _silu(a):
    return a * jax.nn.sigmoid(a)


def _swiglu(ab, name):
    S, F2 = ab.shape
    F = F2 // 2
    tr = _pick(S, (128, 64, 32, 16, 8))

    def body(a_ref, b_ref, f_ref):
        f_ref[...] = (_silu(a_ref[...].astype(F32)) * b_ref[...].astype(F32)).astype(BF16)

    return pl.pallas_call(
        body, name=name, grid=(S // tr,), in_specs=[_row_spec(tr, F, 0), _row_spec(tr, F, 1)],
        out_specs=_row_spec(tr, F), out_shape=jax.ShapeDtypeStruct((S, F), BF16),
        compiler_params=_params(("parallel",)))(ab, ab)


def _swiglu_bwd(ab, df, name):
    S, F2 = ab.shape
    F = F2 // 2
    tr = _pick(S, (128, 64, 32, 16, 8))

    def body(a_ref, b_ref, df_ref, da_ref, db_ref):
        a = a_ref[...].astype(F32)
        sg = jax.nn.sigmoid(a)
        dfv = df_ref[...].astype(F32)
        da_ref[...] = (dfv * b_ref[...].astype(F32) * (sg * (1.0 + a * (1.0 - sg)))).astype(BF16)
        db_ref[...] = (dfv * a * sg).astype(BF16)

    def body2(a_ref, b_ref, df_ref, o_ref):
        body(a_ref, b_ref, df_ref, o_ref.at[:, pl.ds(0, F)], o_ref.at[:, pl.ds(F, F)])

    return pl.pallas_call(
        body2, name=name, grid=(S // tr,),
        in_specs=[_row_spec(tr, F, 0), _row_spec(tr, F, 1), _row_spec(tr, F)],
        out_specs=_row_spec(tr, F2), out_shape=jax.ShapeDtypeStruct((S, F2), BF16),
        compiler_params=_params(("parallel",)))(ab, ab, df)


def _shift_rows(v, n):
    return pltpu.roll(v, n, 0)


def _conv_fwd(p, w8, name):
    S, D3 = p.shape
    D = D3 // 3
    tr = _rows(S)
    nb8 = tr // 8

    def body(gb_ref, gc_ref, hx_ref, gcp_ref, hxp_ref, w_ref, o_ref):
        i = pl.program_id(0)
        y = gc_ref[...].astype(F32) * hx_ref[...].astype(F32)
        yp = jnp.where(i > 0, gcp_ref[...].astype(F32) * hxp_ref[...].astype(F32), 0.0)
        w0, w1, w2 = w_ref[0:1, :], w_ref[1:2, :], w_ref[2:3, :]
        conv = w0 * _shift_rows(y, 2) + w1 * _shift_rows(y, 1) + w2 * y
        o_ref[...] = (gb_ref[...].astype(F32) * conv).astype(BF16)
        rid = lax.broadcasted_iota(jnp.int32, (8, D), 0)
        y8 = y[0:8, :]
        y1 = jnp.where(rid < 1, _shift_rows(yp, 1), _shift_rows(y8, 1))
        y2 = jnp.where(rid < 2, _shift_rows(yp, 2), _shift_rows(y8, 2))
        conv8 = w0 * y2 + w1 * y1 + w2 * y8
        o_ref[0:8, :] = (gb_ref[0:8, :].astype(F32) * conv8).astype(BF16)

    def col(c):
        return pl.BlockSpec((tr, D), lambda i: (i, c))

    def prev8(c):
        return pl.BlockSpec((8, D), lambda i: (jnp.maximum(i * nb8 - 1, 0), c))

    return pl.pallas_call(
        body, name=name, grid=(S // tr,),
        in_specs=[col(0), col(1), col(2), prev8(1), prev8(2), _vec_spec(8, D)],
        out_specs=_row_spec(tr, D), out_shape=jax.ShapeDtypeStruct((S, D), BF16),
        compiler_params=_params(("parallel",)))(p, p, p, p, p, w8)


def _conv_bwd(p, do, w8, name):
    S, D3 = p.shape
    D = D3 // 3
    tr = _rows(S)
    nb8 = tr // 8
    nt = S // tr

    def body(gb_ref, gc_ref, hx_ref, gcp_ref, hxp_ref, do_ref, gbn_ref, don_ref, w_ref, dp_ref, dw_ref):
        i = pl.program_id(0)
        gb = gb_ref[...].astype(F32)
        gc = gc_ref[...].astype(F32)
        hx = hx_ref[...].astype(F32)
        dov = do_ref[...].astype(F32)
        y = gc * hx
        yp = jnp.where(i > 0, gcp_ref[...].astype(F32) * hxp_ref[...].astype(F32), 0.0)
        dconv = dov * gb
        dcn = jnp.where(i < nt - 1, don_ref[...].astype(F32) * gbn_ref[...].astype(F32), 0.0)
        w0, w1, w2 = w_ref[0:1, :], w_ref[1:2, :], w_ref[2:3, :]
        rid = lax.broadcasted_iota(jnp.int32, (tr, D), 0)
        rid8 = lax.broadcasted_iota(jnp.int32, (8, D), 0)
        yp1 = jnp.concatenate([_shift_rows(yp, 1), jnp.zeros((tr - 8, D), F32)], axis=0)
        yp2 = jnp.concatenate([_shift_rows(yp, 2), jnp.zeros((tr - 8, D), F32)], axis=0)
        y1 = jnp.where(rid < 1, yp1, _shift_rows(y, 1))
        y2 = jnp.where(rid < 2, yp2, _shift_rows(y, 2))
        conv = w0 * y2 + w1 * y1 + w2 * y
        dn1 = jnp.concatenate([jnp.zeros((tr - 8, D), F32), _shift_rows(dcn, 7)], axis=0)
        dn2 = jnp.concatenate([jnp.zeros((tr - 8, D), F32), _shift_rows(dcn, 6)], axis=0)
        d1 = jnp.where(rid >= tr - 1, dn1, _shift_rows(dconv, tr - 1))
        d2 = jnp.where(rid >= tr - 2, dn2, _shift_rows(dconv, tr - 2))
        dy = w2 * dconv + w1 * d1 + w0 * d2
        dp_ref[:, pl.ds(0, D)] = (dov * conv).astype(BF16)
        dp_ref[:, pl.ds(D, D)] = (dy * hx).astype(BF16)
        dp_ref[:, pl.ds(2 * D, D)] = (dy * gc).astype(BF16)

        @pl.when(i == 0)
        def _():
            dw_ref[...] = jnp.zeros_like(dw_ref)

        upd = jnp.where(rid8 == 0, jnp.sum(dconv * y2, axis=0, keepdims=True),
                        jnp.where(rid8 == 1, jnp.sum(dconv * y1, axis=0, keepdims=True),
                                  jnp.where(rid8 == 2, jnp.sum(dconv * y, axis=0, keepdims=True), 0.0)))
        dw_ref[...] += upd

    def col(c):
        return pl.BlockSpec((tr, D), lambda i: (i, c))

    def prev8(c):
        return pl.BlockSpec((8, D), lambda i: (jnp.maximum(i * nb8 - 1, 0), c))

    def next8(c):
        return pl.BlockSpec((8, D), lambda i: (jnp.minimum((i + 1) * nb8, S // 8 - 1), c))

    return pl.pallas_call(
        body, name=name, grid=(nt,),
        in_specs=[col(0), col(1), col(2), prev8(1), prev8(2), col(0), next8(0), next8(0), _vec_spec(8, D)],
        out_specs=(_row_spec(tr, D3), _vec_spec(8, D)),
        out_shape=(jax.ShapeDtypeStruct((S, D3), BF16), jax.ShapeDtypeStruct((8, D), F32)),
        compiler_params=_params(("arbitrary",)))(p, p, p, p, p, do, p, do, w8)


_GELU_C = math.sqrt(2.0 / math.pi)


def _gelu(v):
    return 0.5 * v * (1.0 + jnp.tanh(_GELU_C * (v + 0.044715 * v * v * v)))


def _gelu_grad(v):
    t = jnp.tanh(_GELU_C * (v + 0.044715 * v * v * v))
    return 0.5 * (1.0 + t) + 0.5 * v * (1.0 - t * t) * _GELU_C * (1.0 + 3.0 * 0.044715 * v * v)


def _tril(w):
    r = lax.broadcasted_iota(jnp.int32, (CHUNK, CHUNK), 0)
    c = lax.broadcasted_iota(jnp.int32, (CHUNK, CHUNK), 1)
    return jnp.where(r >= c, w, 0.0)


def _sgu_fwd(z, w, bT, n_attn, name):
    S = z.shape[0]
    G = w.shape[0]
    W = G * CHUNK
    tr = _pick(S, (512, 256, 128))
    ucol = 3 * n_attn * HEAD_DIM // W

    def body(u_ref, v_ref, w_ref, b_ref, o_ref):
        for g in range(G):
            wt = _tril(w_ref[g]).astype(BF16)
            for ci in range(tr // CHUNK):
                rows, cols = pl.ds(ci * CHUNK, CHUNK), pl.ds(g * CHUNK, CHUNK)
                gv = _gelu(v_ref[rows, cols].astype(F32)).astype(BF16)
                mixed = jnp.dot(wt, gv, preferred_element_type=F32) + b_ref[:, g:g + 1]
                o_ref[rows, cols] = (_gelu(u_ref[rows, cols].astype(F32)) * mixed).astype(BF16)

    return pl.pallas_call(
        body, name=name, grid=(S // tr,),
        in_specs=[_row_spec(tr, W, ucol), _row_spec(tr, W, ucol + 1),
                  pl.BlockSpec((G, CHUNK, CHUNK), lambda i: (0, 0, 0)), _vec_spec(CHUNK, G)],
        out_specs=_row_spec(tr, W), out_shape=jax.ShapeDtypeStruct((S, W), BF16),
        compiler_params=_params(("parallel",)))(z, z, w, bT)


def _sgu_bwd(z, dcat, w, bT, n_attn, name):
    S = z.shape[0]
    G = w.shape[0]
    W = G * CHUNK
    tr = _pick(S, (512, 256, 128))
    ucol = 3 * n_attn * HEAD_DIM // W
    dcol = n_attn * HEAD_DIM // W

    def body(u_ref, v_ref, d_ref, w_ref, b_ref, o_ref, dw_ref, db_ref):
        @pl.when(pl.program_id(0) == 0)
        def _():
            dw_ref[...] = jnp.zeros_like(dw_ref)
            db_ref[...] = jnp.zeros_like(db_ref)

        lane = lax.broadcasted_iota(jnp.int32, (CHUNK, G), 1)
        for g in range(G):
            wtf = _tril(w_ref[g])
            wt = wtf.astype(BF16)
            dw_acc = jnp.zeros((CHUNK, CHUNK), F32)
            db_acc = jnp.zeros((CHUNK, 1), F32)
            for ci in range(tr // CHUNK):
                rows, cols = pl.ds(ci * CHUNK, CHUNK), pl.ds(g * CHUNK, CHUNK)
                uv = u_ref[rows, cols].astype(F32)
                vv = v_ref[rows, cols].astype(F32)
                dov = d_ref[rows, cols]
                gv = _gelu(vv).astype(BF16)
                mixed = jnp.dot(wt, gv, preferred_element_type=F32) + b_ref[:, g:g + 1]
                dmixed = dov * _gelu(uv)
                dmb = dmixed.astype(BF16)
                dgv = lax.dot_general(wt, dmb, (((0,), (0,)), ((), ())), preferred_element_type=F32)
                o_ref[rows, cols] = (dov * mixed * _gelu_grad(uv)).astype(BF16)
                o_ref[rows, pl.ds(W + g * CHUNK, CHUNK)] = (dgv * _gelu_grad(vv)).astype(BF16)
                dw_acc += lax.dot_general(dmb, gv, (((1,), (1,)), ((), ())), preferred_element_type=F32)
                db_acc += jnp.sum(dmixed, axis=1, keepdims=True)
            dw_ref[g] += _tril(dw_acc)
            db_ref[...] += jnp.where(lane == g, db_acc, 0.0)

    return pl.pallas_call(
        body, name=name, grid=(S // tr,),
        in_specs=[_row_spec(tr, W, ucol), _row_spec(tr, W, ucol + 1), _row_spec(tr, W, dcol),
                  pl.BlockSpec((G, CHUNK, CHUNK), lambda i: (0, 0, 0)), _vec_spec(CHUNK, G)],
        out_specs=(_row_spec(tr, 2 * W), pl.BlockSpec((G, CHUNK, CHUNK), lambda i: (0, 0, 0)), _vec_spec(CHUNK, G)),
        out_shape=(jax.ShapeDtypeStruct((S, 2 * W), BF16), jax.ShapeDtypeStruct((G, CHUNK, CHUNK), F32),
                   jax.ShapeDtypeStruct((CHUNK, G), F32)),
        compiler_params=_params(("arbitrary",)))(z, z, dcat, w, bT)


def _rope(v, cs, sa, sb):
    return v * cs + pltpu.roll(v, HEAD_DIM - ROPE_DIM // 2, 1) * sa + pltpu.roll(v, ROPE_DIM // 2, 1) * sb


def _rope_t(d, cs, sa, sb):
    return d * cs + pltpu.roll(d * sa, ROPE_DIM // 2, 1) + pltpu.roll(d * sb, HEAD_DIM - ROPE_DIM // 2, 1)


def _qkv_prep(z, cs, sa, sb, n_attn, name):
    S = z.shape[0]
    A = n_attn * HEAD_DIM
    tr = _rows(S)

    def body(q_ref, k_ref, v_ref, c_ref, a_ref, b_ref, qo_ref, ko_ref, vo_ref):
        cv, av, bv = c_ref[...], a_ref[...], b_ref[...]
        for h in range(n_attn):
            cols = pl.ds(h * HEAD_DIM, HEAD_DIM)
            qo_ref[:, cols] = _rope(q_ref[:, cols].astype(F32), cv, av, bv)
            ko_ref[:, cols] = _rope(k_ref[:, cols].astype(F32), cv, av, bv)
        vo_ref[...] = v_ref[...].astype(F32)

    tab = _row_spec(tr, HEAD_DIM)
    out = jax.ShapeDtypeStruct((S, A), F32)
    return pl.pallas_call(
        body, name=name, grid=(S // tr,),
        in_specs=[_row_spec(tr, A, 0), _row_spec(tr, A, 1), _row_spec(tr, A, 2), tab, tab, tab],
        out_specs=(_row_spec(tr, A),) * 3, out_shape=(out,) * 3,
        compiler_params=_params(("parallel",)))(z, z, z, cs, sa, sb)


def _dqkv_post(dq, dk, dv, cs, sa, sb, n_attn, name):
    S, A = dq.shape
    tr = _rows(S)

    def body(q_ref, k_ref, v_ref, c_ref, a_ref, b_ref, o_ref):
        cv, av, bv = c_ref[...], a_ref[...], b_ref[...]
        for h in range(n_attn):
            cols = pl.ds(h * HEAD_DIM, HEAD_DIM)
            o_ref[:, pl.ds(h * HEAD_DIM, HEAD_DIM)] = _rope_t(q_ref[:, cols], cv, av, bv).astype(BF16)
            o_ref[:, pl.ds(A + h * HEAD_DIM, HEAD_DIM)] = _rope_t(k_ref[:, cols], cv, av, bv).astype(BF16)
        o_ref[:, pl.ds(2 * A, A)] = v_ref[...].astype(BF16)

    tab = _row_spec(tr, HEAD_DIM)
    return pl.pallas_call(
        body, name=name, grid=(S // tr,),
        in_specs=[_row_spec(tr, A)] * 3 + [tab, tab, tab],
        out_specs=_row_spec(tr, 3 * A), out_shape=jax.ShapeDtypeStruct((S, 3 * A), BF16),
        compiler_params=_params(("parallel",)))(dq, dk, dv, cs, sa, sb)


def _block_rows(d, S, it):
    nblk = S // (d * ATTN_BLOCK)
    r = it // nblk
    jb = it % nblk
    q0 = r + d * ATTN_BLOCK * jb
    k0 = r + d * ATTN_BLOCK * jnp.maximum(jb - 1, 0)
    off = jnp.where(jb > 0, ATTN_BLOCK, 0)
    return q0, k0, off


def _band(off):
    a = lax.broadcasted_iota(jnp.int32, (ATTN_BLOCK, 2 * ATTN_BLOCK), 0) + off
    kj = lax.broadcasted_iota(jnp.int32, (ATTN_BLOCK, 2 * ATTN_BLOCK), 1)
    return (kj <= a) & (kj >= a - ATTN_BLOCK)


_NT = (((1,), (1,)), ((), ()))
_TN = (((0,), (0,)), ((), ()))


def _attn_fwd(q, k, v, n_attn, name):
    S, A = q.shape
    scale = HEAD_DIM ** -0.5

    def body(q_hbm, k_hbm, v_hbm, o_hbm, lse_hbm, qs, ks, vs, acc, ms, ls, ob, sem):
        h = pl.program_id(0)
        cols = pl.ds(pl.multiple_of(h * HEAD_DIM, HEAD_DIM), HEAD_DIM)
        cps = [pltpu.make_async_copy(src.at[:, cols], dst, sem.at[i])
               for i, (src, dst) in enumerate(((q_hbm, qs), (k_hbm, ks), (v_hbm, vs)))]
        for cp in cps:
            cp.start()
        acc[...] = jnp.zeros_like(acc)
        ms[...] = jnp.full_like(ms, MASKED)
        ls[...] = jnp.zeros_like(ls)
        for cp in cps:
            cp.wait()
        n_blocks = S // ATTN_BLOCK
        lanes = n_blocks // ATTN_LANES_FWD
        for d in DILATIONS:
            def step(it, carry, d=d):
                loaded = []
                for u in range(ATTN_LANES_FWD):
                    q0, k0, off = _block_rows(d, S, it + u * lanes)
                    qrows = pl.ds(q0, ATTN_BLOCK, stride=d)
                    krows = pl.ds(k0, 2 * ATTN_BLOCK, stride=d)
                    loaded.append((qrows, off, qs[qrows, :], ks[krows, :], vs[krows, :],
                                   ms[qrows, :], ls[qrows, :], acc[qrows, :]))
                results = []
                for qrows, off, qv, kv, vv, m_old, l_old, a_old in loaded:
                    s = lax.dot_general(qv.astype(BF16), kv.astype(BF16), _NT, preferred_element_type=F32) * scale
                    s = jnp.where(_band(off), s, MASKED)
                    m_new = jnp.maximum(m_old, jnp.max(s, axis=-1, keepdims=True))
                    alpha = jnp.exp(m_old - m_new)
                    p = jnp.exp(s - m_new)
                    l_new = alpha * l_old + jnp.sum(p, axis=-1, keepdims=True)
                    a_new = alpha * a_old + jnp.dot(p.astype(BF16), vv.astype(BF16), preferred_element_type=F32)
                    results.append((qrows, m_new, l_new, a_new))
                for qrows, m_new, l_new, a_new in results:
                    ms[qrows, :] = m_new
                    ls[qrows, :] = l_new
                    acc[qrows, :] = a_new
                return carry
            lax.fori_loop(0, lanes, step, 0)
        ob[...] = (acc[...] / ls[...]).astype(BF16)
        ms[...] = ms[...] + jnp.log(ls[...])
        out = [pltpu.make_async_copy(ob, o_hbm.at[:, cols], sem.at[0]),
               pltpu.make_async_copy(ms, lse_hbm.at[h], sem.at[1])]
        for cp in out:
            cp.start()
        for cp in out:
            cp.wait()

    return pl.pallas_call(
        body, name=name, grid=(n_attn,), in_specs=[ANY, ANY, ANY], out_specs=(ANY, ANY),
        out_shape=(jax.ShapeDtypeStruct((S, A), BF16), jax.ShapeDtypeStruct((n_attn, S, 1), F32)),
        scratch_shapes=[pltpu.VMEM((S, HEAD_DIM), F32)] * 4 + [pltpu.VMEM((S, 1), F32)] * 2
        + [pltpu.VMEM((S, HEAD_DIM), BF16), pltpu.SemaphoreType.DMA((3,))],
        compiler_params=_params(("arbitrary",)))(q, k, v)


def _attn_bwd(q, k, v, o, lse, dcat, n_attn, name):
    S, A = q.shape
    scale = HEAD_DIM ** -0.5

    def body(q_hbm, k_hbm, v_hbm, o_hbm, lse_hbm, do_hbm, dq_hbm, dk_hbm, dv_hbm,
             qs, ks, vs, dos, dqs, dks, dvs, lses, dls, ob, sem):
        h = pl.program_id(0)
        cols = pl.ds(pl.multiple_of(h * HEAD_DIM, HEAD_DIM), HEAD_DIM)
        cps = [pltpu.make_async_copy(src.at[:, cols], dst, sem.at[i])
               for i, (src, dst) in enumerate(((q_hbm, qs), (k_hbm, ks), (v_hbm, vs), (do_hbm, dos), (o_hbm, ob)))]
        cps.append(pltpu.make_async_copy(lse_hbm.at[h], lses, sem.at[5]))
        for cp in cps:
            cp.start()
        dqs[...] = jnp.zeros_like(dqs)
        dks[...] = jnp.zeros_like(dks)
        dvs[...] = jnp.zeros_like(dvs)
        for cp in cps:
            cp.wait()
        dls[...] = jnp.sum(dos[...] * ob[...].astype(F32), axis=-1, keepdims=True)
        n_blocks = S // ATTN_BLOCK
        lanes = n_blocks // ATTN_LANES_BWD
        assert lanes % 2 == 0
        for d in DILATIONS:
            def step(it, carry, d=d):
                loaded = []
                for u in range(ATTN_LANES_BWD):
                    q0, k0, off = _block_rows(d, S, it + u * lanes)
                    qrows = pl.ds(q0, ATTN_BLOCK, stride=d)
                    krows = pl.ds(k0, 2 * ATTN_BLOCK, stride=d)
                    loaded.append((qrows, krows, off, qs[qrows, :], ks[krows, :], vs[krows, :], dos[qrows, :],
                                   lses[qrows, :], dls[qrows, :], dqs[qrows, :], dks[krows, :], dvs[krows, :]))
                results = []
                for qrows, krows, off, qv, kv, vv, dov, lse_v, dl_v, dq_old, dk_old, dv_old in loaded:
                    qb, kb, vb, dob = qv.astype(BF16), kv.astype(BF16), vv.astype(BF16), dov.astype(BF16)
                    s = lax.dot_general(qb, kb, _NT, preferred_element_type=F32) * scale
                    p = jnp.where(_band(off), jnp.exp(s - lse_v), 0.0)
                    pb = p.astype(BF16)
                    dp = lax.dot_general(dob, vb, _NT, preferred_element_type=F32)
                    ds = (p * (dp - dl_v) * scale).astype(BF16)
                    results.append((qrows, krows,
                                    dq_old + jnp.dot(ds, kb, preferred_element_type=F32),
                                    dk_old + lax.dot_general(ds, qb, _TN, preferred_element_type=F32),
                                    dv_old + lax.dot_general(pb, dob, _TN, preferred_element_type=F32)))
                for qrows, krows, dq_new, dk_new, dv_new in results:
                    dqs[qrows, :] = dq_new
                    dks[krows, :] = dk_new
                    dvs[krows, :] = dv_new
                return carry
            lax.fori_loop(0, lanes, step, 0)
        out = [pltpu.make_async_copy(src, dst.at[:, cols], sem.at[i])
               for i, (src, dst) in enumerate(((dqs, dq_hbm), (dks, dk_hbm), (dvs, dv_hbm)))]
        for cp in out:
            cp.start()
        for cp in out:
            cp.wait()

    grad = jax.ShapeDtypeStruct((S, A), F32)
    return pl.pallas_call(
        body, name=name, grid=(n_attn,), in_specs=[ANY] * 6, out_specs=(ANY, ANY, ANY), out_shape=(grad,) * 3,
        scratch_shapes=[pltpu.VMEM((S, HEAD_DIM), F32)] * 7 + [pltpu.VMEM((S, 1), F32)] * 2
        + [pltpu.VMEM((S, HEAD_DIM), BF16), pltpu.SemaphoreType.DMA((6,))],
        compiler_params=_params(("arbitrary",)))(q, k, v, o, lse, dcat)


def _ada_fwd(c_act, ada_w, name):
    L, D, n = ada_w.shape
    tn = _pick(n, (512, 256, 128))

    def body(c_ref, w_ref, o_ref):
        o_ref[...] = jnp.dot(c_ref[...], w_ref[...], preferred_element_type=F32)

    return pl.pallas_call(
        body, name=name, grid=(L, n // tn),
        in_specs=[pl.BlockSpec((N_DEV, D), lambda l, j: (0, 0)), pl.BlockSpec((None, D, tn), lambda l, j: (l, 0, j))],
        out_specs=pl.BlockSpec((None, N_DEV, tn), lambda l, j: (l, 0, j)),
        out_shape=jax.ShapeDtypeStruct((L, N_DEV, n), F32),
        compiler_params=_params(("parallel", "parallel")))(c_act, ada_w)


def _ada_bwd(c_act, dmod, name):
    L, _, n = dmod.shape
    D = c_act.shape[1]
    tn = _pick(n, (512, 256, 128))

    def body(c_ref, d_ref, o_ref):
        o_ref[...] = lax.dot_general(c_ref[...], d_ref[...], _TN, preferred_element_type=F32)

    return pl.pallas_call(
        body, name=name, grid=(L, n // tn),
        in_specs=[pl.BlockSpec((N_DEV, D), lambda l, j: (0, 0)), pl.BlockSpec((None, N_DEV, tn), lambda l, j: (l, 0, j))],
        out_specs=pl.BlockSpec((None, D, tn), lambda l, j: (l, 0, j)),
        out_shape=jax.ShapeDtypeStruct((L, D, n), F32),
        compiler_params=_params(("parallel", "parallel")))(c_act, dmod)


def _adamw(w, g, m, v, name):
    shape = w.shape
    C = shape[-1]
    R = w.size // C
    w2, g2, m2, v2 = (t.reshape(R, C) for t in (w, g, m, v))
    tr = _pick(R, (256, 128, 64, 32, 16, 8))
    tc = _pick(C, (2048, 1536, 1408, 1024, 512, 256, 128))

    def body(w_ref, g_ref, m_ref, v_ref, d_ref, mo_ref, vo_ref):
        gv = g_ref[...]
        mn = ADAM_B1 * m_ref[...] + (1.0 - ADAM_B1) * gv
        vn = ADAM_B2 * v_ref[...] + (1.0 - ADAM_B2) * (gv * gv)
        m_hat = mn / (1.0 - ADAM_B1 ** ADAM_STEP)
        v_hat = vn / (1.0 - ADAM_B2 ** ADAM_STEP)
        d_ref[...] = -ADAM_LR * (m_hat / (jnp.sqrt(v_hat) + ADAM_EPS) + ADAM_WD * w_ref[...])
        mo_ref[...] = mn
        vo_ref[...] = vn

    spec = pl.BlockSpec((tr, tc), lambda i, j: (i, j))
    out = jax.ShapeDtypeStruct((R, C), F32)
    d, mn, vn = pl.pallas_call(
        body, name=name, grid=(R // tr, C // tc), in_specs=[spec] * 4, out_specs=(spec,) * 3, out_shape=(out,) * 3,
        compiler_params=_params(("parallel", "parallel")))(w2, g2, m2, v2)
    return d.reshape(shape), mn.reshape(shape), vn.reshape(shape)


def _sum_leading(t, name):
    n, R, C = t.shape
    tr = _pick(R, (256, 128, 64, 32, 16, 8))

    def body(t_ref, o_ref):
        acc = t_ref[0]
        for i in range(1, n):
            acc = acc + t_ref[i]
        o_ref[...] = acc

    return pl.pallas_call(
        body, name=name, grid=(R // tr,), in_specs=[pl.BlockSpec((n, tr, C), lambda i: (0, i, 0))],
        out_specs=pl.BlockSpec((tr, C), lambda i: (i, 0)), out_shape=jax.ShapeDtypeStruct((R, C), F32),
        compiler_params=_params(("parallel",)))(t)


def _coords():
    return lax.axis_index("x"), lax.axis_index("y"), lax.axis_index("c")


def _other_chips(x, y):
    return [(1 - x, y), (x, 1 - y), (1 - x, 1 - y)]


def _all_gather8(t, name):
    R, C = t.shape

    def body(x_ref, out_ref, send_sems, recv_sems, local_sem):
        x, y, c = _coords()
        me, sibling = (x, y, c), (x, y, 1 - c)
        chips = _other_chips(x, y)

        def slot(px, py, pc):
            return out_ref.at[4 * px + 2 * py + pc]

        def copy(k, block, to, src=None):
            return pltpu.make_async_remote_copy(
                src_ref=slot(*block) if src is None else src, dst_ref=slot(*block),
                send_sem=send_sems.at[k], recv_sem=recv_sems.at[k], device_id=to, device_id_type=MESH)

        mine = pltpu.make_async_copy(x_ref, slot(*me), local_sem)
        mine.start()
        first = [copy(0, me, sibling, src=x_ref)]
        first += [copy(1 + j, me, (*chip, c), src=x_ref) for j, chip in enumerate(chips)]
        for cp in first:
            cp.start()
        passed = [copy(4 + j, (*chip, c), sibling) for j, chip in enumerate(chips)]
        for j, chip in enumerate(chips):
            copy(1 + j, (*chip, c), me).wait_recv()
            passed[j].start()
        copy(0, sibling, me).wait_recv()
        for j, chip in enumerate(chips):
            copy(4 + j, (*chip, 1 - c), me).wait_recv()
        for cp in first + passed:
            cp.wait_send()
        mine.wait()

    return pl.pallas_call(
        body, name=name, out_shape=jax.ShapeDtypeStruct((N_DEV, R, C), t.dtype),
        in_specs=[pl.BlockSpec(memory_space=pltpu.VMEM)], out_specs=pl.BlockSpec(memory_space=pltpu.VMEM),
        scratch_shapes=[pltpu.SemaphoreType.DMA((7,)), pltpu.SemaphoreType.DMA((7,)), pltpu.SemaphoreType.DMA],
        compiler_params=pltpu.CompilerParams(vmem_limit_bytes=VMEM_LIMIT_BYTES))(t)


def _window(ref, r0, nr, c0, nc):
    return ref.at[pl.ds(r0, nr), pl.ds(c0, nc)]


LOCAL_CHUNKS = 4


def _gather_job(pieces, out_shape):
    n = len(pieces)

    def ctx(ins, outs, sems):
        x, y, c = _coords()
        buf = outs[0]

        def place(p, chip_idx, r0, nr):
            _, _, kind, base = pieces[p]
            r, cs = ins[p].shape[1], ins[p].shape[2]
            if kind == "row":
                return _window(buf, base + chip_idx * r + r0, nr, 0, cs)
            return _window(buf, r0, nr, base + chip_idx * cs, cs)

        def ici(p, j, chip, src, dst):
            return pltpu.make_async_remote_copy(
                src_ref=src, dst_ref=dst, send_sem=sems[0].at[3 * p + j], recv_sem=sems[1].at[3 * p + j],
                device_id=(*chip, c), device_id_type=MESH)

        def d2d(p, j, win):
            return pltpu.make_async_remote_copy(
                src_ref=win, dst_ref=win, send_sem=sems[2].at[3 * p + j], recv_sem=sems[3].at[3 * p + j],
                device_id=(x, y, 1 - c), device_id_type=MESH)

        def local(p):
            lidx, r = pieces[p][1], ins[p].shape[1]
            rc = r // (2 * LOCAL_CHUNKS)
            return [pltpu.make_async_copy(ins[p].at[lidx, pl.ds(q * rc, rc), :], place(p, 2 * x + y, q * rc, rc),
                                          sems[4].at[2 * LOCAL_CHUNKS * p + q]) for q in range(2 * LOCAL_CHUNKS)]

        return x, y, c, _other_chips(x, y), place, ici, d2d, local

    def start(ins, outs, sems):
        x, y, c, chips, place, ici, d2d, local = ctx(ins, outs, sems)
        for p in range(n):
            lidx, rh = pieces[p][1], ins[p].shape[1] // 2
            for j, chip in enumerate(chips):
                ici(p, j, chip, ins[p].at[lidx, pl.ds(c * rh, rh), :], place(p, 2 * x + y, c * rh, rh)).start()
        for p in range(n):
            for cp in local(p):
                cp.start()

    def finish(ins, outs, sems):
        x, y, c, chips, place, ici, d2d, local = ctx(ins, outs, sems)
        for p in range(n):
            rh = ins[p].shape[1] // 2
            for j, chip in enumerate(chips):
                landed = place(p, 2 * chip[0] + chip[1], c * rh, rh)
                ici(p, j, chip, landed, landed).wait_recv()
                d2d(p, j, landed).start()
        for p in range(n):
            lidx, rh = pieces[p][1], ins[p].shape[1] // 2
            for j, chip in enumerate(chips):
                theirs = place(p, 2 * chip[0] + chip[1], (1 - c) * rh, rh)
                d2d(p, j, theirs).wait_recv()
                d2d(p, j, place(p, 2 * chip[0] + chip[1], c * rh, rh)).wait_send()
                ici(p, j, chip, ins[p].at[lidx, pl.ds(c * rh, rh), :], place(p, 2 * x + y, c * rh, rh)).wait_send()
            for cp in local(p):
                cp.wait()

    return dict(
        ins=[p[0] for p in pieces], outs=[jax.ShapeDtypeStruct(out_shape, BF16)], start=start, finish=finish,
        sems=[pltpu.SemaphoreType.DMA((3 * n,))] * 4 + [pltpu.SemaphoreType.DMA((2 * LOCAL_CHUNKS * n,))])


def _scatter_job(parts, plan):
    n = len(parts)

    def shard_shape(i):
        kind, _, size = plan[i]
        R, C = parts[i].shape
        return (size, C) if kind == "row" else (R, size)

    def copies(ins, outs, sems):
        x, y, c = _coords()
        cps = []
        for i in range(n):
            kind, base, size = plan[i]
            R, C = ins[i].shape
            for j, chip in enumerate(_other_chips(x, y)):
                their = 2 * chip[0] + chip[1]
                if kind == "row":
                    src = _window(ins[i], base + their * size, size, 0, C)
                else:
                    src = _window(ins[i], 0, R, base + their * size, size)
                cps.append(pltpu.make_async_remote_copy(
                    src_ref=src, dst_ref=outs[i].at[j], send_sem=sems[0].at[3 * i + j],
                    recv_sem=sems[1].at[3 * i + j], device_id=(*chip, c), device_id_type=MESH))
        return cps

    def start(ins, outs, sems):
        for cp in copies(ins, outs, sems):
            cp.start()

    def finish(ins, outs, sems):
        for cp in copies(ins, outs, sems):
            cp.wait()

    return dict(
        ins=list(parts), outs=[jax.ShapeDtypeStruct((3,) + shard_shape(i), F32) for i in range(n)],
        start=start, finish=finish, sems=[pltpu.SemaphoreType.DMA((3 * n,))] * 2)


def _run_job(job, name):
    n_i, n_o = len(job["ins"]), len(job["outs"])

    def body(*refs):
        ins, outs, sems = refs[:n_i], refs[n_i:n_i + n_o], refs[n_i + n_o:]
        job["start"](ins, outs, sems)
        job["finish"](ins, outs, sems)

    return pl.pallas_call(
        body, name=name, in_specs=[ANY] * n_i, out_specs=tuple([ANY] * n_o), out_shape=tuple(job["outs"]),
        scratch_shapes=list(job["sems"]),
        compiler_params=pltpu.CompilerParams(vmem_limit_bytes=VMEM_LIMIT_BYTES))(*job["ins"])


def _sibling_exchange(srcs, name):
    n = len(srcs)

    def body(*refs):
        ins, outs = refs[:n], refs[n:2 * n]
        send_sems, recv_sems = refs[2 * n:]
        x, y, c = _coords()
        cps = []
        for i in range(n):
            cp = pltpu.make_async_remote_copy(
                src_ref=ins[i].at[1 - c], dst_ref=outs[i], send_sem=send_sems.at[i], recv_sem=recv_sems.at[i],
                device_id=(x, y, 1 - c), device_id_type=MESH)
            cp.start()
            cps.append(cp)
        for cp in cps:
            cp.wait()

    return pl.pallas_call(
        body, name=name, in_specs=[ANY] * n, out_specs=tuple([ANY] * n),
        out_shape=tuple(jax.ShapeDtypeStruct(s.shape[1:], s.dtype) for s in srcs),
        scratch_shapes=[pltpu.SemaphoreType.DMA((n,))] * 2,
        compiler_params=pltpu.CompilerParams(vmem_limit_bytes=VMEM_LIMIT_BYTES))(*srcs)


def _sibling_share(both, name):
    n = len(both)

    def body(*refs):
        outs = refs[n:2 * n]
        send_sems, recv_sems = refs[2 * n:]
        x, y, c = _coords()
        cps = []
        for i in range(n):
            cp = pltpu.make_async_remote_copy(
                src_ref=outs[i].at[c], dst_ref=outs[i].at[c], send_sem=send_sems.at[i], recv_sem=recv_sems.at[i],
                device_id=(x, y, 1 - c), device_id_type=MESH)
            cp.start()
            cps.append(cp)
        for cp in cps:
            cp.wait()

    return pl.pallas_call(
        body, name=name, in_specs=[ANY] * n, out_specs=tuple([ANY] * n),
        out_shape=tuple(jax.ShapeDtypeStruct(b.shape, b.dtype) for b in both),
        input_output_aliases={i: i for i in range(n)},
        scratch_shapes=[pltpu.SemaphoreType.DMA((n,))] * 2,
        compiler_params=pltpu.CompilerParams(vmem_limit_bytes=VMEM_LIMIT_BYTES))(*both)


def _add_half(full3, recv, core, name):
    _, Rh, C = full3.shape
    tr = _pick(Rh, (256, 176, 128, 64, 32, 16, 8))
    tc = _pick(C, (2048, 1536, 1408, 1024, 512, 256, 128))

    def body(c_ref, a_ref, b_ref, o_ref):
        o_ref[...] = a_ref[...] + b_ref[...]

    return pl.pallas_call(
        body, name=name,
        grid_spec=pltpu.PrefetchScalarGridSpec(
            num_scalar_prefetch=1, grid=(Rh // tr, C // tc),
            in_specs=[pl.BlockSpec((None, tr, tc), lambda i, j, cr: (cr[0], i, j)),
                      pl.BlockSpec((tr, tc), lambda i, j, cr: (i, j))],
            out_specs=pl.BlockSpec((tr, tc), lambda i, j, cr: (i, j))),
        out_shape=jax.ShapeDtypeStruct((Rh, C), F32),
        compiler_params=_params(("parallel", "parallel")))(core, full3, recv)


def _add_scattered(part, recv, kind, base, size, core_chip, name):
    _, rs, cs = recv.shape
    tr = _pick(rs, (256, 176, 128, 64, 32, 16, 8))
    tc = _pick(cs, (2048, 1536, 1408, 1024, 512, 256, 128))
    assert base % size == 0
    if kind == "row":
        pidx = lambda i, j, cr: ((base // size + cr[1]) * (rs // tr) + i, j)
    else:
        pidx = lambda i, j, cr: (i, (base // size + cr[1]) * (cs // tc) + j)

    def body(c_ref, a_ref, r_ref, o_ref):
        o_ref[...] = ((a_ref[...] + r_ref[0]) + r_ref[1]) + r_ref[2]

    return pl.pallas_call(
        body, name=name,
        grid_spec=pltpu.PrefetchScalarGridSpec(
            num_scalar_prefetch=1, grid=(rs // tr, cs // tc),
            in_specs=[pl.BlockSpec((tr, tc), pidx), pl.BlockSpec((3, tr, tc), lambda i, j, cr: (0, i, j))],
            out_specs=pl.BlockSpec((None, tr, tc), lambda i, j, cr: (cr[0], i, j))),
        out_shape=jax.ShapeDtypeStruct((2, rs, cs), F32),
        compiler_params=_params(("parallel", "parallel")))(core_chip, part, recv)


def _rs_prepare(g, windows, core_chip, tag):
    R, C = g.shape
    if windows[0][0] == "row":
        size = windows[0][2]
        t = g.reshape(N_CHIPS, 2, size // 2, C).transpose(1, 0, 2, 3).reshape(2, R // 2, C)
        windows = [(k, b // 2, s // 2) for k, b, s in windows]
    else:
        t = g.reshape(2, R // 2, C)
    (recv,) = _sibling_exchange([t], f"rs_swap_{tag}")
    return _add_half(t, recv, core_chip, f"rs_add_half_{tag}"), list(windows)


def _rs_finish(part, windows, got, core_chip, tag):
    both = [_add_scattered(part, r, k, b, s, core_chip, f"rs_add_{tag}_{n}")
            for n, (r, (k, b, s)) in enumerate(zip(got, windows))]
    both = _sibling_share(both, f"rs_share_{tag}")
    return [t.reshape(2 * t.shape[1], t.shape[2]) for t in both]


def _rope_tables(positions, S):
    half = ROPE_DIM // 2
    inv_freq = ROPE_THETA ** (-jnp.arange(0, ROPE_DIM, 2, dtype=F32) / ROPE_DIM)
    ang = positions.reshape(S, 1).astype(F32) * inv_freq[None, :]
    cos, sin = jnp.cos(ang), jnp.sin(ang)
    zeros = jnp.zeros((S, half), F32)
    rest0 = jnp.zeros((S, HEAD_DIM - ROPE_DIM), F32)
    cs = jnp.concatenate([cos, cos, jnp.ones((S, HEAD_DIM - ROPE_DIM), F32)], axis=1)
    sa = jnp.concatenate([-sin, zeros, rest0], axis=1)
    sb = jnp.concatenate([zeros, sin, rest0], axis=1)
    return cs, sa, sb


def kernel(x, c, positions, ada_w, ada_b, norm_mix, norm_ffn, ab_w_in, sgu_w, sgu_b, ab_w_out, conv_w_in, conv_w, conv_w_out, ffn_w_gate, ffn_w_up, ffn_w_down, final_norm, loss_target, m_ada_w, m_ada_b, m_norm_mix, m_norm_ffn, m_ab_w_in, m_sgu_w, m_sgu_b, m_ab_w_out, m_conv_w_in, m_conv_w, m_conv_w_out, m_ffn_w_gate, m_ffn_w_up, m_ffn_w_down, m_final_norm, v_ada_w, v_ada_b, v_norm_mix, v_norm_ffn, v_ab_w_in, v_sgu_w, v_sgu_b, v_ab_w_out, v_conv_w_in, v_conv_w, v_conv_w_out, v_ffn_w_gate, v_ffn_w_up, v_ffn_w_down, v_final_norm):
    S, D = x.shape[1], x.shape[2]
    L = ada_w.shape[0]
    n_mix_heads = D // HEAD_DIM
    n_attn = 3 * n_mix_heads // 4
    A = n_attn * HEAD_DIM
    G = n_mix_heads - n_attn
    F = ffn_w_gate.shape[2] * N_CHIPS
    mix_in = ab_w_in.shape[2] * N_CHIPS
    xi, yi, ci = _coords()
    chip = 2 * xi + yi
    dev = 4 * xi + 2 * yi + ci
    core1 = jnp.reshape(ci, (1,)).astype(jnp.int32)
    chip1 = jnp.reshape(chip, (1,)).astype(jnp.int32)
    x2 = x.reshape(S, D)
    target = loss_target.reshape(S, D)

    n_conv = conv_w.size
    w0 = D + n_conv
    w0p = -(-w0 // 128) * 128
    pack = jnp.zeros((8, w0p), F32).at[0, :D].set(c[0]).at[0, D:w0].set(conv_w.reshape(-1))
    g0 = _all_gather8(pack, "gather_cond")
    c_all = g0[:, 0, :D]
    c_act = c_all * jax.nn.sigmoid(c_all)
    conv_full = jnp.concatenate(
        [g0[2 * j, 0, D:w0].reshape(conv_w.shape) for j in range(N_CHIPS)], axis=2)
    mod_part = _ada_fwd(c_act, ada_w, "ada_fwd")
    n_ada = ada_w.shape[2]
    g1 = _all_gather8(mod_part.reshape(L * N_DEV, n_ada), "gather_mod")
    mod_all = jnp.concatenate([g1[2 * j].reshape(L, N_DEV, n_ada) for j in range(N_CHIPS)], axis=2)
    mod = lax.dynamic_index_in_dim(mod_all, dev, axis=1, keepdims=False) + ada_b
    mods = mod.reshape(L, 6, 1, D)
    cs, sa, sb = _rope_tables(positions, S)

    bf = lambda t: t.astype(BF16)
    w_in_e, w_out_e = bf(ab_w_in), bf(ab_w_out)
    w_in_o, w_out_o = bf(conv_w_in), bf(conv_w_out)
    w_gate, w_up, w_down = bf(ffn_w_gate), bf(ffn_w_up), bf(ffn_w_down)
    def gather_jobs(l):
        i = l // 2
        first, n_in_cols = ((w_in_e, w_out_e), mix_in) if l % 2 == 0 else ((w_in_o, w_out_o), 3 * D)
        return [_gather_job([(first[0], i, "col", 0)], (D, n_in_cols)),
                _gather_job([(first[1], i, "row", 0)], (D, D)),
                _gather_job([(w_gate, l, "col", 0), (w_up, l, "col", F)], (D, 2 * F)),
                _gather_job([(w_down, l, "row", 0)], (F, D))]

    layer_w = [[_run_job(job, f"gather_w0_{n}")[0] for n, job in enumerate(gather_jobs(0))]]

    saved = []
    xc = x2
    pending = None
    for l in range(L):
        i = l // 2
        w_in, w_out, w_gu, w_dn = layer_w[l]
        nxt = gather_jobs(l + 1) if l + 1 < L else [None] * 4
        got_w = []

        def mm_fwd(a, b, out_dtype, name, job):
            if job is None:
                return _mm(a, b, "nn", out_dtype, name)
            out, w_next = _mm(a, b, "nn", out_dtype, name, job=job)
            got_w.append(w_next)
            return out

        sh_m, sc_m, g_m, sh_f, sc_f, g_f = (mods[l, t] for t in range(6))
        weff_m = norm_mix[l][None, :] * (1.0 + sc_m)
        weff_f = norm_ffn[l][None, :] * (1.0 + sc_f)
        if pending is None:
            _, h = _norm_mod(xc, None, None, weff_m, sh_m, f"norm_mix{l}")
        else:
            xc, h = _norm_mod(xc, pending[0], pending[1], weff_m, sh_m, f"norm_mix{l}")
        z = mm_fwd(h, w_in, BF16, f"mm_in{l}", nxt[0])
        st = dict(x=xc, h=h, z=z, weff_m=weff_m, weff_f=weff_f, g_m=g_m, g_f=g_f, sc_m=sc_m, sc_f=sc_f)
        if l % 2 == 0:
            q, k, v = _qkv_prep(z, cs, sa, sb, n_attn, f"qkv_prep{l}")
            o, lse = _attn_fwd(q, k, v, n_attn, f"attn_fwd{l}")
            bT = sgu_b[i].T
            so = _sgu_fwd(z, sgu_w[i], bT, n_attn, f"sgu_fwd{l}")
            cat = jnp.concatenate([o, so], axis=1)
            st.update(q=q, k=k, v=v, o=o, lse=lse, bT=bT)
        else:
            w8 = jnp.zeros((8, D), F32).at[:3].set(conv_full[i])
            cat = _conv_fwd(z, w8, f"conv_fwd{l}")
            st.update(w8=w8)
        mix = mm_fwd(cat, w_out, F32, f"mm_out{l}", nxt[1])
        x1, h2 = _norm_mod(xc, mix, g_m, weff_f, sh_f, f"norm_ffn{l}")
        ab = mm_fwd(h2, w_gu, BF16, f"mm_gu{l}", nxt[2])
        f = _swiglu(ab, f"swiglu{l}")
        yv = mm_fwd(f, w_dn, F32, f"mm_down{l}", nxt[3])
        if got_w:
            layer_w.append(got_w)
        st.update(cat=cat, mix=mix, x1=x1, h2=h2, ab=ab, f=f, y=yv)
        saved.append(st)
        xc = x1
        pending = (yv, g_f)

    dx, loss11, dfinal = _loss_head(xc, pending[0], pending[1], final_norm[None, :], target, "loss_head")
    loss = lax.psum(loss11[0, 0], ("x", "y", "c"))

    dmods = [None] * L
    dnorm_mix, dnorm_ffn = [None] * L, [None] * L
    big = {l: {} for l in range(L)}
    core_chip = jnp.concatenate([core1, chip1])
    dsgu_w, dsgu_b, dconv = [None] * (L - L // 2), [None] * (L - L // 2), [None] * (L // 2)

    def mm_bwd(a, b, mode, out_dtype, name, pending):
        if pending is None:
            return _mm(a, b, mode, out_dtype, name)
        part, windows, lay, keys = pending
        out, *got = _mm(a, b, mode, out_dtype, name, job=_scatter_job([part] * len(windows), windows))
        for key, red in zip(keys, _rs_finish(part, windows, got, core_chip, f"{lay}_{keys[0]}")):
            big[lay][key] = red
        return out

    carried = {}
    for l in reversed(range(L)):
        i = l // 2
        st = saved[l]
        w_in, w_out, w_gu, w_dn = layer_w[l]
        dy, dg_f = _gate_bwd(dx, st["y"], st["g_f"], f"gate_f_bwd{l}")
        df = mm_bwd(dy, w_dn, "nt", BF16, f"mm_down_dx{l}", carried.pop("out", None))
        dw_dn = mm_bwd(st["f"], dy, "tn", F32, f"mm_down_dw{l}", carried.pop("in", None))
        p_dn, win_dn = _rs_prepare(dw_dn, [("row", 0, F // N_CHIPS)], core_chip, f"{l}_down")
        dab = _swiglu_bwd(st["ab"], df, f"swiglu_bwd{l}")
        dh2 = mm_bwd(dab, w_gu, "nt", F32, f"mm_gu_dx{l}", (p_dn, win_dn, l, ["down"]))
        dw_gu = _mm(st["h2"], dab, "tn", F32, f"mm_gu_dw{l}")
        p_gu, win_gu = _rs_prepare(dw_gu, [("col", 0, F // N_CHIPS), ("col", F, F // N_CHIPS)], core_chip, f"{l}_gu")
        dx1, dsh_f, dweff_f = _norm_mod_bwd(dh2, st["x1"], st["weff_f"], dx, f"norm_ffn_bwd{l}")
        dmix, dg_m = _gate_bwd(dx1, st["mix"], st["g_m"], f"gate_m_bwd{l}")
        dcat = _mm(dmix, w_out, "nt", F32, f"mm_out_dx{l}")
        dw_out = _mm(st["cat"], dmix, "tn", F32, f"mm_out_dw{l}")
        p_out, win_out = _rs_prepare(dw_out, [("row", 0, D // N_CHIPS)], core_chip, f"{l}_out")
        if l % 2 == 0:
            dq, dk, dv = _attn_bwd(st["q"], st["k"], st["v"], st["o"], st["lse"], dcat, n_attn, f"attn_bwd{l}")
            dqkv = _dqkv_post(dq, dk, dv, cs, sa, sb, n_attn, f"dqkv_post{l}")
            duv, dsgu_w[i], dbT = _sgu_bwd(st["z"], dcat, sgu_w[i], st["bT"], n_attn, f"sgu_bwd{l}")
            dsgu_b[i] = dbT.T
            dz = jnp.concatenate([dqkv, duv], axis=1)
        else:
            dz, dw8 = _conv_bwd(st["z"], dcat, st["w8"], f"conv_bwd{l}")
            dconv[i] = dw8[:3]
        dh = mm_bwd(dz, w_in, "nt", F32, f"mm_in_dx{l}", (p_gu, win_gu[:1], l, ["gate"]))
        dw_in = mm_bwd(st["h"], dz, "tn", F32, f"mm_in_dw{l}", (p_gu, win_gu[1:], l, ["up"]))
        p_in, win_in = _rs_prepare(dw_in, [("col", 0, w_in.shape[1] // N_CHIPS)], core_chip, f"{l}_in")
        carried = {"out": (p_out, win_out, l, ["out"]), "in": (p_in, win_in, l, ["in"])}
        dx, dsh_m, dweff_m = _norm_mod_bwd(dh, st["x"], st["weff_m"], dx1, f"norm_mix_bwd{l}")
        dmods[l] = jnp.concatenate(
            [dsh_m, dweff_m * norm_mix[l][None, :], dg_m, dsh_f, dweff_f * norm_ffn[l][None, :], dg_f], axis=1)
        dnorm_mix[l] = dweff_m * (1.0 + st["sc_m"])
        dnorm_ffn[l] = dweff_f * (1.0 + st["sc_f"])
    rest = [carried["out"], carried["in"]]
    got = _run_job(_scatter_job([p[0] for p in rest], [p[1][0] for p in rest]), "rs_scatter_last")
    for (part, windows, lay, keys), g in zip(rest, got):
        (big[lay][keys[0]],) = _rs_finish(part, windows, [g], core_chip, f"{lay}_{keys[0]}")
    grad_x = dx.reshape(1, S, D)

    dmod = jnp.concatenate(dmods, axis=0)
    small = [dmod.reshape(-1), jnp.concatenate(dnorm_mix, 0).reshape(-1), jnp.concatenate(dnorm_ffn, 0).reshape(-1),
             jnp.stack(dsgu_w).reshape(-1), jnp.stack(dsgu_b).reshape(-1), jnp.stack(dconv).reshape(-1), dfinal.reshape(-1)]
    sizes = [t.size for t in small]
    flat = jnp.concatenate(small)
    n_flat = flat.size
    rows = -(-n_flat // (128 * 8)) * 8
    flat = jnp.concatenate([flat, jnp.zeros((rows * 128 - n_flat,), F32)]).reshape(rows, 128)
    g2 = _all_gather8(flat, "gather_small")
    tot = _sum_leading(g2, "sum_small").reshape(-1)
    offs = [0]
    for s in sizes:
        offs.append(offs[-1] + s)
    take = lambda n, shape: tot[offs[n]:offs[n + 1]].reshape(shape)
    g_ada_b = take(0, ada_b.shape)
    g_norm_mix = take(1, norm_mix.shape)
    g_norm_ffn = take(2, norm_ffn.shape)
    g_sgu_w = take(3, sgu_w.shape)
    g_sgu_b = take(4, sgu_b.shape)
    g_conv_full = take(5, conv_full.shape)
    n_cw = conv_w.shape[2]
    g_conv_w = lax.dynamic_slice_in_dim(g_conv_full, chip * n_cw, n_cw, axis=2)
    g_final = take(6, final_norm.shape)
    dmod_all = g2[:, :, :].reshape(N_DEV, -1)[:, :offs[1]].reshape(N_DEV, L, 6 * D)
    dmod_mine = lax.dynamic_slice_in_dim(dmod_all, chip * n_ada, n_ada, axis=2).transpose(1, 0, 2)
    g_ada_w = _ada_bwd(c_act, dmod_mine, "ada_bwd")

    def stack(key, layers):
        return jnp.stack([big[l][key] for l in layers])

    even, odd, every = list(range(0, L, 2)), list(range(1, L, 2)), list(range(L))
    grads = dict(
        ada_w=g_ada_w, ada_b=g_ada_b, norm_mix=g_norm_mix, norm_ffn=g_norm_ffn,
        ab_w_in=stack("in", even), sgu_w=g_sgu_w, sgu_b=g_sgu_b, ab_w_out=stack("out", even),
        conv_w_in=stack("in", odd), conv_w=g_conv_w, conv_w_out=stack("out", odd),
        ffn_w_gate=stack("gate", every), ffn_w_up=stack("up", every), ffn_w_down=stack("down", every),
        final_norm=g_final)
    weights = dict(ada_w=ada_w, ada_b=ada_b, norm_mix=norm_mix, norm_ffn=norm_ffn, ab_w_in=ab_w_in, sgu_w=sgu_w,
                   sgu_b=sgu_b, ab_w_out=ab_w_out, conv_w_in=conv_w_in, conv_w=conv_w, conv_w_out=conv_w_out,
                   ffn_w_gate=ffn_w_gate, ffn_w_up=ffn_w_up, ffn_w_down=ffn_w_down, final_norm=final_norm)
    ms = dict(ada_w=m_ada_w, ada_b=m_ada_b, norm_mix=m_norm_mix, norm_ffn=m_norm_ffn, ab_w_in=m_ab_w_in, sgu_w=m_sgu_w,
              sgu_b=m_sgu_b, ab_w_out=m_ab_w_out, conv_w_in=m_conv_w_in, conv_w=m_conv_w, conv_w_out=m_conv_w_out,
              ffn_w_gate=m_ffn_w_gate, ffn_w_up=m_ffn_w_up, ffn_w_down=m_ffn_w_down, final_norm=m_final_norm)
    vs = dict(ada_w=v_ada_w, ada_b=v_ada_b, norm_mix=v_norm_mix, norm_ffn=v_norm_ffn, ab_w_in=v_ab_w_in, sgu_w=v_sgu_w,
              sgu_b=v_sgu_b, ab_w_out=v_ab_w_out, conv_w_in=v_conv_w_in, conv_w=v_conv_w, conv_w_out=v_conv_w_out,
              ffn_w_gate=v_ffn_w_gate, ffn_w_up=v_ffn_w_up, ffn_w_down=v_ffn_w_down, final_norm=v_final_norm)
    names = list(weights)
    deltas, new_m, new_v = {}, {}, {}
    for n in names:
        w, g = weights[n], grads[n]
        if w.ndim == 1:
            d_, m_, v_ = _adamw(w[None, :], g[None, :], ms[n][None, :], vs[n][None, :], f"adamw_{n}")
            deltas[n], new_m[n], new_v[n] = d_[0], m_[0], v_[0]
        else:
            deltas[n], new_m[n], new_v[n] = _adamw(w, g, ms[n], vs[n], f"adamw_{n}")
    return (loss, grad_x, *[grads[n] for n in names], *[deltas[n] for n in names],
            *[new_m[n] for n in names], *[new_v[n] for n in names])
```

```python
import functools
import math

import jax
import jax.numpy as jnp
from jax import lax
from jax.experimental import pallas as pl
from jax.experimental.pallas import tpu as pltpu

F32 = jnp.float32
BF16 = jnp.bfloat16
HEAD_DIM = 128
CHUNK = 128
ATTN_BLOCK = 128
ATTN_LANES_FWD = 4
ATTN_LANES_BWD = 2
DILATIONS = (1, 4, 16)
ROPE_DIM = HEAD_DIM // 4
ROPE_THETA = 500000.0
EPS = 1e-6
MASKED = -1e30
ADAM_LR, ADAM_B1, ADAM_B2, ADAM_EPS, ADAM_WD, ADAM_STEP = 0.001, 0.9, 0.999, 1e-08, 0.01, 10
VMEM_LIMIT_BYTES = 56 * 1024 * 1024
MESH = pl.DeviceIdType.MESH
ANY = pl.BlockSpec(memory_space=pl.ANY)
N_CHIPS = 4
N_DEV = 8


def _pick(n, cands):
    for t in cands:
        if n % t == 0:
            return t
    return n


def _params(sem):
    return pltpu.CompilerParams(dimension_semantics=sem, vmem_limit_bytes=VMEM_LIMIT_BYTES)


def _mm(a, b, mode, out_dtype, name, layer=None, job=None):
    bshape = b.shape[1:] if layer is not None else b.shape
    if mode == "nn":
        (M, K), (K2, N) = a.shape, bshape
    elif mode == "nt":
        (M, K), (N, K2) = a.shape, bshape
    else:
        (K, M), (K2, N) = a.shape, bshape
    assert K == K2, (a.shape, b.shape, mode)
    if mode == "tn":
        tm = _pick(M, (2048, 1408, 1024, 512, 256, 128))
        tn = _pick(N, (1408, 1536, 1024, 512, 256, 128))
        tk = _pick(K, (512, 256, 128))
    else:
        tm = _pick(M, (1024, 512, 256, 128))
        tk = K if K <= 2048 else _pick(K, (1408, 1536, 1024, 512, 256, 128))
        tn = _pick(N, (512, 256, 128)) if tk == K else _pick(N, (1024, 512, 256, 128))
    nk = K // tk
    dims = {"nn": (((1,), (0,)), ((), ())), "nt": (((1,), (1,)), ((), ())), "tn": (((0,), (0,)), ((), ()))}[mode]

    in_place = out_dtype == F32

    n_ji = len(job["ins"]) if job else 0
    n_jo = len(job["outs"]) if job else 0
    n_acc = 1 if nk > 1 and not in_place else 0
    grid = (M // tm, N // tn, nk)

    def body(a_ref, b_ref, *rest):
        jin, o_ref, jout = rest[:n_ji], rest[n_ji], rest[n_ji + 1:n_ji + 1 + n_jo]
        scratch = rest[n_ji + 1 + n_jo:]
        acc, sems = scratch[:n_acc], scratch[n_acc:]
        i, j, k = pl.program_id(0), pl.program_id(1), pl.program_id(2)
        if job:
            @pl.when((i == 0) & (j == 0) & (k == 0))
            def _():
                job["start"](jin, jout, sems)

        def product():
            return lax.dot_general(a_ref[...].astype(BF16), b_ref[...].astype(BF16), dims, preferred_element_type=F32)

        if nk == 1:
            o_ref[...] = product().astype(o_ref.dtype)
        else:
            acc_ref = o_ref if in_place else acc[0]

            @pl.when(k == 0)
            def _():
                acc_ref[...] = jnp.zeros_like(acc_ref)

            acc_ref[...] += product()

            if not in_place:
                @pl.when(k == nk - 1)
                def _():
                    o_ref[...] = acc_ref[...].astype(o_ref.dtype)

        if job:
            @pl.when((i == grid[0] - 1) & (j == grid[1] - 1) & (k == grid[2] - 1))
            def _():
                job["finish"](jin, jout, sems)

    if mode == "tn":
        a_spec = pl.BlockSpec((tk, tm), lambda i, j, k: (k, i))
    else:
        a_spec = pl.BlockSpec((tm, tk), lambda i, j, k: (i, k))
    if mode == "nt":
        bblk, bidx = (tn, tk), (lambda i, j, k: (j, k))
    else:
        bblk, bidx = (tk, tn), (lambda i, j, k: (k, j))
    if layer is not None:
        b_spec = pl.BlockSpec((None,) + bblk, lambda i, j, k: (layer,) + bidx(i, j, k))
    else:
        b_spec = pl.BlockSpec(bblk, bidx)
    out_spec = pl.BlockSpec((tm, tn), lambda i, j, k: (i, j))
    out_shape = jax.ShapeDtypeStruct((M, N), out_dtype)
    acc_scratch = [pltpu.VMEM((tm, tn), F32)] * n_acc
    if not job:
        return pl.pallas_call(
            body, name=name, grid=grid, in_specs=[a_spec, b_spec], out_specs=out_spec, out_shape=out_shape,
            scratch_shapes=acc_scratch, compiler_params=_params(("parallel", "parallel", "arbitrary")),
        )(a, b)
    return pl.pallas_call(
        body, name=name, grid=grid, in_specs=[a_spec, b_spec] + [ANY] * n_ji,
        out_specs=(out_spec,) + (ANY,) * n_jo, out_shape=(out_shape,) + tuple(job["outs"]),
        scratch_shapes=acc_scratch + list(job["sems"]),
        compiler_params=_params(("arbitrary", "arbitrary", "arbitrary")),
    )(a, b, *job["ins"])


def _rows(S):
    return _pick(S, (256, 128, 64, 32, 16, 8))


def _row_spec(tr, width, col=0):
    return pl.BlockSpec((tr, width), lambda i: (i, col))


def _vec_spec(rows, width):
    return pl.BlockSpec((rows, width), lambda i: (0, 0))


def _rms(xv):
    return lax.rsqrt(jnp.mean(xv * xv, axis=-1, keepdims=True) + EPS)


def _norm_mod(x, y, g, w_eff, sh, name):
    S, D = x.shape
    tr = _rows(S)
    fused = y is not None

    def body(*refs):
        if fused:
            x_ref, y_ref, g_ref, w_ref, s_ref, x1_ref, h_ref = refs
            xv = x_ref[...] + g_ref[...] * y_ref[...]
            x1_ref[...] = xv
        else:
            x_ref, w_ref, s_ref, h_ref = refs
            xv = x_ref[...]
        h_ref[...] = (xv * _rms(xv) * w_ref[...] + s_ref[...]).astype(BF16)

    big, vec = _row_spec(tr, D), _vec_spec(1, D)
    if fused:
        ins, in_specs = (x, y, g, w_eff, sh), [big, big, vec, vec, vec]
        out_shape = (jax.ShapeDtypeStruct((S, D), F32), jax.ShapeDtypeStruct((S, D), BF16))
        out_specs = (big, big)
    else:
        ins, in_specs = (x, w_eff, sh), [big, vec, vec]
        out_shape = jax.ShapeDtypeStruct((S, D), BF16)
        out_specs = big
    out = pl.pallas_call(body, name=name, grid=(S // tr,), in_specs=in_specs, out_specs=out_specs,
                         out_shape=out_shape, compiler_params=_params(("parallel",)))(*ins)
    return out if fused else (None, out)


def _norm_mod_bwd(dh, x, w_eff, dres, name):
    S, D = x.shape
    tr = _rows(S)

    def body(dh_ref, x_ref, w_ref, r_ref, dx_ref, dsh_ref, dw_ref):
        xv = x_ref[...]
        dhv = dh_ref[...].astype(F32)
        r = _rms(xv)
        xn = xv * r
        dxn = dhv * w_ref[...]
        dx_ref[...] = r_ref[...] + r * (dxn - xn * jnp.mean(dxn * xn, axis=-1, keepdims=True))

        @pl.when(pl.program_id(0) == 0)
        def _():
            dsh_ref[...] = jnp.zeros_like(dsh_ref)
            dw_ref[...] = jnp.zeros_like(dw_ref)

        dsh_ref[...] += jnp.sum(dhv, axis=0, keepdims=True)
        dw_ref[...] += jnp.sum(dhv * xn, axis=0, keepdims=True)

    big, vec = _row_spec(tr, D), _vec_spec(1, D)
    return pl.pallas_call(
        body, name=name, grid=(S // tr,), in_specs=[big, big, vec, big], out_specs=(big, vec, vec),
        out_shape=(jax.ShapeDtypeStruct((S, D), F32), jax.ShapeDtypeStruct((1, D), F32), jax.ShapeDtypeStruct((1, D), F32)),
        compiler_params=_params(("arbitrary",)))(dh, x, w_eff, dres)


def _gate_bwd(dx, y, g, name):
    S, D = dx.shape
    tr = _rows(S)

    def body(dx_ref, y_ref, g_ref, dy_ref, dg_ref):
        dxv = dx_ref[...]
        dy_ref[...] = (dxv * g_ref[...]).astype(BF16)

        @pl.when(pl.program_id(0) == 0)
        def _():
            dg_ref[...] = jnp.zeros_like(dg_ref)

        dg_ref[...] += jnp.sum(dxv * y_ref[...], axis=0, keepdims=True)

    big, vec = _row_spec(tr, D), _vec_spec(1, D)
    return pl.pallas_call(
        body, name=name, grid=(S // tr,), in_specs=[big, big, vec], out_specs=(big, vec),
        out_shape=(jax.ShapeDtypeStruct((S, D), BF16), jax.ShapeDtypeStruct((1, D), F32)),
        compiler_params=_params(("arbitrary",)))(dx, y, g)


def _loss_head(x, y, g, gamma, target, name):
    S, D = x.shape
    tr = _rows(S)

    def body(x_ref, y_ref, g_ref, gm_ref, t_ref, dx_ref, loss_ref, dgm_ref):
        xv = x_ref[...] + g_ref[...] * y_ref[...]
        r = _rms(xv)
        xn = xv * r
        err = xn * gm_ref[...] - t_ref[...]
        dout = err * (1.0 / D)
        dxn = dout * gm_ref[...]
        dx_ref[...] = r * (dxn - xn * jnp.mean(dxn * xn, axis=-1, keepdims=True))

        @pl.when(pl.program_id(0) == 0)
        def _():
            loss_ref[...] = jnp.zeros_like(loss_ref)
            dgm_ref[...] = jnp.zeros_like(dgm_ref)

        loss_ref[...] += 0.5 * jnp.sum(jnp.mean(err * err, axis=-1, keepdims=True), axis=0, keepdims=True)
        dgm_ref[...] += jnp.sum(dout * xn, axis=0, keepdims=True)

    big, vec = _row_spec(tr, D), _vec_spec(1, D)
    return pl.pallas_call(
        body, name=name, grid=(S // tr,), in_specs=[big, big, vec, vec, big],
        out_specs=(big, _vec_spec(1, 1), vec),
        out_shape=(jax.ShapeDtypeStruct((S, D), F32), jax.ShapeDtypeStruct((1, 1), F32), jax.ShapeDtypeStruct((1, D), F32)),
        compiler_params=_params(("arbitrary",)))(x, y, g, gamma, target)


def _silu(a):
    return a * jax.nn.sigmoid(a)


def _swiglu(ab, name):
    S, F2 = ab.shape
    F = F2 // 2
    tr = _pick(S, (128, 64, 32, 16, 8))

    def body(a_ref, b_ref, f_ref):
        f_ref[...] = (_silu(a_ref[...].astype(F32)) * b_ref[...].astype(F32)).astype(BF16)

    return pl.pallas_call(
        body, name=name, grid=(S // tr,), in_specs=[_row_spec(tr, F, 0), _row_spec(tr, F, 1)],
        out_specs=_row_spec(tr, F), out_shape=jax.ShapeDtypeStruct((S, F), BF16),
        compiler_params=_params(("parallel",)))(ab, ab)


def _swiglu_bwd(ab, df, name):
    S, F2 = ab.shape
    F = F2 // 2
    tr = _pick(S, (128, 64, 32, 16, 8))

    def body(a_ref, b_ref, df_ref, da_ref, db_ref):
        a = a_ref[...].astype(F32)
        sg = jax.nn.sigmoid(a)
        dfv = df_ref[...].astype(F32)
        da_ref[...] = (dfv * b_ref[...].astype(F32) * (sg * (1.0 + a * (1.0 - sg)))).astype(BF16)
        db_ref[...] = (dfv * a * sg).astype(BF16)

    def body2(a_ref, b_ref, df_ref, o_ref):
        body(a_ref, b_ref, df_ref, o_ref.at[:, pl.ds(0, F)], o_ref.at[:, pl.ds(F, F)])

    return pl.pallas_call(
        body2, name=name, grid=(S // tr,),
        in_specs=[_row_spec(tr, F, 0), _row_spec(tr, F, 1), _row_spec(tr, F)],
        out_specs=_row_spec(tr, F2), out_shape=jax.ShapeDtypeStruct((S, F2), BF16),
        compiler_params=_params(("parallel",)))(ab, ab, df)


def _shift_rows(v, n):
    return pltpu.roll(v, n, 0)


def _conv_fwd(p, w8, name):
    S, D3 = p.shape
    D = D3 // 3
    tr = _rows(S)
    nb8 = tr // 8

    def body(gb_ref, gc_ref, hx_ref, gcp_ref, hxp_ref, w_ref, o_ref):
        i = pl.program_id(0)
        y = gc_ref[...].astype(F32) * hx_ref[...].astype(F32)
        yp = jnp.where(i > 0, gcp_ref[...].astype(F32) * hxp_ref[...].astype(F32), 0.0)
        w0, w1, w2 = w_ref[0:1, :], w_ref[1:2, :], w_ref[2:3, :]
        conv = w0 * _shift_rows(y, 2) + w1 * _shift_rows(y, 1) + w2 * y
        o_ref[...] = (gb_ref[...].astype(F32) * conv).astype(BF16)
        rid = lax.broadcasted_iota(jnp.int32, (8, D), 0)
        y8 = y[0:8, :]
        y1 = jnp.where(rid < 1, _shift_rows(yp, 1), _shift_rows(y8, 1))
        y2 = jnp.where(rid < 2, _shift_rows(yp, 2), _shift_rows(y8, 2))
        conv8 = w0 * y2 + w1 * y1 + w2 * y8
        o_ref[0:8, :] = (gb_ref[0:8, :].astype(F32) * conv8).astype(BF16)

    def col(c):
        return pl.BlockSpec((tr, D), lambda i: (i, c))

    def prev8(c):
        return pl.BlockSpec((8, D), lambda i: (jnp.maximum(i * nb8 - 1, 0), c))

    return pl.pallas_call(
        body, name=name, grid=(S // tr,),
        in_specs=[col(0), col(1), col(2), prev8(1), prev8(2), _vec_spec(8, D)],
        out_specs=_row_spec(tr, D), out_shape=jax.ShapeDtypeStruct((S, D), BF16),
        compiler_params=_params(("parallel",)))(p, p, p, p, p, w8)


def _conv_bwd(p, do, w8, name):
    S, D3 = p.shape
    D = D3 // 3
    tr = _rows(S)
    nb8 = tr // 8
    nt = S // tr

    def body(gb_ref, gc_ref, hx_ref, gcp_ref, hxp_ref, do_ref, gbn_ref, don_ref, w_ref, dp_ref, dw_ref):
        i = pl.program_id(0)
        gb = gb_ref[...].astype(F32)
        gc = gc_ref[...].astype(F32)
        hx = hx_ref[...].astype(F32)
        dov = do_ref[...].astype(F32)
        y = gc * hx
        yp = jnp.where(i > 0, gcp_ref[...].astype(F32) * hxp_ref[...].astype(F32), 0.0)
        dconv = dov * gb
        dcn = jnp.where(i < nt - 1, don_ref[...].astype(F32) * gbn_ref[...].astype(F32), 0.0)
        w0, w1, w2 = w_ref[0:1, :], w_ref[1:2, :], w_ref[2:3, :]
        rid = lax.broadcasted_iota(jnp.int32, (tr, D), 0)
        rid8 = lax.broadcasted_iota(jnp.int32, (8, D), 0)
        yp1 = jnp.concatenate([_shift_rows(yp, 1), jnp.zeros((tr - 8, D), F32)], axis=0)
        yp2 = jnp.concatenate([_shift_rows(yp, 2), jnp.zeros((tr - 8, D), F32)], axis=0)
        y1 = jnp.where(rid < 1, yp1, _shift_rows(y, 1))
        y2 = jnp.where(rid < 2, yp2, _shift_rows(y, 2))
        conv = w0 * y2 + w1 * y1 + w2 * y
        dn1 = jnp.concatenate([jnp.zeros((tr - 8, D), F32), _shift_rows(dcn, 7)], axis=0)
        dn2 = jnp.concatenate([jnp.zeros((tr - 8, D), F32), _shift_rows(dcn, 6)], axis=0)
        d1 = jnp.where(rid >= tr - 1, dn1, _shift_rows(dconv, tr - 1))
        d2 = jnp.where(rid >= tr - 2, dn2, _shift_rows(dconv, tr - 2))
        dy = w2 * dconv + w1 * d1 + w0 * d2
        dp_ref[:, pl.ds(0, D)] = (dov * conv).astype(BF16)
        dp_ref[:, pl.ds(D, D)] = (dy * hx).astype(BF16)
        dp_ref[:, pl.ds(2 * D, D)] = (dy * gc).astype(BF16)

        @pl.when(i == 0)
        def _():
            dw_ref[...] = jnp.zeros_like(dw_ref)

        upd = jnp.where(rid8 == 0, jnp.sum(dconv * y2, axis=0, keepdims=True),
                        jnp.where(rid8 == 1, jnp.sum(dconv * y1, axis=0, keepdims=True),
                                  jnp.where(rid8 == 2, jnp.sum(dconv * y, axis=0, keepdims=True), 0.0)))
        dw_ref[...] += upd

    def col(c):
        return pl.BlockSpec((tr, D), lambda i: (i, c))

    def prev8(c):
        return pl.BlockSpec((8, D), lambda i: (jnp.maximum(i * nb8 - 1, 0), c))

    def next8(c):
        return pl.BlockSpec((8, D), lambda i: (jnp.minimum((i + 1) * nb8, S // 8 - 1), c))

    return pl.pallas_call(
        body, name=name, grid=(nt,),
        in_specs=[col(0), col(1), col(2), prev8(1), prev8(2), col(0), next8(0), next8(0), _vec_spec(8, D)],
        out_specs=(_row_spec(tr, D3), _vec_spec(8, D)),
        out_shape=(jax.ShapeDtypeStruct((S, D3), BF16), jax.ShapeDtypeStruct((8, D), F32)),
        compiler_params=_params(("arbitrary",)))(p, p, p, p, p, do, p, do, w8)


_GELU_C = math.sqrt(2.0 / math.pi)


def _gelu(v):
    return 0.5 * v * (1.0 + jnp.tanh(_GELU_C * (v + 0.044715 * v * v * v)))


def _gelu_grad(v):
    t = jnp.tanh(_GELU_C * (v + 0.044715 * v * v * v))
    return 0.5 * (1.0 + t) + 0.5 * v * (1.0 - t * t) * _GELU_C * (1.0 + 3.0 * 0.044715 * v * v)


def _tril(w):
    r = lax.broadcasted_iota(jnp.int32, (CHUNK, CHUNK), 0)
    c = lax.broadcasted_iota(jnp.int32, (CHUNK, CHUNK), 1)
    return jnp.where(r >= c, w, 0.0)


def _sgu_fwd(z, w, bT, n_attn, name):
    S = z.shape[0]
    G = w.shape[0]
    W = G * CHUNK
    tr = _pick(S, (512, 256, 128))
    ucol = 3 * n_attn * HEAD_DIM // W

    def body(u_ref, v_ref, w_ref, b_ref, o_ref):
        for g in range(G):
            wt = _tril(w_ref[g]).astype(BF16)
            for ci in range(tr // CHUNK):
                rows, cols = pl.ds(ci * CHUNK, CHUNK), pl.ds(g * CHUNK, CHUNK)
                gv = _gelu(v_ref[rows, cols].astype(F32)).astype(BF16)
                mixed = jnp.dot(wt, gv, preferred_element_type=F32) + b_ref[:, g:g + 1]
                o_ref[rows, cols] = (_gelu(u_ref[rows, cols].astype(F32)) * mixed).astype(BF16)

    return pl.pallas_call(
        body, name=name, grid=(S // tr,),
        in_specs=[_row_spec(tr, W, ucol), _row_spec(tr, W, ucol + 1),
                  pl.BlockSpec((G, CHUNK, CHUNK), lambda i: (0, 0, 0)), _vec_spec(CHUNK, G)],
        out_specs=_row_spec(tr, W), out_shape=jax.ShapeDtypeStruct((S, W), BF16),
        compiler_params=_params(("parallel",)))(z, z, w, bT)


def _sgu_bwd(z, dcat, w, bT, n_attn, name):
    S = z.shape[0]
    G = w.shape[0]
    W = G * CHUNK
    tr = _pick(S, (512, 256, 128))
    ucol = 3 * n_attn * HEAD_DIM // W
    dcol = n_attn * HEAD_DIM // W

    def body(u_ref, v_ref, d_ref, w_ref, b_ref, o_ref, dw_ref, db_ref):
        @pl.when(pl.program_id(0) == 0)
        def _():
            dw_ref[...] = jnp.zeros_like(dw_ref)
            db_ref[...] = jnp.zeros_like(db_ref)

        lane = lax.broadcasted_iota(jnp.int32, (CHUNK, G), 1)
        for g in range(G):
            wtf = _tril(w_ref[g])
            wt = wtf.astype(BF16)
            dw_acc = jnp.zeros((CHUNK, CHUNK), F32)
            db_acc = jnp.zeros((CHUNK, 1), F32)
            for ci in range(tr // CHUNK):
                rows, cols = pl.ds(ci * CHUNK, CHUNK), pl.ds(g * CHUNK, CHUNK)
                uv = u_ref[rows, cols].astype(F32)
                vv = v_ref[rows, cols].astype(F32)
                dov = d_ref[rows, cols]
                gv = _gelu(vv).astype(BF16)
                mixed = jnp.dot(wt, gv, preferred_element_type=F32) + b_ref[:, g:g + 1]
                dmixed = dov * _gelu(uv)
                dmb = dmixed.astype(BF16)
                dgv = lax.dot_general(wt, dmb, (((0,), (0,)), ((), ())), preferred_element_type=F32)
                o_ref[rows, cols] = (dov * mixed * _gelu_grad(uv)).astype(BF16)
                o_ref[rows, pl.ds(W + g * CHUNK, CHUNK)] = (dgv * _gelu_grad(vv)).astype(BF16)
                dw_acc += lax.dot_general(dmb, gv, (((1,), (1,)), ((), ())), preferred_element_type=F32)
                db_acc += jnp.sum(dmixed, axis=1, keepdims=True)
            dw_ref[g] += _tril(dw_acc)
            db_ref[...] += jnp.where(lane == g, db_acc, 0.0)

    return pl.pallas_call(
        body, name=name, grid=(S // tr,),
        in_specs=[_row_spec(tr, W, ucol), _row_spec(tr, W, ucol + 1), _row_spec(tr, W, dcol),
                  pl.BlockSpec((G, CHUNK, CHUNK), lambda i: (0, 0, 0)), _vec_spec(CHUNK, G)],
        out_specs=(_row_spec(tr, 2 * W), pl.BlockSpec((G, CHUNK, CHUNK), lambda i: (0, 0, 0)), _vec_spec(CHUNK, G)),
        out_shape=(jax.ShapeDtypeStruct((S, 2 * W), BF16), jax.ShapeDtypeStruct((G, CHUNK, CHUNK), F32),
                   jax.ShapeDtypeStruct((CHUNK, G), F32)),
        compiler_params=_params(("arbitrary",)))(z, z, dcat, w, bT)


def _rope(v, cs, sa, sb):
    return v * cs + pltpu.roll(v, HEAD_DIM - ROPE_DIM // 2, 1) * sa + pltpu.roll(v, ROPE_DIM // 2, 1) * sb


def _rope_t(d, cs, sa, sb):
    return d * cs + pltpu.roll(d * sa, ROPE_DIM // 2, 1) + pltpu.roll(d * sb, HEAD_DIM - ROPE_DIM // 2, 1)


def _qkv_prep(z, cs, sa, sb, n_attn, name):
    S = z.shape[0]
    A = n_attn * HEAD_DIM
    tr = _rows(S)

    def body(q_ref, k_ref, v_ref, c_ref, a_ref, b_ref, qo_ref, ko_ref, vo_ref):
        cv, av, bv = c_ref[...], a_ref[...], b_ref[...]
        for h in range(n_attn):
            cols = pl.ds(h * HEAD_DIM, HEAD_DIM)
            qo_ref[:, cols] = _rope(q_ref[:, cols].astype(F32), cv, av, bv)
            ko_ref[:, cols] = _rope(k_ref[:, cols].astype(F32), cv, av, bv)
        vo_ref[...] = v_ref[...].astype(F32)

    tab = _row_spec(tr, HEAD_DIM)
    out = jax.ShapeDtypeStruct((S, A), F32)
    return pl.pallas_call(
        body, name=name, grid=(S // tr,),
        in_specs=[_row_spec(tr, A, 0), _row_spec(tr, A, 1), _row_spec(tr, A, 2), tab, tab, tab],
        out_specs=(_row_spec(tr, A),) * 3, out_shape=(out,) * 3,
        compiler_params=_params(("parallel",)))(z, z, z, cs, sa, sb)


def _dqkv_post(dq, dk, dv, cs, sa, sb, n_attn, name):
    S, A = dq.shape
    tr = _rows(S)

    def body(q_ref, k_ref, v_ref, c_ref, a_ref, b_ref, o_ref):
        cv, av, bv = c_ref[...], a_ref[...], b_ref[...]
        for h in range(n_attn):
            cols = pl.ds(h * HEAD_DIM, HEAD_DIM)
            o_ref[:, pl.ds(h * HEAD_DIM, HEAD_DIM)] = _rope_t(q_ref[:, cols], cv, av, bv).astype(BF16)
            o_ref[:, pl.ds(A + h * HEAD_DIM, HEAD_DIM)] = _rope_t(k_ref[:, cols], cv, av, bv).astype(BF16)
        o_ref[:, pl.ds(2 * A, A)] = v_ref[...].astype(BF16)

    tab = _row_spec(tr, HEAD_DIM)
    return pl.pallas_call(
        body, name=name, grid=(S // tr,),
        in_specs=[_row_spec(tr, A)] * 3 + [tab, tab, tab],
        out_specs=_row_spec(tr, 3 * A), out_shape=jax.ShapeDtypeStruct((S, 3 * A), BF16),
        compiler_params=_params(("parallel",)))(dq, dk, dv, cs, sa, sb)


def _block_rows(d, S, it):
    nblk = S // (d * ATTN_BLOCK)
    r = it // nblk
    jb = it % nblk
    q0 = r + d * ATTN_BLOCK * jb
    k0 = r + d * ATTN_BLOCK * jnp.maximum(jb - 1, 0)
    off = jnp.where(jb > 0, ATTN_BLOCK, 0)
    return q0, k0, off


def _band(off):
    a = lax.broadcasted_iota(jnp.int32, (ATTN_BLOCK, 2 * ATTN_BLOCK), 0) + off
    kj = lax.broadcasted_iota(jnp.int32, (ATTN_BLOCK, 2 * ATTN_BLOCK), 1)
    return (kj <= a) & (kj >= a - ATTN_BLOCK)


_NT = (((1,), (1,)), ((), ()))
_TN = (((0,), (0,)), ((), ()))


def _attn_fwd(q, k, v, n_attn, name, job=None):
    S, A = q.shape
    scale = HEAD_DIM ** -0.5
    n_ji = len(job["ins"]) if job else 0
    n_jo = len(job["outs"]) if job else 0

    def body(q_hbm, k_hbm, v_hbm, *rest):
        jin, (o_hbm, lse_hbm), jout = rest[:n_ji], rest[n_ji:n_ji + 2], rest[n_ji + 2:n_ji + 2 + n_jo]
        qs, ks, vs, acc, ms, ls, ob, sem = rest[n_ji + 2 + n_jo:n_ji + 2 + n_jo + 8]
        jsems = rest[n_ji + 2 + n_jo + 8:]
        h = pl.program_id(0)
        if job:
            @pl.when(h == 0)
            def _():
                job["start"](jin, jout, jsems)

        cols = pl.ds(pl.multiple_of(h * HEAD_DIM, HEAD_DIM), HEAD_DIM)
        cps = [pltpu.make_async_copy(src.at[:, cols], dst, sem.at[i])
               for i, (src, dst) in enumerate(((q_hbm, qs), (k_hbm, ks), (v_hbm, vs)))]
        for cp in cps:
            cp.start()
        acc[...] = jnp.zeros_like(acc)
        ms[...] = jnp.full_like(ms, MASKED)
        ls[...] = jnp.zeros_like(ls)
        for cp in cps:
            cp.wait()
        n_blocks = S // ATTN_BLOCK
        lanes = n_blocks // ATTN_LANES_FWD
        for d in DILATIONS:
            def step(it, carry, d=d):
                loaded = []
                for u in range(ATTN_LANES_FWD):
                    q0, k0, off = _block_rows(d, S, it + u * lanes)
                    qrows = pl.ds(q0, ATTN_BLOCK, stride=d)
                    krows = pl.ds(k0, 2 * ATTN_BLOCK, stride=d)
                    loaded.append((qrows, off, qs[qrows, :], ks[krows, :], vs[krows, :],
                                   ms[qrows, :], ls[qrows, :], acc[qrows, :]))
                results = []
                for qrows, off, qv, kv, vv, m_old, l_old, a_old in loaded:
                    s = lax.dot_general(qv.astype(BF16), kv.astype(BF16), _NT, preferred_element_type=F32) * scale
                    s = jnp.where(_band(off), s, MASKED)
                    m_new = jnp.maximum(m_old, jnp.max(s, axis=-1, keepdims=True))
                    alpha = jnp.exp(m_old - m_new)
                    p = jnp.exp(s - m_new)
                    l_new = alpha * l_old + jnp.sum(p, axis=-1, keepdims=True)
                    a_new = alpha * a_old + jnp.dot(p.astype(BF16), vv.astype(BF16), preferred_element_type=F32)
                    results.append((qrows, m_new, l_new, a_new))
                for qrows, m_new, l_new, a_new in results:
                    ms[qrows, :] = m_new
                    ls[qrows, :] = l_new
                    acc[qrows, :] = a_new
                return carry
            lax.fori_loop(0, lanes, step, 0)
        ob[...] = (acc[...] / ls[...]).astype(BF16)
        ms[...] = ms[...] + jnp.log(ls[...])
        out = [pltpu.make_async_copy(ob, o_hbm.at[:, cols], sem.at[0]),
               pltpu.make_async_copy(ms, lse_hbm.at[h], sem.at[1])]
        for cp in out:
            cp.start()
        for cp in out:
            cp.wait()
        if job:
            @pl.when(h == n_attn - 1)
            def _():
                job["finish"](jin, jout, jsems)

    res = pl.pallas_call(
        body, name=name, grid=(n_attn,), in_specs=[ANY] * (3 + n_ji), out_specs=(ANY,) * (2 + n_jo),
        out_shape=(jax.ShapeDtypeStruct((S, A), BF16), jax.ShapeDtypeStruct((n_attn, S, 1), F32))
        + tuple(job["outs"] if job else ()),
        scratch_shapes=[pltpu.VMEM((S, HEAD_DIM), F32)] * 4 + [pltpu.VMEM((S, 1), F32)] * 2
        + [pltpu.VMEM((S, HEAD_DIM), BF16), pltpu.SemaphoreType.DMA((3,))] + list(job["sems"] if job else ()),
        compiler_params=_params(("arbitrary",)))(q, k, v, *(job["ins"] if job else ()))
    return res[0], res[1], list(res[2:])


def _attn_bwd(q, k, v, o, lse, dcat, n_attn, name):
    S, A = q.shape
    scale = HEAD_DIM ** -0.5

    def body(q_hbm, k_hbm, v_hbm, o_hbm, lse_hbm, do_hbm, dq_hbm, dk_hbm, dv_hbm,
             qs, ks, vs, dos, dqs, dks, dvs, lses, dls, ob, sem):
        h = pl.program_id(0)
        cols = pl.ds(pl.multiple_of(h * HEAD_DIM, HEAD_DIM), HEAD_DIM)
        cps = [pltpu.make_async_copy(src.at[:, cols], dst, sem.at[i])
               for i, (src, dst) in enumerate(((q_hbm, qs), (k_hbm, ks), (v_hbm, vs), (do_hbm, dos), (o_hbm, ob)))]
        cps.append(pltpu.make_async_copy(lse_hbm.at[h], lses, sem.at[5]))
        for cp in cps:
            cp.start()
        dqs[...] = jnp.zeros_like(dqs)
        dks[...] = jnp.zeros_like(dks)
        dvs[...] = jnp.zeros_like(dvs)
        for cp in cps:
            cp.wait()
        dls[...] = jnp.sum(dos[...] * ob[...].astype(F32), axis=-1, keepdims=True)
        n_blocks = S // ATTN_BLOCK
        lanes = n_blocks // ATTN_LANES_BWD
        assert lanes % 2 == 0
        for d in DILATIONS:
            def step(it, carry, d=d):
                loaded = []
                for u in range(ATTN_LANES_BWD):
                    q0, k0, off = _block_rows(d, S, it + u * lanes)
                    qrows = pl.ds(q0, ATTN_BLOCK, stride=d)
                    krows = pl.ds(k0, 2 * ATTN_BLOCK, stride=d)
                    loaded.append((qrows, krows, off, qs[qrows, :], ks[krows, :], vs[krows, :], dos[qrows, :],
                                   lses[qrows, :], dls[qrows, :], dqs[qrows, :], dks[krows, :], dvs[krows, :]))
                results = []
                for qrows, krows, off, qv, kv, vv, dov, lse_v, dl_v, dq_old, dk_old, dv_old in loaded:
                    qb, kb, vb, dob = qv.astype(BF16), kv.astype(BF16), vv.astype(BF16), dov.astype(BF16)
                    s = lax.dot_general(qb, kb, _NT, preferred_element_type=F32) * scale
                    p = jnp.where(_band(off), jnp.exp(s - lse_v), 0.0)
                    pb = p.astype(BF16)
                    dp = lax.dot_general(dob, vb, _NT, preferred_element_type=F32)
                    ds = (p * (dp - dl_v) * scale).astype(BF16)
                    results.append((qrows, krows,
                                    dq_old + jnp.dot(ds, kb, preferred_element_type=F32),
                                    dk_old + lax.dot_general(ds, qb, _TN, preferred_element_type=F32),
                                    dv_old + lax.dot_general(pb, dob, _TN, preferred_element_type=F32)))
                for qrows, krows, dq_new, dk_new, dv_new in results:
                    dqs[qrows, :] = dq_new
                    dks[krows, :] = dk_new
                    dvs[krows, :] = dv_new
                return carry
            lax.fori_loop(0, lanes, step, 0)
        out = [pltpu.make_async_copy(src, dst.at[:, cols], sem.at[i])
               for i, (src, dst) in enumerate(((dqs, dq_hbm), (dks, dk_hbm), (dvs, dv_hbm)))]
        for cp in out:
            cp.start()
        for cp in out:
            cp.wait()

    grad = jax.ShapeDtypeStruct((S, A), F32)
    return pl.pallas_call(
        body, name=name, grid=(n_attn,), in_specs=[ANY] * 6, out_specs=(ANY, ANY, ANY), out_shape=(grad,) * 3,
        scratch_shapes=[pltpu.VMEM((S, HEAD_DIM), F32)] * 7 + [pltpu.VMEM((S, 1), F32)] * 2
        + [pltpu.VMEM((S, HEAD_DIM), BF16), pltpu.SemaphoreType.DMA((6,))],
        compiler_params=_params(("arbitrary",)))(q, k, v, o, lse, dcat)


def _ada_fwd(c_act, ada_w, name):
    L, D, n = ada_w.shape
    tn = _pick(n, (512, 256, 128))

    def body(c_ref, w_ref, o_ref):
        o_ref[...] = jnp.dot(c_ref[...], w_ref[...], preferred_element_type=F32)

    return pl.pallas_call(
        body, name=name, grid=(L, n // tn),
        in_specs=[pl.BlockSpec((N_DEV, D), lambda l, j: (0, 0)), pl.BlockSpec((None, D, tn), lambda l, j: (l, 0, j))],
        out_specs=pl.BlockSpec((None, N_DEV, tn), lambda l, j: (l, 0, j)),
        out_shape=jax.ShapeDtypeStruct((L, N_DEV, n), F32),
        compiler_params=_params(("parallel", "parallel")))(c_act, ada_w)


def _ada_bwd(c_act, dmod, name):
    L, _, n = dmod.shape
    D = c_act.shape[1]
    tn = _pick(n, (512, 256, 128))

    def body(c_ref, d_ref, o_ref):
        o_ref[...] = lax.dot_general(c_ref[...], d_ref[...], _TN, preferred_element_type=F32)

    return pl.pallas_call(
        body, name=name, grid=(L, n // tn),
        in_specs=[pl.BlockSpec((N_DEV, D), lambda l, j: (0, 0)), pl.BlockSpec((None, N_DEV, tn), lambda l, j: (l, 0, j))],
        out_specs=pl.BlockSpec((None, D, tn), lambda l, j: (l, 0, j)),
        out_shape=jax.ShapeDtypeStruct((L, D, n), F32),
        compiler_params=_params(("parallel", "parallel")))(c_act, dmod)


def _adamw(w, g, m, v, name, job=None):
    shape = w.shape
    C = shape[-1]
    R = w.size // C
    w2, g2, m2, v2 = (t.reshape(R, C) for t in (w, g, m, v))
    tr = _pick(R, (256, 128, 64, 32, 16, 8))
    tc = _pick(C, (2048, 1536, 1408, 1024, 512, 256, 128))
    n_ji = len(job["ins"]) if job else 0
    n_jo = len(job["outs"]) if job else 0
    grid = (R // tr, C // tc)

    def body(w_ref, g_ref, m_ref, v_ref, *rest):
        jin, (d_ref, mo_ref, vo_ref) = rest[:n_ji], rest[n_ji:n_ji + 3]
        jout, jsems = rest[n_ji + 3:n_ji + 3 + n_jo], rest[n_ji + 3 + n_jo:]
        i, j = pl.program_id(0), pl.program_id(1)
        if job:
            @pl.when((i == 0) & (j == 0))
            def _():
                job["start"](jin, jout, jsems)

        gv = g_ref[...]
        mn = ADAM_B1 * m_ref[...] + (1.0 - ADAM_B1) * gv
        vn = ADAM_B2 * v_ref[...] + (1.0 - ADAM_B2) * (gv * gv)
        m_hat = mn / (1.0 - ADAM_B1 ** ADAM_STEP)
        v_hat = vn / (1.0 - ADAM_B2 ** ADAM_STEP)
        d_ref[...] = -ADAM_LR * (m_hat / (jnp.sqrt(v_hat) + ADAM_EPS) + ADAM_WD * w_ref[...])
        mo_ref[...] = mn
        vo_ref[...] = vn
        if job:
            @pl.when((i == grid[0] - 1) & (j == grid[1] - 1))
            def _():
                job["finish"](jin, jout, jsems)

    spec = pl.BlockSpec((tr, tc), lambda i, j: (i, j))
    out = jax.ShapeDtypeStruct((R, C), F32)
    d, mn, vn, *got = pl.pallas_call(
        body, name=name, grid=grid, in_specs=[spec] * 4 + [ANY] * n_ji, out_specs=(spec,) * 3 + (ANY,) * n_jo,
        out_shape=(out,) * 3 + tuple(job["outs"] if job else ()), scratch_shapes=list(job["sems"] if job else ()),
        compiler_params=_params(("arbitrary", "arbitrary") if job else ("parallel", "parallel")),
    )(w2, g2, m2, v2, *(job["ins"] if job else ()))
    res = (d.reshape(shape), mn.reshape(shape), vn.reshape(shape))
    return res + (got,) if job else res


def _sum_leading(t, name):
    n, R, C = t.shape
    tr = _pick(R, (256, 128, 64, 32, 16, 8))

    def body(t_ref, o_ref):
        acc = t_ref[0]
        for i in range(1, n):
            acc = acc + t_ref[i]
        o_ref[...] = acc

    return pl.pallas_call(
        body, name=name, grid=(R // tr,), in_specs=[pl.BlockSpec((n, tr, C), lambda i: (0, i, 0))],
        out_specs=pl.BlockSpec((tr, C), lambda i: (i, 0)), out_shape=jax.ShapeDtypeStruct((R, C), F32),
        compiler_params=_params(("parallel",)))(t)


def _coords():
    return lax.axis_index("x"), lax.axis_index("y"), lax.axis_index("c")


def _other_chips(x, y):
    return [(1 - x, y), (x, 1 - y), (1 - x, 1 - y)]


def _all_gather8(t, name):
    R, C = t.shape

    def body(x_ref, out_ref, send_sems, recv_sems, local_sem):
        x, y, c = _coords()
        me, sibling = (x, y, c), (x, y, 1 - c)
        chips = _other_chips(x, y)

        def slot(px, py, pc):
            return out_ref.at[4 * px + 2 * py + pc]

        def copy(k, block, to, src=None):
            return pltpu.make_async_remote_copy(
                src_ref=slot(*block) if src is None else src, dst_ref=slot(*block),
                send_sem=send_sems.at[k], recv_sem=recv_sems.at[k], device_id=to, device_id_type=MESH)

        mine = pltpu.make_async_copy(x_ref, slot(*me), local_sem)
        mine.start()
        first = [copy(0, me, sibling, src=x_ref)]
        first += [copy(1 + j, me, (*chip, c), src=x_ref) for j, chip in enumerate(chips)]
        for cp in first:
            cp.start()
        passed = [copy(4 + j, (*chip, c), sibling) for j, chip in enumerate(chips)]
        for j, chip in enumerate(chips):
            copy(1 + j, (*chip, c), me).wait_recv()
            passed[j].start()
        copy(0, sibling, me).wait_recv()
        for j, chip in enumerate(chips):
            copy(4 + j, (*chip, 1 - c), me).wait_recv()
        for cp in first + passed:
            cp.wait_send()
        mine.wait()

    return pl.pallas_call(
        body, name=name, out_shape=jax.ShapeDtypeStruct((N_DEV, R, C), t.dtype),
        in_specs=[pl.BlockSpec(memory_space=pltpu.VMEM)], out_specs=pl.BlockSpec(memory_space=pltpu.VMEM),
        scratch_shapes=[pltpu.SemaphoreType.DMA((7,)), pltpu.SemaphoreType.DMA((7,)), pltpu.SemaphoreType.DMA],
        compiler_params=pltpu.CompilerParams(vmem_limit_bytes=VMEM_LIMIT_BYTES))(t)


def _window(ref, r0, nr, c0, nc):
    return ref.at[pl.ds(r0, nr), pl.ds(c0, nc)]


LOCAL_CHUNKS = 4


def _gather_job(pieces, out_shape):
    n = len(pieces)

    def ctx(ins, outs, sems):
        x, y, c = _coords()
        buf = outs[0]

        def place(p, chip_idx, r0, nr):
            _, _, kind, base = pieces[p]
            r, cs = ins[p].shape[1], ins[p].shape[2]
            if kind == "row":
                return _window(buf, base + chip_idx * r + r0, nr, 0, cs)
            return _window(buf, r0, nr, base + chip_idx * cs, cs)

        def ici(p, j, chip, src, dst):
            return pltpu.make_async_remote_copy(
                src_ref=src, dst_ref=dst, send_sem=sems[0].at[3 * p + j], recv_sem=sems[1].at[3 * p + j],
                device_id=(*chip, c), device_id_type=MESH)

        def d2d(p, j, win):
            return pltpu.make_async_remote_copy(
                src_ref=win, dst_ref=win, send_sem=sems[2].at[3 * p + j], recv_sem=sems[3].at[3 * p + j],
                device_id=(x, y, 1 - c), device_id_type=MESH)

        def local(p):
            lidx, r = pieces[p][1], ins[p].shape[1]
            rc = r // (2 * LOCAL_CHUNKS)
            return [pltpu.make_async_copy(ins[p].at[lidx, pl.ds(q * rc, rc), :], place(p, 2 * x + y, q * rc, rc),
                                          sems[4].at[2 * LOCAL_CHUNKS * p + q]) for q in range(2 * LOCAL_CHUNKS)]

        return x, y, c, _other_chips(x, y), place, ici, d2d, local

    def start(ins, outs, sems):
        x, y, c, chips, place, ici, d2d, local = ctx(ins, outs, sems)
        for p in range(n):
            lidx, rh = pieces[p][1], ins[p].shape[1] // 2
            for j, chip in enumerate(chips):
                ici(p, j, chip, ins[p].at[lidx, pl.ds(c * rh, rh), :], place(p, 2 * x + y, c * rh, rh)).start()
        for p in range(n):
            for cp in local(p):
                cp.start()

    def finish(ins, outs, sems):
        x, y, c, chips, place, ici, d2d, local = ctx(ins, outs, sems)
        for p in range(n):
            rh = ins[p].shape[1] // 2
            for j, chip in enumerate(chips):
                landed = place(p, 2 * chip[0] + chip[1], c * rh, rh)
                ici(p, j, chip, landed, landed).wait_recv()
                d2d(p, j, landed).start()
        for p in range(n):
            lidx, rh = pieces[p][1], ins[p].shape[1] // 2
            for j, chip in enumerate(chips):
                theirs = place(p, 2 * chip[0] + chip[1], (1 - c) * rh, rh)
                d2d(p, j, theirs).wait_recv()
                d2d(p, j, place(p, 2 * chip[0] + chip[1], c * rh, rh)).wait_send()
                ici(p, j, chip, ins[p].at[lidx, pl.ds(c * rh, rh), :], place(p, 2 * x + y, c * rh, rh)).wait_send()
            for cp in local(p):
                cp.wait()

    return dict(
        ins=[p[0] for p in pieces], outs=[jax.ShapeDtypeStruct(out_shape, BF16)], start=start, finish=finish,
        sems=[pltpu.SemaphoreType.DMA((3 * n,))] * 4 + [pltpu.SemaphoreType.DMA((2 * LOCAL_CHUNKS * n,))])


def _scatter_job(parts, plan):
    n = len(parts)

    def shard_shape(i):
        kind, _, size = plan[i]
        R, C = parts[i].shape
        return (size, C) if kind == "row" else (R, size)

    def copies(ins, outs, sems):
        x, y, c = _coords()
        cps = []
        for i in range(n):
            kind, base, size = plan[i]
            R, C = ins[i].shape
            for j, chip in enumerate(_other_chips(x, y)):
                their = 2 * chip[0] + chip[1]
                if kind == "row":
                    src = _window(ins[i], base + their * size, size, 0, C)
                else:
                    src = _window(ins[i], 0, R, base + their * size, size)
                cps.append(pltpu.make_async_remote_copy(
                    src_ref=src, dst_ref=outs[i].at[j], send_sem=sems[0].at[3 * i + j],
                    recv_sem=sems[1].at[3 * i + j], device_id=(*chip, c), device_id_type=MESH))
        return cps

    def start(ins, outs, sems):
        for cp in copies(ins, outs, sems):
            cp.start()

    def finish(ins, outs, sems):
        for cp in copies(ins, outs, sems):
            cp.wait()

    return dict(
        ins=list(parts), outs=[jax.ShapeDtypeStruct((3,) + shard_shape(i), F32) for i in range(n)],
        start=start, finish=finish, sems=[pltpu.SemaphoreType.DMA((3 * n,))] * 2)


def _swap_job(t):
    def copy(ins, outs, sems):
        x, y, c = _coords()
        return pltpu.make_async_remote_copy(
            src_ref=ins[0].at[1 - c], dst_ref=outs[0], send_sem=sems[0].at[0], recv_sem=sems[1].at[0],
            device_id=(x, y, 1 - c), device_id_type=MESH)

    return dict(ins=[t], outs=[jax.ShapeDtypeStruct(t.shape[1:], t.dtype)],
                start=lambda ins, outs, sems: copy(ins, outs, sems).start(),
                finish=lambda ins, outs, sems: copy(ins, outs, sems).wait(),
                sems=[pltpu.SemaphoreType.DMA((1,))] * 2)


def _join_jobs(jobs):
    jobs = [j for j in jobs if j]
    if not jobs:
        return None
    if len(jobs) == 1:
        return jobs[0]

    def each(fn_name, ins, outs, sems):
        i = o = s = 0
        for j in jobs:
            ni, no, ns = len(j["ins"]), len(j["outs"]), len(j["sems"])
            j[fn_name](ins[i:i + ni], outs[o:o + no], sems[s:s + ns])
            i, o, s = i + ni, o + no, s + ns

    return dict(ins=[a for j in jobs for a in j["ins"]], outs=[a for j in jobs for a in j["outs"]],
                sems=[a for j in jobs for a in j["sems"]],
                start=lambda ins, outs, sems: each("start", ins, outs, sems),
                finish=lambda ins, outs, sems: each("finish", ins, outs, sems))


def _hosted(call, jobs):
    jobs = [j for j in jobs if j]
    if not jobs:
        return call(None), []
    out, *rest = call(_join_jobs(jobs))
    per = []
    for j in jobs:
        n = len(j["outs"])
        per.append(rest[:n])
        rest = rest[n:]
    return out, per


def _run_job(job, name):
    n_i, n_o = len(job["ins"]), len(job["outs"])

    def body(*refs):
        ins, outs, sems = refs[:n_i], refs[n_i:n_i + n_o], refs[n_i + n_o:]
        job["start"](ins, outs, sems)
        job["finish"](ins, outs, sems)

    return pl.pallas_call(
        body, name=name, in_specs=[ANY] * n_i, out_specs=tuple([ANY] * n_o), out_shape=tuple(job["outs"]),
        scratch_shapes=list(job["sems"]),
        compiler_params=pltpu.CompilerParams(vmem_limit_bytes=VMEM_LIMIT_BYTES))(*job["ins"])


def _sibling_share(both, name):
    n = len(both)

    def body(*refs):
        outs = refs[n:2 * n]
        send_sems, recv_sems = refs[2 * n:]
        x, y, c = _coords()
        cps = []
        for i in range(n):
            cp = pltpu.make_async_remote_copy(
                src_ref=outs[i].at[c], dst_ref=outs[i].at[c], send_sem=send_sems.at[i], recv_sem=recv_sems.at[i],
                device_id=(x, y, 1 - c), device_id_type=MESH)
            cp.start()
            cps.append(cp)
        for cp in cps:
            cp.wait()

    return pl.pallas_call(
        body, name=name, in_specs=[ANY] * n, out_specs=tuple([ANY] * n),
        out_shape=tuple(jax.ShapeDtypeStruct(b.shape, b.dtype) for b in both),
        input_output_aliases={i: i for i in range(n)},
        scratch_shapes=[pltpu.SemaphoreType.DMA((n,))] * 2,
        compiler_params=pltpu.CompilerParams(vmem_limit_bytes=VMEM_LIMIT_BYTES))(*both)


def _add_half(full3, recv, core, name):
    _, Rh, C = full3.shape
    tr = _pick(Rh, (256, 176, 128, 64, 32, 16, 8))
    tc = _pick(C, (2048, 1536, 1408, 1024, 512, 256, 128))

    def body(c_ref, a_ref, b_ref, o_ref):
        o_ref[...] = a_ref[...] + b_ref[...]

    return pl.pallas_call(
        body, name=name,
        grid_spec=pltpu.PrefetchScalarGridSpec(
            num_scalar_prefetch=1, grid=(Rh // tr, C // tc),
            in_specs=[pl.BlockSpec((None, tr, tc), lambda i, j, cr: (cr[0], i, j)),
                      pl.BlockSpec((tr, tc), lambda i, j, cr: (i, j))],
            out_specs=pl.BlockSpec((tr, tc), lambda i, j, cr: (i, j))),
        out_shape=jax.ShapeDtypeStruct((Rh, C), F32),
        compiler_params=_params(("parallel", "parallel")))(core, full3, recv)


def _add_scattered(part, recv, kind, base, size, core_chip, name):
    _, rs, cs = recv.shape
    tr = _pick(rs, (256, 176, 128, 64, 32, 16, 8))
    tc = _pick(cs, (2048, 1536, 1408, 1024, 512, 256, 128))
    assert base % size == 0
    if kind == "row":
        pidx = lambda i, j, cr: ((base // size + cr[1]) * (rs // tr) + i, j)
    else:
        pidx = lambda i, j, cr: (i, (base // size + cr[1]) * (cs // tc) + j)

    def body(c_ref, a_ref, r_ref, o_ref):
        o_ref[...] = ((a_ref[...] + r_ref[0]) + r_ref[1]) + r_ref[2]

    return pl.pallas_call(
        body, name=name,
        grid_spec=pltpu.PrefetchScalarGridSpec(
            num_scalar_prefetch=1, grid=(rs // tr, cs // tc),
            in_specs=[pl.BlockSpec((tr, tc), pidx), pl.BlockSpec((3, tr, tc), lambda i, j, cr: (0, i, j))],
            out_specs=pl.BlockSpec((None, tr, tc), lambda i, j, cr: (cr[0], i, j))),
        out_shape=jax.ShapeDtypeStruct((2, rs, cs), F32),
        compiler_params=_params(("parallel", "parallel")))(core_chip, part, recv)


def _rs_split(g, windows):
    R, C = g.shape
    if windows[0][0] == "row":
        size = windows[0][2]
        t = g.reshape(N_CHIPS, 2, size // 2, C).transpose(1, 0, 2, 3).reshape(2, R // 2, C)
        return t, [(k, b // 2, s // 2) for k, b, s in windows]
    return g.reshape(2, R // 2, C), list(windows)


def _rs_finish(part, windows, got, core_chip, tag):
    both = [_add_scattered(part, r, k, b, s, core_chip, f"rs_add_{tag}_{n}")
            for n, (r, (k, b, s)) in enumerate(zip(got, windows))]
    both = _sibling_share(both, f"rs_share_{tag}")
    return [t.reshape(2 * t.shape[1], t.shape[2]) for t in both]


def _rope_tables(positions, S):
    half = ROPE_DIM // 2
    inv_freq = ROPE_THETA ** (-jnp.arange(0, ROPE_DIM, 2, dtype=F32) / ROPE_DIM)
    ang = positions.reshape(S, 1).astype(F32) * inv_freq[None, :]
    cos, sin = jnp.cos(ang), jnp.sin(ang)
    zeros = jnp.zeros((S, half), F32)
    rest0 = jnp.zeros((S, HEAD_DIM - ROPE_DIM), F32)
    cs = jnp.concatenate([cos, cos, jnp.ones((S, HEAD_DIM - ROPE_DIM), F32)], axis=1)
    sa = jnp.concatenate([-sin, zeros, rest0], axis=1)
    sb = jnp.concatenate([zeros, sin, rest0], axis=1)
    return cs, sa, sb


def kernel(x, c, positions, ada_w, ada_b, norm_mix, norm_ffn, ab_w_in, sgu_w, sgu_b, ab_w_out, conv_w_in, conv_w, conv_w_out, ffn_w_gate, ffn_w_up, ffn_w_down, final_norm, loss_target, m_ada_w, m_ada_b, m_norm_mix, m_norm_ffn, m_ab_w_in, m_sgu_w, m_sgu_b, m_ab_w_out, m_conv_w_in, m_conv_w, m_conv_w_out, m_ffn_w_gate, m_ffn_w_up, m_ffn_w_down, m_final_norm, v_ada_w, v_ada_b, v_norm_mix, v_norm_ffn, v_ab_w_in, v_sgu_w, v_sgu_b, v_ab_w_out, v_conv_w_in, v_conv_w, v_conv_w_out, v_ffn_w_gate, v_ffn_w_up, v_ffn_w_down, v_final_norm):
    S, D = x.shape[1], x.shape[2]
    L = ada_w.shape[0]
    n_mix_heads = D // HEAD_DIM
    n_attn = 3 * n_mix_heads // 4
    A = n_attn * HEAD_DIM
    G = n_mix_heads - n_attn
    F = ffn_w_gate.shape[2] * N_CHIPS
    mix_in = ab_w_in.shape[2] * N_CHIPS
    xi, yi, ci = _coords()
    chip = 2 * xi + yi
    dev = 4 * xi + 2 * yi + ci
    core1 = jnp.reshape(ci, (1,)).astype(jnp.int32)
    chip1 = jnp.reshape(chip, (1,)).astype(jnp.int32)
    x2 = x.reshape(S, D)
    target = loss_target.reshape(S, D)

    n_conv = conv_w.size
    w0 = D + n_conv
    w0p = -(-w0 // 128) * 128
    pack = jnp.zeros((8, w0p), F32).at[0, :D].set(c[0]).at[0, D:w0].set(conv_w.reshape(-1))
    g0 = _all_gather8(pack, "gather_cond")
    c_all = g0[:, 0, :D]
    c_act = c_all * jax.nn.sigmoid(c_all)
    conv_full = jnp.concatenate(
        [g0[2 * j, 0, D:w0].reshape(conv_w.shape) for j in range(N_CHIPS)], axis=2)
    mod_part = _ada_fwd(c_act, ada_w, "ada_fwd")
    n_ada = ada_w.shape[2]
    g1 = _all_gather8(mod_part.reshape(L * N_DEV, n_ada), "gather_mod")
    mod_all = jnp.concatenate([g1[2 * j].reshape(L, N_DEV, n_ada) for j in range(N_CHIPS)], axis=2)
    mod = lax.dynamic_index_in_dim(mod_all, dev, axis=1, keepdims=False) + ada_b
    mods = mod.reshape(L, 6, 1, D)
    cs, sa, sb = _rope_tables(positions, S)

    bf = lambda t: t.astype(BF16)
    w_in_e, w_out_e = bf(ab_w_in), bf(ab_w_out)
    w_in_o, w_out_o = bf(conv_w_in), bf(conv_w_out)
    w_gate, w_up, w_down = bf(ffn_w_gate), bf(ffn_w_up), bf(ffn_w_down)
    def gather_jobs(l):
        i = l // 2
        first, n_in_cols = ((w_in_e, w_out_e), mix_in) if l % 2 == 0 else ((w_in_o, w_out_o), 3 * D)
        return [_gather_job([(first[0], i, "col", 0)], (D, n_in_cols)),
                _gather_job([(first[1], i, "row", 0)], (D, D)),
                _gather_job([(w_gate, l, "col", 0), (w_up, l, "col", F)], (D, 2 * F)),
                _gather_job([(w_down, l, "row", 0)], (F, D))]

    first_jobs = gather_jobs(0)
    layer_w = [[_run_job(first_jobs[0], "gather_w0_in")[0], None, None, None]]

    saved = []
    xc = x2
    pending = None
    for l in range(L):
        i = l // 2
        nxt = gather_jobs(l + 1) if l + 1 < L else [None] * 4
        if l + 1 < L:
            layer_w.append([None] * 4)

        def mm_fwd(a, b, out_dtype, name, slot, also=None):
            jobs = [nxt[slot], first_jobs[also] if also is not None else None]
            out, got = _hosted(lambda job: _mm(a, b, "nn", out_dtype, name, job=job), jobs)
            if nxt[slot] is not None:
                layer_w[l + 1][slot] = got.pop(0)[0]
            if also is not None:
                layer_w[l][also] = got.pop(0)[0]
            return out

        sh_m, sc_m, g_m, sh_f, sc_f, g_f = (mods[l, t] for t in range(6))
        weff_m = norm_mix[l][None, :] * (1.0 + sc_m)
        weff_f = norm_ffn[l][None, :] * (1.0 + sc_f)
        if pending is None:
            _, h = _norm_mod(xc, None, None, weff_m, sh_m, f"norm_mix{l}")
        else:
            xc, h = _norm_mod(xc, pending[0], pending[1], weff_m, sh_m, f"norm_mix{l}")
        z = mm_fwd(h, layer_w[l][0], BF16, f"mm_in{l}", 0, also=1 if l == 0 else None)
        st = dict(x=xc, h=h, z=z, weff_m=weff_m, weff_f=weff_f, g_m=g_m, g_f=g_f, sc_m=sc_m, sc_f=sc_f)
        if l % 2 == 0:
            q, k, v = _qkv_prep(z, cs, sa, sb, n_attn, f"qkv_prep{l}")
            o, lse, got = _attn_fwd(q, k, v, n_attn, f"attn_fwd{l}",
                                    job=_join_jobs(first_jobs[2:]) if l == 0 else None)
            if l == 0:
                layer_w[0][2], layer_w[0][3] = got
            bT = sgu_b[i].T
            so = _sgu_fwd(z, sgu_w[i], bT, n_attn, f"sgu_fwd{l}")
            cat = jnp.concatenate([o, so], axis=1)
            st.update(q=q, k=k, v=v, o=o, lse=lse, bT=bT)
        else:
            w8 = jnp.zeros((8, D), F32).at[:3].set(conv_full[i])
            cat = _conv_fwd(z, w8, f"conv_fwd{l}")
            st.update(w8=w8)
        mix = mm_fwd(cat, layer_w[l][1], F32, f"mm_out{l}", 1)
        x1, h2 = _norm_mod(xc, mix, g_m, weff_f, sh_f, f"norm_ffn{l}")
        ab = mm_fwd(h2, layer_w[l][2], BF16, f"mm_gu{l}", 2)
        f = _swiglu(ab, f"swiglu{l}")
        yv = mm_fwd(f, layer_w[l][3], F32, f"mm_down{l}", 3)
        st.update(cat=cat, mix=mix, x1=x1, h2=h2, ab=ab, f=f, y=yv)
        saved.append(st)
        xc = x1
        pending = (yv, g_f)

    dx, loss11, dfinal = _loss_head(xc, pending[0], pending[1], final_norm[None, :], target, "loss_head")
    loss = lax.psum(loss11[0, 0], ("x", "y", "c"))

    dmods = [None] * L
    dnorm_mix, dnorm_ffn = [None] * L, [None] * L
    big = {l: {} for l in range(L)}
    core_chip = jnp.concatenate([core1, chip1])
    dsgu_w, dsgu_b, dconv = [None] * (L - L // 2), [None] * (L - L // 2), [None] * (L // 2)

    def mm_bwd(a, b, mode, out_dtype, name, scatters=(), swap=None):
        jobs = [_scatter_job([p[0]] * len(p[1]), p[1]) for p in scatters]
        jobs += [_swap_job(swap[0])] if swap else []
        out, got = _hosted(lambda job: _mm(a, b, mode, out_dtype, name, job=job), jobs)
        for (part, windows, lay, keys), recv in zip(scatters, got):
            for key, red in zip(keys, _rs_finish(part, windows, recv, core_chip, f"{lay}_{keys[0]}")):
                big[lay][key] = red
        half = _add_half(swap[0], got[-1][0], core_chip, f"rs_add_half_{swap[1]}") if swap else None
        return out, half

    above = None
    for l in reversed(range(L)):
        i = l // 2
        st = saved[l]
        w_in, w_out, w_gu, w_dn = layer_w[l]
        dy, dg_f = _gate_bwd(dx, st["y"], st["g_f"], f"gate_f_bwd{l}")
        if above is None:
            df, _ = mm_bwd(dy, w_dn, "nt", BF16, f"mm_down_dx{l}")
            dw_dn, _ = mm_bwd(st["f"], dy, "tn", F32, f"mm_down_dw{l}")
        else:
            s_out, t_in, win_in = above
            df, p_in = mm_bwd(dy, w_dn, "nt", BF16, f"mm_down_dx{l}", [s_out], (t_in, f"{l + 1}_in"))
            dw_dn, _ = mm_bwd(st["f"], dy, "tn", F32, f"mm_down_dw{l}", [(p_in, win_in, l + 1, ["in"])])
        t_dn, win_dn = _rs_split(dw_dn, [("row", 0, F // N_CHIPS)])
        dab = _swiglu_bwd(st["ab"], df, f"swiglu_bwd{l}")
        dh2, p_dn = mm_bwd(dab, w_gu, "nt", F32, f"mm_gu_dx{l}", swap=(t_dn, f"{l}_down"))
        dw_gu, _ = mm_bwd(st["h2"], dab, "tn", F32, f"mm_gu_dw{l}", [(p_dn, win_dn, l, ["down"])])
        t_gu, win_gu = _rs_split(dw_gu, [("col", 0, F // N_CHIPS), ("col", F, F // N_CHIPS)])
        dx1, dsh_f, dweff_f = _norm_mod_bwd(dh2, st["x1"], st["weff_f"], dx, f"norm_ffn_bwd{l}")
        dmix, dg_m = _gate_bwd(dx1, st["mix"], st["g_m"], f"gate_m_bwd{l}")
        dcat, p_gu = mm_bwd(dmix, w_out, "nt", F32, f"mm_out_dx{l}", swap=(t_gu, f"{l}_gu"))
        dw_out = _mm(st["cat"], dmix, "tn", F32, f"mm_out_dw{l}")
        t_out, win_out = _rs_split(dw_out, [("row", 0, D // N_CHIPS)])
        if l % 2 == 0:
            dq, dk, dv = _attn_bwd(st["q"], st["k"], st["v"], st["o"], st["lse"], dcat, n_attn, f"attn_bwd{l}")
            dqkv = _dqkv_post(dq, dk, dv, cs, sa, sb, n_attn, f"dqkv_post{l}")
            duv, dsgu_w[i], dbT = _sgu_bwd(st["z"], dcat, sgu_w[i], st["bT"], n_attn, f"sgu_bwd{l}")
            dsgu_b[i] = dbT.T
            dz = jnp.concatenate([dqkv, duv], axis=1)
        else:
            dz, dw8 = _conv_bwd(st["z"], dcat, st["w8"], f"conv_bwd{l}")
            dconv[i] = dw8[:3]
        dh, p_out = mm_bwd(dz, w_in, "nt", F32, f"mm_in_dx{l}", [(p_gu, win_gu[:1], l, ["gate"])],
                           (t_out, f"{l}_out"))
        s_out = (p_out, win_out, l, ["out"])
        dw_in, _ = mm_bwd(st["h"], dz, "tn", F32, f"mm_in_dw{l}",
                          [(p_gu, win_gu[1:], l, ["up"])] + ([s_out] if l == 0 else []))
        t_in, win_in = _rs_split(dw_in, [("col", 0, w_in.shape[1] // N_CHIPS)])
        above = (s_out, t_in, win_in)
        dx, dsh_m, dweff_m = _norm_mod_bwd(dh, st["x"], st["weff_m"], dx1, f"norm_mix_bwd{l}")
        dmods[l] = jnp.concatenate(
            [dsh_m, dweff_m * norm_mix[l][None, :], dg_m, dsh_f, dweff_f * norm_ffn[l][None, :], dg_f], axis=1)
        dnorm_mix[l] = dweff_m * (1.0 + st["sc_m"])
        dnorm_ffn[l] = dweff_f * (1.0 + st["sc_f"])
    _, t_in, win_in = above
    (recv_in,) = _run_job(_swap_job(t_in), "rs_swap_0_in")
    p_in0 = _add_half(t_in, recv_in, core_chip, "rs_add_half_0_in")
    grad_x = dx.reshape(1, S, D)

    dmod = jnp.concatenate(dmods, axis=0)
    small = [dmod.reshape(-1), jnp.concatenate(dnorm_mix, 0).reshape(-1), jnp.concatenate(dnorm_ffn, 0).reshape(-1),
             jnp.stack(dsgu_w).reshape(-1), jnp.stack(dsgu_b).reshape(-1), jnp.stack(dconv).reshape(-1), dfinal.reshape(-1)]
    sizes = [t.size for t in small]
    flat = jnp.concatenate(small)
    n_flat = flat.size
    rows = -(-n_flat // (128 * 8)) * 8
    flat = jnp.concatenate([flat, jnp.zeros((rows * 128 - n_flat,), F32)]).reshape(rows, 128)
    g2 = _all_gather8(flat, "gather_small")
    tot = _sum_leading(g2, "sum_small").reshape(-1)
    offs = [0]
    for s in sizes:
        offs.append(offs[-1] + s)
    take = lambda n, shape: tot[offs[n]:offs[n + 1]].reshape(shape)
    g_ada_b = take(0, ada_b.shape)
    g_norm_mix = take(1, norm_mix.shape)
    g_norm_ffn = take(2, norm_ffn.shape)
    g_sgu_w = take(3, sgu_w.shape)
    g_sgu_b = take(4, sgu_b.shape)
    g_conv_full = take(5, conv_full.shape)
    n_cw = conv_w.shape[2]
    g_conv_w = lax.dynamic_slice_in_dim(g_conv_full, chip * n_cw, n_cw, axis=2)
    g_final = take(6, final_norm.shape)
    dmod_all = g2[:, :, :].reshape(N_DEV, -1)[:, :offs[1]].reshape(N_DEV, L, 6 * D)
    dmod_mine = lax.dynamic_slice_in_dim(dmod_all, chip * n_ada, n_ada, axis=2).transpose(1, 0, 2)
    g_ada_w = _ada_bwd(c_act, dmod_mine, "ada_bwd")
    *ada_update, got = _adamw(ada_w, g_ada_w, m_ada_w, v_ada_w, "adamw_ada_w", job=_scatter_job([p_in0], win_in))
    (big[0]["in"],) = _rs_finish(p_in0, win_in, got, core_chip, "0_in")

    def stack(key, layers):
        return jnp.stack([big[l][key] for l in layers])

    even, odd, every = list(range(0, L, 2)), list(range(1, L, 2)), list(range(L))
    grads = dict(
        ada_w=g_ada_w, ada_b=g_ada_b, norm_mix=g_norm_mix, norm_ffn=g_norm_ffn,
        ab_w_in=stack("in", even), sgu_w=g_sgu_w, sgu_b=g_sgu_b, ab_w_out=stack("out", even),
        conv_w_in=stack("in", odd), conv_w=g_conv_w, conv_w_out=stack("out", odd),
        ffn_w_gate=stack("gate", every), ffn_w_up=stack("up", every), ffn_w_down=stack("down", every),
        final_norm=g_final)
    weights = dict(ada_w=ada_w, ada_b=ada_b, norm_mix=norm_mix, norm_ffn=norm_ffn, ab_w_in=ab_w_in, sgu_w=sgu_w,
                   sgu_b=sgu_b, ab_w_out=ab_w_out, conv_w_in=conv_w_in, conv_w=conv_w, conv_w_out=conv_w_out,
                   ffn_w_gate=ffn_w_gate, ffn_w_up=ffn_w_up, ffn_w_down=ffn_w_down, final_norm=final_norm)
    ms = dict(ada_w=m_ada_w, ada_b=m_ada_b, norm_mix=m_norm_mix, norm_ffn=m_norm_ffn, ab_w_in=m_ab_w_in, sgu_w=m_sgu_w,
              sgu_b=m_sgu_b, ab_w_out=m_ab_w_out, conv_w_in=m_conv_w_in, conv_w=m_conv_w, conv_w_out=m_conv_w_out,
              ffn_w_gate=m_ffn_w_gate, ffn_w_up=m_ffn_w_up, ffn_w_down=m_ffn_w_down, final_norm=m_final_norm)
    vs = dict(ada_w=v_ada_w, ada_b=v_ada_b, norm_mix=v_norm_mix, norm_ffn=v_norm_ffn, ab_w_in=v_ab_w_in, sgu_w=v_sgu_w,
              sgu_b=v_sgu_b, ab_w_out=v_ab_w_out, conv_w_in=v_conv_w_in, conv_w=v_conv_w, conv_w_out=v_conv_w_out,
              ffn_w_gate=v_ffn_w_gate, ffn_w_up=v_ffn_w_up, ffn_w_down=v_ffn_w_down, final_norm=v_final_norm)
    names = list(weights)
    deltas, new_m, new_v = {}, {}, {}
    for n in names:
        w, g = weights[n], grads[n]
        if n == "ada_w":
            deltas[n], new_m[n], new_v[n] = ada_update
        elif w.ndim == 1:
            d_, m_, v_ = _adamw(w[None, :], g[None, :], ms[n][None, :], vs[n][None, :], f"adamw_{n}")
            deltas[n], new_m[n], new_v[n] = d_[0], m_[0], v_[0]
        else:
            deltas[n], new_m[n], new_v[n] = _adamw(w, g, ms[n], vs[n], f"adamw_{n}")
    return (loss, grad_x, *[grads[n] for n in names], *[deltas[n] for n in names],
            *[new_m[n] for n in names], *[new_v[n] for n in names])
```

```python
import functools
import math

import jax
import jax.numpy as jnp
from jax import lax
from jax.experimental import pallas as pl
from jax.experimental.pallas import tpu as pltpu

F32 = jnp.float32
BF16 = jnp.bfloat16
HEAD_DIM = 128
CHUNK = 128
ATTN_BLOCK = 128
ATTN_LANES_FWD = 4
ATTN_LANES_BWD = 4
DILATIONS = (1, 4, 16)
ROPE_DIM = HEAD_DIM // 4
ROPE_THETA = 500000.0
EPS = 1e-6
MASKED = -1e30
ADAM_LR, ADAM_B1, ADAM_B2, ADAM_EPS, ADAM_WD, ADAM_STEP = 0.001, 0.9, 0.999, 1e-08, 0.01, 10
VMEM_LIMIT_BYTES = 56 * 1024 * 1024
MESH = pl.DeviceIdType.MESH
ANY = pl.BlockSpec(memory_space=pl.ANY)
N_CHIPS = 4
N_DEV = 8


def _pick(n, cands):
    for t in cands:
        if n % t == 0:
            return t
    return n


def _params(sem):
    return pltpu.CompilerParams(dimension_semantics=sem, vmem_limit_bytes=VMEM_LIMIT_BYTES)


MM_VMEM_BUDGET = 44 * 1024 * 1024
_TILES = (2048, 1536, 1408, 1024, 512, 256, 128)


def _mm_tiles(M, N, K, out_bytes):
    best = None
    for tm in [t for t in _TILES if M % t == 0] or [M]:
        for tn in [t for t in _TILES if N % t == 0] or [N]:
            for tk in [t for t in _TILES if K % t == 0] or [K]:
                nk = K // tk
                vmem = 2 * 2 * (tm * tk + tk * tn) + 2 * out_bytes * tm * tn + 4 * tm * tn
                vmem += 4 * tm * tn if nk > 1 and out_bytes == 2 else 0
                if vmem > MM_VMEM_BUDGET:
                    continue
                key = ((M // tm) * (N // tn) * nk, -tk, -tm)
                if best is None or key < best[0]:
                    best = (key, (tm, tn, tk))
    assert best is not None, (M, N, K)
    return best[1]


def _mm(a, b, mode, out_dtype, name, layer=None, job=None):
    bshape = b.shape[1:] if layer is not None else b.shape
    if mode == "nn":
        (M, K), (K2, N) = a.shape, bshape
    elif mode == "nt":
        (M, K), (N, K2) = a.shape, bshape
    else:
        (K, M), (K2, N) = a.shape, bshape
    assert K == K2, (a.shape, b.shape, mode)
    in_place = out_dtype == F32
    tm, tn, tk = _mm_tiles(M, N, K, 4 if in_place else 2)
    nk = K // tk
    dims = {"nn": (((1,), (0,)), ((), ())), "nt": (((1,), (1,)), ((), ())), "tn": (((0,), (0,)), ((), ()))}[mode]

    n_ji = len(job["ins"]) if job else 0
    n_jo = len(job["outs"]) if job else 0
    n_acc = 1 if nk > 1 and not in_place else 0
    grid = (M // tm, N // tn, nk)

    def body(a_ref, b_ref, *rest):
        jin, o_ref, jout = rest[:n_ji], rest[n_ji], rest[n_ji + 1:n_ji + 1 + n_jo]
        scratch = rest[n_ji + 1 + n_jo:]
        acc, sems = scratch[:n_acc], scratch[n_acc:]
        i, j, k = pl.program_id(0), pl.program_id(1), pl.program_id(2)
        if job:
            @pl.when((i == 0) & (j == 0) & (k == 0))
            def _():
                job["start"](jin, jout, sems)

        def product():
            return lax.dot_general(a_ref[...].astype(BF16), b_ref[...].astype(BF16), dims, preferred_element_type=F32)

        if nk == 1:
            o_ref[...] = product().astype(o_ref.dtype)
        else:
            acc_ref = o_ref if in_place else acc[0]

            @pl.when(k == 0)
            def _():
                acc_ref[...] = jnp.zeros_like(acc_ref)

            acc_ref[...] += product()

            if not in_place:
                @pl.when(k == nk - 1)
                def _():
                    o_ref[...] = acc_ref[...].astype(o_ref.dtype)

        if job:
            @pl.when((i == grid[0] - 1) & (j == grid[1] - 1) & (k == grid[2] - 1))
            def _():
                job["finish"](jin, jout, sems)

    if mode == "tn":
        a_spec = pl.BlockSpec((tk, tm), lambda i, j, k: (k, i))
    else:
        a_spec = pl.BlockSpec((tm, tk), lambda i, j, k: (i, k))
    if mode == "nt":
        bblk, bidx = (tn, tk), (lambda i, j, k: (j, k))
    else:
        bblk, bidx = (tk, tn), (lambda i, j, k: (k, j))
    if layer is not None:
        b_spec = pl.BlockSpec((None,) + bblk, lambda i, j, k: (layer,) + bidx(i, j, k))
    else:
        b_spec = pl.BlockSpec(bblk, bidx)
    out_spec = pl.BlockSpec((tm, tn), lambda i, j, k: (i, j))
    out_shape = jax.ShapeDtypeStruct((M, N), out_dtype)
    acc_scratch = [pltpu.VMEM((tm, tn), F32)] * n_acc
    if not job:
        return pl.pallas_call(
            body, name=name, grid=grid, in_specs=[a_spec, b_spec], out_specs=out_spec, out_shape=out_shape,
            scratch_shapes=acc_scratch, compiler_params=_params(("parallel", "parallel", "arbitrary")),
        )(a, b)
    return pl.pallas_call(
        body, name=name, grid=grid, in_specs=[a_spec, b_spec] + [ANY] * n_ji,
        out_specs=(out_spec,) + (ANY,) * n_jo, out_shape=(out_shape,) + tuple(job["outs"]),
        scratch_shapes=acc_scratch + list(job["sems"]),
        compiler_params=_params(("arbitrary", "arbitrary", "arbitrary")),
    )(a, b, *job["ins"])


def _rows(S):
    return _pick(S, (256, 128, 64, 32, 16, 8))


def _row_spec(tr, width, col=0):
    return pl.BlockSpec((tr, width), lambda i: (i, col))


def _vec_spec(rows, width):
    return pl.BlockSpec((rows, width), lambda i: (0, 0))


def _rms(xv):
    return lax.rsqrt(jnp.mean(xv * xv, axis=-1, keepdims=True) + EPS)


def _norm_mod(x, y, g, w_eff, sh, name):
    S, D = x.shape
    tr = _rows(S)
    fused = y is not None

    def body(*refs):
        if fused:
            x_ref, y_ref, g_ref, w_ref, s_ref, x1_ref, h_ref = refs
            xv = x_ref[...] + g_ref[...] * y_ref[...]
            x1_ref[...] = xv
        else:
            x_ref, w_ref, s_ref, h_ref = refs
            xv = x_ref[...]
        h_ref[...] = (xv * _rms(xv) * w_ref[...] + s_ref[...]).astype(BF16)

    big, vec = _row_spec(tr, D), _vec_spec(1, D)
    if fused:
        ins, in_specs = (x, y, g, w_eff, sh), [big, big, vec, vec, vec]
        out_shape = (jax.ShapeDtypeStruct((S, D), F32), jax.ShapeDtypeStruct((S, D), BF16))
        out_specs = (big, big)
    else:
        ins, in_specs = (x, w_eff, sh), [big, vec, vec]
        out_shape = jax.ShapeDtypeStruct((S, D), BF16)
        out_specs = big
    out = pl.pallas_call(body, name=name, grid=(S // tr,), in_specs=in_specs, out_specs=out_specs,
                         out_shape=out_shape, compiler_params=_params(("parallel",)))(*ins)
    return out if fused else (None, out)


def _norm_mod_bwd(dh, x, w_eff, dres, name):
    S, D = x.shape
    tr = _rows(S)

    def body(dh_ref, x_ref, w_ref, r_ref, dx_ref, dsh_ref, dw_ref):
        xv = x_ref[...]
        dhv = dh_ref[...].astype(F32)
        r = _rms(xv)
        xn = xv * r
        dxn = dhv * w_ref[...]
        dx_ref[...] = r_ref[...] + r * (dxn - xn * jnp.mean(dxn * xn, axis=-1, keepdims=True))

        @pl.when(pl.program_id(0) == 0)
        def _():
            dsh_ref[...] = jnp.zeros_like(dsh_ref)
            dw_ref[...] = jnp.zeros_like(dw_ref)

        dsh_ref[...] += jnp.sum(dhv, axis=0, keepdims=True)
        dw_ref[...] += jnp.sum(dhv * xn, axis=0, keepdims=True)

    big, vec = _row_spec(tr, D), _vec_spec(1, D)
    return pl.pallas_call(
        body, name=name, grid=(S // tr,), in_specs=[big, big, vec, big], out_specs=(big, vec, vec),
        out_shape=(jax.ShapeDtypeStruct((S, D), F32), jax.ShapeDtypeStruct((1, D), F32), jax.ShapeDtypeStruct((1, D), F32)),
        compiler_params=_params(("arbitrary",)))(dh, x, w_eff, dres)


def _gate_bwd(dx, y, g, name):
    S, D = dx.shape
    tr = _rows(S)

    def body(dx_ref, y_ref, g_ref, dy_ref, dg_ref):
        dxv = dx_ref[...]
        dy_ref[...] = (dxv * g_ref[...]).astype(BF16)

        @pl.when(pl.program_id(0) == 0)
        def _():
            dg_ref[...] = jnp.zeros_like(dg_ref)

        dg_ref[...] += jnp.sum(dxv * y_ref[...], axis=0, keepdims=True)

    big, vec = _row_spec(tr, D), _vec_spec(1, D)
    return pl.pallas_call(
        body, name=name, grid=(S // tr,), in_specs=[big, big, vec], out_specs=(big, vec),
        out_shape=(jax.ShapeDtypeStruct((S, D), BF16), jax.ShapeDtypeStruct((1, D), F32)),
        compiler_params=_params(("arbitrary",)))(dx, y, g)


def _loss_head(x, y, g, gamma, target, name):
    S, D = x.shape
    tr = _rows(S)

    def body(x_ref, y_ref, g_ref, gm_ref, t_ref, dx_ref, loss_ref, dgm_ref):
        xv = x_ref[...] + g_ref[...] * y_ref[...]
        r = _rms(xv)
        xn = xv * r
        err = xn * gm_ref[...] - t_ref[...]
        dout = err * (1.0 / D)
        dxn = dout * gm_ref[...]
        dx_ref[...] = r * (dxn - xn * jnp.mean(dxn * xn, axis=-1, keepdims=True))

        @pl.when(pl.program_id(0) == 0)
        def _():
            loss_ref[...] = jnp.zeros_like(loss_ref)
            dgm_ref[...] = jnp.zeros_like(dgm_ref)

        loss_ref[...] += 0.5 * jnp.sum(jnp.mean(err * err, axis=-1, keepdims=True), axis=0, keepdims=True)
        dgm_ref[...] += jnp.sum(dout * xn, axis=0, keepdims=True)

    big, vec = _row_spec(tr, D), _vec_spec(1, D)
    return pl.pallas_call(
        body, name=name, grid=(S // tr,), in_specs=[big, big, vec, vec, big],
        out_specs=(big, _vec_spec(1, 1), vec),
        out_shape=(jax.ShapeDtypeStruct((S, D), F32), jax.ShapeDtypeStruct((1, 1), F32), jax.ShapeDtypeStruct((1, D), F32)),
        compiler_params=_params(("arbitrary",)))(x, y, g, gamma, target)


def _silu(a):
    return a * jax.nn.sigmoid(a)


def _swiglu(ab, name):
    S, F2 = ab.shape
    F = F2 // 2
    tr = _pick(S, (128, 64, 32, 16, 8))

    def body(a_ref, b_ref, f_ref):
        f_ref[...] = (_silu(a_ref[...].astype(F32)) * b_ref[...].astype(F32)).astype(BF16)

    return pl.pallas_call(
        body, name=name, grid=(S // tr,), in_specs=[_row_spec(tr, F, 0), _row_spec(tr, F, 1)],
        out_specs=_row_spec(tr, F), out_shape=jax.ShapeDtypeStruct((S, F), BF16),
        compiler_params=_params(("parallel",)))(ab, ab)


def _swiglu_bwd(ab, df, name):
    S, F2 = ab.shape
    F = F2 // 2
    tr = _pick(S, (128, 64, 32, 16, 8))

    def body(a_ref, b_ref, df_ref, da_ref, db_ref):
        a = a_ref[...].astype(F32)
        sg = jax.nn.sigmoid(a)
        dfv = df_ref[...].astype(F32)
        da_ref[...] = (dfv * b_ref[...].astype(F32) * (sg * (1.0 + a * (1.0 - sg)))).astype(BF16)
        db_ref[...] = (dfv * a * sg).astype(BF16)

    def body2(a_ref, b_ref, df_ref, o_ref):
        body(a_ref, b_ref, df_ref, o_ref.at[:, pl.ds(0, F)], o_ref.at[:, pl.ds(F, F)])

    return pl.pallas_call(
        body2, name=name, grid=(S // tr,),
        in_specs=[_row_spec(tr, F, 0), _row_spec(tr, F, 1), _row_spec(tr, F)],
        out_specs=_row_spec(tr, F2), out_shape=jax.ShapeDtypeStruct((S, F2), BF16),
        compiler_params=_params(("parallel",)))(ab, ab, df)


def _shift_rows(v, n):
    return pltpu.roll(v, n, 0)


def _conv_fwd(p, w8, name):
    S, D3 = p.shape
    D = D3 // 3
    tr = _rows(S)
    nb8 = tr // 8

    def body(gb_ref, gc_ref, hx_ref, gcp_ref, hxp_ref, w_ref, o_ref):
        i = pl.program_id(0)
        y = gc_ref[...].astype(F32) * hx_ref[...].astype(F32)
        yp = jnp.where(i > 0, gcp_ref[...].astype(F32) * hxp_ref[...].astype(F32), 0.0)
        w0, w1, w2 = w_ref[0:1, :], w_ref[1:2, :], w_ref[2:3, :]
        conv = w0 * _shift_rows(y, 2) + w1 * _shift_rows(y, 1) + w2 * y
        o_ref[...] = (gb_ref[...].astype(F32) * conv).astype(BF16)
        rid = lax.broadcasted_iota(jnp.int32, (8, D), 0)
        y8 = y[0:8, :]
        y1 = jnp.where(rid < 1, _shift_rows(yp, 1), _shift_rows(y8, 1))
        y2 = jnp.where(rid < 2, _shift_rows(yp, 2), _shift_rows(y8, 2))
        conv8 = w0 * y2 + w1 * y1 + w2 * y8
        o_ref[0:8, :] = (gb_ref[0:8, :].astype(F32) * conv8).astype(BF16)

    def col(c):
        return pl.BlockSpec((tr, D), lambda i: (i, c))

    def prev8(c):
        return pl.BlockSpec((8, D), lambda i: (jnp.maximum(i * nb8 - 1, 0), c))

    return pl.pallas_call(
        body, name=name, grid=(S // tr,),
        in_specs=[col(0), col(1), col(2), prev8(1), prev8(2), _vec_spec(8, D)],
        out_specs=_row_spec(tr, D), out_shape=jax.ShapeDtypeStruct((S, D), BF16),
        compiler_params=_params(("parallel",)))(p, p, p, p, p, w8)


def _conv_bwd(p, do, w8, name):
    S, D3 = p.shape
    D = D3 // 3
    tr = _rows(S)
    nb8 = tr // 8
    nt = S // tr

    def body(gb_ref, gc_ref, hx_ref, gcp_ref, hxp_ref, do_ref, gbn_ref, don_ref, w_ref, dp_ref, dw_ref):
        i = pl.program_id(0)
        gb = gb_ref[...].astype(F32)
        gc = gc_ref[...].astype(F32)
        hx = hx_ref[...].astype(F32)
        dov = do_ref[...].astype(F32)
        y = gc * hx
        yp = jnp.where(i > 0, gcp_ref[...].astype(F32) * hxp_ref[...].astype(F32), 0.0)
        dconv = dov * gb
        dcn = jnp.where(i < nt - 1, don_ref[...].astype(F32) * gbn_ref[...].astype(F32), 0.0)
        w0, w1, w2 = w_ref[0:1, :], w_ref[1:2, :], w_ref[2:3, :]
        rid = lax.broadcasted_iota(jnp.int32, (tr, D), 0)
        rid8 = lax.broadcasted_iota(jnp.int32, (8, D), 0)
        yp1 = jnp.concatenate([_shift_rows(yp, 1), jnp.zeros((tr - 8, D), F32)], axis=0)
        yp2 = jnp.concatenate([_shift_rows(yp, 2), jnp.zeros((tr - 8, D), F32)], axis=0)
        y1 = jnp.where(rid < 1, yp1, _shift_rows(y, 1))
        y2 = jnp.where(rid < 2, yp2, _shift_rows(y, 2))
        conv = w0 * y2 + w1 * y1 + w2 * y
        dn1 = jnp.concatenate([jnp.zeros((tr - 8, D), F32), _shift_rows(dcn, 7)], axis=0)
        dn2 = jnp.concatenate([jnp.zeros((tr - 8, D), F32), _shift_rows(dcn, 6)], axis=0)
        d1 = jnp.where(rid >= tr - 1, dn1, _shift_rows(dconv, tr - 1))
        d2 = jnp.where(rid >= tr - 2, dn2, _shift_rows(dconv, tr - 2))
        dy = w2 * dconv + w1 * d1 + w0 * d2
        dp_ref[:, pl.ds(0, D)] = (dov * conv).astype(BF16)
        dp_ref[:, pl.ds(D, D)] = (dy * hx).astype(BF16)
        dp_ref[:, pl.ds(2 * D, D)] = (dy * gc).astype(BF16)

        @pl.when(i == 0)
        def _():
            dw_ref[...] = jnp.zeros_like(dw_ref)

        upd = jnp.where(rid8 == 0, jnp.sum(dconv * y2, axis=0, keepdims=True),
                        jnp.where(rid8 == 1, jnp.sum(dconv * y1, axis=0, keepdims=True),
                                  jnp.where(rid8 == 2, jnp.sum(dconv * y, axis=0, keepdims=True), 0.0)))
        dw_ref[...] += upd

    def col(c):
        return pl.BlockSpec((tr, D), lambda i: (i, c))

    def prev8(c):
        return pl.BlockSpec((8, D), lambda i: (jnp.maximum(i * nb8 - 1, 0), c))

    def next8(c):
        return pl.BlockSpec((8, D), lambda i: (jnp.minimum((i + 1) * nb8, S // 8 - 1), c))

    return pl.pallas_call(
        body, name=name, grid=(nt,),
        in_specs=[col(0), col(1), col(2), prev8(1), prev8(2), col(0), next8(0), next8(0), _vec_spec(8, D)],
        out_specs=(_row_spec(tr, D3), _vec_spec(8, D)),
        out_shape=(jax.ShapeDtypeStruct((S, D3), BF16), jax.ShapeDtypeStruct((8, D), F32)),
        compiler_params=_params(("arbitrary",)))(p, p, p, p, p, do, p, do, w8)


_GELU_C = math.sqrt(2.0 / math.pi)


def _gelu(v):
    return 0.5 * v * (1.0 + jnp.tanh(_GELU_C * (v + 0.044715 * v * v * v)))


def _gelu_grad(v):
    t = jnp.tanh(_GELU_C * (v + 0.044715 * v * v * v))
    return 0.5 * (1.0 + t) + 0.5 * v * (1.0 - t * t) * _GELU_C * (1.0 + 3.0 * 0.044715 * v * v)


def _tril(w):
    r = lax.broadcasted_iota(jnp.int32, (CHUNK, CHUNK), 0)
    c = lax.broadcasted_iota(jnp.int32, (CHUNK, CHUNK), 1)
    return jnp.where(r >= c, w, 0.0)


def _sgu_fwd(z, w, bT, n_attn, name):
    S = z.shape[0]
    G = w.shape[0]
    W = G * CHUNK
    tr = _pick(S, (512, 256, 128))
    ucol = 3 * n_attn * HEAD_DIM // W

    def body(u_ref, v_ref, w_ref, b_ref, o_ref):
        for g in range(G):
            wt = _tril(w_ref[g]).astype(BF16)
            for ci in range(tr // CHUNK):
                rows, cols = pl.ds(ci * CHUNK, CHUNK), pl.ds(g * CHUNK, CHUNK)
                gv = _gelu(v_ref[rows, cols].astype(F32)).astype(BF16)
                mixed = jnp.dot(wt, gv, preferred_element_type=F32) + b_ref[:, g:g + 1]
                o_ref[rows, cols] = (_gelu(u_ref[rows, cols].astype(F32)) * mixed).astype(BF16)

    return pl.pallas_call(
        body, name=name, grid=(S // tr,),
        in_specs=[_row_spec(tr, W, ucol), _row_spec(tr, W, ucol + 1),
                  pl.BlockSpec((G, CHUNK, CHUNK), lambda i: (0, 0, 0)), _vec_spec(CHUNK, G)],
        out_specs=_row_spec(tr, W), out_shape=jax.ShapeDtypeStruct((S, W), BF16),
        compiler_params=_params(("parallel",)))(z, z, w, bT)


def _sgu_bwd(z, dcat, w, bT, n_attn, name):
    S = z.shape[0]
    G = w.shape[0]
    W = G * CHUNK
    tr = _pick(S, (512, 256, 128))
    ucol = 3 * n_attn * HEAD_DIM // W
    dcol = n_attn * HEAD_DIM // W

    def body(u_ref, v_ref, d_ref, w_ref, b_ref, o_ref, dw_ref, db_ref):
        @pl.when(pl.program_id(0) == 0)
        def _():
            dw_ref[...] = jnp.zeros_like(dw_ref)
            db_ref[...] = jnp.zeros_like(db_ref)

        lane = lax.broadcasted_iota(jnp.int32, (CHUNK, G), 1)
        for g in range(G):
            wtf = _tril(w_ref[g])
            wt = wtf.astype(BF16)
            dw_acc = jnp.zeros((CHUNK, CHUNK), F32)
            db_acc = jnp.zeros((CHUNK, 1), F32)
            for ci in range(tr // CHUNK):
                rows, cols = pl.ds(ci * CHUNK, CHUNK), pl.ds(g * CHUNK, CHUNK)
                uv = u_ref[rows, cols].astype(F32)
                vv = v_ref[rows, cols].astype(F32)
                dov = d_ref[rows, cols]
                gv = _gelu(vv).astype(BF16)
                mixed = jnp.dot(wt, gv, preferred_element_type=F32) + b_ref[:, g:g + 1]
                dmixed = dov * _gelu(uv)
                dmb = dmixed.astype(BF16)
                dgv = lax.dot_general(wt, dmb, (((0,), (0,)), ((), ())), preferred_element_type=F32)
                o_ref[rows, cols] = (dov * mixed * _gelu_grad(uv)).astype(BF16)
                o_ref[rows, pl.ds(W + g * CHUNK, CHUNK)] = (dgv * _gelu_grad(vv)).astype(BF16)
                dw_acc += lax.dot_general(dmb, gv, (((1,), (1,)), ((), ())), preferred_element_type=F32)
                db_acc += jnp.sum(dmixed, axis=1, keepdims=True)
            dw_ref[g] += _tril(dw_acc)
            db_ref[...] += jnp.where(lane == g, db_acc, 0.0)

    return pl.pallas_call(
        body, name=name, grid=(S // tr,),
        in_specs=[_row_spec(tr, W, ucol), _row_spec(tr, W, ucol + 1), _row_spec(tr, W, dcol),
                  pl.BlockSpec((G, CHUNK, CHUNK), lambda i: (0, 0, 0)), _vec_spec(CHUNK, G)],
        out_specs=(_row_spec(tr, 2 * W), pl.BlockSpec((G, CHUNK, CHUNK), lambda i: (0, 0, 0)), _vec_spec(CHUNK, G)),
        out_shape=(jax.ShapeDtypeStruct((S, 2 * W), BF16), jax.ShapeDtypeStruct((G, CHUNK, CHUNK), F32),
                   jax.ShapeDtypeStruct((CHUNK, G), F32)),
        compiler_params=_params(("arbitrary",)))(z, z, dcat, w, bT)


def _rope(v, cs, sa, sb):
    return v * cs + pltpu.roll(v, HEAD_DIM - ROPE_DIM // 2, 1) * sa + pltpu.roll(v, ROPE_DIM // 2, 1) * sb


def _rope_t(d, cs, sa, sb):
    return d * cs + pltpu.roll(d * sa, ROPE_DIM // 2, 1) + pltpu.roll(d * sb, HEAD_DIM - ROPE_DIM // 2, 1)


def _qkv_prep(z, cs, sa, sb, n_attn, name):
    S = z.shape[0]
    A = n_attn * HEAD_DIM
    tr = _rows(S)

    def body(q_ref, k_ref, v_ref, c_ref, a_ref, b_ref, qo_ref, ko_ref, vo_ref):
        cv, av, bv = c_ref[...], a_ref[...], b_ref[...]
        for h in range(n_attn):
            cols = pl.ds(h * HEAD_DIM, HEAD_DIM)
            qo_ref[:, cols] = _rope(q_ref[:, cols].astype(F32), cv, av, bv)
            ko_ref[:, cols] = _rope(k_ref[:, cols].astype(F32), cv, av, bv)
        vo_ref[...] = v_ref[...].astype(F32)

    tab = _row_spec(tr, HEAD_DIM)
    out = jax.ShapeDtypeStruct((S, A), F32)
    return pl.pallas_call(
        body, name=name, grid=(S // tr,),
        in_specs=[_row_spec(tr, A, 0), _row_spec(tr, A, 1), _row_spec(tr, A, 2), tab, tab, tab],
        out_specs=(_row_spec(tr, A),) * 3, out_shape=(out,) * 3,
        compiler_params=_params(("parallel",)))(z, z, z, cs, sa, sb)


def _dqkv_post(dq, dk, dv, cs, sa, sb, n_attn, name):
    S, A = dq.shape
    tr = _rows(S)

    def body(q_ref, k_ref, v_ref, c_ref, a_ref, b_ref, o_ref):
        cv, av, bv = c_ref[...], a_ref[...], b_ref[...]
        for h in range(n_attn):
            cols = pl.ds(h * HEAD_DIM, HEAD_DIM)
            o_ref[:, pl.ds(h * HEAD_DIM, HEAD_DIM)] = _rope_t(q_ref[:, cols], cv, av, bv).astype(BF16)
            o_ref[:, pl.ds(A + h * HEAD_DIM, HEAD_DIM)] = _rope_t(k_ref[:, cols], cv, av, bv).astype(BF16)
        o_ref[:, pl.ds(2 * A, A)] = v_ref[...].astype(BF16)

    tab = _row_spec(tr, HEAD_DIM)
    return pl.pallas_call(
        body, name=name, grid=(S // tr,),
        in_specs=[_row_spec(tr, A)] * 3 + [tab, tab, tab],
        out_specs=_row_spec(tr, 3 * A), out_shape=jax.ShapeDtypeStruct((S, 3 * A), BF16),
        compiler_params=_params(("parallel",)))(dq, dk, dv, cs, sa, sb)


def _block_rows(d, S, it):
    nblk = S // (d * ATTN_BLOCK)
    r = it // nblk
    jb = it % nblk
    q0 = r + d * ATTN_BLOCK * jb
    k0 = r + d * ATTN_BLOCK * jnp.maximum(jb - 1, 0)
    off = jnp.where(jb > 0, ATTN_BLOCK, 0)
    return q0, k0, off


def _band(off):
    a = lax.broadcasted_iota(jnp.int32, (ATTN_BLOCK, 2 * ATTN_BLOCK), 0) + off
    kj = lax.broadcasted_iota(jnp.int32, (ATTN_BLOCK, 2 * ATTN_BLOCK), 1)
    return (kj <= a) & (kj >= a - ATTN_BLOCK)


_NT = (((1,), (1,)), ((), ()))
_TN = (((0,), (0,)), ((), ()))


def _attn_fwd(q, k, v, n_attn, name, job=None):
    S, A = q.shape
    scale = HEAD_DIM ** -0.5
    n_ji = len(job["ins"]) if job else 0
    n_jo = len(job["outs"]) if job else 0

    def body(q_hbm, k_hbm, v_hbm, *rest):
        jin, (o_hbm, lse_hbm), jout = rest[:n_ji], rest[n_ji:n_ji + 2], rest[n_ji + 2:n_ji + 2 + n_jo]
        qs, ks, vs, acc, ms, ls, ob, sem = rest[n_ji + 2 + n_jo:n_ji + 2 + n_jo + 8]
        jsems = rest[n_ji + 2 + n_jo + 8:]
        h = pl.program_id(0)
        if job:
            @pl.when(h == 0)
            def _():
                job["start"](jin, jout, jsems)

        cols = pl.ds(pl.multiple_of(h * HEAD_DIM, HEAD_DIM), HEAD_DIM)
        cps = [pltpu.make_async_copy(src.at[:, cols], dst, sem.at[i])
               for i, (src, dst) in enumerate(((q_hbm, qs), (k_hbm, ks), (v_hbm, vs)))]
        for cp in cps:
            cp.start()
        acc[...] = jnp.zeros_like(acc)
        ms[...] = jnp.full_like(ms, MASKED)
        ls[...] = jnp.zeros_like(ls)
        for cp in cps:
            cp.wait()
        n_blocks = S // ATTN_BLOCK
        lanes = n_blocks // ATTN_LANES_FWD
        for d in DILATIONS:
            def step(it, carry, d=d):
                results = []
                for u in range(ATTN_LANES_FWD):
                    q0, k0, off = _block_rows(d, S, it + u * lanes)
                    qrows = pl.ds(q0, ATTN_BLOCK, stride=d)
                    krows = pl.ds(k0, 2 * ATTN_BLOCK, stride=d)
                    s = lax.dot_general(qs[qrows, :].astype(BF16), ks[krows, :].astype(BF16), _NT,
                                        preferred_element_type=F32) * scale
                    s = jnp.where(_band(off), s, MASKED)
                    m_old = ms[qrows, :]
                    m_new = jnp.maximum(m_old, jnp.max(s, axis=-1, keepdims=True))
                    alpha = jnp.exp(m_old - m_new)
                    p = jnp.exp(s - m_new)
                    l_new = alpha * ls[qrows, :] + jnp.sum(p, axis=-1, keepdims=True)
                    pv = jnp.dot(p.astype(BF16), vs[krows, :].astype(BF16), preferred_element_type=F32)
                    results.append((qrows, m_new, l_new, alpha * acc[qrows, :] + pv))
                for qrows, m_new, l_new, a_new in results:
                    ms[qrows, :] = m_new
                    ls[qrows, :] = l_new
                    acc[qrows, :] = a_new
                return carry
            lax.fori_loop(0, lanes, step, 0)
        ob[...] = (acc[...] / ls[...]).astype(BF16)
        ms[...] = ms[...] + jnp.log(ls[...])
        out = [pltpu.make_async_copy(ob, o_hbm.at[:, cols], sem.at[0]),
               pltpu.make_async_copy(ms, lse_hbm.at[h], sem.at[1])]
        for cp in out:
            cp.start()
        for cp in out:
            cp.wait()
        if job:
            @pl.when(h == n_attn - 1)
            def _():
                job["finish"](jin, jout, jsems)

    res = pl.pallas_call(
        body, name=name, grid=(n_attn,), in_specs=[ANY] * (3 + n_ji), out_specs=(ANY,) * (2 + n_jo),
        out_shape=(jax.ShapeDtypeStruct((S, A), BF16), jax.ShapeDtypeStruct((n_attn, S, 1), F32))
        + tuple(job["outs"] if job else ()),
        scratch_shapes=[pltpu.VMEM((S, HEAD_DIM), F32)] * 4 + [pltpu.VMEM((S, 1), F32)] * 2
        + [pltpu.VMEM((S, HEAD_DIM), BF16), pltpu.SemaphoreType.DMA((3,))] + list(job["sems"] if job else ()),
        compiler_params=_params(("arbitrary",)))(q, k, v, *(job["ins"] if job else ()))
    return res[0], res[1], list(res[2:])


def _attn_bwd(q, k, v, o, lse, dcat, n_attn, name):
    S, A = q.shape
    scale = HEAD_DIM ** -0.5

    def body(q_hbm, k_hbm, v_hbm, o_hbm, lse_hbm, do_hbm, dq_hbm, dk_hbm, dv_hbm,
             qs, ks, vs, dos, dqs, dks, dvs, lses, dls, ob, sem):
        h = pl.program_id(0)
        cols = pl.ds(pl.multiple_of(h * HEAD_DIM, HEAD_DIM), HEAD_DIM)
        cps = [pltpu.make_async_copy(src.at[:, cols], dst, sem.at[i])
               for i, (src, dst) in enumerate(((q_hbm, qs), (k_hbm, ks), (v_hbm, vs), (do_hbm, dos), (o_hbm, ob)))]
        cps.append(pltpu.make_async_copy(lse_hbm.at[h], lses, sem.at[5]))
        for cp in cps:
            cp.start()
        dqs[...] = jnp.zeros_like(dqs)
        dks[...] = jnp.zeros_like(dks)
        dvs[...] = jnp.zeros_like(dvs)
        for cp in cps:
            cp.wait()
        dls[...] = jnp.sum(dos[...] * ob[...].astype(F32), axis=-1, keepdims=True)
        n_blocks = S // ATTN_BLOCK
        lanes = n_blocks // ATTN_LANES_BWD
        assert lanes % 2 == 0
        for d in DILATIONS:
            def step(it, carry, d=d):
                results = []
                for u in range(ATTN_LANES_BWD):
                    q0, k0, off = _block_rows(d, S, it + u * lanes)
                    qrows = pl.ds(q0, ATTN_BLOCK, stride=d)
                    krows = pl.ds(k0, 2 * ATTN_BLOCK, stride=d)
                    qb, kb = qs[qrows, :].astype(BF16), ks[krows, :].astype(BF16)
                    s = lax.dot_general(qb, kb, _NT, preferred_element_type=F32) * scale
                    p = jnp.where(_band(off), jnp.exp(s - lses[qrows, :]), 0.0)
                    dob = dos[qrows, :].astype(BF16)
                    dp = lax.dot_general(dob, vs[krows, :].astype(BF16), _NT, preferred_element_type=F32)
                    ds = (p * (dp - dls[qrows, :]) * scale).astype(BF16)
                    results.append((qrows, krows,
                                    dqs[qrows, :] + jnp.dot(ds, kb, preferred_element_type=F32),
                                    dks[krows, :] + lax.dot_general(ds, qb, _TN, preferred_element_type=F32),
                                    dvs[krows, :] + lax.dot_general(p.astype(BF16), dob, _TN,
                                                                    preferred_element_type=F32)))
                for qrows, krows, dq_new, dk_new, dv_new in results:
                    dqs[qrows, :] = dq_new
                    dks[krows, :] = dk_new
                    dvs[krows, :] = dv_new
                return carry
            lax.fori_loop(0, lanes, step, 0)
        out = [pltpu.make_async_copy(src, dst.at[:, cols], sem.at[i])
               for i, (src, dst) in enumerate(((dqs, dq_hbm), (dks, dk_hbm), (dvs, dv_hbm)))]
        for cp in out:
            cp.start()
        for cp in out:
            cp.wait()

    grad = jax.ShapeDtypeStruct((S, A), F32)
    return pl.pallas_call(
        body, name=name, grid=(n_attn,), in_specs=[ANY] * 6, out_specs=(ANY, ANY, ANY), out_shape=(grad,) * 3,
        scratch_shapes=[pltpu.VMEM((S, HEAD_DIM), F32)] * 7 + [pltpu.VMEM((S, 1), F32)] * 2
        + [pltpu.VMEM((S, HEAD_DIM), BF16), pltpu.SemaphoreType.DMA((6,))],
        compiler_params=_params(("arbitrary",)))(q, k, v, o, lse, dcat)


def _ada_fwd(c_act, ada_w, name):
    L, D, n = ada_w.shape
    tn = _pick(n, (512, 256, 128))

    def body(c_ref, w_ref, o_ref):
        o_ref[...] = jnp.dot(c_ref[...], w_ref[...], preferred_element_type=F32)

    return pl.pallas_call(
        body, name=name, grid=(L, n // tn),
        in_specs=[pl.BlockSpec((N_DEV, D), lambda l, j: (0, 0)), pl.BlockSpec((None, D, tn), lambda l, j: (l, 0, j))],
        out_specs=pl.BlockSpec((None, N_DEV, tn), lambda l, j: (l, 0, j)),
        out_shape=jax.ShapeDtypeStruct((L, N_DEV, n), F32),
        compiler_params=_params(("parallel", "parallel")))(c_act, ada_w)


def _ada_bwd(c_act, dmod, name):
    L, _, n = dmod.shape
    D = c_act.shape[1]
    tn = _pick(n, (512, 256, 128))

    def body(c_ref, d_ref, o_ref):
        o_ref[...] = lax.dot_general(c_ref[...], d_ref[...], _TN, preferred_element_type=F32)

    return pl.pallas_call(
        body, name=name, grid=(L, n // tn),
        in_specs=[pl.BlockSpec((N_DEV, D), lambda l, j: (0, 0)), pl.BlockSpec((None, N_DEV, tn), lambda l, j: (l, 0, j))],
        out_specs=pl.BlockSpec((None, D, tn), lambda l, j: (l, 0, j)),
        out_shape=jax.ShapeDtypeStruct((L, D, n), F32),
        compiler_params=_params(("parallel", "parallel")))(c_act, dmod)


def _adamw(w, g, m, v, name, job=None):
    shape = w.shape
    C = shape[-1]
    R = w.size // C
    w2, g2, m2, v2 = (t.reshape(R, C) for t in (w, g, m, v))
    tr = _pick(R, (256, 128, 64, 32, 16, 8))
    tc = _pick(C, (2048, 1536, 1408, 1024, 512, 256, 128))
    n_ji = len(job["ins"]) if job else 0
    n_jo = len(job["outs"]) if job else 0
    grid = (R // tr, C // tc)

    def body(w_ref, g_ref, m_ref, v_ref, *rest):
        jin, (d_ref, mo_ref, vo_ref) = rest[:n_ji], rest[n_ji:n_ji + 3]
        jout, jsems = rest[n_ji + 3:n_ji + 3 + n_jo], rest[n_ji + 3 + n_jo:]
        i, j = pl.program_id(0), pl.program_id(1)
        if job:
            @pl.when((i == 0) & (j == 0))
            def _():
                job["start"](jin, jout, jsems)

        gv = g_ref[...]
        mn = ADAM_B1 * m_ref[...] + (1.0 - ADAM_B1) * gv
        vn = ADAM_B2 * v_ref[...] + (1.0 - ADAM_B2) * (gv * gv)
        m_hat = mn / (1.0 - ADAM_B1 ** ADAM_STEP)
        v_hat = vn / (1.0 - ADAM_B2 ** ADAM_STEP)
        d_ref[...] = -ADAM_LR * (m_hat / (jnp.sqrt(v_hat) + ADAM_EPS) + ADAM_WD * w_ref[...])
        mo_ref[...] = mn
        vo_ref[...] = vn
        if job:
            @pl.when((i == grid[0] - 1) & (j == grid[1] - 1))
            def _():
                job["finish"](jin, jout, jsems)

    spec = pl.BlockSpec((tr, tc), lambda i, j: (i, j))
    out = jax.ShapeDtypeStruct((R, C), F32)
    d, mn, vn, *got = pl.pallas_call(
        body, name=name, grid=grid, in_specs=[spec] * 4 + [ANY] * n_ji, out_specs=(spec,) * 3 + (ANY,) * n_jo,
        out_shape=(out,) * 3 + tuple(job["outs"] if job else ()), scratch_shapes=list(job["sems"] if job else ()),
        compiler_params=_params(("arbitrary", "arbitrary") if job else ("parallel", "parallel")),
    )(w2, g2, m2, v2, *(job["ins"] if job else ()))
    res = (d.reshape(shape), mn.reshape(shape), vn.reshape(shape))
    return res + (got,) if job else res


def _sum_leading(t, name):
    n, R, C = t.shape
    tr = _pick(R, (256, 128, 64, 32, 16, 8))

    def body(t_ref, o_ref):
        acc = t_ref[0]
        for i in range(1, n):
            acc = acc + t_ref[i]
        o_ref[...] = acc

    return pl.pallas_call(
        body, name=name, grid=(R // tr,), in_specs=[pl.BlockSpec((n, tr, C), lambda i: (0, i, 0))],
        out_specs=pl.BlockSpec((tr, C), lambda i: (i, 0)), out_shape=jax.ShapeDtypeStruct((R, C), F32),
        compiler_params=_params(("parallel",)))(t)


def _coords():
    return lax.axis_index("x"), lax.axis_index("y"), lax.axis_index("c")


def _other_chips(x, y):
    return [(1 - x, y), (x, 1 - y), (1 - x, 1 - y)]


def _all_gather8(t, name):
    R, C = t.shape

    def body(x_ref, out_ref, send_sems, recv_sems, local_sem):
        x, y, c = _coords()
        me, sibling = (x, y, c), (x, y, 1 - c)
        chips = _other_chips(x, y)

        def slot(px, py, pc):
            return out_ref.at[4 * px + 2 * py + pc]

        def copy(k, block, to, src=None):
            return pltpu.make_async_remote_copy(
                src_ref=slot(*block) if src is None else src, dst_ref=slot(*block),
                send_sem=send_sems.at[k], recv_sem=recv_sems.at[k], device_id=to, device_id_type=MESH)

        mine = pltpu.make_async_copy(x_ref, slot(*me), local_sem)
        mine.start()
        first = [copy(0, me, sibling, src=x_ref)]
        first += [copy(1 + j, me, (*chip, c), src=x_ref) for j, chip in enumerate(chips)]
        for cp in first:
            cp.start()
        passed = [copy(4 + j, (*chip, c), sibling) for j, chip in enumerate(chips)]
        for j, chip in enumerate(chips):
            copy(1 + j, (*chip, c), me).wait_recv()
            passed[j].start()
        copy(0, sibling, me).wait_recv()
        for j, chip in enumerate(chips):
            copy(4 + j, (*chip, 1 - c), me).wait_recv()
        for cp in first + passed:
            cp.wait_send()
        mine.wait()

    return pl.pallas_call(
        body, name=name, out_shape=jax.ShapeDtypeStruct((N_DEV, R, C), t.dtype),
        in_specs=[pl.BlockSpec(memory_space=pltpu.VMEM)], out_specs=pl.BlockSpec(memory_space=pltpu.VMEM),
        scratch_shapes=[pltpu.SemaphoreType.DMA((7,)), pltpu.SemaphoreType.DMA((7,)), pltpu.SemaphoreType.DMA],
        compiler_params=pltpu.CompilerParams(vmem_limit_bytes=VMEM_LIMIT_BYTES))(t)


def _window(ref, r0, nr, c0, nc):
    return ref.at[pl.ds(r0, nr), pl.ds(c0, nc)]


LOCAL_CHUNKS = 4


def _gather_job(pieces, out_shape):
    n = len(pieces)

    def ctx(ins, outs, sems):
        x, y, c = _coords()
        buf = outs[0]

        def place(p, chip_idx, r0, nr):
            _, _, kind, base = pieces[p]
            r, cs = ins[p].shape[1], ins[p].shape[2]
            if kind == "row":
                return _window(buf, base + chip_idx * r + r0, nr, 0, cs)
            return _window(buf, r0, nr, base + chip_idx * cs, cs)

        def ici(p, j, chip, src, dst):
            return pltpu.make_async_remote_copy(
                src_ref=src, dst_ref=dst, send_sem=sems[0].at[3 * p + j], recv_sem=sems[1].at[3 * p + j],
                device_id=(*chip, c), device_id_type=MESH)

        def d2d(p, j, win):
            return pltpu.make_async_remote_copy(
                src_ref=win, dst_ref=win, send_sem=sems[2].at[3 * p + j], recv_sem=sems[3].at[3 * p + j],
                device_id=(x, y, 1 - c), device_id_type=MESH)

        def local(p):
            lidx, r = pieces[p][1], ins[p].shape[1]
            rc = r // (2 * LOCAL_CHUNKS)
            return [pltpu.make_async_copy(ins[p].at[lidx, pl.ds(q * rc, rc), :], place(p, 2 * x + y, q * rc, rc),
                                          sems[4].at[2 * LOCAL_CHUNKS * p + q]) for q in range(2 * LOCAL_CHUNKS)]

        return x, y, c, _other_chips(x, y), place, ici, d2d, local

    def start(ins, outs, sems):
        x, y, c, chips, place, ici, d2d, local = ctx(ins, outs, sems)
        for p in range(n):
            lidx, rh = pieces[p][1], ins[p].shape[1] // 2
            for j, chip in enumerate(chips):
                ici(p, j, chip, ins[p].at[lidx, pl.ds(c * rh, rh), :], place(p, 2 * x + y, c * rh, rh)).start()
        for p in range(n):
            for cp in local(p):
                cp.start()

    def finish(ins, outs, sems):
        x, y, c, chips, place, ici, d2d, local = ctx(ins, outs, sems)
        for p in range(n):
            rh = ins[p].shape[1] // 2
            for j, chip in enumerate(chips):
                landed = place(p, 2 * chip[0] + chip[1], c * rh, rh)
                ici(p, j, chip, landed, landed).wait_recv()
                d2d(p, j, landed).start()
        for p in range(n):
            lidx, rh = pieces[p][1], ins[p].shape[1] // 2
            for j, chip in enumerate(chips):
                theirs = place(p, 2 * chip[0] + chip[1], (1 - c) * rh, rh)
                d2d(p, j, theirs).wait_recv()
                d2d(p, j, place(p, 2 * chip[0] + chip[1], c * rh, rh)).wait_send()
                ici(p, j, chip, ins[p].at[lidx, pl.ds(c * rh, rh), :], place(p, 2 * x + y, c * rh, rh)).wait_send()
            for cp in local(p):
                cp.wait()

    return dict(
        ins=[p[0] for p in pieces], outs=[jax.ShapeDtypeStruct(out_shape, BF16)], start=start, finish=finish,
        sems=[pltpu.SemaphoreType.DMA((3 * n,))] * 4 + [pltpu.SemaphoreType.DMA((2 * LOCAL_CHUNKS * n,))])


def _scatter_job(parts, plan):
    n = len(parts)

    def shard_shape(i):
        kind, _, size = plan[i]
        R, C = parts[i].shape
        return (size, C) if kind == "row" else (R, size)

    def copies(ins, outs, sems):
        x, y, c = _coords()
        cps = []
        for i in range(n):
            kind, base, size = plan[i]
            R, C = ins[i].shape
            for j, chip in enumerate(_other_chips(x, y)):
                their = 2 * chip[0] + chip[1]
                if kind == "row":
                    src = _window(ins[i], base + their * size, size, 0, C)
                else:
                    src = _window(ins[i], 0, R, base + their * size, size)
                cps.append(pltpu.make_async_remote_copy(
                    src_ref=src, dst_ref=outs[i].at[j], send_sem=sems[0].at[3 * i + j],
                    recv_sem=sems[1].at[3 * i + j], device_id=(*chip, c), device_id_type=MESH))
        return cps

    def start(ins, outs, sems):
        for cp in copies(ins, outs, sems):
            cp.start()

    def finish(ins, outs, sems):
        for cp in copies(ins, outs, sems):
            cp.wait()

    return dict(
        ins=list(parts), outs=[jax.ShapeDtypeStruct((3,) + shard_shape(i), F32) for i in range(n)],
        start=start, finish=finish, sems=[pltpu.SemaphoreType.DMA((3 * n,))] * 2)


def _swap_job(t):
    def copy(ins, outs, sems):
        x, y, c = _coords()
        return pltpu.make_async_remote_copy(
            src_ref=ins[0].at[1 - c], dst_ref=outs[0], send_sem=sems[0].at[0], recv_sem=sems[1].at[0],
            device_id=(x, y, 1 - c), device_id_type=MESH)

    return dict(ins=[t], outs=[jax.ShapeDtypeStruct(t.shape[1:], t.dtype)],
                start=lambda ins, outs, sems: copy(ins, outs, sems).start(),
                finish=lambda ins, outs, sems: copy(ins, outs, sems).wait(),
                sems=[pltpu.SemaphoreType.DMA((1,))] * 2)


def _join_jobs(jobs):
    jobs = [j for j in jobs if j]
    if not jobs:
        return None
    if len(jobs) == 1:
        return jobs[0]

    def each(fn_name, ins, outs, sems):
        i = o = s = 0
        for j in jobs:
            ni, no, ns = len(j["ins"]), len(j["outs"]), len(j["sems"])
            j[fn_name](ins[i:i + ni], outs[o:o + no], sems[s:s + ns])
            i, o, s = i + ni, o + no, s + ns

    return dict(ins=[a for j in jobs for a in j["ins"]], outs=[a for j in jobs for a in j["outs"]],
                sems=[a for j in jobs for a in j["sems"]],
                start=lambda ins, outs, sems: each("start", ins, outs, sems),
                finish=lambda ins, outs, sems: each("finish", ins, outs, sems))


def _hosted(call, jobs):
    jobs = [j for j in jobs if j]
    if not jobs:
        return call(None), []
    out, *rest = call(_join_jobs(jobs))
    per = []
    for j in jobs:
        n = len(j["outs"])
        per.append(rest[:n])
        rest = rest[n:]
    return out, per


def _run_job(job, name):
    n_i, n_o = len(job["ins"]), len(job["outs"])

    def body(*refs):
        ins, outs, sems = refs[:n_i], refs[n_i:n_i + n_o], refs[n_i + n_o:]
        job["start"](ins, outs, sems)
        job["finish"](ins, outs, sems)

    return pl.pallas_call(
        body, name=name, in_specs=[ANY] * n_i, out_specs=tuple([ANY] * n_o), out_shape=tuple(job["outs"]),
        scratch_shapes=list(job["sems"]),
        compiler_params=pltpu.CompilerParams(vmem_limit_bytes=VMEM_LIMIT_BYTES))(*job["ins"])


def _sibling_share(both, name):
    n = len(both)

    def body(*refs):
        outs = refs[n:2 * n]
        send_sems, recv_sems = refs[2 * n:]
        x, y, c = _coords()
        cps = []
        for i in range(n):
            cp = pltpu.make_async_remote_copy(
                src_ref=outs[i].at[c], dst_ref=outs[i].at[c], send_sem=send_sems.at[i], recv_sem=recv_sems.at[i],
                device_id=(x, y, 1 - c), device_id_type=MESH)
            cp.start()
            cps.append(cp)
        for cp in cps:
            cp.wait()

    return pl.pallas_call(
        body, name=name, in_specs=[ANY] * n, out_specs=tuple([ANY] * n),
        out_shape=tuple(jax.ShapeDtypeStruct(b.shape, b.dtype) for b in both),
        input_output_aliases={i: i for i in range(n)},
        scratch_shapes=[pltpu.SemaphoreType.DMA((n,))] * 2,
        compiler_params=pltpu.CompilerParams(vmem_limit_bytes=VMEM_LIMIT_BYTES))(*both)


def _add_half(full3, recv, core, name):
    _, Rh, C = full3.shape
    tr = _pick(Rh, (256, 176, 128, 64, 32, 16, 8))
    tc = _pick(C, (2048, 1536, 1408, 1024, 512, 256, 128))

    def body(c_ref, a_ref, b_ref, o_ref):
        o_ref[...] = a_ref[...] + b_ref[...]

    return pl.pallas_call(
        body, name=name,
        grid_spec=pltpu.PrefetchScalarGridSpec(
            num_scalar_prefetch=1, grid=(Rh // tr, C // tc),
            in_specs=[pl.BlockSpec((None, tr, tc), lambda i, j, cr: (cr[0], i, j)),
                      pl.BlockSpec((tr, tc), lambda i, j, cr: (i, j))],
            out_specs=pl.BlockSpec((tr, tc), lambda i, j, cr: (i, j))),
        out_shape=jax.ShapeDtypeStruct((Rh, C), F32),
        compiler_params=_params(("parallel", "parallel")))(core, full3, recv)


def _add_scattered(part, recv, kind, base, size, core_chip, name):
    _, rs, cs = recv.shape
    tr = _pick(rs, (256, 176, 128, 64, 32, 16, 8))
    tc = _pick(cs, (2048, 1536, 1408, 1024, 512, 256, 128))
    assert base % size == 0
    if kind == "row":
        pidx = lambda i, j, cr: ((base // size + cr[1]) * (rs // tr) + i, j)
    else:
        pidx = lambda i, j, cr: (i, (base // size + cr[1]) * (cs // tc) + j)

    def body(c_ref, a_ref, r_ref, o_ref):
        o_ref[...] = ((a_ref[...] + r_ref[0]) + r_ref[1]) + r_ref[2]

    return pl.pallas_call(
        body, name=name,
        grid_spec=pltpu.PrefetchScalarGridSpec(
            num_scalar_prefetch=1, grid=(rs // tr, cs // tc),
            in_specs=[pl.BlockSpec((tr, tc), pidx), pl.BlockSpec((3, tr, tc), lambda i, j, cr: (0, i, j))],
            out_specs=pl.BlockSpec((None, tr, tc), lambda i, j, cr: (cr[0], i, j))),
        out_shape=jax.ShapeDtypeStruct((2, rs, cs), F32),
        compiler_params=_params(("parallel", "parallel")))(core_chip, part, recv)


def _rs_split(g, windows):
    R, C = g.shape
    if windows[0][0] == "row":
        size = windows[0][2]
        t = g.reshape(N_CHIPS, 2, size // 2, C).transpose(1, 0, 2, 3).reshape(2, R // 2, C)
        return t, [(k, b // 2, s // 2) for k, b, s in windows]
    return g.reshape(2, R // 2, C), list(windows)


def _rs_finish(part, windows, got, core_chip, tag):
    both = [_add_scattered(part, r, k, b, s, core_chip, f"rs_add_{tag}_{n}")
            for n, (r, (k, b, s)) in enumerate(zip(got, windows))]
    both = _sibling_share(both, f"rs_share_{tag}")
    return [t.reshape(2 * t.shape[1], t.shape[2]) for t in both]


def _rope_tables(positions, S):
    half = ROPE_DIM // 2
    inv_freq = ROPE_THETA ** (-jnp.arange(0, ROPE_DIM, 2, dtype=F32) / ROPE_DIM)
    ang = positions.reshape(S, 1).astype(F32) * inv_freq[None, :]
    cos, sin = jnp.cos(ang), jnp.sin(ang)
    zeros = jnp.zeros((S, half), F32)
    rest0 = jnp.zeros((S, HEAD_DIM - ROPE_DIM), F32)
    cs = jnp.concatenate([cos, cos, jnp.ones((S, HEAD_DIM - ROPE_DIM), F32)], axis=1)
    sa = jnp.concatenate([-sin, zeros, rest0], axis=1)
    sb = jnp.concatenate([zeros, sin, rest0], axis=1)
    return cs, sa, sb


def kernel(x, c, positions, ada_w, ada_b, norm_mix, norm_ffn, ab_w_in, sgu_w, sgu_b, ab_w_out, conv_w_in, conv_w, conv_w_out, ffn_w_gate, ffn_w_up, ffn_w_down, final_norm, loss_target, m_ada_w, m_ada_b, m_norm_mix, m_norm_ffn, m_ab_w_in, m_sgu_w, m_sgu_b, m_ab_w_out, m_conv_w_in, m_conv_w, m_conv_w_out, m_ffn_w_gate, m_ffn_w_up, m_ffn_w_down, m_final_norm, v_ada_w, v_ada_b, v_norm_mix, v_norm_ffn, v_ab_w_in, v_sgu_w, v_sgu_b, v_ab_w_out, v_conv_w_in, v_conv_w, v_conv_w_out, v_ffn_w_gate, v_ffn_w_up, v_ffn_w_down, v_final_norm):
    S, D = x.shape[1], x.shape[2]
    L = ada_w.shape[0]
    n_mix_heads = D // HEAD_DIM
    n_attn = 3 * n_mix_heads // 4
    A = n_attn * HEAD_DIM
    G = n_mix_heads - n_attn
    F = ffn_w_gate.shape[2] * N_CHIPS
    mix_in = ab_w_in.shape[2] * N_CHIPS
    xi, yi, ci = _coords()
    chip = 2 * xi + yi
    dev = 4 * xi + 2 * yi + ci
    core1 = jnp.reshape(ci, (1,)).astype(jnp.int32)
    chip1 = jnp.reshape(chip, (1,)).astype(jnp.int32)
    x2 = x.reshape(S, D)
    target = loss_target.reshape(S, D)

    n_conv = conv_w.size
    w0 = D + n_conv
    w0p = -(-w0 // 128) * 128
    pack = jnp.zeros((8, w0p), F32).at[0, :D].set(c[0]).at[0, D:w0].set(conv_w.reshape(-1))
    g0 = _all_gather8(pack, "gather_cond")
    c_all = g0[:, 0, :D]
    c_act = c_all * jax.nn.sigmoid(c_all)
    conv_full = jnp.concatenate(
        [g0[2 * j, 0, D:w0].reshape(conv_w.shape) for j in range(N_CHIPS)], axis=2)
    mod_part = _ada_fwd(c_act, ada_w, "ada_fwd")
    n_ada = ada_w.shape[2]
    g1 = _all_gather8(mod_part.reshape(L * N_DEV, n_ada), "gather_mod")
    mod_all = jnp.concatenate([g1[2 * j].reshape(L, N_DEV, n_ada) for j in range(N_CHIPS)], axis=2)
    mod = lax.dynamic_index_in_dim(mod_all, dev, axis=1, keepdims=False) + ada_b
    mods = mod.reshape(L, 6, 1, D)
    cs, sa, sb = _rope_tables(positions, S)

    bf = lambda t: t.astype(BF16)
    w_in_e, w_out_e = bf(ab_w_in), bf(ab_w_out)
    w_in_o, w_out_o = bf(conv_w_in), bf(conv_w_out)
    w_gate, w_up, w_down = bf(ffn_w_gate), bf(ffn_w_up), bf(ffn_w_down)
    def gather_jobs(l):
        i = l // 2
        first, n_in_cols = ((w_in_e, w_out_e), mix_in) if l % 2 == 0 else ((w_in_o, w_out_o), 3 * D)
        return [_gather_job([(first[0], i, "col", 0)], (D, n_in_cols)),
                _gather_job([(first[1], i, "row", 0)], (D, D)),
                _gather_job([(w_gate, l, "col", 0), (w_up, l, "col", F)], (D, 2 * F)),
                _gather_job([(w_down, l, "row", 0)], (F, D))]

    first_jobs = gather_jobs(0)
    layer_w = [[_run_job(first_jobs[0], "gather_w0_in")[0], None, None, None]]

    saved = []
    xc = x2
    pending = None
    for l in range(L):
        i = l // 2
        nxt = gather_jobs(l + 1) if l + 1 < L else [None] * 4
        if l + 1 < L:
            layer_w.append([None] * 4)

        def mm_fwd(a, b, out_dtype, name, slot, also=None):
            jobs = [nxt[slot], first_jobs[also] if also is not None else None]
            out, got = _hosted(lambda job: _mm(a, b, "nn", out_dtype, name, job=job), jobs)
            if nxt[slot] is not None:
                layer_w[l + 1][slot] = got.pop(0)[0]
            if also is not None:
                layer_w[l][also] = got.pop(0)[0]
            return out

        sh_m, sc_m, g_m, sh_f, sc_f, g_f = (mods[l, t] for t in range(6))
        weff_m = norm_mix[l][None, :] * (1.0 + sc_m)
        weff_f = norm_ffn[l][None, :] * (1.0 + sc_f)
        if pending is None:
            _, h = _norm_mod(xc, None, None, weff_m, sh_m, f"norm_mix{l}")
        else:
            xc, h = _norm_mod(xc, pending[0], pending[1], weff_m, sh_m, f"norm_mix{l}")
        z = mm_fwd(h, layer_w[l][0], BF16, f"mm_in{l}", 0, also=1 if l == 0 else None)
        st = dict(x=xc, h=h, z=z, weff_m=weff_m, weff_f=weff_f, g_m=g_m, g_f=g_f, sc_m=sc_m, sc_f=sc_f)
        if l % 2 == 0:
            q, k, v = _qkv_prep(z, cs, sa, sb, n_attn, f"qkv_prep{l}")
            o, lse, got = _attn_fwd(q, k, v, n_attn, f"attn_fwd{l}",
                                    job=_join_jobs(first_jobs[2:]) if l == 0 else None)
            if l == 0:
                layer_w[0][2], layer_w[0][3] = got
            bT = sgu_b[i].T
            so = _sgu_fwd(z, sgu_w[i], bT, n_attn, f"sgu_fwd{l}")
            cat = jnp.concatenate([o, so], axis=1)
            st.update(q=q, k=k, v=v, o=o, lse=lse, bT=bT)
        else:
            w8 = jnp.zeros((8, D), F32).at[:3].set(conv_full[i])
            cat = _conv_fwd(z, w8, f"conv_fwd{l}")
            st.update(w8=w8)
        mix = mm_fwd(cat, layer_w[l][1], F32, f"mm_out{l}", 1)
        x1, h2 = _norm_mod(xc, mix, g_m, weff_f, sh_f, f"norm_ffn{l}")
        ab = mm_fwd(h2, layer_w[l][2], BF16, f"mm_gu{l}", 2)
        f = _swiglu(ab, f"swiglu{l}")
        yv = mm_fwd(f, layer_w[l][3], F32, f"mm_down{l}", 3)
        st.update(cat=cat, mix=mix, x1=x1, h2=h2, ab=ab, f=f, y=yv)
        saved.append(st)
        xc = x1
        pending = (yv, g_f)

    dx, loss11, dfinal = _loss_head(xc, pending[0], pending[1], final_norm[None, :], target, "loss_head")
    loss = lax.psum(loss11[0, 0], ("x", "y", "c"))

    dmods = [None] * L
    dnorm_mix, dnorm_ffn = [None] * L, [None] * L
    big = {l: {} for l in range(L)}
    core_chip = jnp.concatenate([core1, chip1])
    dsgu_w, dsgu_b, dconv = [None] * (L - L // 2), [None] * (L - L // 2), [None] * (L // 2)

    def mm_bwd(a, b, mode, out_dtype, name, scatters=(), swap=None):
        jobs = [_scatter_job([p[0]] * len(p[1]), p[1]) for p in scatters]
        jobs += [_swap_job(swap[0])] if swap else []
        out, got = _hosted(lambda job: _mm(a, b, mode, out_dtype, name, job=job), jobs)
        for (part, windows, lay, keys), recv in zip(scatters, got):
            for key, red in zip(keys, _rs_finish(part, windows, recv, core_chip, f"{lay}_{keys[0]}")):
                big[lay][key] = red
        half = _add_half(swap[0], got[-1][0], core_chip, f"rs_add_half_{swap[1]}") if swap else None
        return out, half

    above = None
    for l in reversed(range(L)):
        i = l // 2
        st = saved[l]
        w_in, w_out, w_gu, w_dn = layer_w[l]
        dy, dg_f = _gate_bwd(dx, st["y"], st["g_f"], f"gate_f_bwd{l}")
        if above is None:
            df, _ = mm_bwd(dy, w_dn, "nt", BF16, f"mm_down_dx{l}")
            dw_dn, _ = mm_bwd(st["f"], dy, "tn", F32, f"mm_down_dw{l}")
        else:
            s_out, t_in, win_in = above
            df, p_in = mm_bwd(dy, w_dn, "nt", BF16, f"mm_down_dx{l}", [s_out], (t_in, f"{l + 1}_in"))
            dw_dn, _ = mm_bwd(st["f"], dy, "tn", F32, f"mm_down_dw{l}", [(p_in, win_in, l + 1, ["in"])])
        t_dn, win_dn = _rs_split(dw_dn, [("row", 0, F // N_CHIPS)])
        dab = _swiglu_bwd(st["ab"], df, f"swiglu_bwd{l}")
        dh2, p_dn = mm_bwd(dab, w_gu, "nt", F32, f"mm_gu_dx{l}", swap=(t_dn, f"{l}_down"))
        dw_gu, _ = mm_bwd(st["h2"], dab, "tn", F32, f"mm_gu_dw{l}", [(p_dn, win_dn, l, ["down"])])
        t_gu, win_gu = _rs_split(dw_gu, [("col", 0, F // N_CHIPS), ("col", F, F // N_CHIPS)])
        dx1, dsh_f, dweff_f = _norm_mod_bwd(dh2, st["x1"], st["weff_f"], dx, f"norm_ffn_bwd{l}")
        dmix, dg_m = _gate_bwd(dx1, st["mix"], st["g_m"], f"gate_m_bwd{l}")
        dcat, p_gu = mm_bwd(dmix, w_out, "nt", F32, f"mm_out_dx{l}", swap=(t_gu, f"{l}_gu"))
        dw_out = _mm(st["cat"], dmix, "tn", F32, f"mm_out_dw{l}")
        t_out, win_out = _rs_split(dw_out, [("row", 0, D // N_CHIPS)])
        if l % 2 == 0:
            dq, dk, dv = _attn_bwd(st["q"], st["k"], st["v"], st["o"], st["lse"], dcat, n_attn, f"attn_bwd{l}")
            dqkv = _dqkv_post(dq, dk, dv, cs, sa, sb, n_attn, f"dqkv_post{l}")
            duv, dsgu_w[i], dbT = _sgu_bwd(st["z"], dcat, sgu_w[i], st["bT"], n_attn, f"sgu_bwd{l}")
            dsgu_b[i] = dbT.T
            dz = jnp.concatenate([dqkv, duv], axis=1)
        else:
            dz, dw8 = _conv_bwd(st["z"], dcat, st["w8"], f"conv_bwd{l}")
            dconv[i] = dw8[:3]
        dh, p_out = mm_bwd(dz, w_in, "nt", F32, f"mm_in_dx{l}", [(p_gu, win_gu[:1], l, ["gate"])],
                           (t_out, f"{l}_out"))
        s_out = (p_out, win_out, l, ["out"])
        dw_in, _ = mm_bwd(st["h"], dz, "tn", F32, f"mm_in_dw{l}",
                          [(p_gu, win_gu[1:], l, ["up"])] + ([s_out] if l == 0 else []))
        t_in, win_in = _rs_split(dw_in, [("col", 0, w_in.shape[1] // N_CHIPS)])
        above = (s_out, t_in, win_in)
        dx, dsh_m, dweff_m = _norm_mod_bwd(dh, st["x"], st["weff_m"], dx1, f"norm_mix_bwd{l}")
        dmods[l] = jnp.concatenate(
            [dsh_m, dweff_m * norm_mix[l][None, :], dg_m, dsh_f, dweff_f * norm_ffn[l][None, :], dg_f], axis=1)
        dnorm_mix[l] = dweff_m * (1.0 + st["sc_m"])
        dnorm_ffn[l] = dweff_f * (1.0 + st["sc_f"])
    _, t_in, win_in = above
    (recv_in,) = _run_job(_swap_job(t_in), "rs_swap_0_in")
    p_in0 = _add_half(t_in, recv_in, core_chip, "rs_add_half_0_in")
    grad_x = dx.reshape(1, S, D)

    dmod = jnp.concatenate(dmods, axis=0)
    small = [dmod.reshape(-1), jnp.concatenate(dnorm_mix, 0).reshape(-1), jnp.concatenate(dnorm_ffn, 0).reshape(-1),
             jnp.stack(dsgu_w).reshape(-1), jnp.stack(dsgu_b).reshape(-1), jnp.stack(dconv).reshape(-1), dfinal.reshape(-1)]
    sizes = [t.size for t in small]
    flat = jnp.concatenate(small)
    n_flat = flat.size
    rows = -(-n_flat // (128 * 8)) * 8
    flat = jnp.concatenate([flat, jnp.zeros((rows * 128 - n_flat,), F32)]).reshape(rows, 128)
    g2 = _all_gather8(flat, "gather_small")
    tot = _sum_leading(g2, "sum_small").reshape(-1)
    offs = [0]
    for s in sizes:
        offs.append(offs[-1] + s)
    take = lambda n, shape: tot[offs[n]:offs[n + 1]].reshape(shape)
    g_ada_b = take(0, ada_b.shape)
    g_norm_mix = take(1, norm_mix.shape)
    g_norm_ffn = take(2, norm_ffn.shape)
    g_sgu_w = take(3, sgu_w.shape)
    g_sgu_b = take(4, sgu_b.shape)
    g_conv_full = take(5, conv_full.shape)
    n_cw = conv_w.shape[2]
    g_conv_w = lax.dynamic_slice_in_dim(g_conv_full, chip * n_cw, n_cw, axis=2)
    g_final = take(6, final_norm.shape)
    dmod_all = g2[:, :, :].reshape(N_DEV, -1)[:, :offs[1]].reshape(N_DEV, L, 6 * D)
    dmod_mine = lax.dynamic_slice_in_dim(dmod_all, chip * n_ada, n_ada, axis=2).transpose(1, 0, 2)
    g_ada_w = _ada_bwd(c_act, dmod_mine, "ada_bwd")
    *ada_update, got = _adamw(ada_w, g_ada_w, m_ada_w, v_ada_w, "adamw_ada_w", job=_scatter_job([p_in0], win_in))
    (big[0]["in"],) = _rs_finish(p_in0, win_in, got, core_chip, "0_in")

    def stack(key, layers):
        return jnp.stack([big[l][key] for l in layers])

    even, odd, every = list(range(0, L, 2)), list(range(1, L, 2)), list(range(L))
    grads = dict(
        ada_w=g_ada_w, ada_b=g_ada_b, norm_mix=g_norm_mix, norm_ffn=g_norm_ffn,
        ab_w_in=stack("in", even), sgu_w=g_sgu_w, sgu_b=g_sgu_b, ab_w_out=stack("out", even),
        conv_w_in=stack("in", odd), conv_w=g_conv_w, conv_w_out=stack("out", odd),
        ffn_w_gate=stack("gate", every), ffn_w_up=stack("up", every), ffn_w_down=stack("down", every),
        final_norm=g_final)
    weights = dict(ada_w=ada_w, ada_b=ada_b, norm_mix=norm_mix, norm_ffn=norm_ffn, ab_w_in=ab_w_in, sgu_w=sgu_w,
                   sgu_b=sgu_b, ab_w_out=ab_w_out, conv_w_in=conv_w_in, conv_w=conv_w, conv_w_out=conv_w_out,
                   ffn_w_gate=ffn_w_gate, ffn_w_up=ffn_w_up, ffn_w_down=ffn_w_down, final_norm=final_norm)
    ms = dict(ada_w=m_ada_w, ada_b=m_ada_b, norm_mix=m_norm_mix, norm_ffn=m_norm_ffn, ab_w_in=m_ab_w_in, sgu_w=m_sgu_w,
              sgu_b=m_sgu_b, ab_w_out=m_ab_w_out, conv_w_in=m_conv_w_in, conv_w=m_conv_w, conv_w_out=m_conv_w_out,
              ffn_w_gate=m_ffn_w_gate, ffn_w_up=m_ffn_w_up, ffn_w_down=m_ffn_w_down, final_norm=m_final_norm)
    vs = dict(ada_w=v_ada_w, ada_b=v_ada_b, norm_mix=v_norm_mix, norm_ffn=v_norm_ffn, ab_w_in=v_ab_w_in, sgu_w=v_sgu_w,
              sgu_b=v_sgu_b, ab_w_out=v_ab_w_out, conv_w_in=v_conv_w_in, conv_w=v_conv_w, conv_w_out=v_conv_w_out,
              ffn_w_gate=v_ffn_w_gate, ffn_w_up=v_ffn_w_up, ffn_w_down=v_ffn_w_down, final_norm=v_final_norm)
    names = list(weights)
    deltas, new_m, new_v = {}, {}, {}
    for n in names:
        w, g = weights[n], grads[n]
        if n == "ada_w":
            deltas[n], new_m[n], new_v[n] = ada_update
        elif w.ndim == 1:
            d_, m_, v_ = _adamw(w[None, :], g[None, :], ms[n][None, :], vs[n][None, :], f"adamw_{n}")
            deltas[n], new_m[n], new_v[n] = d_[0], m_[0], v_[0]
        else:
            deltas[n], new_m[n], new_v[n] = _adamw(w, g, ms[n], vs[n], f"adamw_{n}")
    return (loss, grad_x, *[grads[n] for n in names], *[deltas[n] for n in names],
            *[new_m[n] for n in names], *[new_v[n] for n in names])
```

```python
import functools
import math

import jax
import jax.numpy as jnp
from jax import lax
from jax.experimental import pallas as pl
from jax.experimental.pallas import tpu as pltpu

F32 = jnp.float32
BF16 = jnp.bfloat16
HEAD_DIM = 128
CHUNK = 128
ATTN_BLOCK = 128
ATTN_LANES_FWD = 2
ATTN_LANES_BWD = 2
DILATIONS = (1, 4, 16)
ROPE_DIM = HEAD_DIM // 4
ROPE_THETA = 500000.0
EPS = 1e-6
MASKED = -1e30
ADAM_LR, ADAM_B1, ADAM_B2, ADAM_EPS, ADAM_WD, ADAM_STEP = 0.001, 0.9, 0.999, 1e-08, 0.01, 10
VMEM_LIMIT_BYTES = 56 * 1024 * 1024
MESH = pl.DeviceIdType.MESH
ANY = pl.BlockSpec(memory_space=pl.ANY)
N_CHIPS = 4
N_DEV = 8


def _pick(n, cands):
    for t in cands:
        if n % t == 0:
            return t
    return n


def _params(sem):
    return pltpu.CompilerParams(dimension_semantics=sem, vmem_limit_bytes=VMEM_LIMIT_BYTES)


MM_VMEM_BUDGET = 44 * 1024 * 1024
_TILES = (2048, 1536, 1408, 1024, 512, 256, 128)


def _mm_tiles(M, N, K, out_bytes):
    best = None
    for tm in [t for t in _TILES if M % t == 0] or [M]:
        for tn in [t for t in _TILES if N % t == 0] or [N]:
            for tk in [t for t in _TILES if K % t == 0] or [K]:
                nk = K // tk
                vmem = 2 * 2 * (tm * tk + tk * tn) + 2 * out_bytes * tm * tn + 4 * tm * tn
                vmem += 4 * tm * tn if nk > 1 and out_bytes == 2 else 0
                if vmem > MM_VMEM_BUDGET:
                    continue
                key = ((M // tm) * (N // tn) * nk, -tk, -tm)
                if best is None or key < best[0]:
                    best = (key, (tm, tn, tk))
    assert best is not None, (M, N, K)
    return best[1]


def _mm(a, b, mode, out_dtype, name, layer=None, job=None):
    bshape = b.shape[1:] if layer is not None else b.shape
    if mode == "nn":
        (M, K), (K2, N) = a.shape, bshape
    elif mode == "nt":
        (M, K), (N, K2) = a.shape, bshape
    else:
        (K, M), (K2, N) = a.shape, bshape
    assert K == K2, (a.shape, b.shape, mode)
    in_place = out_dtype == F32
    tm, tn, tk = _mm_tiles(M, N, K, 4 if in_place else 2)
    nk = K // tk
    dims = {"nn": (((1,), (0,)), ((), ())), "nt": (((1,), (1,)), ((), ())), "tn": (((0,), (0,)), ((), ()))}[mode]

    n_ji = len(job["ins"]) if job else 0
    n_jo = len(job["outs"]) if job else 0
    n_acc = 1 if nk > 1 and not in_place else 0
    grid = (M // tm, N // tn, nk)

    def body(a_ref, b_ref, *rest):
        jin, o_ref, jout = rest[:n_ji], rest[n_ji], rest[n_ji + 1:n_ji + 1 + n_jo]
        scratch = rest[n_ji + 1 + n_jo:]
        acc, sems = scratch[:n_acc], scratch[n_acc:]
        i, j, k = pl.program_id(0), pl.program_id(1), pl.program_id(2)
        if job:
            @pl.when((i == 0) & (j == 0) & (k == 0))
            def _():
                job["start"](jin, jout, sems)

        def product():
            return lax.dot_general(a_ref[...].astype(BF16), b_ref[...].astype(BF16), dims, preferred_element_type=F32)

        if nk == 1:
            o_ref[...] = product().astype(o_ref.dtype)
        else:
            acc_ref = o_ref if in_place else acc[0]

            @pl.when(k == 0)
            def _():
                acc_ref[...] = jnp.zeros_like(acc_ref)

            acc_ref[...] += product()

            if not in_place:
                @pl.when(k == nk - 1)
                def _():
                    o_ref[...] = acc_ref[...].astype(o_ref.dtype)

        if job:
            @pl.when((i == grid[0] - 1) & (j == grid[1] - 1) & (k == grid[2] - 1))
            def _():
                job["finish"](jin, jout, sems)

    if mode == "tn":
        a_spec = pl.BlockSpec((tk, tm), lambda i, j, k: (k, i))
    else:
        a_spec = pl.BlockSpec((tm, tk), lambda i, j, k: (i, k))
    if mode == "nt":
        bblk, bidx = (tn, tk), (lambda i, j, k: (j, k))
    else:
        bblk, bidx = (tk, tn), (lambda i, j, k: (k, j))
    if layer is not None:
        b_spec = pl.BlockSpec((None,) + bblk, lambda i, j, k: (layer,) + bidx(i, j, k))
    else:
        b_spec = pl.BlockSpec(bblk, bidx)
    out_spec = pl.BlockSpec((tm, tn), lambda i, j, k: (i, j))
    out_shape = jax.ShapeDtypeStruct((M, N), out_dtype)
    acc_scratch = [pltpu.VMEM((tm, tn), F32)] * n_acc
    if not job:
        return pl.pallas_call(
            body, name=name, grid=grid, in_specs=[a_spec, b_spec], out_specs=out_spec, out_shape=out_shape,
            scratch_shapes=acc_scratch, compiler_params=_params(("parallel", "parallel", "arbitrary")),
        )(a, b)
    return pl.pallas_call(
        body, name=name, grid=grid, in_specs=[a_spec, b_spec] + [ANY] * n_ji,
        out_specs=(out_spec,) + (ANY,) * n_jo, out_shape=(out_shape,) + tuple(job["outs"]),
        scratch_shapes=acc_scratch + list(job["sems"]),
        compiler_params=_params(("arbitrary", "arbitrary", "arbitrary")),
    )(a, b, *job["ins"])


def _rows(S):
    return _pick(S, (256, 128, 64, 32, 16, 8))


def _row_spec(tr, width, col=0):
    return pl.BlockSpec((tr, width), lambda i: (i, col))


def _vec_spec(rows, width):
    return pl.BlockSpec((rows, width), lambda i: (0, 0))


def _rms(xv):
    return lax.rsqrt(jnp.mean(xv * xv, axis=-1, keepdims=True) + EPS)


def _norm_mod(x, y, g, w_eff, sh, name):
    S, D = x.shape
    tr = _rows(S)
    fused = y is not None

    def body(*refs):
        if fused:
            x_ref, y_ref, g_ref, w_ref, s_ref, x1_ref, h_ref = refs
            xv = x_ref[...] + g_ref[...] * y_ref[...]
            x1_ref[...] = xv
        else:
            x_ref, w_ref, s_ref, h_ref = refs
            xv = x_ref[...]
        h_ref[...] = (xv * _rms(xv) * w_ref[...] + s_ref[...]).astype(BF16)

    big, vec = _row_spec(tr, D), _vec_spec(1, D)
    if fused:
        ins, in_specs = (x, y, g, w_eff, sh), [big, big, vec, vec, vec]
        out_shape = (jax.ShapeDtypeStruct((S, D), F32), jax.ShapeDtypeStruct((S, D), BF16))
        out_specs = (big, big)
    else:
        ins, in_specs = (x, w_eff, sh), [big, vec, vec]
        out_shape = jax.ShapeDtypeStruct((S, D), BF16)
        out_specs = big
    out = pl.pallas_call(body, name=name, grid=(S // tr,), in_specs=in_specs, out_specs=out_specs,
                         out_shape=out_shape, compiler_params=_params(("parallel",)))(*ins)
    return out if fused else (None, out)


def _norm_mod_bwd(dh, x, w_eff, dres, y, g, name):
    S, D = x.shape
    tr = _rows(S)
    gated = y is not None

    def body(dh_ref, x_ref, w_ref, r_ref, *rest):
        if gated:
            y_ref, g_ref, dx_ref, dsh_ref, dw_ref, dy_ref, dg_ref = rest
        else:
            dx_ref, dsh_ref, dw_ref = rest
        xv = x_ref[...]
        dhv = dh_ref[...].astype(F32)
        r = _rms(xv)
        xn = xv * r
        dxn = dhv * w_ref[...]
        dxv = r_ref[...] + r * (dxn - xn * jnp.mean(dxn * xn, axis=-1, keepdims=True))
        dx_ref[...] = dxv

        @pl.when(pl.program_id(0) == 0)
        def _():
            dsh_ref[...] = jnp.zeros_like(dsh_ref)
            dw_ref[...] = jnp.zeros_like(dw_ref)
            if gated:
                dg_ref[...] = jnp.zeros_like(dg_ref)

        dsh_ref[...] += jnp.sum(dhv, axis=0, keepdims=True)
        dw_ref[...] += jnp.sum(dhv * xn, axis=0, keepdims=True)
        if gated:
            dy_ref[...] = (dxv * g_ref[...]).astype(BF16)
            dg_ref[...] += jnp.sum(dxv * y_ref[...], axis=0, keepdims=True)

    big, vec = _row_spec(tr, D), _vec_spec(1, D)
    f32v = jax.ShapeDtypeStruct((1, D), F32)
    outs = (jax.ShapeDtypeStruct((S, D), F32), f32v, f32v)
    if not gated:
        return pl.pallas_call(
            body, name=name, grid=(S // tr,), in_specs=[big, big, vec, big], out_specs=(big, vec, vec), out_shape=outs,
            compiler_params=_params(("arbitrary",)))(dh, x, w_eff, dres)
    return pl.pallas_call(
        body, name=name, grid=(S // tr,), in_specs=[big, big, vec, big, big, vec],
        out_specs=(big, vec, vec, big, vec), out_shape=outs + (jax.ShapeDtypeStruct((S, D), BF16), f32v),
        compiler_params=_params(("arbitrary",)))(dh, x, w_eff, dres, y, g)


def _loss_head(x, y, g, gamma, target, name):
    S, D = x.shape
    tr = _rows(S)

    def body(x_ref, y_ref, g_ref, gm_ref, t_ref, dx_ref, loss_ref, dgm_ref, dy_ref, dg_ref):
        yv = y_ref[...]
        xv = x_ref[...] + g_ref[...] * yv
        r = _rms(xv)
        xn = xv * r
        err = xn * gm_ref[...] - t_ref[...]
        dout = err * (1.0 / D)
        dxn = dout * gm_ref[...]
        dxv = r * (dxn - xn * jnp.mean(dxn * xn, axis=-1, keepdims=True))
        dx_ref[...] = dxv
        dy_ref[...] = (dxv * g_ref[...]).astype(BF16)

        @pl.when(pl.program_id(0) == 0)
        def _():
            loss_ref[...] = jnp.zeros_like(loss_ref)
            dgm_ref[...] = jnp.zeros_like(dgm_ref)
            dg_ref[...] = jnp.zeros_like(dg_ref)

        loss_ref[...] += 0.5 * jnp.sum(jnp.mean(err * err, axis=-1, keepdims=True), axis=0, keepdims=True)
        dgm_ref[...] += jnp.sum(dout * xn, axis=0, keepdims=True)
        dg_ref[...] += jnp.sum(dxv * yv, axis=0, keepdims=True)

    big, vec = _row_spec(tr, D), _vec_spec(1, D)
    f32v = jax.ShapeDtypeStruct((1, D), F32)
    return pl.pallas_call(
        body, name=name, grid=(S // tr,), in_specs=[big, big, vec, vec, big],
        out_specs=(big, _vec_spec(1, 1), vec, big, vec),
        out_shape=(jax.ShapeDtypeStruct((S, D), F32), jax.ShapeDtypeStruct((1, 1), F32), f32v,
                   jax.ShapeDtypeStruct((S, D), BF16), f32v),
        compiler_params=_params(("arbitrary",)))(x, y, g, gamma, target)


def _silu(a):
    return a * jax.nn.sigmoid(a)


def _swiglu(ab, name):
    S, F2 = ab.shape
    F = F2 // 2
    tr = _pick(S, (128, 64, 32, 16, 8))

    def body(a_ref, b_ref, f_ref):
        f_ref[...] = (_silu(a_ref[...].astype(F32)) * b_ref[...].astype(F32)).astype(BF16)

    return pl.pallas_call(
        body, name=name, grid=(S // tr,), in_specs=[_row_spec(tr, F, 0), _row_spec(tr, F, 1)],
        out_specs=_row_spec(tr, F), out_shape=jax.ShapeDtypeStruct((S, F), BF16),
        compiler_params=_params(("parallel",)))(ab, ab)


def _swiglu_bwd(ab, df, name):
    S, F2 = ab.shape
    F = F2 // 2
    tr = _pick(S, (128, 64, 32, 16, 8))

    def body(a_ref, b_ref, df_ref, da_ref, db_ref):
        a = a_ref[...].astype(F32)
        sg = jax.nn.sigmoid(a)
        dfv = df_ref[...].astype(F32)
        da_ref[...] = (dfv * b_ref[...].astype(F32) * (sg * (1.0 + a * (1.0 - sg)))).astype(BF16)
        db_ref[...] = (dfv * a * sg).astype(BF16)

    def body2(a_ref, b_ref, df_ref, o_ref):
        body(a_ref, b_ref, df_ref, o_ref.at[:, pl.ds(0, F)], o_ref.at[:, pl.ds(F, F)])

    return pl.pallas_call(
        body2, name=name, grid=(S // tr,),
        in_specs=[_row_spec(tr, F, 0), _row_spec(tr, F, 1), _row_spec(tr, F)],
        out_specs=_row_spec(tr, F2), out_shape=jax.ShapeDtypeStruct((S, F2), BF16),
        compiler_params=_params(("parallel",)))(ab, ab, df)


def _shift_rows(v, n):
    return pltpu.roll(v, n, 0)


def _conv_fwd(p, w8, name):
    S, D3 = p.shape
    D = D3 // 3
    tr = _rows(S)
    nb8 = tr // 8

    def body(gb_ref, gc_ref, hx_ref, gcp_ref, hxp_ref, w_ref, o_ref):
        i = pl.program_id(0)
        y = gc_ref[...].astype(F32) * hx_ref[...].astype(F32)
        yp = jnp.where(i > 0, gcp_ref[...].astype(F32) * hxp_ref[...].astype(F32), 0.0)
        w0, w1, w2 = w_ref[0:1, :], w_ref[1:2, :], w_ref[2:3, :]
        conv = w0 * _shift_rows(y, 2) + w1 * _shift_rows(y, 1) + w2 * y
        o_ref[...] = (gb_ref[...].astype(F32) * conv).astype(BF16)
        rid = lax.broadcasted_iota(jnp.int32, (8, D), 0)
        y8 = y[0:8, :]
        y1 = jnp.where(rid < 1, _shift_rows(yp, 1), _shift_rows(y8, 1))
        y2 = jnp.where(rid < 2, _shift_rows(yp, 2), _shift_rows(y8, 2))
        conv8 = w0 * y2 + w1 * y1 + w2 * y8
        o_ref[0:8, :] = (gb_ref[0:8, :].astype(F32) * conv8).astype(BF16)

    def col(c):
        return pl.BlockSpec((tr, D), lambda i: (i, c))

    def prev8(c):
        return pl.BlockSpec((8, D), lambda i: (jnp.maximum(i * nb8 - 1, 0), c))

    return pl.pallas_call(
        body, name=name, grid=(S // tr,),
        in_specs=[col(0), col(1), col(2), prev8(1), prev8(2), _vec_spec(8, D)],
        out_specs=_row_spec(tr, D), out_shape=jax.ShapeDtypeStruct((S, D), BF16),
        compiler_params=_params(("parallel",)))(p, p, p, p, p, w8)


def _conv_bwd(p, do, w8, name):
    S, D3 = p.shape
    D = D3 // 3
    tr = _rows(S)
    nb8 = tr // 8
    nt = S // tr

    def body(gb_ref, gc_ref, hx_ref, gcp_ref, hxp_ref, do_ref, gbn_ref, don_ref, w_ref, dp_ref, dw_ref):
        i = pl.program_id(0)
        gb = gb_ref[...].astype(F32)
        gc = gc_ref[...].astype(F32)
        hx = hx_ref[...].astype(F32)
        dov = do_ref[...].astype(F32)
        y = gc * hx
        yp = jnp.where(i > 0, gcp_ref[...].astype(F32) * hxp_ref[...].astype(F32), 0.0)
        dconv = dov * gb
        dcn = jnp.where(i < nt - 1, don_ref[...].astype(F32) * gbn_ref[...].astype(F32), 0.0)
        w0, w1, w2 = w_ref[0:1, :], w_ref[1:2, :], w_ref[2:3, :]
        rid = lax.broadcasted_iota(jnp.int32, (tr, D), 0)
        rid8 = lax.broadcasted_iota(jnp.int32, (8, D), 0)
        yp1 = jnp.concatenate([_shift_rows(yp, 1), jnp.zeros((tr - 8, D), F32)], axis=0)
        yp2 = jnp.concatenate([_shift_rows(yp, 2), jnp.zeros((tr - 8, D), F32)], axis=0)
        y1 = jnp.where(rid < 1, yp1, _shift_rows(y, 1))
        y2 = jnp.where(rid < 2, yp2, _shift_rows(y, 2))
        conv = w0 * y2 + w1 * y1 + w2 * y
        dn1 = jnp.concatenate([jnp.zeros((tr - 8, D), F32), _shift_rows(dcn, 7)], axis=0)
        dn2 = jnp.concatenate([jnp.zeros((tr - 8, D), F32), _shift_rows(dcn, 6)], axis=0)
        d1 = jnp.where(rid >= tr - 1, dn1, _shift_rows(dconv, tr - 1))
        d2 = jnp.where(rid >= tr - 2, dn2, _shift_rows(dconv, tr - 2))
        dy = w2 * dconv + w1 * d1 + w0 * d2
        dp_ref[:, pl.ds(0, D)] = (dov * conv).astype(BF16)
        dp_ref[:, pl.ds(D, D)] = (dy * hx).astype(BF16)
        dp_ref[:, pl.ds(2 * D, D)] = (dy * gc).astype(BF16)

        @pl.when(i == 0)
        def _():
            dw_ref[...] = jnp.zeros_like(dw_ref)

        upd = jnp.where(rid8 == 0, jnp.sum(dconv * y2, axis=0, keepdims=True),
                        jnp.where(rid8 == 1, jnp.sum(dconv * y1, axis=0, keepdims=True),
                                  jnp.where(rid8 == 2, jnp.sum(dconv * y, axis=0, keepdims=True), 0.0)))
        dw_ref[...] += upd

    def col(c):
        return pl.BlockSpec((tr, D), lambda i: (i, c))

    def prev8(c):
        return pl.BlockSpec((8, D), lambda i: (jnp.maximum(i * nb8 - 1, 0), c))

    def next8(c):
        return pl.BlockSpec((8, D), lambda i: (jnp.minimum((i + 1) * nb8, S // 8 - 1), c))

    return pl.pallas_call(
        body, name=name, grid=(nt,),
        in_specs=[col(0), col(1), col(2), prev8(1), prev8(2), col(0), next8(0), next8(0), _vec_spec(8, D)],
        out_specs=(_row_spec(tr, D3), _vec_spec(8, D)),
        out_shape=(jax.ShapeDtypeStruct((S, D3), BF16), jax.ShapeDtypeStruct((8, D), F32)),
        compiler_params=_params(("arbitrary",)))(p, p, p, p, p, do, p, do, w8)


_GELU_C = math.sqrt(2.0 / math.pi)


def _gelu(v):
    return 0.5 * v * (1.0 + jnp.tanh(_GELU_C * (v + 0.044715 * v * v * v)))


def _gelu_grad(v):
    t = jnp.tanh(_GELU_C * (v + 0.044715 * v * v * v))
    return 0.5 * (1.0 + t) + 0.5 * v * (1.0 - t * t) * _GELU_C * (1.0 + 3.0 * 0.044715 * v * v)


def _tril(w):
    r = lax.broadcasted_iota(jnp.int32, (CHUNK, CHUNK), 0)
    c = lax.broadcasted_iota(jnp.int32, (CHUNK, CHUNK), 1)
    return jnp.where(r >= c, w, 0.0)


def _sgu_fwd(z, w, bT, n_attn, name):
    S = z.shape[0]
    G = w.shape[0]
    W = G * CHUNK
    tr = _pick(S, (512, 256, 128))
    ucol = 3 * n_attn * HEAD_DIM // W

    def body(u_ref, v_ref, w_ref, b_ref, o_ref):
        for g in range(G):
            wt = _tril(w_ref[g]).astype(BF16)
            for ci in range(tr // CHUNK):
                rows, cols = pl.ds(ci * CHUNK, CHUNK), pl.ds(g * CHUNK, CHUNK)
                gv = _gelu(v_ref[rows, cols].astype(F32)).astype(BF16)
                mixed = jnp.dot(wt, gv, preferred_element_type=F32) + b_ref[:, g:g + 1]
                o_ref[rows, cols] = (_gelu(u_ref[rows, cols].astype(F32)) * mixed).astype(BF16)

    return pl.pallas_call(
        body, name=name, grid=(S // tr,),
        in_specs=[_row_spec(tr, W, ucol), _row_spec(tr, W, ucol + 1),
                  pl.BlockSpec((G, CHUNK, CHUNK), lambda i: (0, 0, 0)), _vec_spec(CHUNK, G)],
        out_specs=_row_spec(tr, W), out_shape=jax.ShapeDtypeStruct((S, W), BF16),
        compiler_params=_params(("parallel",)))(z, z, w, bT)


def _sgu_bwd(z, dcat, w, bT, n_attn, name):
    S = z.shape[0]
    G = w.shape[0]
    W = G * CHUNK
    tr = _pick(S, (512, 256, 128))
    ucol = 3 * n_attn * HEAD_DIM // W
    dcol = n_attn * HEAD_DIM // W

    def body(u_ref, v_ref, d_ref, w_ref, b_ref, o_ref, dw_ref, db_ref):
        @pl.when(pl.program_id(0) == 0)
        def _():
            dw_ref[...] = jnp.zeros_like(dw_ref)
            db_ref[...] = jnp.zeros_like(db_ref)

        lane = lax.broadcasted_iota(jnp.int32, (CHUNK, G), 1)
        for g in range(G):
            wtf = _tril(w_ref[g])
            wt = wtf.astype(BF16)
            dw_acc = jnp.zeros((CHUNK, CHUNK), F32)
            db_acc = jnp.zeros((CHUNK, 1), F32)
            for ci in range(tr // CHUNK):
                rows, cols = pl.ds(ci * CHUNK, CHUNK), pl.ds(g * CHUNK, CHUNK)
                uv = u_ref[rows, cols].astype(F32)
                vv = v_ref[rows, cols].astype(F32)
                dov = d_ref[rows, cols]
                gv = _gelu(vv).astype(BF16)
                mixed = jnp.dot(wt, gv, preferred_element_type=F32) + b_ref[:, g:g + 1]
                dmixed = dov * _gelu(uv)
                dmb = dmixed.astype(BF16)
                dgv = lax.dot_general(wt, dmb, (((0,), (0,)), ((), ())), preferred_element_type=F32)
                o_ref[rows, cols] = (dov * mixed * _gelu_grad(uv)).astype(BF16)
                o_ref[rows, pl.ds(W + g * CHUNK, CHUNK)] = (dgv * _gelu_grad(vv)).astype(BF16)
                dw_acc += lax.dot_general(dmb, gv, (((1,), (1,)), ((), ())), preferred_element_type=F32)
                db_acc += jnp.sum(dmixed, axis=1, keepdims=True)
            dw_ref[g] += _tril(dw_acc)
            db_ref[...] += jnp.where(lane == g, db_acc, 0.0)

    return pl.pallas_call(
        body, name=name, grid=(S // tr,),
        in_specs=[_row_spec(tr, W, ucol), _row_spec(tr, W, ucol + 1), _row_spec(tr, W, dcol),
                  pl.BlockSpec((G, CHUNK, CHUNK), lambda i: (0, 0, 0)), _vec_spec(CHUNK, G)],
        out_specs=(_row_spec(tr, 2 * W), pl.BlockSpec((G, CHUNK, CHUNK), lambda i: (0, 0, 0)), _vec_spec(CHUNK, G)),
        out_shape=(jax.ShapeDtypeStruct((S, 2 * W), BF16), jax.ShapeDtypeStruct((G, CHUNK, CHUNK), F32),
                   jax.ShapeDtypeStruct((CHUNK, G), F32)),
        compiler_params=_params(("arbitrary",)))(z, z, dcat, w, bT)


def _rope(v, cs, sa, sb):
    return v * cs + pltpu.roll(v, HEAD_DIM - ROPE_DIM // 2, 1) * sa + pltpu.roll(v, ROPE_DIM // 2, 1) * sb


def _rope_t(d, cs, sa, sb):
    return d * cs + pltpu.roll(d * sa, ROPE_DIM // 2, 1) + pltpu.roll(d * sb, HEAD_DIM - ROPE_DIM // 2, 1)


def _qkv_prep(z, cs, sa, sb, n_attn, name):
    S = z.shape[0]
    A = n_attn * HEAD_DIM
    tr = _rows(S)

    def body(q_ref, k_ref, v_ref, c_ref, a_ref, b_ref, qo_ref, ko_ref, vo_ref):
        cv, av, bv = c_ref[...], a_ref[...], b_ref[...]
        for h in range(n_attn):
            cols = pl.ds(h * HEAD_DIM, HEAD_DIM)
            qo_ref[:, cols] = _rope(q_ref[:, cols].astype(F32), cv, av, bv)
            ko_ref[:, cols] = _rope(k_ref[:, cols].astype(F32), cv, av, bv)
        vo_ref[...] = v_ref[...].astype(F32)

    tab = _row_spec(tr, HEAD_DIM)
    out = jax.ShapeDtypeStruct((S, A), F32)
    return pl.pallas_call(
        body, name=name, grid=(S // tr,),
        in_specs=[_row_spec(tr, A, 0), _row_spec(tr, A, 1), _row_spec(tr, A, 2), tab, tab, tab],
        out_specs=(_row_spec(tr, A),) * 3, out_shape=(out,) * 3,
        compiler_params=_params(("parallel",)))(z, z, z, cs, sa, sb)


def _dqkv_post(dq, dk, dv, cs, sa, sb, n_attn, name):
    S, A = dq.shape
    tr = _rows(S)

    def body(q_ref, k_ref, v_ref, c_ref, a_ref, b_ref, o_ref):
        cv, av, bv = c_ref[...], a_ref[...], b_ref[...]
        for h in range(n_attn):
            cols = pl.ds(h * HEAD_DIM, HEAD_DIM)
            o_ref[:, pl.ds(h * HEAD_DIM, HEAD_DIM)] = _rope_t(q_ref[:, cols], cv, av, bv).astype(BF16)
            o_ref[:, pl.ds(A + h * HEAD_DIM, HEAD_DIM)] = _rope_t(k_ref[:, cols], cv, av, bv).astype(BF16)
        o_ref[:, pl.ds(2 * A, A)] = v_ref[...].astype(BF16)

    tab = _row_spec(tr, HEAD_DIM)
    return pl.pallas_call(
        body, name=name, grid=(S // tr,),
        in_specs=[_row_spec(tr, A)] * 3 + [tab, tab, tab],
        out_specs=_row_spec(tr, 3 * A), out_shape=jax.ShapeDtypeStruct((S, 3 * A), BF16),
        compiler_params=_params(("parallel",)))(dq, dk, dv, cs, sa, sb)


def _group(d, S):
    return 2 if S // (d * ATTN_BLOCK) >= 3 else 1


def _block_rows(d, S, it):
    G = _group(d, S)
    ngrp = S // (d * ATTN_BLOCK * G)
    r = it // ngrp
    jb = (it % ngrp) * G
    kb = jnp.maximum(jb - 1, 0)
    return r + d * ATTN_BLOCK * jb, r + d * ATTN_BLOCK * kb, ATTN_BLOCK * (jb - kb)


def _band(off, G):
    a = lax.broadcasted_iota(jnp.int32, (G * ATTN_BLOCK, (G + 1) * ATTN_BLOCK), 0) + off
    kj = lax.broadcasted_iota(jnp.int32, (G * ATTN_BLOCK, (G + 1) * ATTN_BLOCK), 1)
    return (kj <= a) & (kj >= a - ATTN_BLOCK)


_NT = (((1,), (1,)), ((), ()))
_TN = (((0,), (0,)), ((), ()))


def _attn_fwd(q, k, v, n_attn, name, job=None):
    S, A = q.shape
    scale = HEAD_DIM ** -0.5
    n_ji = len(job["ins"]) if job else 0
    n_jo = len(job["outs"]) if job else 0

    def body(q_hbm, k_hbm, v_hbm, *rest):
        jin, (o_hbm, lse_hbm), jout = rest[:n_ji], rest[n_ji:n_ji + 2], rest[n_ji + 2:n_ji + 2 + n_jo]
        qs, ks, vs, acc, ms, ls, ob, sem = rest[n_ji + 2 + n_jo:n_ji + 2 + n_jo + 8]
        jsems = rest[n_ji + 2 + n_jo + 8:]
        h = pl.program_id(0)
        if job:
            @pl.when(h == 0)
            def _():
                job["start"](jin, jout, jsems)

        cols = pl.ds(pl.multiple_of(h * HEAD_DIM, HEAD_DIM), HEAD_DIM)
        cps = [pltpu.make_async_copy(src.at[:, cols], dst, sem.at[i])
               for i, (src, dst) in enumerate(((q_hbm, qs), (k_hbm, ks), (v_hbm, vs)))]
        for cp in cps:
            cp.start()
        acc[...] = jnp.zeros_like(acc)
        ms[...] = jnp.full_like(ms, MASKED)
        ls[...] = jnp.zeros_like(ls)
        for cp in cps:
            cp.wait()
        for d in DILATIONS:
            G = _group(d, S)
            lanes = S // (ATTN_BLOCK * G * ATTN_LANES_FWD)

            def step(it, carry, d=d, G=G, lanes=lanes):
                results = []
                for u in range(ATTN_LANES_FWD):
                    q0, k0, off = _block_rows(d, S, it + u * lanes)
                    qrows = pl.ds(q0, G * ATTN_BLOCK, stride=d)
                    krows = pl.ds(k0, (G + 1) * ATTN_BLOCK, stride=d)
                    s = lax.dot_general(qs[qrows, :].astype(BF16), ks[krows, :].astype(BF16), _NT,
                                        preferred_element_type=F32) * scale
                    s = jnp.where(_band(off, G), s, MASKED)
                    m_old = ms[qrows, :]
                    m_new = jnp.maximum(m_old, jnp.max(s, axis=-1, keepdims=True))
                    alpha = jnp.exp(m_old - m_new)
                    p = jnp.exp(s - m_new)
                    l_new = alpha * ls[qrows, :] + jnp.sum(p, axis=-1, keepdims=True)
                    pv = jnp.dot(p.astype(BF16), vs[krows, :].astype(BF16), preferred_element_type=F32)
                    results.append((qrows, m_new, l_new, alpha * acc[qrows, :] + pv))
                for qrows, m_new, l_new, a_new in results:
                    ms[qrows, :] = m_new
                    ls[qrows, :] = l_new
                    acc[qrows, :] = a_new
                return carry
            lax.fori_loop(0, lanes, step, 0)
        ob[...] = (acc[...] / ls[...]).astype(BF16)
        ms[...] = ms[...] + jnp.log(ls[...])
        out = [pltpu.make_async_copy(ob, o_hbm.at[:, cols], sem.at[0]),
               pltpu.make_async_copy(ms, lse_hbm.at[h], sem.at[1])]
        for cp in out:
            cp.start()
        for cp in out:
            cp.wait()
        if job:
            @pl.when(h == n_attn - 1)
            def _():
                job["finish"](jin, jout, jsems)

    res = pl.pallas_call(
        body, name=name, grid=(n_attn,), in_specs=[ANY] * (3 + n_ji), out_specs=(ANY,) * (2 + n_jo),
        out_shape=(jax.ShapeDtypeStruct((S, A), BF16), jax.ShapeDtypeStruct((n_attn, S, 1), F32))
        + tuple(job["outs"] if job else ()),
        scratch_shapes=[pltpu.VMEM((S, HEAD_DIM), F32)] * 4 + [pltpu.VMEM((S, 1), F32)] * 2
        + [pltpu.VMEM((S, HEAD_DIM), BF16), pltpu.SemaphoreType.DMA((3,))] + list(job["sems"] if job else ()),
        compiler_params=_params(("arbitrary",)))(q, k, v, *(job["ins"] if job else ()))
    return res[0], res[1], list(res[2:])


def _attn_bwd(q, k, v, o, lse, dcat, n_attn, name):
    S, A = q.shape
    scale = HEAD_DIM ** -0.5

    def body(q_hbm, k_hbm, v_hbm, o_hbm, lse_hbm, do_hbm, dq_hbm, dk_hbm, dv_hbm,
             qs, ks, vs, dos, dqs, dks, dvs, lses, dls, ob, sem):
        h = pl.program_id(0)
        cols = pl.ds(pl.multiple_of(h * HEAD_DIM, HEAD_DIM), HEAD_DIM)
        cps = [pltpu.make_async_copy(src.at[:, cols], dst, sem.at[i])
               for i, (src, dst) in enumerate(((q_hbm, qs), (k_hbm, ks), (v_hbm, vs), (do_hbm, dos), (o_hbm, ob)))]
        cps.append(pltpu.make_async_copy(lse_hbm.at[h], lses, sem.at[5]))
        for cp in cps:
            cp.start()
        dqs[...] = jnp.zeros_like(dqs)
        dks[...] = jnp.zeros_like(dks)
        dvs[...] = jnp.zeros_like(dvs)
        for cp in cps:
            cp.wait()
        dls[...] = jnp.sum(dos[...] * ob[...].astype(F32), axis=-1, keepdims=True)
        for d in DILATIONS:
            G = _group(d, S)
            lanes = S // (ATTN_BLOCK * G * ATTN_LANES_BWD)
            assert lanes % 2 == 0

            def step(it, carry, d=d, G=G, lanes=lanes):
                results = []
                for u in range(ATTN_LANES_BWD):
                    q0, k0, off = _block_rows(d, S, it + u * lanes)
                    qrows = pl.ds(q0, G * ATTN_BLOCK, stride=d)
                    krows = pl.ds(k0, (G + 1) * ATTN_BLOCK, stride=d)
                    qb, kb = qs[qrows, :].astype(BF16), ks[krows, :].astype(BF16)
                    s = lax.dot_general(qb, kb, _NT, preferred_element_type=F32) * scale
                    p = jnp.where(_band(off, G), jnp.exp(s - lses[qrows, :]), 0.0)
                    dob = dos[qrows, :].astype(BF16)
                    dp = lax.dot_general(dob, vs[krows, :].astype(BF16), _NT, preferred_element_type=F32)
                    ds = (p * (dp - dls[qrows, :]) * scale).astype(BF16)
                    results.append((qrows, krows,
                                    dqs[qrows, :] + jnp.dot(ds, kb, preferred_element_type=F32),
                                    dks[krows, :] + lax.dot_general(ds, qb, _TN, preferred_element_type=F32),
                                    dvs[krows, :] + lax.dot_general(p.astype(BF16), dob, _TN,
                                                                    preferred_element_type=F32)))
                for qrows, krows, dq_new, dk_new, dv_new in results:
                    dqs[qrows, :] = dq_new
                    dks[krows, :] = dk_new
                    dvs[krows, :] = dv_new
                return carry
            lax.fori_loop(0, lanes, step, 0)
        out = [pltpu.make_async_copy(src, dst.at[:, cols], sem.at[i])
               for i, (src, dst) in enumerate(((dqs, dq_hbm), (dks, dk_hbm), (dvs, dv_hbm)))]
        for cp in out:
            cp.start()
        for cp in out:
            cp.wait()

    grad = jax.ShapeDtypeStruct((S, A), F32)
    return pl.pallas_call(
        body, name=name, grid=(n_attn,), in_specs=[ANY] * 6, out_specs=(ANY, ANY, ANY), out_shape=(grad,) * 3,
        scratch_shapes=[pltpu.VMEM((S, HEAD_DIM), F32)] * 7 + [pltpu.VMEM((S, 1), F32)] * 2
        + [pltpu.VMEM((S, HEAD_DIM), BF16), pltpu.SemaphoreType.DMA((6,))],
        compiler_params=_params(("arbitrary",)))(q, k, v, o, lse, dcat)


def _ada_fwd(c_act, ada_w, name):
    L, D, n = ada_w.shape
    tn = _pick(n, (512, 256, 128))

    def body(c_ref, w_ref, o_ref):
        o_ref[...] = jnp.dot(c_ref[...], w_ref[...], preferred_element_type=F32)

    return pl.pallas_call(
        body, name=name, grid=(L, n // tn),
        in_specs=[pl.BlockSpec((N_DEV, D), lambda l, j: (0, 0)), pl.BlockSpec((None, D, tn), lambda l, j: (l, 0, j))],
        out_specs=pl.BlockSpec((None, N_DEV, tn), lambda l, j: (l, 0, j)),
        out_shape=jax.ShapeDtypeStruct((L, N_DEV, n), F32),
        compiler_params=_params(("parallel", "parallel")))(c_act, ada_w)


def _ada_bwd(c_act, dmod, name):
    L, _, n = dmod.shape
    D = c_act.shape[1]
    tn = _pick(n, (512, 256, 128))

    def body(c_ref, d_ref, o_ref):
        o_ref[...] = lax.dot_general(c_ref[...], d_ref[...], _TN, preferred_element_type=F32)

    return pl.pallas_call(
        body, name=name, grid=(L, n // tn),
        in_specs=[pl.BlockSpec((N_DEV, D), lambda l, j: (0, 0)), pl.BlockSpec((None, N_DEV, tn), lambda l, j: (l, 0, j))],
        out_specs=pl.BlockSpec((None, D, tn), lambda l, j: (l, 0, j)),
        out_shape=jax.ShapeDtypeStruct((L, D, n), F32),
        compiler_params=_params(("parallel", "parallel")))(c_act, dmod)


def _adamw(w, g, m, v, name, job=None):
    shape = w.shape
    C = shape[-1]
    R = w.size // C
    w2, g2, m2, v2 = (t.reshape(R, C) for t in (w, g, m, v))
    tr = _pick(R, (256, 128, 64, 32, 16, 8))
    tc = _pick(C, (2048, 1536, 1408, 1024, 512, 256, 128))
    n_ji = len(job["ins"]) if job else 0
    n_jo = len(job["outs"]) if job else 0
    grid = (R // tr, C // tc)

    def body(w_ref, g_ref, m_ref, v_ref, *rest):
        jin, (d_ref, mo_ref, vo_ref) = rest[:n_ji], rest[n_ji:n_ji + 3]
        jout, jsems = rest[n_ji + 3:n_ji + 3 + n_jo], rest[n_ji + 3 + n_jo:]
        i, j = pl.program_id(0), pl.program_id(1)
        if job:
            @pl.when((i == 0) & (j == 0))
            def _():
                job["start"](jin, jout, jsems)

        gv = g_ref[...]
        mn = ADAM_B1 * m_ref[...] + (1.0 - ADAM_B1) * gv
        vn = ADAM_B2 * v_ref[...] + (1.0 - ADAM_B2) * (gv * gv)
        m_hat = mn / (1.0 - ADAM_B1 ** ADAM_STEP)
        v_hat = vn / (1.0 - ADAM_B2 ** ADAM_STEP)
        d_ref[...] = -ADAM_LR * (m_hat / (jnp.sqrt(v_hat) + ADAM_EPS) + ADAM_WD * w_ref[...])
        mo_ref[...] = mn
        vo_ref[...] = vn
        if job:
            @pl.when((i == grid[0] - 1) & (j == grid[1] - 1))
            def _():
                job["finish"](jin, jout, jsems)

    spec = pl.BlockSpec((tr, tc), lambda i, j: (i, j))
    out = jax.ShapeDtypeStruct((R, C), F32)
    d, mn, vn, *got = pl.pallas_call(
        body, name=name, grid=grid, in_specs=[spec] * 4 + [ANY] * n_ji, out_specs=(spec,) * 3 + (ANY,) * n_jo,
        out_shape=(out,) * 3 + tuple(job["outs"] if job else ()), scratch_shapes=list(job["sems"] if job else ()),
        compiler_params=_params(("arbitrary", "arbitrary") if job else ("parallel", "parallel")),
    )(w2, g2, m2, v2, *(job["ins"] if job else ()))
    res = (d.reshape(shape), mn.reshape(shape), vn.reshape(shape))
    return res + (got,) if job else res


def _sum_leading(t, name):
    n, R, C = t.shape
    tr = _pick(R, (256, 128, 64, 32, 16, 8))

    def body(t_ref, o_ref):
        acc = t_ref[0]
        for i in range(1, n):
            acc = acc + t_ref[i]
        o_ref[...] = acc

    return pl.pallas_call(
        body, name=name, grid=(R // tr,), in_specs=[pl.BlockSpec((n, tr, C), lambda i: (0, i, 0))],
        out_specs=pl.BlockSpec((tr, C), lambda i: (i, 0)), out_shape=jax.ShapeDtypeStruct((R, C), F32),
        compiler_params=_params(("parallel",)))(t)


def _coords():
    return lax.axis_index("x"), lax.axis_index("y"), lax.axis_index("c")


def _other_chips(x, y):
    return [(1 - x, y), (x, 1 - y), (1 - x, 1 - y)]


def _all_gather8(t, name):
    R, C = t.shape

    def body(x_ref, out_ref, send_sems, recv_sems, local_sem):
        x, y, c = _coords()
        me, sibling = (x, y, c), (x, y, 1 - c)
        chips = _other_chips(x, y)

        def slot(px, py, pc):
            return out_ref.at[4 * px + 2 * py + pc]

        def copy(k, block, to, src=None):
            return pltpu.make_async_remote_copy(
                src_ref=slot(*block) if src is None else src, dst_ref=slot(*block),
                send_sem=send_sems.at[k], recv_sem=recv_sems.at[k], device_id=to, device_id_type=MESH)

        mine = pltpu.make_async_copy(x_ref, slot(*me), local_sem)
        mine.start()
        first = [copy(0, me, sibling, src=x_ref)]
        first += [copy(1 + j, me, (*chip, c), src=x_ref) for j, chip in enumerate(chips)]
        for cp in first:
            cp.start()
        passed = [copy(4 + j, (*chip, c), sibling) for j, chip in enumerate(chips)]
        for j, chip in enumerate(chips):
            copy(1 + j, (*chip, c), me).wait_recv()
            passed[j].start()
        copy(0, sibling, me).wait_recv()
        for j, chip in enumerate(chips):
            copy(4 + j, (*chip, 1 - c), me).wait_recv()
        for cp in first + passed:
            cp.wait_send()
        mine.wait()

    return pl.pallas_call(
        body, name=name, out_shape=jax.ShapeDtypeStruct((N_DEV, R, C), t.dtype),
        in_specs=[pl.BlockSpec(memory_space=pltpu.VMEM)], out_specs=pl.BlockSpec(memory_space=pltpu.VMEM),
        scratch_shapes=[pltpu.SemaphoreType.DMA((7,)), pltpu.SemaphoreType.DMA((7,)), pltpu.SemaphoreType.DMA],
        compiler_params=pltpu.CompilerParams(vmem_limit_bytes=VMEM_LIMIT_BYTES))(t)


def _window(ref, r0, nr, c0, nc):
    return ref.at[pl.ds(r0, nr), pl.ds(c0, nc)]


LOCAL_CHUNKS = 4


def _gather_job(pieces, out_shape):
    n = len(pieces)

    def ctx(ins, outs, sems):
        x, y, c = _coords()
        buf = outs[0]

        def place(p, chip_idx, r0, nr):
            _, _, kind, base = pieces[p]
            r, cs = ins[p].shape[1], ins[p].shape[2]
            if kind == "row":
                return _window(buf, base + chip_idx * r + r0, nr, 0, cs)
            return _window(buf, r0, nr, base + chip_idx * cs, cs)

        def ici(p, j, chip, src, dst):
            return pltpu.make_async_remote_copy(
                src_ref=src, dst_ref=dst, send_sem=sems[0].at[3 * p + j], recv_sem=sems[1].at[3 * p + j],
                device_id=(*chip, c), device_id_type=MESH)

        def d2d(p, j, win):
            return pltpu.make_async_remote_copy(
                src_ref=win, dst_ref=win, send_sem=sems[2].at[3 * p + j], recv_sem=sems[3].at[3 * p + j],
                device_id=(x, y, 1 - c), device_id_type=MESH)

        def local(p):
            lidx, r = pieces[p][1], ins[p].shape[1]
            rc = r // (2 * LOCAL_CHUNKS)
            return [pltpu.make_async_copy(ins[p].at[lidx, pl.ds(q * rc, rc), :], place(p, 2 * x + y, q * rc, rc),
                                          sems[4].at[2 * LOCAL_CHUNKS * p + q]) for q in range(2 * LOCAL_CHUNKS)]

        return x, y, c, _other_chips(x, y), place, ici, d2d, local

    def start(ins, outs, sems):
        x, y, c, chips, place, ici, d2d, local = ctx(ins, outs, sems)
        for p in range(n):
            lidx, rh = pieces[p][1], ins[p].shape[1] // 2
            for j, chip in enumerate(chips):
                ici(p, j, chip, ins[p].at[lidx, pl.ds(c * rh, rh), :], place(p, 2 * x + y, c * rh, rh)).start()
        for p in range(n):
            for cp in local(p):
                cp.start()

    def finish(ins, outs, sems):
        x, y, c, chips, place, ici, d2d, local = ctx(ins, outs, sems)
        for p in range(n):
            rh = ins[p].shape[1] // 2
            for j, chip in enumerate(chips):
                landed = place(p, 2 * chip[0] + chip[1], c * rh, rh)
                ici(p, j, chip, landed, landed).wait_recv()
                d2d(p, j, landed).start()
        for p in range(n):
            lidx, rh = pieces[p][1], ins[p].shape[1] // 2
            for j, chip in enumerate(chips):
                theirs = place(p, 2 * chip[0] + chip[1], (1 - c) * rh, rh)
                d2d(p, j, theirs).wait_recv()
                d2d(p, j, place(p, 2 * chip[0] + chip[1], c * rh, rh)).wait_send()
                ici(p, j, chip, ins[p].at[lidx, pl.ds(c * rh, rh), :], place(p, 2 * x + y, c * rh, rh)).wait_send()
            for cp in local(p):
                cp.wait()

    return dict(
        ins=[p[0] for p in pieces], outs=[jax.ShapeDtypeStruct(out_shape, BF16)], start=start, finish=finish,
        sems=[pltpu.SemaphoreType.DMA((3 * n,))] * 4 + [pltpu.SemaphoreType.DMA((2 * LOCAL_CHUNKS * n,))])


def _scatter_job(parts, plan):
    n = len(parts)

    def shard_shape(i):
        kind, _, size = plan[i]
        R, C = parts[i].shape
        return (size, C) if kind == "row" else (R, size)

    def copies(ins, outs, sems):
        x, y, c = _coords()
        cps = []
        for i in range(n):
            kind, base, size = plan[i]
            R, C = ins[i].shape
            for j, chip in enumerate(_other_chips(x, y)):
                their = 2 * chip[0] + chip[1]
                if kind == "row":
                    src = _window(ins[i], base + their * size, size, 0, C)
                else:
                    src = _window(ins[i], 0, R, base + their * size, size)
                cps.append(pltpu.make_async_remote_copy(
                    src_ref=src, dst_ref=outs[i].at[j], send_sem=sems[0].at[3 * i + j],
                    recv_sem=sems[1].at[3 * i + j], device_id=(*chip, c), device_id_type=MESH))
        return cps

    def start(ins, outs, sems):
        for cp in copies(ins, outs, sems):
            cp.start()

    def finish(ins, outs, sems):
        for cp in copies(ins, outs, sems):
            cp.wait()

    return dict(
        ins=list(parts), outs=[jax.ShapeDtypeStruct((3,) + shard_shape(i), F32) for i in range(n)],
        start=start, finish=finish, sems=[pltpu.SemaphoreType.DMA((3 * n,))] * 2)


def _swap_job(t):
    def copy(ins, outs, sems):
        x, y, c = _coords()
        return pltpu.make_async_remote_copy(
            src_ref=ins[0].at[1 - c], dst_ref=outs[0], send_sem=sems[0].at[0], recv_sem=sems[1].at[0],
            device_id=(x, y, 1 - c), device_id_type=MESH)

    return dict(ins=[t], outs=[jax.ShapeDtypeStruct(t.shape[1:], t.dtype)],
                start=lambda ins, outs, sems: copy(ins, outs, sems).start(),
                finish=lambda ins, outs, sems: copy(ins, outs, sems).wait(),
                sems=[pltpu.SemaphoreType.DMA((1,))] * 2)


def _join_jobs(jobs):
    jobs = [j for j in jobs if j]
    if not jobs:
        return None
    if len(jobs) == 1:
        return jobs[0]

    def each(fn_name, ins, outs, sems):
        i = o = s = 0
        for j in jobs:
            ni, no, ns = len(j["ins"]), len(j["outs"]), len(j["sems"])
            j[fn_name](ins[i:i + ni], outs[o:o + no], sems[s:s + ns])
            i, o, s = i + ni, o + no, s + ns

    return dict(ins=[a for j in jobs for a in j["ins"]], outs=[a for j in jobs for a in j["outs"]],
                sems=[a for j in jobs for a in j["sems"]],
                start=lambda ins, outs, sems: each("start", ins, outs, sems),
                finish=lambda ins, outs, sems: each("finish", ins, outs, sems))


def _hosted(call, jobs):
    jobs = [j for j in jobs if j]
    if not jobs:
        return call(None), []
    out, *rest = call(_join_jobs(jobs))
    per = []
    for j in jobs:
        n = len(j["outs"])
        per.append(rest[:n])
        rest = rest[n:]
    return out, per


def _run_job(job, name):
    n_i, n_o = len(job["ins"]), len(job["outs"])

    def body(*refs):
        ins, outs, sems = refs[:n_i], refs[n_i:n_i + n_o], refs[n_i + n_o:]
        job["start"](ins, outs, sems)
        job["finish"](ins, outs, sems)

    return pl.pallas_call(
        body, name=name, in_specs=[ANY] * n_i, out_specs=tuple([ANY] * n_o), out_shape=tuple(job["outs"]),
        scratch_shapes=list(job["sems"]),
        compiler_params=pltpu.CompilerParams(vmem_limit_bytes=VMEM_LIMIT_BYTES))(*job["ins"])


def _sibling_share(both, name):
    n = len(both)

    def body(*refs):
        outs = refs[n:2 * n]
        send_sems, recv_sems = refs[2 * n:]
        x, y, c = _coords()
        cps = []
        for i in range(n):
            cp = pltpu.make_async_remote_copy(
                src_ref=outs[i].at[c], dst_ref=outs[i].at[c], send_sem=send_sems.at[i], recv_sem=recv_sems.at[i],
                device_id=(x, y, 1 - c), device_id_type=MESH)
            cp.start()
            cps.append(cp)
        for cp in cps:
            cp.wait()

    return pl.pallas_call(
        body, name=name, in_specs=[ANY] * n, out_specs=tuple([ANY] * n),
        out_shape=tuple(jax.ShapeDtypeStruct(b.shape, b.dtype) for b in both),
        input_output_aliases={i: i for i in range(n)},
        scratch_shapes=[pltpu.SemaphoreType.DMA((n,))] * 2,
        compiler_params=pltpu.CompilerParams(vmem_limit_bytes=VMEM_LIMIT_BYTES))(*both)


def _add_half(full3, recv, core, name):
    _, Rh, C = full3.shape
    tr = _pick(Rh, (256, 176, 128, 64, 32, 16, 8))
    tc = _pick(C, (2048, 1536, 1408, 1024, 512, 256, 128))

    def body(c_ref, a_ref, b_ref, o_ref):
        o_ref[...] = a_ref[...] + b_ref[...]

    return pl.pallas_call(
        body, name=name,
        grid_spec=pltpu.PrefetchScalarGridSpec(
            num_scalar_prefetch=1, grid=(Rh // tr, C // tc),
            in_specs=[pl.BlockSpec((None, tr, tc), lambda i, j, cr: (cr[0], i, j)),
                      pl.BlockSpec((tr, tc), lambda i, j, cr: (i, j))],
            out_specs=pl.BlockSpec((tr, tc), lambda i, j, cr: (i, j))),
        out_shape=jax.ShapeDtypeStruct((Rh, C), F32),
        compiler_params=_params(("parallel", "parallel")))(core, full3, recv)


def _add_scattered(part, recv, kind, base, size, core_chip, name):
    _, rs, cs = recv.shape
    tr = _pick(rs, (256, 176, 128, 64, 32, 16, 8))
    tc = _pick(cs, (2048, 1536, 1408, 1024, 512, 256, 128))
    assert base % size == 0
    if kind == "row":
        pidx = lambda i, j, cr: ((base // size + cr[1]) * (rs // tr) + i, j)
    else:
        pidx = lambda i, j, cr: (i, (base // size + cr[1]) * (cs // tc) + j)

    def body(c_ref, a_ref, r_ref, o_ref):
        o_ref[...] = ((a_ref[...] + r_ref[0]) + r_ref[1]) + r_ref[2]

    return pl.pallas_call(
        body, name=name,
        grid_spec=pltpu.PrefetchScalarGridSpec(
            num_scalar_prefetch=1, grid=(rs // tr, cs // tc),
            in_specs=[pl.BlockSpec((tr, tc), pidx), pl.BlockSpec((3, tr, tc), lambda i, j, cr: (0, i, j))],
            out_specs=pl.BlockSpec((None, tr, tc), lambda i, j, cr: (cr[0], i, j))),
        out_shape=jax.ShapeDtypeStruct((2, rs, cs), F32),
        compiler_params=_params(("parallel", "parallel")))(core_chip, part, recv)


def _rs_split(g, windows):
    R, C = g.shape
    if windows[0][0] == "row":
        size = windows[0][2]
        t = g.reshape(N_CHIPS, 2, size // 2, C).transpose(1, 0, 2, 3).reshape(2, R // 2, C)
        return t, [(k, b // 2, s // 2) for k, b, s in windows]
    return g.reshape(2, R // 2, C), list(windows)


def _rs_finish(part, windows, got, core_chip, tag):
    both = [_add_scattered(part, r, k, b, s, core_chip, f"rs_add_{tag}_{n}")
            for n, (r, (k, b, s)) in enumerate(zip(got, windows))]
    both = _sibling_share(both, f"rs_share_{tag}")
    return [t.reshape(2 * t.shape[1], t.shape[2]) for t in both]


def _rope_tables(positions, S):
    half = ROPE_DIM // 2
    inv_freq = ROPE_THETA ** (-jnp.arange(0, ROPE_DIM, 2, dtype=F32) / ROPE_DIM)
    ang = positions.reshape(S, 1).astype(F32) * inv_freq[None, :]
    cos, sin = jnp.cos(ang), jnp.sin(ang)
    zeros = jnp.zeros((S, half), F32)
    rest0 = jnp.zeros((S, HEAD_DIM - ROPE_DIM), F32)
    cs = jnp.concatenate([cos, cos, jnp.ones((S, HEAD_DIM - ROPE_DIM), F32)], axis=1)
    sa = jnp.concatenate([-sin, zeros, rest0], axis=1)
    sb = jnp.concatenate([zeros, sin, rest0], axis=1)
    return cs, sa, sb


def kernel(x, c, positions, ada_w, ada_b, norm_mix, norm_ffn, ab_w_in, sgu_w, sgu_b, ab_w_out, conv_w_in, conv_w, conv_w_out, ffn_w_gate, ffn_w_up, ffn_w_down, final_norm, loss_target, m_ada_w, m_ada_b, m_norm_mix, m_norm_ffn, m_ab_w_in, m_sgu_w, m_sgu_b, m_ab_w_out, m_conv_w_in, m_conv_w, m_conv_w_out, m_ffn_w_gate, m_ffn_w_up, m_ffn_w_down, m_final_norm, v_ada_w, v_ada_b, v_norm_mix, v_norm_ffn, v_ab_w_in, v_sgu_w, v_sgu_b, v_ab_w_out, v_conv_w_in, v_conv_w, v_conv_w_out, v_ffn_w_gate, v_ffn_w_up, v_ffn_w_down, v_final_norm):
    S, D = x.shape[1], x.shape[2]
    L = ada_w.shape[0]
    n_mix_heads = D // HEAD_DIM
    n_attn = 3 * n_mix_heads // 4
    A = n_attn * HEAD_DIM
    G = n_mix_heads - n_attn
    F = ffn_w_gate.shape[2] * N_CHIPS
    mix_in = ab_w_in.shape[2] * N_CHIPS
    xi, yi, ci = _coords()
    chip = 2 * xi + yi
    dev = 4 * xi + 2 * yi + ci
    core1 = jnp.reshape(ci, (1,)).astype(jnp.int32)
    chip1 = jnp.reshape(chip, (1,)).astype(jnp.int32)
    x2 = x.reshape(S, D)
    target = loss_target.reshape(S, D)

    n_conv = conv_w.size
    w0 = D + n_conv
    w0p = -(-w0 // 128) * 128
    pack = jnp.zeros((8, w0p), F32).at[0, :D].set(c[0]).at[0, D:w0].set(conv_w.reshape(-1))
    g0 = _all_gather8(pack, "gather_cond")
    c_all = g0[:, 0, :D]
    c_act = c_all * jax.nn.sigmoid(c_all)
    conv_full = jnp.concatenate(
        [g0[2 * j, 0, D:w0].reshape(conv_w.shape) for j in range(N_CHIPS)], axis=2)
    mod_part = _ada_fwd(c_act, ada_w, "ada_fwd")
    n_ada = ada_w.shape[2]
    g1 = _all_gather8(mod_part.reshape(L * N_DEV, n_ada), "gather_mod")
    mod_all = jnp.concatenate([g1[2 * j].reshape(L, N_DEV, n_ada) for j in range(N_CHIPS)], axis=2)
    mod = lax.dynamic_index_in_dim(mod_all, dev, axis=1, keepdims=False) + ada_b
    mods = mod.reshape(L, 6, 1, D)
    cs, sa, sb = _rope_tables(positions, S)

    bf = lambda t: t.astype(BF16)
    w_in_e, w_out_e = bf(ab_w_in), bf(ab_w_out)
    w_in_o, w_out_o = bf(conv_w_in), bf(conv_w_out)
    w_gate, w_up, w_down = bf(ffn_w_gate), bf(ffn_w_up), bf(ffn_w_down)
    def gather_jobs(l):
        i = l // 2
        first, n_in_cols = ((w_in_e, w_out_e), mix_in) if l % 2 == 0 else ((w_in_o, w_out_o), 3 * D)
        return [_gather_job([(first[0], i, "col", 0)], (D, n_in_cols)),
                _gather_job([(first[1], i, "row", 0)], (D, D)),
                _gather_job([(w_gate, l, "col", 0), (w_up, l, "col", F)], (D, 2 * F)),
                _gather_job([(w_down, l, "row", 0)], (F, D))]

    first_jobs = gather_jobs(0)
    layer_w = [[_run_job(first_jobs[0], "gather_w0_in")[0], None, None, None]]

    saved = []
    xc = x2
    pending = None
    for l in range(L):
        i = l // 2
        nxt = gather_jobs(l + 1) if l + 1 < L else [None] * 4
        if l + 1 < L:
            layer_w.append([None] * 4)

        def mm_fwd(a, b, out_dtype, name, slot, also=None):
            jobs = [nxt[slot], first_jobs[also] if also is not None else None]
            out, got = _hosted(lambda job: _mm(a, b, "nn", out_dtype, name, job=job), jobs)
            if nxt[slot] is not None:
                layer_w[l + 1][slot] = got.pop(0)[0]
            if also is not None:
                layer_w[l][also] = got.pop(0)[0]
            return out

        sh_m, sc_m, g_m, sh_f, sc_f, g_f = (mods[l, t] for t in range(6))
        weff_m = norm_mix[l][None, :] * (1.0 + sc_m)
        weff_f = norm_ffn[l][None, :] * (1.0 + sc_f)
        if pending is None:
            _, h = _norm_mod(xc, None, None, weff_m, sh_m, f"norm_mix{l}")
        else:
            xc, h = _norm_mod(xc, pending[0], pending[1], weff_m, sh_m, f"norm_mix{l}")
        z = mm_fwd(h, layer_w[l][0], BF16, f"mm_in{l}", 0, also=1 if l == 0 else None)
        st = dict(x=xc, h=h, z=z, weff_m=weff_m, weff_f=weff_f, g_m=g_m, g_f=g_f, sc_m=sc_m, sc_f=sc_f)
        if l % 2 == 0:
            q, k, v = _qkv_prep(z, cs, sa, sb, n_attn, f"qkv_prep{l}")
            o, lse, got = _attn_fwd(q, k, v, n_attn, f"attn_fwd{l}",
                                    job=_join_jobs(first_jobs[2:]) if l == 0 else None)
            if l == 0:
                layer_w[0][2], layer_w[0][3] = got
            bT = sgu_b[i].T
            so = _sgu_fwd(z, sgu_w[i], bT, n_attn, f"sgu_fwd{l}")
            cat = jnp.concatenate([o, so], axis=1)
            st.update(q=q, k=k, v=v, o=o, lse=lse, bT=bT)
        else:
            w8 = jnp.zeros((8, D), F32).at[:3].set(conv_full[i])
            cat = _conv_fwd(z, w8, f"conv_fwd{l}")
            st.update(w8=w8)
        mix = mm_fwd(cat, layer_w[l][1], F32, f"mm_out{l}", 1)
        x1, h2 = _norm_mod(xc, mix, g_m, weff_f, sh_f, f"norm_ffn{l}")
        ab = mm_fwd(h2, layer_w[l][2], BF16, f"mm_gu{l}", 2)
        f = _swiglu(ab, f"swiglu{l}")
        yv = mm_fwd(f, layer_w[l][3], F32, f"mm_down{l}", 3)
        st.update(cat=cat, mix=mix, x1=x1, h2=h2, ab=ab, f=f, y=yv)
        saved.append(st)
        xc = x1
        pending = (yv, g_f)

    dx, loss11, dfinal, dy, dg_f = _loss_head(xc, pending[0], pending[1], final_norm[None, :], target, "loss_head")
    loss = lax.psum(loss11[0, 0], ("x", "y", "c"))

    dmods = [None] * L
    dnorm_mix, dnorm_ffn = [None] * L, [None] * L
    big = {l: {} for l in range(L)}
    core_chip = jnp.concatenate([core1, chip1])
    dsgu_w, dsgu_b, dconv = [None] * (L - L // 2), [None] * (L - L // 2), [None] * (L // 2)

    def mm_bwd(a, b, mode, out_dtype, name, scatters=(), swap=None):
        jobs = [_scatter_job([p[0]] * len(p[1]), p[1]) for p in scatters]
        jobs += [_swap_job(swap[0])] if swap else []
        out, got = _hosted(lambda job: _mm(a, b, mode, out_dtype, name, job=job), jobs)
        for (part, windows, lay, keys), recv in zip(scatters, got):
            for key, red in zip(keys, _rs_finish(part, windows, recv, core_chip, f"{lay}_{keys[0]}")):
                big[lay][key] = red
        half = _add_half(swap[0], got[-1][0], core_chip, f"rs_add_half_{swap[1]}") if swap else None
        return out, half

    above = None
    for l in reversed(range(L)):
        i = l // 2
        st = saved[l]
        w_in, w_out, w_gu, w_dn = layer_w[l]
        if above is None:
            df, _ = mm_bwd(dy, w_dn, "nt", BF16, f"mm_down_dx{l}")
            dw_dn, _ = mm_bwd(st["f"], dy, "tn", F32, f"mm_down_dw{l}")
        else:
            s_out, t_in, win_in = above
            df, p_in = mm_bwd(dy, w_dn, "nt", BF16, f"mm_down_dx{l}", [s_out], (t_in, f"{l + 1}_in"))
            dw_dn, _ = mm_bwd(st["f"], dy, "tn", F32, f"mm_down_dw{l}", [(p_in, win_in, l + 1, ["in"])])
        t_dn, win_dn = _rs_split(dw_dn, [("row", 0, F // N_CHIPS)])
        dab = _swiglu_bwd(st["ab"], df, f"swiglu_bwd{l}")
        dh2, p_dn = mm_bwd(dab, w_gu, "nt", F32, f"mm_gu_dx{l}", swap=(t_dn, f"{l}_down"))
        dw_gu, _ = mm_bwd(st["h2"], dab, "tn", F32, f"mm_gu_dw{l}", [(p_dn, win_dn, l, ["down"])])
        t_gu, win_gu = _rs_split(dw_gu, [("col", 0, F // N_CHIPS), ("col", F, F // N_CHIPS)])
        dx1, dsh_f, dweff_f, dmix, dg_m = _norm_mod_bwd(dh2, st["x1"], st["weff_f"], dx, st["mix"], st["g_m"],
                                                        f"norm_ffn_bwd{l}")
        dcat, p_gu = mm_bwd(dmix, w_out, "nt", F32, f"mm_out_dx{l}", swap=(t_gu, f"{l}_gu"))
        dw_out = _mm(st["cat"], dmix, "tn", F32, f"mm_out_dw{l}")
        t_out, win_out = _rs_split(dw_out, [("row", 0, D // N_CHIPS)])
        if l % 2 == 0:
            dq, dk, dv = _attn_bwd(st["q"], st["k"], st["v"], st["o"], st["lse"], dcat, n_attn, f"attn_bwd{l}")
            dqkv = _dqkv_post(dq, dk, dv, cs, sa, sb, n_attn, f"dqkv_post{l}")
            duv, dsgu_w[i], dbT = _sgu_bwd(st["z"], dcat, sgu_w[i], st["bT"], n_attn, f"sgu_bwd{l}")
            dsgu_b[i] = dbT.T
            dz = jnp.concatenate([dqkv, duv], axis=1)
        else:
            dz, dw8 = _conv_bwd(st["z"], dcat, st["w8"], f"conv_bwd{l}")
            dconv[i] = dw8[:3]
        dh, p_out = mm_bwd(dz, w_in, "nt", F32, f"mm_in_dx{l}", [(p_gu, win_gu[:1], l, ["gate"])],
                           (t_out, f"{l}_out"))
        s_out = (p_out, win_out, l, ["out"])
        dw_in, _ = mm_bwd(st["h"], dz, "tn", F32, f"mm_in_dw{l}",
                          [(p_gu, win_gu[1:], l, ["up"])] + ([s_out] if l == 0 else []))
        t_in, win_in = _rs_split(dw_in, [("col", 0, w_in.shape[1] // N_CHIPS)])
        above = (s_out, t_in, win_in)
        dmod_f = [dsh_f, dweff_f * norm_ffn[l][None, :], dg_f]
        if l > 0:
            dx, dsh_m, dweff_m, dy, dg_f = _norm_mod_bwd(dh, st["x"], st["weff_m"], dx1, saved[l - 1]["y"],
                                                         saved[l - 1]["g_f"], f"norm_mix_bwd{l}")
        else:
            dx, dsh_m, dweff_m = _norm_mod_bwd(dh, st["x"], st["weff_m"], dx1, None, None, f"norm_mix_bwd{l}")
        dmods[l] = jnp.concatenate([dsh_m, dweff_m * norm_mix[l][None, :], dg_m] + dmod_f, axis=1)
        dnorm_mix[l] = dweff_m * (1.0 + st["sc_m"])
        dnorm_ffn[l] = dweff_f * (1.0 + st["sc_f"])
    _, t_in, win_in = above
    (recv_in,) = _run_job(_swap_job(t_in), "rs_swap_0_in")
    p_in0 = _add_half(t_in, recv_in, core_chip, "rs_add_half_0_in")
    grad_x = dx.reshape(1, S, D)

    dmod = jnp.concatenate(dmods, axis=0)
    small = [dmod.reshape(-1), jnp.concatenate(dnorm_mix, 0).reshape(-1), jnp.concatenate(dnorm_ffn, 0).reshape(-1),
             jnp.stack(dsgu_w).reshape(-1), jnp.stack(dsgu_b).reshape(-1), jnp.stack(dconv).reshape(-1), dfinal.reshape(-1)]
    sizes = [t.size for t in small]
    flat = jnp.concatenate(small)
    n_flat = flat.size
    rows = -(-n_flat // (128 * 8)) * 8
    flat = jnp.concatenate([flat, jnp.zeros((rows * 128 - n_flat,), F32)]).reshape(rows, 128)
    g2 = _all_gather8(flat, "gather_small")
    tot = _sum_leading(g2, "sum_small").reshape(-1)
    offs = [0]
    for s in sizes:
        offs.append(offs[-1] + s)
    take = lambda n, shape: tot[offs[n]:offs[n + 1]].reshape(shape)
    g_ada_b = take(0, ada_b.shape)
    g_norm_mix = take(1, norm_mix.shape)
    g_norm_ffn = take(2, norm_ffn.shape)
    g_sgu_w = take(3, sgu_w.shape)
    g_sgu_b = take(4, sgu_b.shape)
    g_conv_full = take(5, conv_full.shape)
    n_cw = conv_w.shape[2]
    g_conv_w = lax.dynamic_slice_in_dim(g_conv_full, chip * n_cw, n_cw, axis=2)
    g_final = take(6, final_norm.shape)
    dmod_all = g2[:, :, :].reshape(N_DEV, -1)[:, :offs[1]].reshape(N_DEV, L, 6 * D)
    dmod_mine = lax.dynamic_slice_in_dim(dmod_all, chip * n_ada, n_ada, axis=2).transpose(1, 0, 2)
    g_ada_w = _ada_bwd(c_act, dmod_mine, "ada_bwd")
    *ada_update, got = _adamw(ada_w, g_ada_w, m_ada_w, v_ada_w, "adamw_ada_w", job=_scatter_job([p_in0], win_in))
    (big[0]["in"],) = _rs_finish(p_in0, win_in, got, core_chip, "0_in")

    def stack(key, layers):
        return jnp.stack([big[l][key] for l in layers])

    even, odd, every = list(range(0, L, 2)), list(range(1, L, 2)), list(range(L))
    grads = dict(
        ada_w=g_ada_w, ada_b=g_ada_b, norm_mix=g_norm_mix, norm_ffn=g_norm_ffn,
        ab_w_in=stack("in", even), sgu_w=g_sgu_w, sgu_b=g_sgu_b, ab_w_out=stack("out", even),
        conv_w_in=stack("in", odd), conv_w=g_conv_w, conv_w_out=stack("out", odd),
        ffn_w_gate=stack("gate", every), ffn_w_up=stack("up", every), ffn_w_down=stack("down", every),
        final_norm=g_final)
    weights = dict(ada_w=ada_w, ada_b=ada_b, norm_mix=norm_mix, norm_ffn=norm_ffn, ab_w_in=ab_w_in, sgu_w=sgu_w,
                   sgu_b=sgu_b, ab_w_out=ab_w_out, conv_w_in=conv_w_in, conv_w=conv_w, conv_w_out=conv_w_out,
                   ffn_w_gate=ffn_w_gate, ffn_w_up=ffn_w_up, ffn_w_down=ffn_w_down, final_norm=final_norm)
    ms = dict(ada_w=m_ada_w, ada_b=m_ada_b, norm_mix=m_norm_mix, norm_ffn=m_norm_ffn, ab_w_in=m_ab_w_in, sgu_w=m_sgu_w,
              sgu_b=m_sgu_b, ab_w_out=m_ab_w_out, conv_w_in=m_conv_w_in, conv_w=m_conv_w, conv_w_out=m_conv_w_out,
              ffn_w_gate=m_ffn_w_gate, ffn_w_up=m_ffn_w_up, ffn_w_down=m_ffn_w_down, final_norm=m_final_norm)
    vs = dict(ada_w=v_ada_w, ada_b=v_ada_b, norm_mix=v_norm_mix, norm_ffn=v_norm_ffn, ab_w_in=v_ab_w_in, sgu_w=v_sgu_w,
              sgu_b=v_sgu_b, ab_w_out=v_ab_w_out, conv_w_in=v_conv_w_in, conv_w=v_conv_w, conv_w_out=v_conv_w_out,
              ffn_w_gate=v_ffn_w_gate, ffn_w_up=v_ffn_w_up, ffn_w_down=v_ffn_w_down, final_norm=v_final_norm)
    names = list(weights)
    deltas, new_m, new_v = {}, {}, {}
    for n in names:
        w, g = weights[n], grads[n]
        if n == "ada_w":
            deltas[n], new_m[n], new_v[n] = ada_update
        elif w.ndim == 1:
            d_, m_, v_ = _adamw(w[None, :], g[None, :], ms[n][None, :], vs[n][None, :], f"adamw_{n}")
            deltas[n], new_m[n], new_v[n] = d_[0], m_[0], v_[0]
        else:
            deltas[n], new_m[n], new_v[n] = _adamw(w, g, ms[n], vs[n], f"adamw_{n}")
    return (loss, grad_x, *[grads[n] for n in names], *[deltas[n] for n in names],
            *[new_m[n] for n in names], *[new_v[n] for n in names])
```

```python
import functools
import math

import jax
import jax.numpy as jnp
from jax import lax
from jax.experimental import pallas as pl
from jax.experimental.pallas import tpu as pltpu

F32 = jnp.float32
BF16 = jnp.bfloat16
HEAD_DIM = 128
CHUNK = 128
ATTN_BLOCK = 128
ATTN_LANES_FWD = 2
ATTN_LANES_BWD = 2
DILATIONS = (1, 4, 16)
ROPE_DIM = HEAD_DIM // 4
ROPE_THETA = 500000.0
EPS = 1e-6
MASKED = -1e30
ADAM_LR, ADAM_B1, ADAM_B2, ADAM_EPS, ADAM_WD, ADAM_STEP = 0.001, 0.9, 0.999, 1e-08, 0.01, 10
VMEM_LIMIT_BYTES = 56 * 1024 * 1024
MESH = pl.DeviceIdType.MESH
ANY = pl.BlockSpec(memory_space=pl.ANY)
N_CHIPS = 4
N_DEV = 8


def _pick(n, cands):
    for t in cands:
        if n % t == 0:
            return t
    return n


def _params(sem):
    return pltpu.CompilerParams(dimension_semantics=sem, vmem_limit_bytes=VMEM_LIMIT_BYTES)


MM_VMEM_BUDGET = 44 * 1024 * 1024
_TILES = (2048, 1536, 1408, 1024, 512, 256, 128)


def _mm_tiles(M, N, K, out_bytes):
    best = None
    for tm in [t for t in _TILES if M % t == 0] or [M]:
        for tn in [t for t in _TILES if N % t == 0] or [N]:
            for tk in [t for t in _TILES if K % t == 0] or [K]:
                nk = K // tk
                vmem = 2 * 2 * (tm * tk + tk * tn) + 2 * out_bytes * tm * tn + 4 * tm * tn
                vmem += 4 * tm * tn if nk > 1 and out_bytes == 2 else 0
                if vmem > MM_VMEM_BUDGET:
                    continue
                key = ((M // tm) * (N // tn) * nk, -tk, -tm)
                if best is None or key < best[0]:
                    best = (key, (tm, tn, tk))
    assert best is not None, (M, N, K)
    return best[1]


def _mm(a, b, mode, out_dtype, name, layer=None, job=None):
    bshape = b.shape[1:] if layer is not None else b.shape
    if mode == "nn":
        (M, K), (K2, N) = a.shape, bshape
    elif mode == "nt":
        (M, K), (N, K2) = a.shape, bshape
    else:
        (K, M), (K2, N) = a.shape, bshape
    assert K == K2, (a.shape, b.shape, mode)
    in_place = out_dtype == F32
    tm, tn, tk = _mm_tiles(M, N, K, 4 if in_place else 2)
    nk = K // tk
    dims = {"nn": (((1,), (0,)), ((), ())), "nt": (((1,), (1,)), ((), ())), "tn": (((0,), (0,)), ((), ()))}[mode]

    n_ji = len(job["ins"]) if job else 0
    n_jo = len(job["outs"]) if job else 0
    n_acc = 1 if nk > 1 and not in_place else 0
    grid = (M // tm, N // tn, nk)

    def body(a_ref, b_ref, *rest):
        jin, o_ref, jout = rest[:n_ji], rest[n_ji], rest[n_ji + 1:n_ji + 1 + n_jo]
        scratch = rest[n_ji + 1 + n_jo:]
        acc, sems = scratch[:n_acc], scratch[n_acc:]
        i, j, k = pl.program_id(0), pl.program_id(1), pl.program_id(2)
        if job:
            @pl.when((i == 0) & (j == 0) & (k == 0))
            def _():
                job["start"](jin, jout, sems)

        def product():
            return lax.dot_general(a_ref[...].astype(BF16), b_ref[...].astype(BF16), dims, preferred_element_type=F32)

        if nk == 1:
            o_ref[...] = product().astype(o_ref.dtype)
        else:
            acc_ref = o_ref if in_place else acc[0]

            @pl.when(k == 0)
            def _():
                acc_ref[...] = jnp.zeros_like(acc_ref)

            acc_ref[...] += product()

            if not in_place:
                @pl.when(k == nk - 1)
                def _():
                    o_ref[...] = acc_ref[...].astype(o_ref.dtype)

        if job:
            @pl.when((i == grid[0] - 1) & (j == grid[1] - 1) & (k == grid[2] - 1))
            def _():
                job["finish"](jin, jout, sems)

    if mode == "tn":
        a_spec = pl.BlockSpec((tk, tm), lambda i, j, k: (k, i))
    else:
        a_spec = pl.BlockSpec((tm, tk), lambda i, j, k: (i, k))
    if mode == "nt":
        bblk, bidx = (tn, tk), (lambda i, j, k: (j, k))
    else:
        bblk, bidx = (tk, tn), (lambda i, j, k: (k, j))
    if layer is not None:
        b_spec = pl.BlockSpec((None,) + bblk, lambda i, j, k: (layer,) + bidx(i, j, k))
    else:
        b_spec = pl.BlockSpec(bblk, bidx)
    out_spec = pl.BlockSpec((tm, tn), lambda i, j, k: (i, j))
    out_shape = jax.ShapeDtypeStruct((M, N), out_dtype)
    acc_scratch = [pltpu.VMEM((tm, tn), F32)] * n_acc
    if not job:
        return pl.pallas_call(
            body, name=name, grid=grid, in_specs=[a_spec, b_spec], out_specs=out_spec, out_shape=out_shape,
            scratch_shapes=acc_scratch, compiler_params=_params(("parallel", "parallel", "arbitrary")),
        )(a, b)
    return pl.pallas_call(
        body, name=name, grid=grid, in_specs=[a_spec, b_spec] + [ANY] * n_ji,
        out_specs=(out_spec,) + (ANY,) * n_jo, out_shape=(out_shape,) + tuple(job["outs"]),
        scratch_shapes=acc_scratch + list(job["sems"]),
        compiler_params=_params(("arbitrary", "arbitrary", "arbitrary")),
    )(a, b, *job["ins"])


def _rows(S):
    return _pick(S, (256, 128, 64, 32, 16, 8))


def _row_spec(tr, width, col=0):
    return pl.BlockSpec((tr, width), lambda i: (i, col))


def _vec_spec(rows, width):
    return pl.BlockSpec((rows, width), lambda i: (0, 0))


def _rms(xv):
    return lax.rsqrt(jnp.mean(xv * xv, axis=-1, keepdims=True) + EPS)


def _norm_mod(x, y, g, w_eff, sh, name):
    S, D = x.shape
    tr = _rows(S)
    fused = y is not None

    def body(*refs):
        if fused:
            x_ref, y_ref, g_ref, w_ref, s_ref, x1_ref, h_ref = refs
            xv = x_ref[...] + g_ref[...] * y_ref[...]
            x1_ref[...] = xv
        else:
            x_ref, w_ref, s_ref, h_ref = refs
            xv = x_ref[...]
        h_ref[...] = (xv * _rms(xv) * w_ref[...] + s_ref[...]).astype(BF16)

    big, vec = _row_spec(tr, D), _vec_spec(1, D)
    if fused:
        ins, in_specs = (x, y, g, w_eff, sh), [big, big, vec, vec, vec]
        out_shape = (jax.ShapeDtypeStruct((S, D), F32), jax.ShapeDtypeStruct((S, D), BF16))
        out_specs = (big, big)
    else:
        ins, in_specs = (x, w_eff, sh), [big, vec, vec]
        out_shape = jax.ShapeDtypeStruct((S, D), BF16)
        out_specs = big
    out = pl.pallas_call(body, name=name, grid=(S // tr,), in_specs=in_specs, out_specs=out_specs,
                         out_shape=out_shape, compiler_params=_params(("parallel",)))(*ins)
    return out if fused else (None, out)


def _norm_mod_bwd(dh, x, w_eff, dres, y, g, name):
    S, D = x.shape
    tr = _rows(S)
    gated = y is not None

    def body(dh_ref, x_ref, w_ref, r_ref, *rest):
        if gated:
            y_ref, g_ref, dx_ref, dsh_ref, dw_ref, dy_ref, dg_ref = rest
        else:
            dx_ref, dsh_ref, dw_ref = rest
        xv = x_ref[...]
        dhv = dh_ref[...].astype(F32)
        r = _rms(xv)
        xn = xv * r
        dxn = dhv * w_ref[...]
        dxv = r_ref[...] + r * (dxn - xn * jnp.mean(dxn * xn, axis=-1, keepdims=True))
        dx_ref[...] = dxv

        @pl.when(pl.program_id(0) == 0)
        def _():
            dsh_ref[...] = jnp.zeros_like(dsh_ref)
            dw_ref[...] = jnp.zeros_like(dw_ref)
            if gated:
                dg_ref[...] = jnp.zeros_like(dg_ref)

        dsh_ref[...] += jnp.sum(dhv, axis=0, keepdims=True)
        dw_ref[...] += jnp.sum(dhv * xn, axis=0, keepdims=True)
        if gated:
            dy_ref[...] = (dxv * g_ref[...]).astype(BF16)
            dg_ref[...] += jnp.sum(dxv * y_ref[...], axis=0, keepdims=True)

    big, vec = _row_spec(tr, D), _vec_spec(1, D)
    f32v = jax.ShapeDtypeStruct((1, D), F32)
    outs = (jax.ShapeDtypeStruct((S, D), F32), f32v, f32v)
    if not gated:
        return pl.pallas_call(
            body, name=name, grid=(S // tr,), in_specs=[big, big, vec, big], out_specs=(big, vec, vec), out_shape=outs,
            compiler_params=_params(("arbitrary",)))(dh, x, w_eff, dres)
    return pl.pallas_call(
        body, name=name, grid=(S // tr,), in_specs=[big, big, vec, big, big, vec],
        out_specs=(big, vec, vec, big, vec), out_shape=outs + (jax.ShapeDtypeStruct((S, D), BF16), f32v),
        compiler_params=_params(("arbitrary",)))(dh, x, w_eff, dres, y, g)


def _loss_head(x, y, g, gamma, target, name):
    S, D = x.shape
    tr = _rows(S)

    def body(x_ref, y_ref, g_ref, gm_ref, t_ref, dx_ref, loss_ref, dgm_ref, dy_ref, dg_ref):
        yv = y_ref[...]
        xv = x_ref[...] + g_ref[...] * yv
        r = _rms(xv)
        xn = xv * r
        err = xn * gm_ref[...] - t_ref[...]
        dout = err * (1.0 / D)
        dxn = dout * gm_ref[...]
        dxv = r * (dxn - xn * jnp.mean(dxn * xn, axis=-1, keepdims=True))
        dx_ref[...] = dxv
        dy_ref[...] = (dxv * g_ref[...]).astype(BF16)

        @pl.when(pl.program_id(0) == 0)
        def _():
            loss_ref[...] = jnp.zeros_like(loss_ref)
            dgm_ref[...] = jnp.zeros_like(dgm_ref)
            dg_ref[...] = jnp.zeros_like(dg_ref)

        loss_ref[...] += 0.5 * jnp.sum(jnp.mean(err * err, axis=-1, keepdims=True), axis=0, keepdims=True)
        dgm_ref[...] += jnp.sum(dout * xn, axis=0, keepdims=True)
        dg_ref[...] += jnp.sum(dxv * yv, axis=0, keepdims=True)

    big, vec = _row_spec(tr, D), _vec_spec(1, D)
    f32v = jax.ShapeDtypeStruct((1, D), F32)
    return pl.pallas_call(
        body, name=name, grid=(S // tr,), in_specs=[big, big, vec, vec, big],
        out_specs=(big, _vec_spec(1, 1), vec, big, vec),
        out_shape=(jax.ShapeDtypeStruct((S, D), F32), jax.ShapeDtypeStruct((1, 1), F32), f32v,
                   jax.ShapeDtypeStruct((S, D), BF16), f32v),
        compiler_params=_params(("arbitrary",)))(x, y, g, gamma, target)


def _silu(a):
    return a * jax.nn.sigmoid(a)


def _swiglu(ab, name):
    S, F2 = ab.shape
    F = F2 // 2
    tr = _pick(S, (256, 128, 64, 32, 16, 8))

    def body(a_ref, b_ref, f_ref):
        f_ref[...] = (_silu(a_ref[...].astype(F32)) * b_ref[...].astype(F32)).astype(BF16)

    return pl.pallas_call(
        body, name=name, grid=(S // tr,), in_specs=[_row_spec(tr, F, 0), _row_spec(tr, F, 1)],
        out_specs=_row_spec(tr, F), out_shape=jax.ShapeDtypeStruct((S, F), BF16),
        compiler_params=_params(("parallel",)))(ab, ab)


def _swiglu_bwd(ab, df, name):
    S, F2 = ab.shape
    F = F2 // 2
    tr = _pick(S, (256, 128, 64, 32, 16, 8))

    def body(a_ref, b_ref, df_ref, da_ref, db_ref):
        a = a_ref[...].astype(F32)
        sg = jax.nn.sigmoid(a)
        dfv = df_ref[...].astype(F32)
        da_ref[...] = (dfv * b_ref[...].astype(F32) * (sg * (1.0 + a * (1.0 - sg)))).astype(BF16)
        db_ref[...] = (dfv * a * sg).astype(BF16)

    def body2(a_ref, b_ref, df_ref, o_ref):
        body(a_ref, b_ref, df_ref, o_ref.at[:, pl.ds(0, F)], o_ref.at[:, pl.ds(F, F)])

    return pl.pallas_call(
        body2, name=name, grid=(S // tr,),
        in_specs=[_row_spec(tr, F, 0), _row_spec(tr, F, 1), _row_spec(tr, F)],
        out_specs=_row_spec(tr, F2), out_shape=jax.ShapeDtypeStruct((S, F2), BF16),
        compiler_params=_params(("parallel",)))(ab, ab, df)


def _shift_rows(v, n):
    return pltpu.roll(v, n, 0)


def _conv_fwd(p, w8, name):
    S, D3 = p.shape
    D = D3 // 3
    tr = _rows(S)
    nb8 = tr // 8

    def body(gb_ref, gc_ref, hx_ref, gcp_ref, hxp_ref, w_ref, o_ref):
        i = pl.program_id(0)
        y = gc_ref[...].astype(F32) * hx_ref[...].astype(F32)
        yp = jnp.where(i > 0, gcp_ref[...].astype(F32) * hxp_ref[...].astype(F32), 0.0)
        w0, w1, w2 = w_ref[0:1, :], w_ref[1:2, :], w_ref[2:3, :]
        conv = w0 * _shift_rows(y, 2) + w1 * _shift_rows(y, 1) + w2 * y
        o_ref[...] = (gb_ref[...].astype(F32) * conv).astype(BF16)
        rid = lax.broadcasted_iota(jnp.int32, (8, D), 0)
        y8 = y[0:8, :]
        y1 = jnp.where(rid < 1, _shift_rows(yp, 1), _shift_rows(y8, 1))
        y2 = jnp.where(rid < 2, _shift_rows(yp, 2), _shift_rows(y8, 2))
        conv8 = w0 * y2 + w1 * y1 + w2 * y8
        o_ref[0:8, :] = (gb_ref[0:8, :].astype(F32) * conv8).astype(BF16)

    def col(c):
        return pl.BlockSpec((tr, D), lambda i: (i, c))

    def prev8(c):
        return pl.BlockSpec((8, D), lambda i: (jnp.maximum(i * nb8 - 1, 0), c))

    return pl.pallas_call(
        body, name=name, grid=(S // tr,),
        in_specs=[col(0), col(1), col(2), prev8(1), prev8(2), _vec_spec(8, D)],
        out_specs=_row_spec(tr, D), out_shape=jax.ShapeDtypeStruct((S, D), BF16),
        compiler_params=_params(("parallel",)))(p, p, p, p, p, w8)


def _conv_bwd(p, do, w8, name):
    S, D3 = p.shape
    D = D3 // 3
    tr = _rows(S)
    nb8 = tr // 8
    nt = S // tr

    def body(gb_ref, gc_ref, hx_ref, gcp_ref, hxp_ref, do_ref, gbn_ref, don_ref, w_ref, dp_ref, dw_ref):
        i = pl.program_id(0)
        gb = gb_ref[...].astype(F32)
        gc = gc_ref[...].astype(F32)
        hx = hx_ref[...].astype(F32)
        dov = do_ref[...].astype(F32)
        y = gc * hx
        yp = jnp.where(i > 0, gcp_ref[...].astype(F32) * hxp_ref[...].astype(F32), 0.0)
        dconv = dov * gb
        dcn = jnp.where(i < nt - 1, don_ref[...].astype(F32) * gbn_ref[...].astype(F32), 0.0)
        w0, w1, w2 = w_ref[0:1, :], w_ref[1:2, :], w_ref[2:3, :]
        rid = lax.broadcasted_iota(jnp.int32, (tr, D), 0)
        rid8 = lax.broadcasted_iota(jnp.int32, (8, D), 0)
        yp1 = jnp.concatenate([_shift_rows(yp, 1), jnp.zeros((tr - 8, D), F32)], axis=0)
        yp2 = jnp.concatenate([_shift_rows(yp, 2), jnp.zeros((tr - 8, D), F32)], axis=0)
        y1 = jnp.where(rid < 1, yp1, _shift_rows(y, 1))
        y2 = jnp.where(rid < 2, yp2, _shift_rows(y, 2))
        conv = w0 * y2 + w1 * y1 + w2 * y
        dn1 = jnp.concatenate([jnp.zeros((tr - 8, D), F32), _shift_rows(dcn, 7)], axis=0)
        dn2 = jnp.concatenate([jnp.zeros((tr - 8, D), F32), _shift_rows(dcn, 6)], axis=0)
        d1 = jnp.where(rid >= tr - 1, dn1, _shift_rows(dconv, tr - 1))
        d2 = jnp.where(rid >= tr - 2, dn2, _shift_rows(dconv, tr - 2))
        dy = w2 * dconv + w1 * d1 + w0 * d2
        dp_ref[:, pl.ds(0, D)] = (dov * conv).astype(BF16)
        dp_ref[:, pl.ds(D, D)] = (dy * hx).astype(BF16)
        dp_ref[:, pl.ds(2 * D, D)] = (dy * gc).astype(BF16)

        @pl.when(i == 0)
        def _():
            dw_ref[...] = jnp.zeros_like(dw_ref)

        upd = jnp.where(rid8 == 0, jnp.sum(dconv * y2, axis=0, keepdims=True),
                        jnp.where(rid8 == 1, jnp.sum(dconv * y1, axis=0, keepdims=True),
                                  jnp.where(rid8 == 2, jnp.sum(dconv * y, axis=0, keepdims=True), 0.0)))
        dw_ref[...] += upd

    def col(c):
        return pl.BlockSpec((tr, D), lambda i: (i, c))

    def prev8(c):
        return pl.BlockSpec((8, D), lambda i: (jnp.maximum(i * nb8 - 1, 0), c))

    def next8(c):
        return pl.BlockSpec((8, D), lambda i: (jnp.minimum((i + 1) * nb8, S // 8 - 1), c))

    return pl.pallas_call(
        body, name=name, grid=(nt,),
        in_specs=[col(0), col(1), col(2), prev8(1), prev8(2), col(0), next8(0), next8(0), _vec_spec(8, D)],
        out_specs=(_row_spec(tr, D3), _vec_spec(8, D)),
        out_shape=(jax.ShapeDtypeStruct((S, D3), BF16), jax.ShapeDtypeStruct((8, D), F32)),
        compiler_params=_params(("arbitrary",)))(p, p, p, p, p, do, p, do, w8)


_GELU_C = math.sqrt(2.0 / math.pi)


def _gelu(v):
    return 0.5 * v * (1.0 + jnp.tanh(_GELU_C * (v + 0.044715 * v * v * v)))


def _gelu_grad(v):
    t = jnp.tanh(_GELU_C * (v + 0.044715 * v * v * v))
    return 0.5 * (1.0 + t) + 0.5 * v * (1.0 - t * t) * _GELU_C * (1.0 + 3.0 * 0.044715 * v * v)


def _tril(w):
    r = lax.broadcasted_iota(jnp.int32, (CHUNK, CHUNK), 0)
    c = lax.broadcasted_iota(jnp.int32, (CHUNK, CHUNK), 1)
    return jnp.where(r >= c, w, 0.0)


def _sgu_fwd(z, w, bT, n_attn, name):
    S = z.shape[0]
    G = w.shape[0]
    W = G * CHUNK
    tr = _pick(S, (512, 256, 128))
    ucol = 3 * n_attn * HEAD_DIM // W

    def body(u_ref, v_ref, w_ref, b_ref, o_ref):
        for g in range(G):
            wt = _tril(w_ref[g]).astype(BF16)
            for ci in range(tr // CHUNK):
                rows, cols = pl.ds(ci * CHUNK, CHUNK), pl.ds(g * CHUNK, CHUNK)
                gv = _gelu(v_ref[rows, cols].astype(F32)).astype(BF16)
                mixed = jnp.dot(wt, gv, preferred_element_type=F32) + b_ref[:, g:g + 1]
                o_ref[rows, cols] = (_gelu(u_ref[rows, cols].astype(F32)) * mixed).astype(BF16)

    return pl.pallas_call(
        body, name=name, grid=(S // tr,),
        in_specs=[_row_spec(tr, W, ucol), _row_spec(tr, W, ucol + 1),
                  pl.BlockSpec((G, CHUNK, CHUNK), lambda i: (0, 0, 0)), _vec_spec(CHUNK, G)],
        out_specs=_row_spec(tr, W), out_shape=jax.ShapeDtypeStruct((S, W), BF16),
        compiler_params=_params(("parallel",)))(z, z, w, bT)


def _sgu_bwd(z, dcat, w, bT, n_attn, name):
    S = z.shape[0]
    G = w.shape[0]
    W = G * CHUNK
    tr = _pick(S, (512, 256, 128))
    ucol = 3 * n_attn * HEAD_DIM // W
    dcol = n_attn * HEAD_DIM // W

    def body(u_ref, v_ref, d_ref, w_ref, b_ref, o_ref, dw_ref, db_ref):
        @pl.when(pl.program_id(0) == 0)
        def _():
            dw_ref[...] = jnp.zeros_like(dw_ref)
            db_ref[...] = jnp.zeros_like(db_ref)

        lane = lax.broadcasted_iota(jnp.int32, (CHUNK, G), 1)
        for g in range(G):
            wtf = _tril(w_ref[g])
            wt = wtf.astype(BF16)
            dw_acc = jnp.zeros((CHUNK, CHUNK), F32)
            db_acc = jnp.zeros((CHUNK, 1), F32)
            for ci in range(tr // CHUNK):
                rows, cols = pl.ds(ci * CHUNK, CHUNK), pl.ds(g * CHUNK, CHUNK)
                uv = u_ref[rows, cols].astype(F32)
                vv = v_ref[rows, cols].astype(F32)
                dov = d_ref[rows, cols]
                gv = _gelu(vv).astype(BF16)
                mixed = jnp.dot(wt, gv, preferred_element_type=F32) + b_ref[:, g:g + 1]
                dmixed = dov * _gelu(uv)
                dmb = dmixed.astype(BF16)
                dgv = lax.dot_general(wt, dmb, (((0,), (0,)), ((), ())), preferred_element_type=F32)
                o_ref[rows, cols] = (dov * mixed * _gelu_grad(uv)).astype(BF16)
                o_ref[rows, pl.ds(W + g * CHUNK, CHUNK)] = (dgv * _gelu_grad(vv)).astype(BF16)
                dw_acc += lax.dot_general(dmb, gv, (((1,), (1,)), ((), ())), preferred_element_type=F32)
                db_acc += jnp.sum(dmixed, axis=1, keepdims=True)
            dw_ref[g] += _tril(dw_acc)
            db_ref[...] += jnp.where(lane == g, db_acc, 0.0)

    return pl.pallas_call(
        body, name=name, grid=(S // tr,),
        in_specs=[_row_spec(tr, W, ucol), _row_spec(tr, W, ucol + 1), _row_spec(tr, W, dcol),
                  pl.BlockSpec((G, CHUNK, CHUNK), lambda i: (0, 0, 0)), _vec_spec(CHUNK, G)],
        out_specs=(_row_spec(tr, 2 * W), pl.BlockSpec((G, CHUNK, CHUNK), lambda i: (0, 0, 0)), _vec_spec(CHUNK, G)),
        out_shape=(jax.ShapeDtypeStruct((S, 2 * W), BF16), jax.ShapeDtypeStruct((G, CHUNK, CHUNK), F32),
                   jax.ShapeDtypeStruct((CHUNK, G), F32)),
        compiler_params=_params(("arbitrary",)))(z, z, dcat, w, bT)


def _rope(v, cs, sa, sb):
    return v * cs + pltpu.roll(v, HEAD_DIM - ROPE_DIM // 2, 1) * sa + pltpu.roll(v, ROPE_DIM // 2, 1) * sb


def _rope_t(d, cs, sa, sb):
    return d * cs + pltpu.roll(d * sa, ROPE_DIM // 2, 1) + pltpu.roll(d * sb, HEAD_DIM - ROPE_DIM // 2, 1)


def _qkv_prep(z, cs, sa, sb, n_attn, name):
    S = z.shape[0]
    A = n_attn * HEAD_DIM
    tr = _rows(S)

    def body(q_ref, k_ref, v_ref, c_ref, a_ref, b_ref, qo_ref, ko_ref, vo_ref):
        cv, av, bv = c_ref[...], a_ref[...], b_ref[...]
        for h in range(n_attn):
            cols = pl.ds(h * HEAD_DIM, HEAD_DIM)
            qo_ref[:, cols] = _rope(q_ref[:, cols].astype(F32), cv, av, bv)
            ko_ref[:, cols] = _rope(k_ref[:, cols].astype(F32), cv, av, bv)
        vo_ref[...] = v_ref[...].astype(F32)

    tab = _row_spec(tr, HEAD_DIM)
    out = jax.ShapeDtypeStruct((S, A), F32)
    return pl.pallas_call(
        body, name=name, grid=(S // tr,),
        in_specs=[_row_spec(tr, A, 0), _row_spec(tr, A, 1), _row_spec(tr, A, 2), tab, tab, tab],
        out_specs=(_row_spec(tr, A),) * 3, out_shape=(out,) * 3,
        compiler_params=_params(("parallel",)))(z, z, z, cs, sa, sb)


def _dqkv_post(dq, dk, dv, cs, sa, sb, n_attn, name):
    S, A = dq.shape
    tr = _rows(S)

    def body(q_ref, k_ref, v_ref, c_ref, a_ref, b_ref, o_ref):
        cv, av, bv = c_ref[...], a_ref[...], b_ref[...]
        for h in range(n_attn):
            cols = pl.ds(h * HEAD_DIM, HEAD_DIM)
            o_ref[:, pl.ds(h * HEAD_DIM, HEAD_DIM)] = _rope_t(q_ref[:, cols], cv, av, bv).astype(BF16)
            o_ref[:, pl.ds(A + h * HEAD_DIM, HEAD_DIM)] = _rope_t(k_ref[:, cols], cv, av, bv).astype(BF16)
        o_ref[:, pl.ds(2 * A, A)] = v_ref[...].astype(BF16)

    tab = _row_spec(tr, HEAD_DIM)
    return pl.pallas_call(
        body, name=name, grid=(S // tr,),
        in_specs=[_row_spec(tr, A)] * 3 + [tab, tab, tab],
        out_specs=_row_spec(tr, 3 * A), out_shape=jax.ShapeDtypeStruct((S, 3 * A), BF16),
        compiler_params=_params(("parallel",)))(dq, dk, dv, cs, sa, sb)


ATTN_CHUNK = 64
ATTN_PAD = DILATIONS[-1] * ATTN_BLOCK


def _group(d, S):
    return 2 if (S // (d * ATTN_BLOCK)) % 2 == 0 else 1


def _work_item(d, S, it):
    G = _group(d, S)
    ngrp = S // (d * ATTN_BLOCK * G)
    r = it // ngrp
    jb = (it % ngrp) * G
    return r + d * ATTN_BLOCK * jb, ATTN_PAD + r + d * ATTN_BLOCK * (jb - 1), jb == 0


def _chunk_bias(first_row):
    al = first_row + lax.broadcasted_iota(jnp.int32, (ATTN_CHUNK, 2 * ATTN_BLOCK), 0)
    kl = lax.broadcasted_iota(jnp.int32, (ATTN_CHUNK, 2 * ATTN_BLOCK), 1)
    seen = (kl >= al) & (kl <= al + ATTN_BLOCK)
    return jnp.where(seen, 0.0, MASKED), jnp.where(seen & (kl >= ATTN_BLOCK), 0.0, MASKED)


def _chunks(G):
    return [(slice(c * ATTN_CHUNK, (c + 1) * ATTN_CHUNK), c * ATTN_CHUNK // ATTN_BLOCK, (c * ATTN_CHUNK) % ATTN_BLOCK)
            for c in range(G * ATTN_BLOCK // ATTN_CHUNK)]


_NT = (((1,), (1,)), ((), ()))
_TN = (((0,), (0,)), ((), ()))


def _attn_fwd(q, k, v, n_attn, name, job=None):
    S, A = q.shape
    scale = HEAD_DIM ** -0.5
    n_ji = len(job["ins"]) if job else 0
    n_jo = len(job["outs"]) if job else 0

    def body(q_hbm, k_hbm, v_hbm, *rest):
        jin, (o_hbm, lse_hbm), jout = rest[:n_ji], rest[n_ji:n_ji + 2], rest[n_ji + 2:n_ji + 2 + n_jo]
        qs, ks, vs, acc, ms, ls, ob, sem = rest[n_ji + 2 + n_jo:n_ji + 2 + n_jo + 8]
        jsems = rest[n_ji + 2 + n_jo + 8:]
        h = pl.program_id(0)
        if job:
            @pl.when(h == 0)
            def _():
                job["start"](jin, jout, jsems)

        cols = pl.ds(pl.multiple_of(h * HEAD_DIM, HEAD_DIM), HEAD_DIM)
        data = pl.ds(ATTN_PAD, S)
        cps = [pltpu.make_async_copy(src.at[:, cols], dst, sem.at[i])
               for i, (src, dst) in enumerate(((q_hbm, qs), (k_hbm, ks.at[data, :]), (v_hbm, vs.at[data, :])))]
        for cp in cps:
            cp.start()

        @pl.when(h == 0)
        def _():
            ks[pl.ds(0, ATTN_PAD), :] = jnp.zeros((ATTN_PAD, HEAD_DIM), F32)
            vs[pl.ds(0, ATTN_PAD), :] = jnp.zeros((ATTN_PAD, HEAD_DIM), F32)

        acc[...] = jnp.zeros_like(acc)
        ms[...] = jnp.full_like(ms, MASKED)
        ls[...] = jnp.zeros_like(ls)
        for cp in cps:
            cp.wait()
        biases = {r0: _chunk_bias(r0) for r0 in range(0, ATTN_BLOCK, ATTN_CHUNK)}
        for d in DILATIONS:
            G = _group(d, S)
            lanes = S // (ATTN_BLOCK * G * ATTN_LANES_FWD)

            def step(it, carry, d=d, G=G, lanes=lanes):
                groups, chains = [], []
                for u in range(ATTN_LANES_FWD):
                    q0, k0, first = _work_item(d, S, it + u * lanes)
                    qrows = pl.ds(q0, G * ATTN_BLOCK, stride=d)
                    krows = pl.ds(k0, (G + 1) * ATTN_BLOCK, stride=d)
                    qb, kb, vb = (t[rows, :].astype(BF16) for t, rows in ((qs, qrows), (ks, krows), (vs, krows)))
                    m_all, l_all, a_all = ms[qrows, :], ls[qrows, :], acc[qrows, :]
                    groups.append(qrows)
                    for sl, qi, r0 in _chunks(G):
                        ky = slice(qi * ATTN_BLOCK, (qi + 2) * ATTN_BLOCK)
                        bias = jnp.where(first, biases[r0][1], biases[r0][0]) if qi == 0 else biases[r0][0]
                        chains.append((qb[sl], kb[ky], vb[ky], bias, m_all[sl], l_all[sl], a_all[sl]))
                scores = [lax.dot_general(c[0], c[1], _NT, preferred_element_type=F32) for c in chains]
                m_new, l_new, probs, alphas = [], [], [], []
                for (_, _, _, bias, m_old, l_old, _), s in zip(chains, scores):
                    s = s * scale + bias
                    m_c = jnp.maximum(m_old, jnp.max(s, axis=-1, keepdims=True))
                    alpha = jnp.exp(m_old - m_c)
                    p = jnp.exp(s - m_c)
                    m_new.append(m_c)
                    l_new.append(alpha * l_old + jnp.sum(p, axis=-1, keepdims=True))
                    probs.append(p.astype(BF16))
                    alphas.append(alpha)
                a_new = [alpha * c[6] + jnp.dot(p, c[2], preferred_element_type=F32)
                         for c, p, alpha in zip(chains, probs, alphas)]
                n = len(_chunks(G))
                for u, qrows in enumerate(groups):
                    ms[qrows, :] = jnp.concatenate(m_new[u * n:(u + 1) * n], axis=0)
                    ls[qrows, :] = jnp.concatenate(l_new[u * n:(u + 1) * n], axis=0)
                    acc[qrows, :] = jnp.concatenate(a_new[u * n:(u + 1) * n], axis=0)
                return carry
            lax.fori_loop(0, lanes, step, 0)
        ob[...] = (acc[...] / ls[...]).astype(BF16)
        ms[...] = ms[...] + jnp.log(ls[...])
        out = [pltpu.make_async_copy(ob, o_hbm.at[:, cols], sem.at[0]),
               pltpu.make_async_copy(ms, lse_hbm.at[h], sem.at[1])]
        for cp in out:
            cp.start()
        for cp in out:
            cp.wait()
        if job:
            @pl.when(h == n_attn - 1)
            def _():
                job["finish"](jin, jout, jsems)

    res = pl.pallas_call(
        body, name=name, grid=(n_attn,), in_specs=[ANY] * (3 + n_ji), out_specs=(ANY,) * (2 + n_jo),
        out_shape=(jax.ShapeDtypeStruct((S, A), BF16), jax.ShapeDtypeStruct((n_attn, S, 1), F32))
        + tuple(job["outs"] if job else ()),
        scratch_shapes=[pltpu.VMEM((S, HEAD_DIM), F32)] + [pltpu.VMEM((S + ATTN_PAD, HEAD_DIM), F32)] * 2
        + [pltpu.VMEM((S, HEAD_DIM), F32)] + [pltpu.VMEM((S, 1), F32)] * 2
        + [pltpu.VMEM((S, HEAD_DIM), BF16), pltpu.SemaphoreType.DMA((3,))] + list(job["sems"] if job else ()),
        compiler_params=_params(("arbitrary",)))(q, k, v, *(job["ins"] if job else ()))
    return res[0], res[1], list(res[2:])


def _attn_bwd(q, k, v, o, lse, dcat, n_attn, name):
    S, A = q.shape
    scale = HEAD_DIM ** -0.5

    def body(q_hbm, k_hbm, v_hbm, o_hbm, lse_hbm, do_hbm, dq_hbm, dk_hbm, dv_hbm,
             qs, ks, vs, dos, dqs, dks, dvs, lses, dls, ob, sem):
        h = pl.program_id(0)
        cols = pl.ds(pl.multiple_of(h * HEAD_DIM, HEAD_DIM), HEAD_DIM)
        data = pl.ds(ATTN_PAD, S)
        cps = [pltpu.make_async_copy(src.at[:, cols], dst, sem.at[i])
               for i, (src, dst) in enumerate(((q_hbm, qs), (k_hbm, ks.at[data, :]), (v_hbm, vs.at[data, :]),
                                               (do_hbm, dos), (o_hbm, ob)))]
        cps.append(pltpu.make_async_copy(lse_hbm.at[h], lses, sem.at[5]))
        for cp in cps:
            cp.start()

        @pl.when(h == 0)
        def _():
            ks[pl.ds(0, ATTN_PAD), :] = jnp.zeros((ATTN_PAD, HEAD_DIM), F32)
            vs[pl.ds(0, ATTN_PAD), :] = jnp.zeros((ATTN_PAD, HEAD_DIM), F32)

        dqs[...] = jnp.zeros_like(dqs)
        dks[...] = jnp.zeros_like(dks)
        dvs[...] = jnp.zeros_like(dvs)
        for cp in cps:
            cp.wait()
        dls[...] = jnp.sum(dos[...] * ob[...].astype(F32), axis=-1, keepdims=True)
        biases = {r0: _chunk_bias(r0) for r0 in range(0, ATTN_BLOCK, ATTN_CHUNK)}
        for d in DILATIONS:
            G = _group(d, S)
            n_lanes = 1 if d == DILATIONS[-1] else ATTN_LANES_BWD
            lanes = S // (ATTN_BLOCK * G * n_lanes)
            assert lanes % 2 == 0

            def step(it, carry, d=d, G=G, lanes=lanes, n_lanes=n_lanes):
                groups, chains = [], []
                for u in range(n_lanes):
                    q0, k0, first = _work_item(d, S, it + u * lanes)
                    qrows = pl.ds(q0, G * ATTN_BLOCK, stride=d)
                    krows = pl.ds(k0, (G + 1) * ATTN_BLOCK, stride=d)
                    qb, kb, vb, dob = (t[rows, :].astype(BF16)
                                       for t, rows in ((qs, qrows), (ks, krows), (vs, krows), (dos, qrows)))
                    lse_v, dl_v, dq_old = lses[qrows, :], dls[qrows, :], dqs[qrows, :]
                    groups.append((qrows, krows, dks[krows, :], dvs[krows, :]))
                    for sl, qi, r0 in _chunks(G):
                        ky = slice(qi * ATTN_BLOCK, (qi + 2) * ATTN_BLOCK)
                        bias = jnp.where(first, biases[r0][1], biases[r0][0]) if qi == 0 else biases[r0][0]
                        chains.append((qb[sl], kb[ky], vb[ky], dob[sl], bias, lse_v[sl], dl_v[sl], dq_old[sl], u, qi))
                scores = [lax.dot_general(c[0], c[1], _NT, preferred_element_type=F32) for c in chains]
                dprobs = [lax.dot_general(c[3], c[2], _NT, preferred_element_type=F32) for c in chains]
                probs, dscores = [], []
                for c, s, dp in zip(chains, scores, dprobs):
                    p = jnp.exp(s * scale + c[4] - c[5])
                    probs.append(p.astype(BF16))
                    dscores.append((p * (dp - c[6]) * scale).astype(BF16))
                dq_new = [c[7] + jnp.dot(ds, c[1], preferred_element_type=F32) for c, ds in zip(chains, dscores)]
                dk_add = [[None] * (G + 1) for _ in groups]
                dv_add = [[None] * (G + 1) for _ in groups]
                for c, ds, p in zip(chains, dscores, probs):
                    dk_c = lax.dot_general(ds, c[0], _TN, preferred_element_type=F32)
                    dv_c = lax.dot_general(p, c[3], _TN, preferred_element_type=F32)
                    u, qi = c[8], c[9]
                    for t in range(2):
                        half = slice(t * ATTN_BLOCK, (t + 1) * ATTN_BLOCK)
                        dk_add[u][qi + t] = dk_c[half] if dk_add[u][qi + t] is None else dk_add[u][qi + t] + dk_c[half]
                        dv_add[u][qi + t] = dv_c[half] if dv_add[u][qi + t] is None else dv_add[u][qi + t] + dv_c[half]
                n = len(_chunks(G))
                for u, (qrows, krows, dk_old, dv_old) in enumerate(groups):
                    dqs[qrows, :] = jnp.concatenate(dq_new[u * n:(u + 1) * n], axis=0)
                    dks[krows, :] = dk_old + jnp.concatenate(dk_add[u], axis=0)
                    dvs[krows, :] = dv_old + jnp.concatenate(dv_add[u], axis=0)
                return carry
            lax.fori_loop(0, lanes, step, 0)
        out = [pltpu.make_async_copy(src, dst.at[:, cols], sem.at[i])
               for i, (src, dst) in enumerate(((dqs, dq_hbm), (dks.at[data, :], dk_hbm), (dvs.at[data, :], dv_hbm)))]
        for cp in out:
            cp.start()
        for cp in out:
            cp.wait()

    grad = jax.ShapeDtypeStruct((S, A), F32)
    plain, padded = pltpu.VMEM((S, HEAD_DIM), F32), pltpu.VMEM((S + ATTN_PAD, HEAD_DIM), F32)
    return pl.pallas_call(
        body, name=name, grid=(n_attn,), in_specs=[ANY] * 6, out_specs=(ANY, ANY, ANY), out_shape=(grad,) * 3,
        scratch_shapes=[plain, padded, padded, plain, plain, padded, padded] + [pltpu.VMEM((S, 1), F32)] * 2
        + [pltpu.VMEM((S, HEAD_DIM), BF16), pltpu.SemaphoreType.DMA((6,))],
        compiler_params=_params(("arbitrary",)))(q, k, v, o, lse, dcat)


def _ada_fwd(c_act, ada_w, name):
    L, D, n = ada_w.shape
    tn = _pick(n, (512, 256, 128))

    def body(c_ref, w_ref, o_ref):
        o_ref[...] = jnp.dot(c_ref[...], w_ref[...], preferred_element_type=F32)

    return pl.pallas_call(
        body, name=name, grid=(L, n // tn),
        in_specs=[pl.BlockSpec((N_DEV, D), lambda l, j: (0, 0)), pl.BlockSpec((None, D, tn), lambda l, j: (l, 0, j))],
        out_specs=pl.BlockSpec((None, N_DEV, tn), lambda l, j: (l, 0, j)),
        out_shape=jax.ShapeDtypeStruct((L, N_DEV, n), F32),
        compiler_params=_params(("parallel", "parallel")))(c_act, ada_w)


def _ada_bwd(c_act, dmod, name):
    L, _, n = dmod.shape
    D = c_act.shape[1]
    tn = _pick(n, (512, 256, 128))

    def body(c_ref, d_ref, o_ref):
        o_ref[...] = lax.dot_general(c_ref[...], d_ref[...], _TN, preferred_element_type=F32)

    return pl.pallas_call(
        body, name=name, grid=(L, n // tn),
        in_specs=[pl.BlockSpec((N_DEV, D), lambda l, j: (0, 0)), pl.BlockSpec((None, N_DEV, tn), lambda l, j: (l, 0, j))],
        out_specs=pl.BlockSpec((None, D, tn), lambda l, j: (l, 0, j)),
        out_shape=jax.ShapeDtypeStruct((L, D, n), F32),
        compiler_params=_params(("parallel", "parallel")))(c_act, dmod)


def _adamw(w, g, m, v, name, job=None):
    shape = w.shape
    C = shape[-1]
    R = w.size // C
    w2, g2, m2, v2 = (t.reshape(R, C) for t in (w, g, m, v))
    tr = _pick(R, (256, 128, 64, 32, 16, 8))
    tc = _pick(C, (2048, 1536, 1408, 1024, 512, 256, 128))
    n_ji = len(job["ins"]) if job else 0
    n_jo = len(job["outs"]) if job else 0
    grid = (R // tr, C // tc)

    def body(w_ref, g_ref, m_ref, v_ref, *rest):
        jin, (d_ref, mo_ref, vo_ref) = rest[:n_ji], rest[n_ji:n_ji + 3]
        jout, jsems = rest[n_ji + 3:n_ji + 3 + n_jo], rest[n_ji + 3 + n_jo:]
        i, j = pl.program_id(0), pl.program_id(1)
        if job:
            @pl.when((i == 0) & (j == 0))
            def _():
                job["start"](jin, jout, jsems)

        gv = g_ref[...]
        mn = ADAM_B1 * m_ref[...] + (1.0 - ADAM_B1) * gv
        vn = ADAM_B2 * v_ref[...] + (1.0 - ADAM_B2) * (gv * gv)
        m_hat = mn / (1.0 - ADAM_B1 ** ADAM_STEP)
        v_hat = vn / (1.0 - ADAM_B2 ** ADAM_STEP)
        d_ref[...] = -ADAM_LR * (m_hat / (jnp.sqrt(v_hat) + ADAM_EPS) + ADAM_WD * w_ref[...])
        mo_ref[...] = mn
        vo_ref[...] = vn
        if job:
            @pl.when((i == grid[0] - 1) & (j == grid[1] - 1))
            def _():
                job["finish"](jin, jout, jsems)

    spec = pl.BlockSpec((tr, tc), lambda i, j: (i, j))
    out = jax.ShapeDtypeStruct((R, C), F32)
    d, mn, vn, *got = pl.pallas_call(
        body, name=name, grid=grid, in_specs=[spec] * 4 + [ANY] * n_ji, out_specs=(spec,) * 3 + (ANY,) * n_jo,
        out_shape=(out,) * 3 + tuple(job["outs"] if job else ()), scratch_shapes=list(job["sems"] if job else ()),
        compiler_params=_params(("arbitrary", "arbitrary") if job else ("parallel", "parallel")),
    )(w2, g2, m2, v2, *(job["ins"] if job else ()))
    res = (d.reshape(shape), mn.reshape(shape), vn.reshape(shape))
    return res + (got,) if job else res


def _sum_leading(t, name):
    n, R, C = t.shape
    tr = _pick(R, (256, 128, 64, 32, 16, 8))

    def body(t_ref, o_ref):
        acc = t_ref[0]
        for i in range(1, n):
            acc = acc + t_ref[i]
        o_ref[...] = acc

    return pl.pallas_call(
        body, name=name, grid=(R // tr,), in_specs=[pl.BlockSpec((n, tr, C), lambda i: (0, i, 0))],
        out_specs=pl.BlockSpec((tr, C), lambda i: (i, 0)), out_shape=jax.ShapeDtypeStruct((R, C), F32),
        compiler_params=_params(("parallel",)))(t)


def _coords():
    return lax.axis_index("x"), lax.axis_index("y"), lax.axis_index("c")


def _other_chips(x, y):
    return [(1 - x, y), (x, 1 - y), (1 - x, 1 - y)]


def _all_gather8(t, name):
    R, C = t.shape

    def body(x_ref, out_ref, send_sems, recv_sems, local_sem):
        x, y, c = _coords()
        me, sibling = (x, y, c), (x, y, 1 - c)
        chips = _other_chips(x, y)

        def slot(px, py, pc):
            return out_ref.at[4 * px + 2 * py + pc]

        def copy(k, block, to, src=None):
            return pltpu.make_async_remote_copy(
                src_ref=slot(*block) if src is None else src, dst_ref=slot(*block),
                send_sem=send_sems.at[k], recv_sem=recv_sems.at[k], device_id=to, device_id_type=MESH)

        mine = pltpu.make_async_copy(x_ref, slot(*me), local_sem)
        mine.start()
        first = [copy(0, me, sibling, src=x_ref)]
        first += [copy(1 + j, me, (*chip, c), src=x_ref) for j, chip in enumerate(chips)]
        for cp in first:
            cp.start()
        passed = [copy(4 + j, (*chip, c), sibling) for j, chip in enumerate(chips)]
        for j, chip in enumerate(chips):
            copy(1 + j, (*chip, c), me).wait_recv()
            passed[j].start()
        copy(0, sibling, me).wait_recv()
        for j, chip in enumerate(chips):
            copy(4 + j, (*chip, 1 - c), me).wait_recv()
        for cp in first + passed:
            cp.wait_send()
        mine.wait()

    return pl.pallas_call(
        body, name=name, out_shape=jax.ShapeDtypeStruct((N_DEV, R, C), t.dtype),
        in_specs=[pl.BlockSpec(memory_space=pltpu.VMEM)], out_specs=pl.BlockSpec(memory_space=pltpu.VMEM),
        scratch_shapes=[pltpu.SemaphoreType.DMA((7,)), pltpu.SemaphoreType.DMA((7,)), pltpu.SemaphoreType.DMA],
        compiler_params=pltpu.CompilerParams(vmem_limit_bytes=VMEM_LIMIT_BYTES))(t)


def _window(ref, r0, nr, c0, nc):
    return ref.at[pl.ds(r0, nr), pl.ds(c0, nc)]


LOCAL_CHUNKS = 4


def _gather_job(pieces, out_shape):
    n = len(pieces)

    def ctx(ins, outs, sems):
        x, y, c = _coords()
        buf = outs[0]

        def place(p, chip_idx, r0, nr):
            _, _, kind, base = pieces[p]
            r, cs = ins[p].shape[1], ins[p].shape[2]
            if kind == "row":
                return _window(buf, base + chip_idx * r + r0, nr, 0, cs)
            return _window(buf, r0, nr, base + chip_idx * cs, cs)

        def ici(p, j, chip, src, dst):
            return pltpu.make_async_remote_copy(
                src_ref=src, dst_ref=dst, send_sem=sems[0].at[3 * p + j], recv_sem=sems[1].at[3 * p + j],
                device_id=(*chip, c), device_id_type=MESH)

        def d2d(p, j, win):
            return pltpu.make_async_remote_copy(
                src_ref=win, dst_ref=win, send_sem=sems[2].at[3 * p + j], recv_sem=sems[3].at[3 * p + j],
                device_id=(x, y, 1 - c), device_id_type=MESH)

        def local(p):
            lidx, r = pieces[p][1], ins[p].shape[1]
            rc = r // (2 * LOCAL_CHUNKS)
            return [pltpu.make_async_copy(ins[p].at[lidx, pl.ds(q * rc, rc), :], place(p, 2 * x + y, q * rc, rc),
                                          sems[4].at[2 * LOCAL_CHUNKS * p + q]) for q in range(2 * LOCAL_CHUNKS)]

        return x, y, c, _other_chips(x, y), place, ici, d2d, local

    def start(ins, outs, sems):
        x, y, c, chips, place, ici, d2d, local = ctx(ins, outs, sems)
        for p in range(n):
            lidx, rh = pieces[p][1], ins[p].shape[1] // 2
            for j, chip in enumerate(chips):
                ici(p, j, chip, ins[p].at[lidx, pl.ds(c * rh, rh), :], place(p, 2 * x + y, c * rh, rh)).start()
        for p in range(n):
            for cp in local(p):
                cp.start()

    def finish(ins, outs, sems):
        x, y, c, chips, place, ici, d2d, local = ctx(ins, outs, sems)
        for p in range(n):
            rh = ins[p].shape[1] // 2
            for j, chip in enumerate(chips):
                landed = place(p, 2 * chip[0] + chip[1], c * rh, rh)
                ici(p, j, chip, landed, landed).wait_recv()
                d2d(p, j, landed).start()
        for p in range(n):
            lidx, rh = pieces[p][1], ins[p].shape[1] // 2
            for j, chip in enumerate(chips):
                theirs = place(p, 2 * chip[0] + chip[1], (1 - c) * rh, rh)
                d2d(p, j, theirs).wait_recv()
                d2d(p, j, place(p, 2 * chip[0] + chip[1], c * rh, rh)).wait_send()
                ici(p, j, chip, ins[p].at[lidx, pl.ds(c * rh, rh), :], place(p, 2 * x + y, c * rh, rh)).wait_send()
            for cp in local(p):
                cp.wait()

    return dict(
        ins=[p[0] for p in pieces], outs=[jax.ShapeDtypeStruct(out_shape, BF16)], start=start, finish=finish,
        sems=[pltpu.SemaphoreType.DMA((3 * n,))] * 4 + [pltpu.SemaphoreType.DMA((2 * LOCAL_CHUNKS * n,))])


def _scatter_job(parts, plan):
    n = len(parts)

    def shard_shape(i):
        kind, _, size = plan[i]
        R, C = parts[i].shape
        return (size, C) if kind == "row" else (R, size)

    def copies(ins, outs, sems):
        x, y, c = _coords()
        cps = []
        for i in range(n):
            kind, base, size = plan[i]
            R, C = ins[i].shape
            for j, chip in enumerate(_other_chips(x, y)):
                their = 2 * chip[0] + chip[1]
                if kind == "row":
                    src = _window(ins[i], base + their * size, size, 0, C)
                else:
                    src = _window(ins[i], 0, R, base + their * size, size)
                cps.append(pltpu.make_async_remote_copy(
                    src_ref=src, dst_ref=outs[i].at[j], send_sem=sems[0].at[3 * i + j],
                    recv_sem=sems[1].at[3 * i + j], device_id=(*chip, c), device_id_type=MESH))
        return cps

    def start(ins, outs, sems):
        for cp in copies(ins, outs, sems):
            cp.start()

    def finish(ins, outs, sems):
        for cp in copies(ins, outs, sems):
            cp.wait()

    return dict(
        ins=list(parts), outs=[jax.ShapeDtypeStruct((3,) + shard_shape(i), F32) for i in range(n)],
        start=start, finish=finish, sems=[pltpu.SemaphoreType.DMA((3 * n,))] * 2)


def _swap_job(t):
    def copy(ins, outs, sems):
        x, y, c = _coords()
        return pltpu.make_async_remote_copy(
            src_ref=ins[0].at[1 - c], dst_ref=outs[0], send_sem=sems[0].at[0], recv_sem=sems[1].at[0],
            device_id=(x, y, 1 - c), device_id_type=MESH)

    return dict(ins=[t], outs=[jax.ShapeDtypeStruct(t.shape[1:], t.dtype)],
                start=lambda ins, outs, sems: copy(ins, outs, sems).start(),
                finish=lambda ins, outs, sems: copy(ins, outs, sems).wait(),
                sems=[pltpu.SemaphoreType.DMA((1,))] * 2)


def _join_jobs(jobs):
    jobs = [j for j in jobs if j]
    if not jobs:
        return None
    if len(jobs) == 1:
        return jobs[0]

    def each(fn_name, ins, outs, sems):
        i = o = s = 0
        for j in jobs:
            ni, no, ns = len(j["ins"]), len(j["outs"]), len(j["sems"])
            j[fn_name](ins[i:i + ni], outs[o:o + no], sems[s:s + ns])
            i, o, s = i + ni, o + no, s + ns

    return dict(ins=[a for j in jobs for a in j["ins"]], outs=[a for j in jobs for a in j["outs"]],
                sems=[a for j in jobs for a in j["sems"]],
                start=lambda ins, outs, sems: each("start", ins, outs, sems),
                finish=lambda ins, outs, sems: each("finish", ins, outs, sems))


def _hosted(call, jobs):
    jobs = [j for j in jobs if j]
    if not jobs:
        return call(None), []
    out, *rest = call(_join_jobs(jobs))
    per = []
    for j in jobs:
        n = len(j["outs"])
        per.append(rest[:n])
        rest = rest[n:]
    return out, per


def _run_job(job, name):
    n_i, n_o = len(job["ins"]), len(job["outs"])

    def body(*refs):
        ins, outs, sems = refs[:n_i], refs[n_i:n_i + n_o], refs[n_i + n_o:]
        job["start"](ins, outs, sems)
        job["finish"](ins, outs, sems)

    return pl.pallas_call(
        body, name=name, in_specs=[ANY] * n_i, out_specs=tuple([ANY] * n_o), out_shape=tuple(job["outs"]),
        scratch_shapes=list(job["sems"]),
        compiler_params=pltpu.CompilerParams(vmem_limit_bytes=VMEM_LIMIT_BYTES))(*job["ins"])


def _sibling_share(both, name):
    n = len(both)

    def body(*refs):
        outs = refs[n:2 * n]
        send_sems, recv_sems = refs[2 * n:]
        x, y, c = _coords()
        cps = []
        for i in range(n):
            cp = pltpu.make_async_remote_copy(
                src_ref=outs[i].at[c], dst_ref=outs[i].at[c], send_sem=send_sems.at[i], recv_sem=recv_sems.at[i],
                device_id=(x, y, 1 - c), device_id_type=MESH)
            cp.start()
            cps.append(cp)
        for cp in cps:
            cp.wait()

    return pl.pallas_call(
        body, name=name, in_specs=[ANY] * n, out_specs=tuple([ANY] * n),
        out_shape=tuple(jax.ShapeDtypeStruct(b.shape, b.dtype) for b in both),
        input_output_aliases={i: i for i in range(n)},
        scratch_shapes=[pltpu.SemaphoreType.DMA((n,))] * 2,
        compiler_params=pltpu.CompilerParams(vmem_limit_bytes=VMEM_LIMIT_BYTES))(*both)


def _add_half(full3, recv, core, name):
    _, Rh, C = full3.shape
    tr = _pick(Rh, (256, 176, 128, 64, 32, 16, 8))
    tc = _pick(C, (2048, 1536, 1408, 1024, 512, 256, 128))

    def body(c_ref, a_ref, b_ref, o_ref):
        o_ref[...] = a_ref[...] + b_ref[...]

    return pl.pallas_call(
        body, name=name,
        grid_spec=pltpu.PrefetchScalarGridSpec(
            num_scalar_prefetch=1, grid=(Rh // tr, C // tc),
            in_specs=[pl.BlockSpec((None, tr, tc), lambda i, j, cr: (cr[0], i, j)),
                      pl.BlockSpec((tr, tc), lambda i, j, cr: (i, j))],
            out_specs=pl.BlockSpec((tr, tc), lambda i, j, cr: (i, j))),
        out_shape=jax.ShapeDtypeStruct((Rh, C), F32),
        compiler_params=_params(("parallel", "parallel")))(core, full3, recv)


def _add_scattered(part, recv, kind, base, size, core_chip, name):
    _, rs, cs = recv.shape
    tr = _pick(rs, (256, 176, 128, 64, 32, 16, 8))
    tc = _pick(cs, (2048, 1536, 1408, 1024, 512, 256, 128))
    assert base % size == 0
    if kind == "row":
        pidx = lambda i, j, cr: ((base // size + cr[1]) * (rs // tr) + i, j)
    else:
        pidx = lambda i, j, cr: (i, (base // size + cr[1]) * (cs // tc) + j)

    def body(c_ref, a_ref, r_ref, o_ref):
        o_ref[...] = ((a_ref[...] + r_ref[0]) + r_ref[1]) + r_ref[2]

    return pl.pallas_call(
        body, name=name,
        grid_spec=pltpu.PrefetchScalarGridSpec(
            num_scalar_prefetch=1, grid=(rs // tr, cs // tc),
            in_specs=[pl.BlockSpec((tr, tc), pidx), pl.BlockSpec((3, tr, tc), lambda i, j, cr: (0, i, j))],
            out_specs=pl.BlockSpec((None, tr, tc), lambda i, j, cr: (cr[0], i, j))),
        out_shape=jax.ShapeDtypeStruct((2, rs, cs), F32),
        compiler_params=_params(("parallel", "parallel")))(core_chip, part, recv)


def _rs_split(g, windows):
    R, C = g.shape
    if windows[0][0] == "row":
        size = windows[0][2]
        t = g.reshape(N_CHIPS, 2, size // 2, C).transpose(1, 0, 2, 3).reshape(2, R // 2, C)
        return t, [(k, b // 2, s // 2) for k, b, s in windows]
    return g.reshape(2, R // 2, C), list(windows)


def _rs_finish(part, windows, got, core_chip, tag):
    both = [_add_scattered(part, r, k, b, s, core_chip, f"rs_add_{tag}_{n}")
            for n, (r, (k, b, s)) in enumerate(zip(got, windows))]
    both = _sibling_share(both, f"rs_share_{tag}")
    return [t.reshape(2 * t.shape[1], t.shape[2]) for t in both]


def _rope_tables(positions, S):
    half = ROPE_DIM // 2
    inv_freq = ROPE_THETA ** (-jnp.arange(0, ROPE_DIM, 2, dtype=F32) / ROPE_DIM)
    ang = positions.reshape(S, 1).astype(F32) * inv_freq[None, :]
    cos, sin = jnp.cos(ang), jnp.sin(ang)
    zeros = jnp.zeros((S, half), F32)
    rest0 = jnp.zeros((S, HEAD_DIM - ROPE_DIM), F32)
    cs = jnp.concatenate([cos, cos, jnp.ones((S, HEAD_DIM - ROPE_DIM), F32)], axis=1)
    sa = jnp.concatenate([-sin, zeros, rest0], axis=1)
    sb = jnp.concatenate([zeros, sin, rest0], axis=1)
    return cs, sa, sb


def kernel(x, c, positions, ada_w, ada_b, norm_mix, norm_ffn, ab_w_in, sgu_w, sgu_b, ab_w_out, conv_w_in, conv_w, conv_w_out, ffn_w_gate, ffn_w_up, ffn_w_down, final_norm, loss_target, m_ada_w, m_ada_b, m_norm_mix, m_norm_ffn, m_ab_w_in, m_sgu_w, m_sgu_b, m_ab_w_out, m_conv_w_in, m_conv_w, m_conv_w_out, m_ffn_w_gate, m_ffn_w_up, m_ffn_w_down, m_final_norm, v_ada_w, v_ada_b, v_norm_mix, v_norm_ffn, v_ab_w_in, v_sgu_w, v_sgu_b, v_ab_w_out, v_conv_w_in, v_conv_w, v_conv_w_out, v_ffn_w_gate, v_ffn_w_up, v_ffn_w_down, v_final_norm):
    S, D = x.shape[1], x.shape[2]
    L = ada_w.shape[0]
    n_mix_heads = D // HEAD_DIM
    n_attn = 3 * n_mix_heads // 4
    A = n_attn * HEAD_DIM
    G = n_mix_heads - n_attn
    F = ffn_w_gate.shape[2] * N_CHIPS
    mix_in = ab_w_in.shape[2] * N_CHIPS
    xi, yi, ci = _coords()
    chip = 2 * xi + yi
    dev = 4 * xi + 2 * yi + ci
    core1 = jnp.reshape(ci, (1,)).astype(jnp.int32)
    chip1 = jnp.reshape(chip, (1,)).astype(jnp.int32)
    x2 = x.reshape(S, D)
    target = loss_target.reshape(S, D)

    n_conv = conv_w.size
    w0 = D + n_conv
    w0p = -(-w0 // 128) * 128
    pack = jnp.zeros((8, w0p), F32).at[0, :D].set(c[0]).at[0, D:w0].set(conv_w.reshape(-1))
    g0 = _all_gather8(pack, "gather_cond")
    c_all = g0[:, 0, :D]
    c_act = c_all * jax.nn.sigmoid(c_all)
    conv_full = jnp.concatenate(
        [g0[2 * j, 0, D:w0].reshape(conv_w.shape) for j in range(N_CHIPS)], axis=2)
    mod_part = _ada_fwd(c_act, ada_w, "ada_fwd")
    n_ada = ada_w.shape[2]
    g1 = _all_gather8(mod_part.reshape(L * N_DEV, n_ada), "gather_mod")
    mod_all = jnp.concatenate([g1[2 * j].reshape(L, N_DEV, n_ada) for j in range(N_CHIPS)], axis=2)
    mod = lax.dynamic_index_in_dim(mod_all, dev, axis=1, keepdims=False) + ada_b
    mods = mod.reshape(L, 6, 1, D)
    cs, sa, sb = _rope_tables(positions, S)

    bf = lambda t: t.astype(BF16)
    w_in_e, w_out_e = bf(ab_w_in), bf(ab_w_out)
    w_in_o, w_out_o = bf(conv_w_in), bf(conv_w_out)
    w_gate, w_up, w_down = bf(ffn_w_gate), bf(ffn_w_up), bf(ffn_w_down)
    def gather_jobs(l):
        i = l // 2
        first, n_in_cols = ((w_in_e, w_out_e), mix_in) if l % 2 == 0 else ((w_in_o, w_out_o), 3 * D)
        return [_gather_job([(first[0], i, "col", 0)], (D, n_in_cols)),
                _gather_job([(first[1], i, "row", 0)], (D, D)),
                _gather_job([(w_gate, l, "col", 0), (w_up, l, "col", F)], (D, 2 * F)),
                _gather_job([(w_down, l, "row", 0)], (F, D))]

    first_jobs = gather_jobs(0)
    layer_w = [[_run_job(first_jobs[0], "gather_w0_in")[0], None, None, None]]

    saved = []
    xc = x2
    pending = None
    for l in range(L):
        i = l // 2
        nxt = gather_jobs(l + 1) if l + 1 < L else [None] * 4
        if l + 1 < L:
            layer_w.append([None] * 4)

        def mm_fwd(a, b, out_dtype, name, slot, also=None):
            jobs = [nxt[slot], first_jobs[also] if also is not None else None]
            out, got = _hosted(lambda job: _mm(a, b, "nn", out_dtype, name, job=job), jobs)
            if nxt[slot] is not None:
                layer_w[l + 1][slot] = got.pop(0)[0]
            if also is not None:
                layer_w[l][also] = got.pop(0)[0]
            return out

        sh_m, sc_m, g_m, sh_f, sc_f, g_f = (mods[l, t] for t in range(6))
        weff_m = norm_mix[l][None, :] * (1.0 + sc_m)
        weff_f = norm_ffn[l][None, :] * (1.0 + sc_f)
        if pending is None:
            _, h = _norm_mod(xc, None, None, weff_m, sh_m, f"norm_mix{l}")
        else:
            xc, h = _norm_mod(xc, pending[0], pending[1], weff_m, sh_m, f"norm_mix{l}")
        z = mm_fwd(h, layer_w[l][0], BF16, f"mm_in{l}", 0, also=1 if l == 0 else None)
        st = dict(x=xc, h=h, z=z, weff_m=weff_m, weff_f=weff_f, g_m=g_m, g_f=g_f, sc_m=sc_m, sc_f=sc_f)
        if l % 2 == 0:
            q, k, v = _qkv_prep(z, cs, sa, sb, n_attn, f"qkv_prep{l}")
            o, lse, got = _attn_fwd(q, k, v, n_attn, f"attn_fwd{l}",
                                    job=_join_jobs(first_jobs[2:]) if l == 0 else None)
            if l == 0:
                layer_w[0][2], layer_w[0][3] = got
            bT = sgu_b[i].T
            so = _sgu_fwd(z, sgu_w[i], bT, n_attn, f"sgu_fwd{l}")
            cat = jnp.concatenate([o, so], axis=1)
            st.update(q=q, k=k, v=v, o=o, lse=lse, bT=bT)
        else:
            w8 = jnp.zeros((8, D), F32).at[:3].set(conv_full[i])
            cat = _conv_fwd(z, w8, f"conv_fwd{l}")
            st.update(w8=w8)
        mix = mm_fwd(cat, layer_w[l][1], F32, f"mm_out{l}", 1)
        x1, h2 = _norm_mod(xc, mix, g_m, weff_f, sh_f, f"norm_ffn{l}")
        ab = mm_fwd(h2, layer_w[l][2], BF16, f"mm_gu{l}", 2)
        f = _swiglu(ab, f"swiglu{l}")
        yv = mm_fwd(f, layer_w[l][3], F32, f"mm_down{l}", 3)
        st.update(cat=cat, mix=mix, x1=x1, h2=h2, ab=ab, f=f, y=yv)
        saved.append(st)
        xc = x1
        pending = (yv, g_f)

    dx, loss11, dfinal, dy, dg_f = _loss_head(xc, pending[0], pending[1], final_norm[None, :], target, "loss_head")
    loss = lax.psum(loss11[0, 0], ("x", "y", "c"))

    dmods = [None] * L
    dnorm_mix, dnorm_ffn = [None] * L, [None] * L
    big = {l: {} for l in range(L)}
    core_chip = jnp.concatenate([core1, chip1])
    dsgu_w, dsgu_b, dconv = [None] * (L - L // 2), [None] * (L - L // 2), [None] * (L // 2)

    def mm_bwd(a, b, mode, out_dtype, name, scatters=(), swap=None):
        jobs = [_scatter_job([p[0]] * len(p[1]), p[1]) for p in scatters]
        jobs += [_swap_job(swap[0])] if swap else []
        out, got = _hosted(lambda job: _mm(a, b, mode, out_dtype, name, job=job), jobs)
        for (part, windows, lay, keys), recv in zip(scatters, got):
            for key, red in zip(keys, _rs_finish(part, windows, recv, core_chip, f"{lay}_{keys[0]}")):
                big[lay][key] = red
        half = _add_half(swap[0], got[-1][0], core_chip, f"rs_add_half_{swap[1]}") if swap else None
        return out, half

    above = None
    for l in reversed(range(L)):
        i = l // 2
        st = saved[l]
        w_in, w_out, w_gu, w_dn = layer_w[l]
        if above is None:
            df, _ = mm_bwd(dy, w_dn, "nt", BF16, f"mm_down_dx{l}")
            dw_dn, _ = mm_bwd(st["f"], dy, "tn", F32, f"mm_down_dw{l}")
        else:
            s_out, t_in, win_in = above
            df, p_in = mm_bwd(dy, w_dn, "nt", BF16, f"mm_down_dx{l}", [s_out], (t_in, f"{l + 1}_in"))
            dw_dn, _ = mm_bwd(st["f"], dy, "tn", F32, f"mm_down_dw{l}", [(p_in, win_in, l + 1, ["in"])])
        t_dn, win_dn = _rs_split(dw_dn, [("row", 0, F // N_CHIPS)])
        dab = _swiglu_bwd(st["ab"], df, f"swiglu_bwd{l}")
        dh2, p_dn = mm_bwd(dab, w_gu, "nt", F32, f"mm_gu_dx{l}", swap=(t_dn, f"{l}_down"))
        dw_gu, _ = mm_bwd(st["h2"], dab, "tn", F32, f"mm_gu_dw{l}", [(p_dn, win_dn, l, ["down"])])
        t_gu, win_gu = _rs_split(dw_gu, [("col", 0, F // N_CHIPS), ("col", F, F // N_CHIPS)])
        dx1, dsh_f, dweff_f, dmix, dg_m = _norm_mod_bwd(dh2, st["x1"], st["weff_f"], dx, st["mix"], st["g_m"],
                                                        f"norm_ffn_bwd{l}")
        dcat, p_gu = mm_bwd(dmix, w_out, "nt", F32, f"mm_out_dx{l}", swap=(t_gu, f"{l}_gu"))
        dw_out = _mm(st["cat"], dmix, "tn", F32, f"mm_out_dw{l}")
        t_out, win_out = _rs_split(dw_out, [("row", 0, D // N_CHIPS)])
        if l % 2 == 0:
            dq, dk, dv = _attn_bwd(st["q"], st["k"], st["v"], st["o"], st["lse"], dcat, n_attn, f"attn_bwd{l}")
            dqkv = _dqkv_post(dq, dk, dv, cs, sa, sb, n_attn, f"dqkv_post{l}")
            duv, dsgu_w[i], dbT = _sgu_bwd(st["z"], dcat, sgu_w[i], st["bT"], n_attn, f"sgu_bwd{l}")
            dsgu_b[i] = dbT.T
            dz = jnp.concatenate([dqkv, duv], axis=1)
        else:
            dz, dw8 = _conv_bwd(st["z"], dcat, st["w8"], f"conv_bwd{l}")
            dconv[i] = dw8[:3]
        dh, p_out = mm_bwd(dz, w_in, "nt", F32, f"mm_in_dx{l}", [(p_gu, win_gu[:1], l, ["gate"])],
                           (t_out, f"{l}_out"))
        s_out = (p_out, win_out, l, ["out"])
        dw_in, _ = mm_bwd(st["h"], dz, "tn", F32, f"mm_in_dw{l}",
                          [(p_gu, win_gu[1:], l, ["up"])] + ([s_out] if l == 0 else []))
        t_in, win_in = _rs_split(dw_in, [("col", 0, w_in.shape[1] // N_CHIPS)])
        above = (s_out, t_in, win_in)
        dmod_f = [dsh_f, dweff_f * norm_ffn[l][None, :], dg_f]
        if l > 0:
            dx, dsh_m, dweff_m, dy, dg_f = _norm_mod_bwd(dh, st["x"], st["weff_m"], dx1, saved[l - 1]["y"],
                                                         saved[l - 1]["g_f"], f"norm_mix_bwd{l}")
        else:
            dx, dsh_m, dweff_m = _norm_mod_bwd(dh, st["x"], st["weff_m"], dx1, None, None, f"norm_mix_bwd{l}")
        dmods[l] = jnp.concatenate([dsh_m, dweff_m * norm_mix[l][None, :], dg_m] + dmod_f, axis=1)
        dnorm_mix[l] = dweff_m * (1.0 + st["sc_m"])
        dnorm_ffn[l] = dweff_f * (1.0 + st["sc_f"])
    _, t_in, win_in = above
    (recv_in,) = _run_job(_swap_job(t_in), "rs_swap_0_in")
    p_in0 = _add_half(t_in, recv_in, core_chip, "rs_add_half_0_in")
    grad_x = dx.reshape(1, S, D)

    dmod = jnp.concatenate(dmods, axis=0)
    small = [dmod.reshape(-1), jnp.concatenate(dnorm_mix, 0).reshape(-1), jnp.concatenate(dnorm_ffn, 0).reshape(-1),
             jnp.stack(dsgu_w).reshape(-1), jnp.stack(dsgu_b).reshape(-1), jnp.stack(dconv).reshape(-1), dfinal.reshape(-1)]
    sizes = [t.size for t in small]
    flat = jnp.concatenate(small)
    n_flat = flat.size
    rows = -(-n_flat // (128 * 8)) * 8
    flat = jnp.concatenate([flat, jnp.zeros((rows * 128 - n_flat,), F32)]).reshape(rows, 128)
    g2 = _all_gather8(flat, "gather_small")
    tot = _sum_leading(g2, "sum_small").reshape(-1)
    offs = [0]
    for s in sizes:
        offs.append(offs[-1] + s)
    take = lambda n, shape: tot[offs[n]:offs[n + 1]].reshape(shape)
    g_ada_b = take(0, ada_b.shape)
    g_norm_mix = take(1, norm_mix.shape)
    g_norm_ffn = take(2, norm_ffn.shape)
    g_sgu_w = take(3, sgu_w.shape)
    g_sgu_b = take(4, sgu_b.shape)
    g_conv_full = take(5, conv_full.shape)
    n_cw = conv_w.shape[2]
    g_conv_w = lax.dynamic_slice_in_dim(g_conv_full, chip * n_cw, n_cw, axis=2)
    g_final = take(6, final_norm.shape)
    dmod_all = g2[:, :, :].reshape(N_DEV, -1)[:, :offs[1]].reshape(N_DEV, L, 6 * D)
    dmod_mine = lax.dynamic_slice_in_dim(dmod_all, chip * n_ada, n_ada, axis=2).transpose(1, 0, 2)
    g_ada_w = _ada_bwd(c_act, dmod_mine, "ada_bwd")
    *ada_update, got = _adamw(ada_w, g_ada_w, m_ada_w, v_ada_w, "adamw_ada_w", job=_scatter_job([p_in0], win_in))
    (big[0]["in"],) = _rs_finish(p_in0, win_in, got, core_chip, "0_in")

    def stack(key, layers):
        return jnp.stack([big[l][key] for l in layers])

    even, odd, every = list(range(0, L, 2)), list(range(1, L, 2)), list(range(L))
    grads = dict(
        ada_w=g_ada_w, ada_b=g_ada_b, norm_mix=g_norm_mix, norm_ffn=g_norm_ffn,
        ab_w_in=stack("in", even), sgu_w=g_sgu_w, sgu_b=g_sgu_b, ab_w_out=stack("out", even),
        conv_w_in=stack("in", odd), conv_w=g_conv_w, conv_w_out=stack("out", odd),
        ffn_w_gate=stack("gate", every), ffn_w_up=stack("up", every), ffn_w_down=stack("down", every),
        final_norm=g_final)
    weights = dict(ada_w=ada_w, ada_b=ada_b, norm_mix=norm_mix, norm_ffn=norm_ffn, ab_w_in=ab_w_in, sgu_w=sgu_w,
                   sgu_b=sgu_b, ab_w_out=ab_w_out, conv_w_in=conv_w_in, conv_w=conv_w, conv_w_out=conv_w_out,
                   ffn_w_gate=ffn_w_gate, ffn_w_up=ffn_w_up, ffn_w_down=ffn_w_down, final_norm=final_norm)
    ms = dict(ada_w=m_ada_w, ada_b=m_ada_b, norm_mix=m_norm_mix, norm_ffn=m_norm_ffn, ab_w_in=m_ab_w_in, sgu_w=m_sgu_w,
              sgu_b=m_sgu_b, ab_w_out=m_ab_w_out, conv_w_in=m_conv_w_in, conv_w=m_conv_w, conv_w_out=m_conv_w_out,
              ffn_w_gate=m_ffn_w_gate, ffn_w_up=m_ffn_w_up, ffn_w_down=m_ffn_w_down, final_norm=m_final_norm)
    vs = dict(ada_w=v_ada_w, ada_b=v_ada_b, norm_mix=v_norm_mix, norm_ffn=v_norm_ffn, ab_w_in=v_ab_w_in, sgu_w=v_sgu_w,
              sgu_b=v_sgu_b, ab_w_out=v_ab_w_out, conv_w_in=v_conv_w_in, conv_w=v_conv_w, conv_w_out=v_conv_w_out,
              ffn_w_gate=v_ffn_w_gate, ffn_w_up=v_ffn_w_up, ffn_w_down=v_ffn_w_down, final_norm=v_final_norm)
    names = list(weights)
    deltas, new_m, new_v = {}, {}, {}
    for n in names:
        w, g = weights[n], grads[n]
        if n == "ada_w":
            deltas[n], new_m[n], new_v[n] = ada_update
        elif w.ndim == 1:
            d_, m_, v_ = _adamw(w[None, :], g[None, :], ms[n][None, :], vs[n][None, :], f"adamw_{n}")
            deltas[n], new_m[n], new_v[n] = d_[0], m_[0], v_[0]
        else:
            deltas[n], new_m[n], new_v[n] = _adamw(w, g, ms[n], vs[n], f"adamw_{n}")
    return (loss, grad_x, *[grads[n] for n in names], *[deltas[n] for n in names],
            *[new_m[n] for n in names], *[new_v[n] for n in names])
```

```python
import functools
import math

import jax
import jax.numpy as jnp
from jax import lax
from jax.experimental import pallas as pl
from jax.experimental.pallas import tpu as pltpu

F32 = jnp.float32
BF16 = jnp.bfloat16
HEAD_DIM = 128
CHUNK = 128
ATTN_BLOCK = 128
ATTN_LANES_FWD = 2
ATTN_LANES_BWD = 2
DILATIONS = (1, 4, 16)
ROPE_DIM = HEAD_DIM // 4
ROPE_THETA = 500000.0
EPS = 1e-6
MASKED = -1e30
ADAM_LR, ADAM_B1, ADAM_B2, ADAM_EPS, ADAM_WD, ADAM_STEP = 0.001, 0.9, 0.999, 1e-08, 0.01, 10
VMEM_LIMIT_BYTES = 56 * 1024 * 1024
MESH = pl.DeviceIdType.MESH
ANY = pl.BlockSpec(memory_space=pl.ANY)
N_CHIPS = 4
N_DEV = 8


def _pick(n, cands):
    for t in cands:
        if n % t == 0:
            return t
    return n


def _params(sem):
    return pltpu.CompilerParams(dimension_semantics=sem, vmem_limit_bytes=VMEM_LIMIT_BYTES)


MM_VMEM_BUDGET = 44 * 1024 * 1024
_TILES = (2048, 1536, 1408, 1024, 512, 256, 128)


def _mm_tiles(M, N, K, out_bytes):
    best = None
    for tm in [t for t in _TILES if M % t == 0] or [M]:
        for tn in [t for t in _TILES if N % t == 0] or [N]:
            for tk in [t for t in _TILES if K % t == 0] or [K]:
                nk = K // tk
                vmem = 2 * 2 * (tm * tk + tk * tn) + 2 * out_bytes * tm * tn + 4 * tm * tn
                vmem += 4 * tm * tn if nk > 1 and out_bytes == 2 else 0
                if vmem > MM_VMEM_BUDGET:
                    continue
                key = ((M // tm) * (N // tn) * nk, -tk, -tm)
                if best is None or key < best[0]:
                    best = (key, (tm, tn, tk))
    assert best is not None, (M, N, K)
    return best[1]


def _mm(a, b, mode, out_dtype, name, layer=None, job=None):
    bshape = b.shape[1:] if layer is not None else b.shape
    if mode == "nn":
        (M, K), (K2, N) = a.shape, bshape
    elif mode == "nt":
        (M, K), (N, K2) = a.shape, bshape
    else:
        (K, M), (K2, N) = a.shape, bshape
    assert K == K2, (a.shape, b.shape, mode)
    in_place = out_dtype == F32
    tm, tn, tk = _mm_tiles(M, N, K, 4 if in_place else 2)
    nk = K // tk
    dims = {"nn": (((1,), (0,)), ((), ())), "nt": (((1,), (1,)), ((), ())), "tn": (((0,), (0,)), ((), ()))}[mode]

    n_ji = len(job["ins"]) if job else 0
    n_jo = len(job["outs"]) if job else 0
    n_acc = 1 if nk > 1 and not in_place else 0
    grid = (M // tm, N // tn, nk)

    def body(a_ref, b_ref, *rest):
        jin, o_ref, jout = rest[:n_ji], rest[n_ji], rest[n_ji + 1:n_ji + 1 + n_jo]
        scratch = rest[n_ji + 1 + n_jo:]
        acc, sems = scratch[:n_acc], scratch[n_acc:]
        i, j, k = pl.program_id(0), pl.program_id(1), pl.program_id(2)
        if job:
            @pl.when((i == 0) & (j == 0) & (k == 0))
            def _():
                job["start"](jin, jout, sems)

        def product():
            return lax.dot_general(a_ref[...].astype(BF16), b_ref[...].astype(BF16), dims, preferred_element_type=F32)

        if nk == 1:
            o_ref[...] = product().astype(o_ref.dtype)
        else:
            acc_ref = o_ref if in_place else acc[0]

            @pl.when(k == 0)
            def _():
                acc_ref[...] = jnp.zeros_like(acc_ref)

            acc_ref[...] += product()

            if not in_place:
                @pl.when(k == nk - 1)
                def _():
                    o_ref[...] = acc_ref[...].astype(o_ref.dtype)

        if job:
            @pl.when((i == grid[0] - 1) & (j == grid[1] - 1) & (k == grid[2] - 1))
            def _():
                job["finish"](jin, jout, sems)

    if mode == "tn":
        a_spec = pl.BlockSpec((tk, tm), lambda i, j, k: (k, i))
    else:
        a_spec = pl.BlockSpec((tm, tk), lambda i, j, k: (i, k))
    if mode == "nt":
        bblk, bidx = (tn, tk), (lambda i, j, k: (j, k))
    else:
        bblk, bidx = (tk, tn), (lambda i, j, k: (k, j))
    if layer is not None:
        b_spec = pl.BlockSpec((None,) + bblk, lambda i, j, k: (layer,) + bidx(i, j, k))
    else:
        b_spec = pl.BlockSpec(bblk, bidx)
    out_spec = pl.BlockSpec((tm, tn), lambda i, j, k: (i, j))
    out_shape = jax.ShapeDtypeStruct((M, N), out_dtype)
    acc_scratch = [pltpu.VMEM((tm, tn), F32)] * n_acc
    if not job:
        return pl.pallas_call(
            body, name=name, grid=grid, in_specs=[a_spec, b_spec], out_specs=out_spec, out_shape=out_shape,
            scratch_shapes=acc_scratch, compiler_params=_params(("parallel", "parallel", "arbitrary")),
        )(a, b)
    return pl.pallas_call(
        body, name=name, grid=grid, in_specs=[a_spec, b_spec] + [ANY] * n_ji,
        out_specs=(out_spec,) + (ANY,) * n_jo, out_shape=(out_shape,) + tuple(job["outs"]),
        scratch_shapes=acc_scratch + list(job["sems"]),
        compiler_params=_params(("arbitrary", "arbitrary", "arbitrary")),
    )(a, b, *job["ins"])


def _rows(S):
    return _pick(S, (256, 128, 64, 32, 16, 8))


def _row_spec(tr, width, col=0):
    return pl.BlockSpec((tr, width), lambda i: (i, col))


def _vec_spec(rows, width):
    return pl.BlockSpec((rows, width), lambda i: (0, 0))


def _rms(xv):
    return lax.rsqrt(jnp.mean(xv * xv, axis=-1, keepdims=True) + EPS)


def _norm_mod(x, y, g, w_eff, sh, name):
    S, D = x.shape
    tr = _rows(S)
    fused = y is not None

    def body(*refs):
        if fused:
            x_ref, y_ref, g_ref, w_ref, s_ref, x1_ref, h_ref = refs
            xv = x_ref[...] + g_ref[...] * y_ref[...]
            x1_ref[...] = xv
        else:
            x_ref, w_ref, s_ref, h_ref = refs
            xv = x_ref[...]
        h_ref[...] = (xv * _rms(xv) * w_ref[...] + s_ref[...]).astype(BF16)

    big, vec = _row_spec(tr, D), _vec_spec(1, D)
    if fused:
        ins, in_specs = (x, y, g, w_eff, sh), [big, big, vec, vec, vec]
        out_shape = (jax.ShapeDtypeStruct((S, D), F32), jax.ShapeDtypeStruct((S, D), BF16))
        out_specs = (big, big)
    else:
        ins, in_specs = (x, w_eff, sh), [big, vec, vec]
        out_shape = jax.ShapeDtypeStruct((S, D), BF16)
        out_specs = big
    out = pl.pallas_call(body, name=name, grid=(S // tr,), in_specs=in_specs, out_specs=out_specs,
                         out_shape=out_shape, compiler_params=_params(("parallel",)))(*ins)
    return out if fused else (None, out)


def _norm_mod_bwd(dh, x, w_eff, dres, y, g, name):
    S, D = x.shape
    tr = _rows(S)
    gated = y is not None

    def body(dh_ref, x_ref, w_ref, r_ref, *rest):
        if gated:
            y_ref, g_ref, dx_ref, dsh_ref, dw_ref, dy_ref, dg_ref = rest
        else:
            dx_ref, dsh_ref, dw_ref = rest
        xv = x_ref[...]
        dhv = dh_ref[...].astype(F32)
        r = _rms(xv)
        xn = xv * r
        dxn = dhv * w_ref[...]
        dxv = r_ref[...] + r * (dxn - xn * jnp.mean(dxn * xn, axis=-1, keepdims=True))
        dx_ref[...] = dxv

        @pl.when(pl.program_id(0) == 0)
        def _():
            dsh_ref[...] = jnp.zeros_like(dsh_ref)
            dw_ref[...] = jnp.zeros_like(dw_ref)
            if gated:
                dg_ref[...] = jnp.zeros_like(dg_ref)

        dsh_ref[...] += jnp.sum(dhv, axis=0, keepdims=True)
        dw_ref[...] += jnp.sum(dhv * xn, axis=0, keepdims=True)
        if gated:
            dy_ref[...] = (dxv * g_ref[...]).astype(BF16)
            dg_ref[...] += jnp.sum(dxv * y_ref[...], axis=0, keepdims=True)

    big, vec = _row_spec(tr, D), _vec_spec(1, D)
    f32v = jax.ShapeDtypeStruct((1, D), F32)
    outs = (jax.ShapeDtypeStruct((S, D), F32), f32v, f32v)
    if not gated:
        return pl.pallas_call(
            body, name=name, grid=(S // tr,), in_specs=[big, big, vec, big], out_specs=(big, vec, vec), out_shape=outs,
            compiler_params=_params(("arbitrary",)))(dh, x, w_eff, dres)
    return pl.pallas_call(
        body, name=name, grid=(S // tr,), in_specs=[big, big, vec, big, big, vec],
        out_specs=(big, vec, vec, big, vec), out_shape=outs + (jax.ShapeDtypeStruct((S, D), BF16), f32v),
        compiler_params=_params(("arbitrary",)))(dh, x, w_eff, dres, y, g)


def _loss_head(x, y, g, gamma, target, name):
    S, D = x.shape
    tr = _rows(S)

    def body(x_ref, y_ref, g_ref, gm_ref, t_ref, dx_ref, loss_ref, dgm_ref, dy_ref, dg_ref):
        yv = y_ref[...]
        xv = x_ref[...] + g_ref[...] * yv
        r = _rms(xv)
        xn = xv * r
        err = xn * gm_ref[...] - t_ref[...]
        dout = err * (1.0 / D)
        dxn = dout * gm_ref[...]
        dxv = r * (dxn - xn * jnp.mean(dxn * xn, axis=-1, keepdims=True))
        dx_ref[...] = dxv
        dy_ref[...] = (dxv * g_ref[...]).astype(BF16)

        @pl.when(pl.program_id(0) == 0)
        def _():
            loss_ref[...] = jnp.zeros_like(loss_ref)
            dgm_ref[...] = jnp.zeros_like(dgm_ref)
            dg_ref[...] = jnp.zeros_like(dg_ref)

        loss_ref[...] += 0.5 * jnp.sum(jnp.mean(err * err, axis=-1, keepdims=True), axis=0, keepdims=True)
        dgm_ref[...] += jnp.sum(dout * xn, axis=0, keepdims=True)
        dg_ref[...] += jnp.sum(dxv * yv, axis=0, keepdims=True)

    big, vec = _row_spec(tr, D), _vec_spec(1, D)
    f32v = jax.ShapeDtypeStruct((1, D), F32)
    return pl.pallas_call(
        body, name=name, grid=(S // tr,), in_specs=[big, big, vec, vec, big],
        out_specs=(big, _vec_spec(1, 1), vec, big, vec),
        out_shape=(jax.ShapeDtypeStruct((S, D), F32), jax.ShapeDtypeStruct((1, 1), F32), f32v,
                   jax.ShapeDtypeStruct((S, D), BF16), f32v),
        compiler_params=_params(("arbitrary",)))(x, y, g, gamma, target)


def _silu(a):
    return a * jax.nn.sigmoid(a)


def _swiglu(ab, name):
    S, F2 = ab.shape
    F = F2 // 2
    tr = _pick(S, (256, 128, 64, 32, 16, 8))

    def body(a_ref, b_ref, f_ref):
        f_ref[...] = (_silu(a_ref[...].astype(F32)) * b_ref[...].astype(F32)).astype(BF16)

    return pl.pallas_call(
        body, name=name, grid=(S // tr,), in_specs=[_row_spec(tr, F, 0), _row_spec(tr, F, 1)],
        out_specs=_row_spec(tr, F), out_shape=jax.ShapeDtypeStruct((S, F), BF16),
        compiler_params=_params(("parallel",)))(ab, ab)


def _swiglu_bwd(ab, df, name):
    S, F2 = ab.shape
    F = F2 // 2
    tr = _pick(S, (256, 128, 64, 32, 16, 8))

    def body(a_ref, b_ref, df_ref, da_ref, db_ref):
        a = a_ref[...].astype(F32)
        sg = jax.nn.sigmoid(a)
        dfv = df_ref[...].astype(F32)
        da_ref[...] = (dfv * b_ref[...].astype(F32) * (sg * (1.0 + a * (1.0 - sg)))).astype(BF16)
        db_ref[...] = (dfv * a * sg).astype(BF16)

    def body2(a_ref, b_ref, df_ref, o_ref):
        body(a_ref, b_ref, df_ref, o_ref.at[:, pl.ds(0, F)], o_ref.at[:, pl.ds(F, F)])

    return pl.pallas_call(
        body2, name=name, grid=(S // tr,),
        in_specs=[_row_spec(tr, F, 0), _row_spec(tr, F, 1), _row_spec(tr, F)],
        out_specs=_row_spec(tr, F2), out_shape=jax.ShapeDtypeStruct((S, F2), BF16),
        compiler_params=_params(("parallel",)))(ab, ab, df)


def _shift_rows(v, n):
    return pltpu.roll(v, n, 0)


def _conv_fwd(p, w8, name):
    S, D3 = p.shape
    D = D3 // 3
    tr = _rows(S)
    nb8 = tr // 8

    def body(gb_ref, gc_ref, hx_ref, gcp_ref, hxp_ref, w_ref, o_ref):
        i = pl.program_id(0)
        y = gc_ref[...].astype(F32) * hx_ref[...].astype(F32)
        yp = jnp.where(i > 0, gcp_ref[...].astype(F32) * hxp_ref[...].astype(F32), 0.0)
        w0, w1, w2 = w_ref[0:1, :], w_ref[1:2, :], w_ref[2:3, :]
        conv = w0 * _shift_rows(y, 2) + w1 * _shift_rows(y, 1) + w2 * y
        o_ref[...] = (gb_ref[...].astype(F32) * conv).astype(BF16)
        rid = lax.broadcasted_iota(jnp.int32, (8, D), 0)
        y8 = y[0:8, :]
        y1 = jnp.where(rid < 1, _shift_rows(yp, 1), _shift_rows(y8, 1))
        y2 = jnp.where(rid < 2, _shift_rows(yp, 2), _shift_rows(y8, 2))
        conv8 = w0 * y2 + w1 * y1 + w2 * y8
        o_ref[0:8, :] = (gb_ref[0:8, :].astype(F32) * conv8).astype(BF16)

    def col(c):
        return pl.BlockSpec((tr, D), lambda i: (i, c))

    def prev8(c):
        return pl.BlockSpec((8, D), lambda i: (jnp.maximum(i * nb8 - 1, 0), c))

    return pl.pallas_call(
        body, name=name, grid=(S // tr,),
        in_specs=[col(0), col(1), col(2), prev8(1), prev8(2), _vec_spec(8, D)],
        out_specs=_row_spec(tr, D), out_shape=jax.ShapeDtypeStruct((S, D), BF16),
        compiler_params=_params(("parallel",)))(p, p, p, p, p, w8)


def _conv_bwd(p, do, w8, name):
    S, D3 = p.shape
    D = D3 // 3
    tr = _rows(S)
    nb8 = tr // 8
    nt = S // tr

    def body(gb_ref, gc_ref, hx_ref, gcp_ref, hxp_ref, do_ref, gbn_ref, don_ref, w_ref, dp_ref, dw_ref):
        i = pl.program_id(0)
        gb = gb_ref[...].astype(F32)
        gc = gc_ref[...].astype(F32)
        hx = hx_ref[...].astype(F32)
        dov = do_ref[...].astype(F32)
        y = gc * hx
        yp = jnp.where(i > 0, gcp_ref[...].astype(F32) * hxp_ref[...].astype(F32), 0.0)
        dconv = dov * gb
        dcn = jnp.where(i < nt - 1, don_ref[...].astype(F32) * gbn_ref[...].astype(F32), 0.0)
        w0, w1, w2 = w_ref[0:1, :], w_ref[1:2, :], w_ref[2:3, :]
        rid = lax.broadcasted_iota(jnp.int32, (tr, D), 0)
        rid8 = lax.broadcasted_iota(jnp.int32, (8, D), 0)
        yp1 = jnp.concatenate([_shift_rows(yp, 1), jnp.zeros((tr - 8, D), F32)], axis=0)
        yp2 = jnp.concatenate([_shift_rows(yp, 2), jnp.zeros((tr - 8, D), F32)], axis=0)
        y1 = jnp.where(rid < 1, yp1, _shift_rows(y, 1))
        y2 = jnp.where(rid < 2, yp2, _shift_rows(y, 2))
        conv = w0 * y2 + w1 * y1 + w2 * y
        dn1 = jnp.concatenate([jnp.zeros((tr - 8, D), F32), _shift_rows(dcn, 7)], axis=0)
        dn2 = jnp.concatenate([jnp.zeros((tr - 8, D), F32), _shift_rows(dcn, 6)], axis=0)
        d1 = jnp.where(rid >= tr - 1, dn1, _shift_rows(dconv, tr - 1))
        d2 = jnp.where(rid >= tr - 2, dn2, _shift_rows(dconv, tr - 2))
        dy = w2 * dconv + w1 * d1 + w0 * d2
        dp_ref[:, pl.ds(0, D)] = (dov * conv).astype(BF16)
        dp_ref[:, pl.ds(D, D)] = (dy * hx).astype(BF16)
        dp_ref[:, pl.ds(2 * D, D)] = (dy * gc).astype(BF16)

        @pl.when(i == 0)
        def _():
            dw_ref[...] = jnp.zeros_like(dw_ref)

        upd = jnp.where(rid8 == 0, jnp.sum(dconv * y2, axis=0, keepdims=True),
                        jnp.where(rid8 == 1, jnp.sum(dconv * y1, axis=0, keepdims=True),
                                  jnp.where(rid8 == 2, jnp.sum(dconv * y, axis=0, keepdims=True), 0.0)))
        dw_ref[...] += upd

    def col(c):
        return pl.BlockSpec((tr, D), lambda i: (i, c))

    def prev8(c):
        return pl.BlockSpec((8, D), lambda i: (jnp.maximum(i * nb8 - 1, 0), c))

    def next8(c):
        return pl.BlockSpec((8, D), lambda i: (jnp.minimum((i + 1) * nb8, S // 8 - 1), c))

    return pl.pallas_call(
        body, name=name, grid=(nt,),
        in_specs=[col(0), col(1), col(2), prev8(1), prev8(2), col(0), next8(0), next8(0), _vec_spec(8, D)],
        out_specs=(_row_spec(tr, D3), _vec_spec(8, D)),
        out_shape=(jax.ShapeDtypeStruct((S, D3), BF16), jax.ShapeDtypeStruct((8, D), F32)),
        compiler_params=_params(("arbitrary",)))(p, p, p, p, p, do, p, do, w8)


_GELU_C = math.sqrt(2.0 / math.pi)


def _gelu(v):
    return 0.5 * v * (1.0 + jnp.tanh(_GELU_C * (v + 0.044715 * v * v * v)))


def _gelu_grad(v):
    t = jnp.tanh(_GELU_C * (v + 0.044715 * v * v * v))
    return 0.5 * (1.0 + t) + 0.5 * v * (1.0 - t * t) * _GELU_C * (1.0 + 3.0 * 0.044715 * v * v)


def _tril(w):
    r = lax.broadcasted_iota(jnp.int32, (CHUNK, CHUNK), 0)
    c = lax.broadcasted_iota(jnp.int32, (CHUNK, CHUNK), 1)
    return jnp.where(r >= c, w, 0.0)


def _sgu_fwd(z, w, bT, n_attn, name):
    S = z.shape[0]
    G = w.shape[0]
    W = G * CHUNK
    tr = _pick(S, (512, 256, 128))
    ucol = 3 * n_attn * HEAD_DIM // W

    def body(u_ref, v_ref, w_ref, b_ref, o_ref):
        for g in range(G):
            wt = _tril(w_ref[g]).astype(BF16)
            for ci in range(tr // CHUNK):
                rows, cols = pl.ds(ci * CHUNK, CHUNK), pl.ds(g * CHUNK, CHUNK)
                gv = _gelu(v_ref[rows, cols].astype(F32)).astype(BF16)
                mixed = jnp.dot(wt, gv, preferred_element_type=F32) + b_ref[:, g:g + 1]
                o_ref[rows, cols] = (_gelu(u_ref[rows, cols].astype(F32)) * mixed).astype(BF16)

    return pl.pallas_call(
        body, name=name, grid=(S // tr,),
        in_specs=[_row_spec(tr, W, ucol), _row_spec(tr, W, ucol + 1),
                  pl.BlockSpec((G, CHUNK, CHUNK), lambda i: (0, 0, 0)), _vec_spec(CHUNK, G)],
        out_specs=_row_spec(tr, W), out_shape=jax.ShapeDtypeStruct((S, W), BF16),
        compiler_params=_params(("parallel",)))(z, z, w, bT)


def _sgu_bwd(z, dcat, w, bT, n_attn, name):
    S = z.shape[0]
    G = w.shape[0]
    W = G * CHUNK
    tr = _pick(S, (512, 256, 128))
    ucol = 3 * n_attn * HEAD_DIM // W
    dcol = n_attn * HEAD_DIM // W

    def body(u_ref, v_ref, d_ref, w_ref, b_ref, o_ref, dw_ref, db_ref):
        @pl.when(pl.program_id(0) == 0)
        def _():
            dw_ref[...] = jnp.zeros_like(dw_ref)
            db_ref[...] = jnp.zeros_like(db_ref)

        lane = lax.broadcasted_iota(jnp.int32, (CHUNK, G), 1)
        for g in range(G):
            wtf = _tril(w_ref[g])
            wt = wtf.astype(BF16)
            dw_acc = jnp.zeros((CHUNK, CHUNK), F32)
            db_acc = jnp.zeros((CHUNK, 1), F32)
            for ci in range(tr // CHUNK):
                rows, cols = pl.ds(ci * CHUNK, CHUNK), pl.ds(g * CHUNK, CHUNK)
                uv = u_ref[rows, cols].astype(F32)
                vv = v_ref[rows, cols].astype(F32)
                dov = d_ref[rows, cols]
                gv = _gelu(vv).astype(BF16)
                mixed = jnp.dot(wt, gv, preferred_element_type=F32) + b_ref[:, g:g + 1]
                dmixed = dov * _gelu(uv)
                dmb = dmixed.astype(BF16)
                dgv = lax.dot_general(wt, dmb, (((0,), (0,)), ((), ())), preferred_element_type=F32)
                o_ref[rows, cols] = (dov * mixed * _gelu_grad(uv)).astype(BF16)
                o_ref[rows, pl.ds(W + g * CHUNK, CHUNK)] = (dgv * _gelu_grad(vv)).astype(BF16)
                dw_acc += lax.dot_general(dmb, gv, (((1,), (1,)), ((), ())), preferred_element_type=F32)
                db_acc += jnp.sum(dmixed, axis=1, keepdims=True)
            dw_ref[g] += _tril(dw_acc)
            db_ref[...] += jnp.where(lane == g, db_acc, 0.0)

    return pl.pallas_call(
        body, name=name, grid=(S // tr,),
        in_specs=[_row_spec(tr, W, ucol), _row_spec(tr, W, ucol + 1), _row_spec(tr, W, dcol),
                  pl.BlockSpec((G, CHUNK, CHUNK), lambda i: (0, 0, 0)), _vec_spec(CHUNK, G)],
        out_specs=(_row_spec(tr, 2 * W), pl.BlockSpec((G, CHUNK, CHUNK), lambda i: (0, 0, 0)), _vec_spec(CHUNK, G)),
        out_shape=(jax.ShapeDtypeStruct((S, 2 * W), BF16), jax.ShapeDtypeStruct((G, CHUNK, CHUNK), F32),
                   jax.ShapeDtypeStruct((CHUNK, G), F32)),
        compiler_params=_params(("arbitrary",)))(z, z, dcat, w, bT)


def _rope(v, cs, sa, sb):
    return v * cs + pltpu.roll(v, HEAD_DIM - ROPE_DIM // 2, 1) * sa + pltpu.roll(v, ROPE_DIM // 2, 1) * sb


def _rope_t(d, cs, sa, sb):
    return d * cs + pltpu.roll(d * sa, ROPE_DIM // 2, 1) + pltpu.roll(d * sb, HEAD_DIM - ROPE_DIM // 2, 1)


def _qkv_prep(z, cs, sa, sb, n_attn, name):
    S = z.shape[0]
    A = n_attn * HEAD_DIM
    tr = _rows(S)

    def body(q_ref, k_ref, v_ref, c_ref, a_ref, b_ref, qo_ref, ko_ref, vo_ref):
        cv, av, bv = c_ref[...], a_ref[...], b_ref[...]
        for h in range(n_attn):
            cols = pl.ds(h * HEAD_DIM, HEAD_DIM)
            qo_ref[:, cols] = _rope(q_ref[:, cols].astype(F32), cv, av, bv)
            ko_ref[:, cols] = _rope(k_ref[:, cols].astype(F32), cv, av, bv)
        vo_ref[...] = v_ref[...].astype(F32)

    tab = _row_spec(tr, HEAD_DIM)
    out = jax.ShapeDtypeStruct((S, A), F32)
    return pl.pallas_call(
        body, name=name, grid=(S // tr,),
        in_specs=[_row_spec(tr, A, 0), _row_spec(tr, A, 1), _row_spec(tr, A, 2), tab, tab, tab],
        out_specs=(_row_spec(tr, A),) * 3, out_shape=(out,) * 3,
        compiler_params=_params(("parallel",)))(z, z, z, cs, sa, sb)


def _dqkv_post(dq, dk, dv, cs, sa, sb, n_attn, name):
    S, A = dq.shape
    tr = _rows(S)

    def body(q_ref, k_ref, v_ref, c_ref, a_ref, b_ref, o_ref):
        cv, av, bv = c_ref[...], a_ref[...], b_ref[...]
        for h in range(n_attn):
            cols = pl.ds(h * HEAD_DIM, HEAD_DIM)
            o_ref[:, pl.ds(h * HEAD_DIM, HEAD_DIM)] = _rope_t(q_ref[:, cols], cv, av, bv).astype(BF16)
            o_ref[:, pl.ds(A + h * HEAD_DIM, HEAD_DIM)] = _rope_t(k_ref[:, cols], cv, av, bv).astype(BF16)
        o_ref[:, pl.ds(2 * A, A)] = v_ref[...].astype(BF16)

    tab = _row_spec(tr, HEAD_DIM)
    return pl.pallas_call(
        body, name=name, grid=(S // tr,),
        in_specs=[_row_spec(tr, A)] * 3 + [tab, tab, tab],
        out_specs=_row_spec(tr, 3 * A), out_shape=jax.ShapeDtypeStruct((S, 3 * A), BF16),
        compiler_params=_params(("parallel",)))(dq, dk, dv, cs, sa, sb)


ATTN_CHUNK = 64
ATTN_PAD = DILATIONS[-1] * ATTN_BLOCK


def _group(d, S):
    return 2 if (S // (d * ATTN_BLOCK)) % 2 == 0 else 1


def _work_item(d, S, it):
    G = _group(d, S)
    ngrp = S // (d * ATTN_BLOCK * G)
    r = it // ngrp
    jb = (it % ngrp) * G
    return r + d * ATTN_BLOCK * jb, ATTN_PAD + r + d * ATTN_BLOCK * (jb - 1), jb == 0


def _chunk_bias(first_row):
    al = first_row + lax.broadcasted_iota(jnp.int32, (ATTN_CHUNK, 2 * ATTN_BLOCK), 0)
    kl = lax.broadcasted_iota(jnp.int32, (ATTN_CHUNK, 2 * ATTN_BLOCK), 1)
    seen = (kl >= al) & (kl <= al + ATTN_BLOCK)
    return jnp.where(seen, 0.0, MASKED), jnp.where(seen & (kl >= ATTN_BLOCK), 0.0, MASKED)


def _chunks(G):
    return [(slice(c * ATTN_CHUNK, (c + 1) * ATTN_CHUNK), c * ATTN_CHUNK // ATTN_BLOCK, (c * ATTN_CHUNK) % ATTN_BLOCK)
            for c in range(G * ATTN_BLOCK // ATTN_CHUNK)]


_NT = (((1,), (1,)), ((), ()))
_TN = (((0,), (0,)), ((), ()))


def _attn_fwd(q, k, v, n_attn, name, job=None):
    S, A = q.shape
    scale = HEAD_DIM ** -0.5
    n_ji = len(job["ins"]) if job else 0
    n_jo = len(job["outs"]) if job else 0

    def body(q_hbm, k_hbm, v_hbm, *rest):
        jin, (o_hbm, lse_hbm), jout = rest[:n_ji], rest[n_ji:n_ji + 2], rest[n_ji + 2:n_ji + 2 + n_jo]
        qs, ks, vs, acc, ms, ls, ob, sem = rest[n_ji + 2 + n_jo:n_ji + 2 + n_jo + 8]
        jsems = rest[n_ji + 2 + n_jo + 8:]
        h = pl.program_id(0)
        if job:
            @pl.when(h == 0)
            def _():
                job["start"](jin, jout, jsems)

        cols = pl.ds(pl.multiple_of(h * HEAD_DIM, HEAD_DIM), HEAD_DIM)
        data = pl.ds(ATTN_PAD, S)
        cps = [pltpu.make_async_copy(src.at[:, cols], dst, sem.at[i])
               for i, (src, dst) in enumerate(((q_hbm, qs), (k_hbm, ks.at[data, :]), (v_hbm, vs.at[data, :])))]
        for cp in cps:
            cp.start()

        @pl.when(h == 0)
        def _():
            ks[pl.ds(0, ATTN_PAD), :] = jnp.zeros((ATTN_PAD, HEAD_DIM), F32)
            vs[pl.ds(0, ATTN_PAD), :] = jnp.zeros((ATTN_PAD, HEAD_DIM), F32)

        acc[...] = jnp.zeros_like(acc)
        ms[...] = jnp.full_like(ms, MASKED)
        ls[...] = jnp.zeros_like(ls)
        for cp in cps:
            cp.wait()
        biases = {r0: _chunk_bias(r0) for r0 in range(0, ATTN_BLOCK, ATTN_CHUNK)}
        for d in DILATIONS:
            G = _group(d, S)
            lanes = S // (ATTN_BLOCK * G * ATTN_LANES_FWD)

            def step(it, carry, d=d, G=G, lanes=lanes):
                groups, chains = [], []
                for u in range(ATTN_LANES_FWD):
                    q0, k0, first = _work_item(d, S, it + u * lanes)
                    qrows = pl.ds(q0, G * ATTN_BLOCK, stride=d)
                    krows = pl.ds(k0, (G + 1) * ATTN_BLOCK, stride=d)
                    qb, kb, vb = (t[rows, :].astype(BF16) for t, rows in ((qs, qrows), (ks, krows), (vs, krows)))
                    m_all, l_all, a_all = ms[qrows, :], ls[qrows, :], acc[qrows, :]
                    groups.append(qrows)
                    kt = kb.T
                    for sl, qi, r0 in _chunks(G):
                        ky = slice(qi * ATTN_BLOCK, (qi + 2) * ATTN_BLOCK)
                        bias = jnp.where(first, biases[r0][1], biases[r0][0]) if qi == 0 else biases[r0][0]
                        chains.append((qb[sl], kt[:, ky], vb[ky], bias, m_all[sl], l_all[sl], a_all[sl]))
                scores = [jnp.dot(c[0], c[1], preferred_element_type=F32) for c in chains]
                m_new, l_new, probs, alphas = [], [], [], []
                for (_, _, _, bias, m_old, l_old, _), s in zip(chains, scores):
                    s = s * scale + bias
                    m_c = jnp.maximum(m_old, jnp.max(s, axis=-1, keepdims=True))
                    alpha = jnp.exp(m_old - m_c)
                    p = jnp.exp(s - m_c)
                    m_new.append(m_c)
                    l_new.append(alpha * l_old + jnp.sum(p, axis=-1, keepdims=True))
                    probs.append(p.astype(BF16))
                    alphas.append(alpha)
                a_new = [alpha * c[6] + jnp.dot(p, c[2], preferred_element_type=F32)
                         for c, p, alpha in zip(chains, probs, alphas)]
                n = len(_chunks(G))
                for u, qrows in enumerate(groups):
                    ms[qrows, :] = jnp.concatenate(m_new[u * n:(u + 1) * n], axis=0)
                    ls[qrows, :] = jnp.concatenate(l_new[u * n:(u + 1) * n], axis=0)
                    acc[qrows, :] = jnp.concatenate(a_new[u * n:(u + 1) * n], axis=0)
                return carry
            lax.fori_loop(0, lanes, step, 0)
        ob[...] = (acc[...] / ls[...]).astype(BF16)
        ms[...] = ms[...] + jnp.log(ls[...])
        out = [pltpu.make_async_copy(ob, o_hbm.at[:, cols], sem.at[0]),
               pltpu.make_async_copy(ms, lse_hbm.at[h], sem.at[1])]
        for cp in out:
            cp.start()
        for cp in out:
            cp.wait()
        if job:
            @pl.when(h == n_attn - 1)
            def _():
                job["finish"](jin, jout, jsems)

    res = pl.pallas_call(
        body, name=name, grid=(n_attn,), in_specs=[ANY] * (3 + n_ji), out_specs=(ANY,) * (2 + n_jo),
        out_shape=(jax.ShapeDtypeStruct((S, A), BF16), jax.ShapeDtypeStruct((n_attn, S, 1), F32))
        + tuple(job["outs"] if job else ()),
        scratch_shapes=[pltpu.VMEM((S, HEAD_DIM), F32)] + [pltpu.VMEM((S + ATTN_PAD, HEAD_DIM), F32)] * 2
        + [pltpu.VMEM((S, HEAD_DIM), F32)] + [pltpu.VMEM((S, 1), F32)] * 2
        + [pltpu.VMEM((S, HEAD_DIM), BF16), pltpu.SemaphoreType.DMA((3,))] + list(job["sems"] if job else ()),
        compiler_params=_params(("arbitrary",)))(q, k, v, *(job["ins"] if job else ()))
    return res[0], res[1], list(res[2:])


def _attn_bwd(q, k, v, o, lse, dcat, n_attn, name):
    S, A = q.shape
    scale = HEAD_DIM ** -0.5

    def body(q_hbm, k_hbm, v_hbm, o_hbm, lse_hbm, do_hbm, dq_hbm, dk_hbm, dv_hbm,
             qs, ks, vs, dos, dqs, dks, dvs, lses, dls, ob, sem):
        h = pl.program_id(0)
        cols = pl.ds(pl.multiple_of(h * HEAD_DIM, HEAD_DIM), HEAD_DIM)
        data = pl.ds(ATTN_PAD, S)
        cps = [pltpu.make_async_copy(src.at[:, cols], dst, sem.at[i])
               for i, (src, dst) in enumerate(((q_hbm, qs), (k_hbm, ks.at[data, :]), (v_hbm, vs.at[data, :]),
                                               (do_hbm, dos), (o_hbm, ob)))]
        cps.append(pltpu.make_async_copy(lse_hbm.at[h], lses, sem.at[5]))
        for cp in cps:
            cp.start()

        @pl.when(h == 0)
        def _():
            ks[pl.ds(0, ATTN_PAD), :] = jnp.zeros((ATTN_PAD, HEAD_DIM), F32)
            vs[pl.ds(0, ATTN_PAD), :] = jnp.zeros((ATTN_PAD, HEAD_DIM), F32)

        dqs[...] = jnp.zeros_like(dqs)
        dks[...] = jnp.zeros_like(dks)
        dvs[...] = jnp.zeros_like(dvs)
        for cp in cps:
            cp.wait()
        dls[...] = jnp.sum(dos[...] * ob[...].astype(F32), axis=-1, keepdims=True)
        biases = {r0: _chunk_bias(r0) for r0 in range(0, ATTN_BLOCK, ATTN_CHUNK)}
        for d in DILATIONS:
            G = _group(d, S)
            n_lanes = 1 if d == DILATIONS[-1] else ATTN_LANES_BWD
            lanes = S // (ATTN_BLOCK * G * n_lanes)
            assert lanes % 2 == 0

            def step(it, carry, d=d, G=G, lanes=lanes, n_lanes=n_lanes):
                groups, chains = [], []
                for u in range(n_lanes):
                    q0, k0, first = _work_item(d, S, it + u * lanes)
                    qrows = pl.ds(q0, G * ATTN_BLOCK, stride=d)
                    krows = pl.ds(k0, (G + 1) * ATTN_BLOCK, stride=d)
                    qb, kb, vb, dob = (t[rows, :].astype(BF16)
                                       for t, rows in ((qs, qrows), (ks, krows), (vs, krows), (dos, qrows)))
                    lse_v, dl_v, dq_old = lses[qrows, :], dls[qrows, :], dqs[qrows, :]
                    groups.append((qrows, krows, dks[krows, :], dvs[krows, :]))
                    kt, vt = kb.T, vb.T
                    for sl, qi, r0 in _chunks(G):
                        ky = slice(qi * ATTN_BLOCK, (qi + 2) * ATTN_BLOCK)
                        bias = jnp.where(first, biases[r0][1], biases[r0][0]) if qi == 0 else biases[r0][0]
                        chains.append((qb[sl], kb[ky], vt[:, ky], dob[sl], bias, lse_v[sl], dl_v[sl], dq_old[sl], u, qi,
                                       kt[:, ky]))
                scores = [jnp.dot(c[0], c[10], preferred_element_type=F32) for c in chains]
                dprobs = [jnp.dot(c[3], c[2], preferred_element_type=F32) for c in chains]
                probs, dscores = [], []
                for c, s, dp in zip(chains, scores, dprobs):
                    p = jnp.exp(s * scale + c[4] - c[5])
                    probs.append(p.astype(BF16))
                    dscores.append((p * (dp - c[6]) * scale).astype(BF16))
                dq_new = [c[7] + jnp.dot(ds, c[1], preferred_element_type=F32) for c, ds in zip(chains, dscores)]
                dk_add = [[None] * (G + 1) for _ in groups]
                dv_add = [[None] * (G + 1) for _ in groups]
                for c, ds, p in zip(chains, dscores, probs):
                    dk_c = lax.dot_general(ds, c[0], _TN, preferred_element_type=F32)
                    dv_c = lax.dot_general(p, c[3], _TN, preferred_element_type=F32)
                    u, qi = c[8], c[9]
                    for t in range(2):
                        half = slice(t * ATTN_BLOCK, (t + 1) * ATTN_BLOCK)
                        dk_add[u][qi + t] = dk_c[half] if dk_add[u][qi + t] is None else dk_add[u][qi + t] + dk_c[half]
                        dv_add[u][qi + t] = dv_c[half] if dv_add[u][qi + t] is None else dv_add[u][qi + t] + dv_c[half]
                n = len(_chunks(G))
                for u, (qrows, krows, dk_old, dv_old) in enumerate(groups):
                    dqs[qrows, :] = jnp.concatenate(dq_new[u * n:(u + 1) * n], axis=0)
                    dks[krows, :] = dk_old + jnp.concatenate(dk_add[u], axis=0)
                    dvs[krows, :] = dv_old + jnp.concatenate(dv_add[u], axis=0)
                return carry
            lax.fori_loop(0, lanes, step, 0)
        out = [pltpu.make_async_copy(src, dst.at[:, cols], sem.at[i])
               for i, (src, dst) in enumerate(((dqs, dq_hbm), (dks.at[data, :], dk_hbm), (dvs.at[data, :], dv_hbm)))]
        for cp in out:
            cp.start()
        for cp in out:
            cp.wait()

    grad = jax.ShapeDtypeStruct((S, A), F32)
    plain, padded = pltpu.VMEM((S, HEAD_DIM), F32), pltpu.VMEM((S + ATTN_PAD, HEAD_DIM), F32)
    return pl.pallas_call(
        body, name=name, grid=(n_attn,), in_specs=[ANY] * 6, out_specs=(ANY, ANY, ANY), out_shape=(grad,) * 3,
        scratch_shapes=[plain, padded, padded, plain, plain, padded, padded] + [pltpu.VMEM((S, 1), F32)] * 2
        + [pltpu.VMEM((S, HEAD_DIM), BF16), pltpu.SemaphoreType.DMA((6,))],
        compiler_params=_params(("arbitrary",)))(q, k, v, o, lse, dcat)


def _ada_fwd(c_act, ada_w, name):
    L, D, n = ada_w.shape
    tn = _pick(n, (512, 256, 128))

    def body(c_ref, w_ref, o_ref):
        o_ref[...] = jnp.dot(c_ref[...], w_ref[...], preferred_element_type=F32)

    return pl.pallas_call(
        body, name=name, grid=(L, n // tn),
        in_specs=[pl.BlockSpec((N_DEV, D), lambda l, j: (0, 0)), pl.BlockSpec((None, D, tn), lambda l, j: (l, 0, j))],
        out_specs=pl.BlockSpec((None, N_DEV, tn), lambda l, j: (l, 0, j)),
        out_shape=jax.ShapeDtypeStruct((L, N_DEV, n), F32),
        compiler_params=_params(("parallel", "parallel")))(c_act, ada_w)


def _ada_bwd(c_act, dmod, name):
    L, _, n = dmod.shape
    D = c_act.shape[1]
    tn = _pick(n, (512, 256, 128))

    def body(c_ref, d_ref, o_ref):
        o_ref[...] = lax.dot_general(c_ref[...], d_ref[...], _TN, preferred_element_type=F32)

    return pl.pallas_call(
        body, name=name, grid=(L, n // tn),
        in_specs=[pl.BlockSpec((N_DEV, D), lambda l, j: (0, 0)), pl.BlockSpec((None, N_DEV, tn), lambda l, j: (l, 0, j))],
        out_specs=pl.BlockSpec((None, D, tn), lambda l, j: (l, 0, j)),
        out_shape=jax.ShapeDtypeStruct((L, D, n), F32),
        compiler_params=_params(("parallel", "parallel")))(c_act, dmod)


def _adamw(w, g, m, v, name, job=None):
    shape = w.shape
    C = shape[-1]
    R = w.size // C
    w2, g2, m2, v2 = (t.reshape(R, C) for t in (w, g, m, v))
    tr = _pick(R, (256, 128, 64, 32, 16, 8))
    tc = _pick(C, (2048, 1536, 1408, 1024, 512, 256, 128))
    n_ji = len(job["ins"]) if job else 0
    n_jo = len(job["outs"]) if job else 0
    grid = (R // tr, C // tc)

    def body(w_ref, g_ref, m_ref, v_ref, *rest):
        jin, (d_ref, mo_ref, vo_ref) = rest[:n_ji], rest[n_ji:n_ji + 3]
        jout, jsems = rest[n_ji + 3:n_ji + 3 + n_jo], rest[n_ji + 3 + n_jo:]
        i, j = pl.program_id(0), pl.program_id(1)
        if job:
            @pl.when((i == 0) & (j == 0))
            def _():
                job["start"](jin, jout, jsems)

        gv = g_ref[...]
        mn = ADAM_B1 * m_ref[...] + (1.0 - ADAM_B1) * gv
        vn = ADAM_B2 * v_ref[...] + (1.0 - ADAM_B2) * (gv * gv)
        m_hat = mn / (1.0 - ADAM_B1 ** ADAM_STEP)
        v_hat = vn / (1.0 - ADAM_B2 ** ADAM_STEP)
        d_ref[...] = -ADAM_LR * (m_hat / (jnp.sqrt(v_hat) + ADAM_EPS) + ADAM_WD * w_ref[...])
        mo_ref[...] = mn
        vo_ref[...] = vn
        if job:
            @pl.when((i == grid[0] - 1) & (j == grid[1] - 1))
            def _():
                job["finish"](jin, jout, jsems)

    spec = pl.BlockSpec((tr, tc), lambda i, j: (i, j))
    out = jax.ShapeDtypeStruct((R, C), F32)
    d, mn, vn, *got = pl.pallas_call(
        body, name=name, grid=grid, in_specs=[spec] * 4 + [ANY] * n_ji, out_specs=(spec,) * 3 + (ANY,) * n_jo,
        out_shape=(out,) * 3 + tuple(job["outs"] if job else ()), scratch_shapes=list(job["sems"] if job else ()),
        compiler_params=_params(("arbitrary", "arbitrary") if job else ("parallel", "parallel")),
    )(w2, g2, m2, v2, *(job["ins"] if job else ()))
    res = (d.reshape(shape), mn.reshape(shape), vn.reshape(shape))
    return res + (got,) if job else res


def _sum_leading(t, name):
    n, R, C = t.shape
    tr = _pick(R, (1024, 512, 256, 128, 64, 32, 16, 8))

    def body(t_ref, o_ref):
        acc = t_ref[0]
        for i in range(1, n):
            acc = acc + t_ref[i]
        o_ref[...] = acc

    return pl.pallas_call(
        body, name=name, grid=(R // tr,), in_specs=[pl.BlockSpec((n, tr, C), lambda i: (0, i, 0))],
        out_specs=pl.BlockSpec((tr, C), lambda i: (i, 0)), out_shape=jax.ShapeDtypeStruct((R, C), F32),
        compiler_params=_params(("parallel",)))(t)


def _coords():
    return lax.axis_index("x"), lax.axis_index("y"), lax.axis_index("c")


def _other_chips(x, y):
    return [(1 - x, y), (x, 1 - y), (1 - x, 1 - y)]


def _all_gather8(t, name):
    R, C = t.shape

    def body(x_ref, out_ref, send_sems, recv_sems, local_sem):
        x, y, c = _coords()
        me, sibling = (x, y, c), (x, y, 1 - c)
        chips = _other_chips(x, y)

        def slot(px, py, pc):
            return out_ref.at[4 * px + 2 * py + pc]

        def copy(k, block, to, src=None):
            return pltpu.make_async_remote_copy(
                src_ref=slot(*block) if src is None else src, dst_ref=slot(*block),
                send_sem=send_sems.at[k], recv_sem=recv_sems.at[k], device_id=to, device_id_type=MESH)

        mine = pltpu.make_async_copy(x_ref, slot(*me), local_sem)
        mine.start()
        first = [copy(0, me, sibling, src=x_ref)]
        first += [copy(1 + j, me, (*chip, c), src=x_ref) for j, chip in enumerate(chips)]
        for cp in first:
            cp.start()
        passed = [copy(4 + j, (*chip, c), sibling) for j, chip in enumerate(chips)]
        for j, chip in enumerate(chips):
            copy(1 + j, (*chip, c), me).wait_recv()
            passed[j].start()
        copy(0, sibling, me).wait_recv()
        for j, chip in enumerate(chips):
            copy(4 + j, (*chip, 1 - c), me).wait_recv()
        for cp in first + passed:
            cp.wait_send()
        mine.wait()

    return pl.pallas_call(
        body, name=name, out_shape=jax.ShapeDtypeStruct((N_DEV, R, C), t.dtype),
        in_specs=[pl.BlockSpec(memory_space=pltpu.VMEM)], out_specs=pl.BlockSpec(memory_space=pltpu.VMEM),
        scratch_shapes=[pltpu.SemaphoreType.DMA((7,)), pltpu.SemaphoreType.DMA((7,)), pltpu.SemaphoreType.DMA],
        compiler_params=pltpu.CompilerParams(vmem_limit_bytes=VMEM_LIMIT_BYTES))(t)


def _window(ref, r0, nr, c0, nc):
    return ref.at[pl.ds(r0, nr), pl.ds(c0, nc)]


LOCAL_CHUNKS = 4


def _gather_job(pieces, out_shape):
    n = len(pieces)

    def ctx(ins, outs, sems):
        x, y, c = _coords()
        buf = outs[0]

        def place(p, chip_idx, r0, nr):
            _, _, kind, base = pieces[p]
            r, cs = ins[p].shape[1], ins[p].shape[2]
            if kind == "row":
                return _window(buf, base + chip_idx * r + r0, nr, 0, cs)
            return _window(buf, r0, nr, base + chip_idx * cs, cs)

        def ici(p, j, chip, src, dst):
            return pltpu.make_async_remote_copy(
                src_ref=src, dst_ref=dst, send_sem=sems[0].at[3 * p + j], recv_sem=sems[1].at[3 * p + j],
                device_id=(*chip, c), device_id_type=MESH)

        def d2d(p, j, win):
            return pltpu.make_async_remote_copy(
                src_ref=win, dst_ref=win, send_sem=sems[2].at[3 * p + j], recv_sem=sems[3].at[3 * p + j],
                device_id=(x, y, 1 - c), device_id_type=MESH)

        def local(p):
            lidx, r = pieces[p][1], ins[p].shape[1]
            rc = r // (2 * LOCAL_CHUNKS)
            return [pltpu.make_async_copy(ins[p].at[lidx, pl.ds(q * rc, rc), :], place(p, 2 * x + y, q * rc, rc),
                                          sems[4].at[2 * LOCAL_CHUNKS * p + q]) for q in range(2 * LOCAL_CHUNKS)]

        return x, y, c, _other_chips(x, y), place, ici, d2d, local

    def start(ins, outs, sems):
        x, y, c, chips, place, ici, d2d, local = ctx(ins, outs, sems)
        for p in range(n):
            lidx, rh = pieces[p][1], ins[p].shape[1] // 2
            for j, chip in enumerate(chips):
                ici(p, j, chip, ins[p].at[lidx, pl.ds(c * rh, rh), :], place(p, 2 * x + y, c * rh, rh)).start()
        for p in range(n):
            for cp in local(p):
                cp.start()

    def finish(ins, outs, sems):
        x, y, c, chips, place, ici, d2d, local = ctx(ins, outs, sems)
        for p in range(n):
            rh = ins[p].shape[1] // 2
            for j, chip in enumerate(chips):
                landed = place(p, 2 * chip[0] + chip[1], c * rh, rh)
                ici(p, j, chip, landed, landed).wait_recv()
                d2d(p, j, landed).start()
        for p in range(n):
            lidx, rh = pieces[p][1], ins[p].shape[1] // 2
            for j, chip in enumerate(chips):
                theirs = place(p, 2 * chip[0] + chip[1], (1 - c) * rh, rh)
                d2d(p, j, theirs).wait_recv()
                d2d(p, j, place(p, 2 * chip[0] + chip[1], c * rh, rh)).wait_send()
                ici(p, j, chip, ins[p].at[lidx, pl.ds(c * rh, rh), :], place(p, 2 * x + y, c * rh, rh)).wait_send()
            for cp in local(p):
                cp.wait()

    return dict(
        ins=[p[0] for p in pieces], outs=[jax.ShapeDtypeStruct(out_shape, BF16)], start=start, finish=finish,
        sems=[pltpu.SemaphoreType.DMA((3 * n,))] * 4 + [pltpu.SemaphoreType.DMA((2 * LOCAL_CHUNKS * n,))])


def _scatter_job(parts, plan):
    n = len(parts)

    def shard_shape(i):
        kind, _, size = plan[i]
        R, C = parts[i].shape
        return (size, C) if kind == "row" else (R, size)

    def copies(ins, outs, sems):
        x, y, c = _coords()
        cps = []
        for i in range(n):
            kind, base, size = plan[i]
            R, C = ins[i].shape
            for j, chip in enumerate(_other_chips(x, y)):
                their = 2 * chip[0] + chip[1]
                if kind == "row":
                    src = _window(ins[i], base + their * size, size, 0, C)
                else:
                    src = _window(ins[i], 0, R, base + their * size, size)
                cps.append(pltpu.make_async_remote_copy(
                    src_ref=src, dst_ref=outs[i].at[j], send_sem=sems[0].at[3 * i + j],
                    recv_sem=sems[1].at[3 * i + j], device_id=(*chip, c), device_id_type=MESH))
        return cps

    def start(ins, outs, sems):
        for cp in copies(ins, outs, sems):
            cp.start()

    def finish(ins, outs, sems):
        for cp in copies(ins, outs, sems):
            cp.wait()

    return dict(
        ins=list(parts), outs=[jax.ShapeDtypeStruct((3,) + shard_shape(i), F32) for i in range(n)],
        start=start, finish=finish, sems=[pltpu.SemaphoreType.DMA((3 * n,))] * 2)


def _swap_job(t):
    def copy(ins, outs, sems):
        x, y, c = _coords()
        return pltpu.make_async_remote_copy(
            src_ref=ins[0].at[1 - c], dst_ref=outs[0], send_sem=sems[0].at[0], recv_sem=sems[1].at[0],
            device_id=(x, y, 1 - c), device_id_type=MESH)

    return dict(ins=[t], outs=[jax.ShapeDtypeStruct(t.shape[1:], t.dtype)],
                start=lambda ins, outs, sems: copy(ins, outs, sems).start(),
                finish=lambda ins, outs, sems: copy(ins, outs, sems).wait(),
                sems=[pltpu.SemaphoreType.DMA((1,))] * 2)


def _join_jobs(jobs):
    jobs = [j for j in jobs if j]
    if not jobs:
        return None
    if len(jobs) == 1:
        return jobs[0]

    def each(fn_name, ins, outs, sems):
        i = o = s = 0
        for j in jobs:
            ni, no, ns = len(j["ins"]), len(j["outs"]), len(j["sems"])
            j[fn_name](ins[i:i + ni], outs[o:o + no], sems[s:s + ns])
            i, o, s = i + ni, o + no, s + ns

    return dict(ins=[a for j in jobs for a in j["ins"]], outs=[a for j in jobs for a in j["outs"]],
                sems=[a for j in jobs for a in j["sems"]],
                start=lambda ins, outs, sems: each("start", ins, outs, sems),
                finish=lambda ins, outs, sems: each("finish", ins, outs, sems))


def _hosted(call, jobs):
    jobs = [j for j in jobs if j]
    if not jobs:
        return call(None), []
    out, *rest = call(_join_jobs(jobs))
    per = []
    for j in jobs:
        n = len(j["outs"])
        per.append(rest[:n])
        rest = rest[n:]
    return out, per


def _run_job(job, name):
    n_i, n_o = len(job["ins"]), len(job["outs"])

    def body(*refs):
        ins, outs, sems = refs[:n_i], refs[n_i:n_i + n_o], refs[n_i + n_o:]
        job["start"](ins, outs, sems)
        job["finish"](ins, outs, sems)

    return pl.pallas_call(
        body, name=name, in_specs=[ANY] * n_i, out_specs=tuple([ANY] * n_o), out_shape=tuple(job["outs"]),
        scratch_shapes=list(job["sems"]),
        compiler_params=pltpu.CompilerParams(vmem_limit_bytes=VMEM_LIMIT_BYTES))(*job["ins"])


def _sibling_share(both, name):
    n = len(both)

    def body(*refs):
        outs = refs[n:2 * n]
        send_sems, recv_sems = refs[2 * n:]
        x, y, c = _coords()
        cps = []
        for i in range(n):
            cp = pltpu.make_async_remote_copy(
                src_ref=outs[i].at[c], dst_ref=outs[i].at[c], send_sem=send_sems.at[i], recv_sem=recv_sems.at[i],
                device_id=(x, y, 1 - c), device_id_type=MESH)
            cp.start()
            cps.append(cp)
        for cp in cps:
            cp.wait()

    return pl.pallas_call(
        body, name=name, in_specs=[ANY] * n, out_specs=tuple([ANY] * n),
        out_shape=tuple(jax.ShapeDtypeStruct(b.shape, b.dtype) for b in both),
        input_output_aliases={i: i for i in range(n)},
        scratch_shapes=[pltpu.SemaphoreType.DMA((n,))] * 2,
        compiler_params=pltpu.CompilerParams(vmem_limit_bytes=VMEM_LIMIT_BYTES))(*both)


def _add_half(full3, recv, core, name):
    _, Rh, C = full3.shape
    tr = _pick(Rh, (256, 176, 128, 64, 32, 16, 8))
    tc = _pick(C, (2048, 1536, 1408, 1024, 512, 256, 128))

    def body(c_ref, a_ref, b_ref, o_ref):
        o_ref[...] = a_ref[...] + b_ref[...]

    return pl.pallas_call(
        body, name=name,
        grid_spec=pltpu.PrefetchScalarGridSpec(
            num_scalar_prefetch=1, grid=(Rh // tr, C // tc),
            in_specs=[pl.BlockSpec((None, tr, tc), lambda i, j, cr: (cr[0], i, j)),
                      pl.BlockSpec((tr, tc), lambda i, j, cr: (i, j))],
            out_specs=pl.BlockSpec((tr, tc), lambda i, j, cr: (i, j))),
        out_shape=jax.ShapeDtypeStruct((Rh, C), F32),
        compiler_params=_params(("parallel", "parallel")))(core, full3, recv)


def _add_scattered(part, recv, kind, base, size, core_chip, name):
    _, rs, cs = recv.shape
    tr = _pick(rs, (256, 176, 128, 64, 32, 16, 8))
    tc = _pick(cs, (2048, 1536, 1408, 1024, 512, 256, 128))
    assert base % size == 0
    if kind == "row":
        pidx = lambda i, j, cr: ((base // size + cr[1]) * (rs // tr) + i, j)
    else:
        pidx = lambda i, j, cr: (i, (base // size + cr[1]) * (cs // tc) + j)

    def body(c_ref, a_ref, r_ref, o_ref):
        o_ref[...] = ((a_ref[...] + r_ref[0]) + r_ref[1]) + r_ref[2]

    return pl.pallas_call(
        body, name=name,
        grid_spec=pltpu.PrefetchScalarGridSpec(
            num_scalar_prefetch=1, grid=(rs // tr, cs // tc),
            in_specs=[pl.BlockSpec((tr, tc), pidx), pl.BlockSpec((3, tr, tc), lambda i, j, cr: (0, i, j))],
            out_specs=pl.BlockSpec((None, tr, tc), lambda i, j, cr: (cr[0], i, j))),
        out_shape=jax.ShapeDtypeStruct((2, rs, cs), F32),
        compiler_params=_params(("parallel", "parallel")))(core_chip, part, recv)


def _rs_split(g, windows):
    R, C = g.shape
    if windows[0][0] == "row":
        size = windows[0][2]
        t = g.reshape(N_CHIPS, 2, size // 2, C).transpose(1, 0, 2, 3).reshape(2, R // 2, C)
        return t, [(k, b // 2, s // 2) for k, b, s in windows]
    return g.reshape(2, R // 2, C), list(windows)


def _rs_finish(part, windows, got, core_chip, tag):
    both = [_add_scattered(part, r, k, b, s, core_chip, f"rs_add_{tag}_{n}")
            for n, (r, (k, b, s)) in enumerate(zip(got, windows))]
    both = _sibling_share(both, f"rs_share_{tag}")
    return [t.reshape(2 * t.shape[1], t.shape[2]) for t in both]


def _rope_tables(positions, S):
    half = ROPE_DIM // 2
    inv_freq = ROPE_THETA ** (-jnp.arange(0, ROPE_DIM, 2, dtype=F32) / ROPE_DIM)
    ang = positions.reshape(S, 1).astype(F32) * inv_freq[None, :]
    cos, sin = jnp.cos(ang), jnp.sin(ang)
    zeros = jnp.zeros((S, half), F32)
    rest0 = jnp.zeros((S, HEAD_DIM - ROPE_DIM), F32)
    cs = jnp.concatenate([cos, cos, jnp.ones((S, HEAD_DIM - ROPE_DIM), F32)], axis=1)
    sa = jnp.concatenate([-sin, zeros, rest0], axis=1)
    sb = jnp.concatenate([zeros, sin, rest0], axis=1)
    return cs, sa, sb


def kernel(x, c, positions, ada_w, ada_b, norm_mix, norm_ffn, ab_w_in, sgu_w, sgu_b, ab_w_out, conv_w_in, conv_w, conv_w_out, ffn_w_gate, ffn_w_up, ffn_w_down, final_norm, loss_target, m_ada_w, m_ada_b, m_norm_mix, m_norm_ffn, m_ab_w_in, m_sgu_w, m_sgu_b, m_ab_w_out, m_conv_w_in, m_conv_w, m_conv_w_out, m_ffn_w_gate, m_ffn_w_up, m_ffn_w_down, m_final_norm, v_ada_w, v_ada_b, v_norm_mix, v_norm_ffn, v_ab_w_in, v_sgu_w, v_sgu_b, v_ab_w_out, v_conv_w_in, v_conv_w, v_conv_w_out, v_ffn_w_gate, v_ffn_w_up, v_ffn_w_down, v_final_norm):
    S, D = x.shape[1], x.shape[2]
    L = ada_w.shape[0]
    n_mix_heads = D // HEAD_DIM
    n_attn = 3 * n_mix_heads // 4
    A = n_attn * HEAD_DIM
    G = n_mix_heads - n_attn
    F = ffn_w_gate.shape[2] * N_CHIPS
    mix_in = ab_w_in.shape[2] * N_CHIPS
    xi, yi, ci = _coords()
    chip = 2 * xi + yi
    dev = 4 * xi + 2 * yi + ci
    core1 = jnp.reshape(ci, (1,)).astype(jnp.int32)
    chip1 = jnp.reshape(chip, (1,)).astype(jnp.int32)
    x2 = x.reshape(S, D)
    target = loss_target.reshape(S, D)

    n_conv = conv_w.size
    w0 = D + n_conv
    w0p = -(-w0 // 128) * 128
    pack = jnp.zeros((8, w0p), F32).at[0, :D].set(c[0]).at[0, D:w0].set(conv_w.reshape(-1))
    g0 = _all_gather8(pack, "gather_cond")
    c_all = g0[:, 0, :D]
    c_act = c_all * jax.nn.sigmoid(c_all)
    conv_full = jnp.concatenate(
        [g0[2 * j, 0, D:w0].reshape(conv_w.shape) for j in range(N_CHIPS)], axis=2)
    mod_part = _ada_fwd(c_act, ada_w, "ada_fwd")
    n_ada = ada_w.shape[2]
    g1 = _all_gather8(mod_part.reshape(L * N_DEV, n_ada), "gather_mod")
    mod_all = jnp.concatenate([g1[2 * j].reshape(L, N_DEV, n_ada) for j in range(N_CHIPS)], axis=2)
    mod = lax.dynamic_index_in_dim(mod_all, dev, axis=1, keepdims=False) + ada_b
    mods = mod.reshape(L, 6, 1, D)
    cs, sa, sb = _rope_tables(positions, S)

    bf = lambda t: t.astype(BF16)
    w_in_e, w_out_e = bf(ab_w_in), bf(ab_w_out)
    w_in_o, w_out_o = bf(conv_w_in), bf(conv_w_out)
    w_gate, w_up, w_down = bf(ffn_w_gate), bf(ffn_w_up), bf(ffn_w_down)
    def gather_jobs(l):
        i = l // 2
        first, n_in_cols = ((w_in_e, w_out_e), mix_in) if l % 2 == 0 else ((w_in_o, w_out_o), 3 * D)
        return [_gather_job([(first[0], i, "col", 0)], (D, n_in_cols)),
                _gather_job([(first[1], i, "row", 0)], (D, D)),
                _gather_job([(w_gate, l, "col", 0), (w_up, l, "col", F)], (D, 2 * F)),
                _gather_job([(w_down, l, "row", 0)], (F, D))]

    first_jobs = gather_jobs(0)
    layer_w = [[_run_job(first_jobs[0], "gather_w0_in")[0], None, None, None]]

    saved = []
    xc = x2
    pending = None
    for l in range(L):
        i = l // 2
        nxt = gather_jobs(l + 1) if l + 1 < L else [None] * 4
        if l + 1 < L:
            layer_w.append([None] * 4)

        def mm_fwd(a, b, out_dtype, name, slot, also=None):
            jobs = [nxt[slot], first_jobs[also] if also is not None else None]
            out, got = _hosted(lambda job: _mm(a, b, "nn", out_dtype, name, job=job), jobs)
            if nxt[slot] is not None:
                layer_w[l + 1][slot] = got.pop(0)[0]
            if also is not None:
                layer_w[l][also] = got.pop(0)[0]
            return out

        sh_m, sc_m, g_m, sh_f, sc_f, g_f = (mods[l, t] for t in range(6))
        weff_m = norm_mix[l][None, :] * (1.0 + sc_m)
        weff_f = norm_ffn[l][None, :] * (1.0 + sc_f)
        if pending is None:
            _, h = _norm_mod(xc, None, None, weff_m, sh_m, f"norm_mix{l}")
        else:
            xc, h = _norm_mod(xc, pending[0], pending[1], weff_m, sh_m, f"norm_mix{l}")
        z = mm_fwd(h, layer_w[l][0], BF16, f"mm_in{l}", 0, also=1 if l == 0 else None)
        st = dict(x=xc, h=h, z=z, weff_m=weff_m, weff_f=weff_f, g_m=g_m, g_f=g_f, sc_m=sc_m, sc_f=sc_f)
        if l % 2 == 0:
            q, k, v = _qkv_prep(z, cs, sa, sb, n_attn, f"qkv_prep{l}")
            o, lse, got = _attn_fwd(q, k, v, n_attn, f"attn_fwd{l}",
                                    job=_join_jobs(first_jobs[2:]) if l == 0 else None)
            if l == 0:
                layer_w[0][2], layer_w[0][3] = got
            bT = sgu_b[i].T
            so = _sgu_fwd(z, sgu_w[i], bT, n_attn, f"sgu_fwd{l}")
            cat = jnp.concatenate([o, so], axis=1)
            st.update(q=q, k=k, v=v, o=o, lse=lse, bT=bT)
        else:
            w8 = jnp.zeros((8, D), F32).at[:3].set(conv_full[i])
            cat = _conv_fwd(z, w8, f"conv_fwd{l}")
            st.update(w8=w8)
        mix = mm_fwd(cat, layer_w[l][1], F32, f"mm_out{l}", 1)
        x1, h2 = _norm_mod(xc, mix, g_m, weff_f, sh_f, f"norm_ffn{l}")
        ab = mm_fwd(h2, layer_w[l][2], BF16, f"mm_gu{l}", 2)
        f = _swiglu(ab, f"swiglu{l}")
        yv = mm_fwd(f, layer_w[l][3], F32, f"mm_down{l}", 3)
        st.update(cat=cat, mix=mix, x1=x1, h2=h2, ab=ab, f=f, y=yv)
        saved.append(st)
        xc = x1
        pending = (yv, g_f)

    dx, loss11, dfinal, dy, dg_f = _loss_head(xc, pending[0], pending[1], final_norm[None, :], target, "loss_head")
    loss = lax.psum(loss11[0, 0], ("x", "y", "c"))

    dmods = [None] * L
    dnorm_mix, dnorm_ffn = [None] * L, [None] * L
    big = {l: {} for l in range(L)}
    core_chip = jnp.concatenate([core1, chip1])
    dsgu_w, dsgu_b, dconv = [None] * (L - L // 2), [None] * (L - L // 2), [None] * (L // 2)

    def mm_bwd(a, b, mode, out_dtype, name, scatters=(), swap=None):
        jobs = [_scatter_job([p[0]] * len(p[1]), p[1]) for p in scatters]
        jobs += [_swap_job(swap[0])] if swap else []
        out, got = _hosted(lambda job: _mm(a, b, mode, out_dtype, name, job=job), jobs)
        for (part, windows, lay, keys), recv in zip(scatters, got):
            for key, red in zip(keys, _rs_finish(part, windows, recv, core_chip, f"{lay}_{keys[0]}")):
                big[lay][key] = red
        half = _add_half(swap[0], got[-1][0], core_chip, f"rs_add_half_{swap[1]}") if swap else None
        return out, half

    above = None
    for l in reversed(range(L)):
        i = l // 2
        st = saved[l]
        w_in, w_out, w_gu, w_dn = layer_w[l]
        if above is None:
            df, _ = mm_bwd(dy, w_dn, "nt", BF16, f"mm_down_dx{l}")
            dw_dn, _ = mm_bwd(st["f"], dy, "tn", F32, f"mm_down_dw{l}")
        else:
            s_out, t_in, win_in = above
            df, p_in = mm_bwd(dy, w_dn, "nt", BF16, f"mm_down_dx{l}", [s_out], (t_in, f"{l + 1}_in"))
            dw_dn, _ = mm_bwd(st["f"], dy, "tn", F32, f"mm_down_dw{l}", [(p_in, win_in, l + 1, ["in"])])
        t_dn, win_dn = _rs_split(dw_dn, [("row", 0, F // N_CHIPS)])
        dab = _swiglu_bwd(st["ab"], df, f"swiglu_bwd{l}")
        dh2, p_dn = mm_bwd(dab, w_gu, "nt", F32, f"mm_gu_dx{l}", swap=(t_dn, f"{l}_down"))
        dw_gu, _ = mm_bwd(st["h2"], dab, "tn", F32, f"mm_gu_dw{l}", [(p_dn, win_dn, l, ["down"])])
        t_gu, win_gu = _rs_split(dw_gu, [("col", 0, F // N_CHIPS), ("col", F, F // N_CHIPS)])
        dx1, dsh_f, dweff_f, dmix, dg_m = _norm_mod_bwd(dh2, st["x1"], st["weff_f"], dx, st["mix"], st["g_m"],
                                                        f"norm_ffn_bwd{l}")
        dcat, p_gu = mm_bwd(dmix, w_out, "nt", F32, f"mm_out_dx{l}", swap=(t_gu, f"{l}_gu"))
        dw_out = _mm(st["cat"], dmix, "tn", F32, f"mm_out_dw{l}")
        t_out, win_out = _rs_split(dw_out, [("row", 0, D // N_CHIPS)])
        if l % 2 == 0:
            dq, dk, dv = _attn_bwd(st["q"], st["k"], st["v"], st["o"], st["lse"], dcat, n_attn, f"attn_bwd{l}")
            dqkv = _dqkv_post(dq, dk, dv, cs, sa, sb, n_attn, f"dqkv_post{l}")
            duv, dsgu_w[i], dbT = _sgu_bwd(st["z"], dcat, sgu_w[i], st["bT"], n_attn, f"sgu_bwd{l}")
            dsgu_b[i] = dbT.T
            dz = jnp.concatenate([dqkv, duv], axis=1)
        else:
            dz, dw8 = _conv_bwd(st["z"], dcat, st["w8"], f"conv_bwd{l}")
            dconv[i] = dw8[:3]
        dh, p_out = mm_bwd(dz, w_in, "nt", F32, f"mm_in_dx{l}", [(p_gu, win_gu[:1], l, ["gate"])],
                           (t_out, f"{l}_out"))
        s_out = (p_out, win_out, l, ["out"])
        dw_in, _ = mm_bwd(st["h"], dz, "tn", F32, f"mm_in_dw{l}",
                          [(p_gu, win_gu[1:], l, ["up"])] + ([s_out] if l == 0 else []))
        t_in, win_in = _rs_split(dw_in, [("col", 0, w_in.shape[1] // N_CHIPS)])
        above = (s_out, t_in, win_in)
        dmod_f = [dsh_f, dweff_f * norm_ffn[l][None, :], dg_f]
        if l > 0:
            dx, dsh_m, dweff_m, dy, dg_f = _norm_mod_bwd(dh, st["x"], st["weff_m"], dx1, saved[l - 1]["y"],
                                                         saved[l - 1]["g_f"], f"norm_mix_bwd{l}")
        else:
            dx, dsh_m, dweff_m = _norm_mod_bwd(dh, st["x"], st["weff_m"], dx1, None, None, f"norm_mix_bwd{l}")
        dmods[l] = jnp.concatenate([dsh_m, dweff_m * norm_mix[l][None, :], dg_m] + dmod_f, axis=1)
        dnorm_mix[l] = dweff_m * (1.0 + st["sc_m"])
        dnorm_ffn[l] = dweff_f * (1.0 + st["sc_f"])
    _, t_in, win_in = above
    (recv_in,) = _run_job(_swap_job(t_in), "rs_swap_0_in")
    p_in0 = _add_half(t_in, recv_in, core_chip, "rs_add_half_0_in")
    grad_x = dx.reshape(1, S, D)

    dmod = jnp.concatenate(dmods, axis=0)
    small = [dmod.reshape(-1), jnp.concatenate(dnorm_mix, 0).reshape(-1), jnp.concatenate(dnorm_ffn, 0).reshape(-1),
             jnp.stack(dsgu_w).reshape(-1), jnp.stack(dsgu_b).reshape(-1), jnp.stack(dconv).reshape(-1), dfinal.reshape(-1)]
    sizes = [t.size for t in small]
    flat = jnp.concatenate(small)
    n_flat = flat.size
    rows = -(-n_flat // (128 * 512)) * 512
    flat = jnp.concatenate([flat, jnp.zeros((rows * 128 - n_flat,), F32)]).reshape(rows, 128)
    g2 = _all_gather8(flat, "gather_small")
    tot = _sum_leading(g2, "sum_small").reshape(-1)
    offs = [0]
    for s in sizes:
        offs.append(offs[-1] + s)
    take = lambda n, shape: tot[offs[n]:offs[n + 1]].reshape(shape)
    g_ada_b = take(0, ada_b.shape)
    g_norm_mix = take(1, norm_mix.shape)
    g_norm_ffn = take(2, norm_ffn.shape)
    g_sgu_w = take(3, sgu_w.shape)
    g_sgu_b = take(4, sgu_b.shape)
    g_conv_full = take(5, conv_full.shape)
    n_cw = conv_w.shape[2]
    g_conv_w = lax.dynamic_slice_in_dim(g_conv_full, chip * n_cw, n_cw, axis=2)
    g_final = take(6, final_norm.shape)
    dmod_all = g2[:, :, :].reshape(N_DEV, -1)[:, :offs[1]].reshape(N_DEV, L, 6 * D)
    dmod_mine = lax.dynamic_slice_in_dim(dmod_all, chip * n_ada, n_ada, axis=2).transpose(1, 0, 2)
    g_ada_w = _ada_bwd(c_act, dmod_mine, "ada_bwd")
    *ada_update, got = _adamw(ada_w, g_ada_w, m_ada_w, v_ada_w, "adamw_ada_w", job=_scatter_job([p_in0], win_in))
    (big[0]["in"],) = _rs_finish(p_in0, win_in, got, core_chip, "0_in")

    def stack(key, layers):
        return jnp.stack([big[l][key] for l in layers])

    even, odd, every = list(range(0, L, 2)), list(range(1, L, 2)), list(range(L))
    grads = dict(
        ada_w=g_ada_w, ada_b=g_ada_b, norm_mix=g_norm_mix, norm_ffn=g_norm_ffn,
        ab_w_in=stack("in", even), sgu_w=g_sgu_w, sgu_b=g_sgu_b, ab_w_out=stack("out", even),
        conv_w_in=stack("in", odd), conv_w=g_conv_w, conv_w_out=stack("out", odd),
        ffn_w_gate=stack("gate", every), ffn_w_up=stack("up", every), ffn_w_down=stack("down", every),
        final_norm=g_final)
    weights = dict(ada_w=ada_w, ada_b=ada_b, norm_mix=norm_mix, norm_ffn=norm_ffn, ab_w_in=ab_w_in, sgu_w=sgu_w,
                   sgu_b=sgu_b, ab_w_out=ab_w_out, conv_w_in=conv_w_in, conv_w=conv_w, conv_w_out=conv_w_out,
                   ffn_w_gate=ffn_w_gate, ffn_w_up=ffn_w_up, ffn_w_down=ffn_w_down, final_norm=final_norm)
    ms = dict(ada_w=m_ada_w, ada_b=m_ada_b, norm_mix=m_norm_mix, norm_ffn=m_norm_ffn, ab_w_in=m_ab_w_in, sgu_w=m_sgu_w,
              sgu_b=m_sgu_b, ab_w_out=m_ab_w_out, conv_w_in=m_conv_w_in, conv_w=m_conv_w, conv_w_out=m_conv_w_out,
              ffn_w_gate=m_ffn_w_gate, ffn_w_up=m_ffn_w_up, ffn_w_down=m_ffn_w_down, final_norm=m_final_norm)
    vs = dict(ada_w=v_ada_w, ada_b=v_ada_b, norm_mix=v_norm_mix, norm_ffn=v_norm_ffn, ab_w_in=v_ab_w_in, sgu_w=v_sgu_w,
              sgu_b=v_sgu_b, ab_w_out=v_ab_w_out, conv_w_in=v_conv_w_in, conv_w=v_conv_w, conv_w_out=v_conv_w_out,
              ffn_w_gate=v_ffn_w_gate, ffn_w_up=v_ffn_w_up, ffn_w_down=v_ffn_w_down, final_norm=v_final_norm)
    names = list(weights)
    deltas, new_m, new_v = {}, {}, {}
    for n in names:
        w, g = weights[n], grads[n]
        if n == "ada_w":
            deltas[n], new_m[n], new_v[n] = ada_update
        elif w.ndim == 1:
            d_, m_, v_ = _adamw(w[None, :], g[None, :], ms[n][None, :], vs[n][None, :], f"adamw_{n}")
            deltas[n], new_m[n], new_v[n] = d_[0], m_[0], v_[0]
        else:
            deltas[n], new_m[n], new_v[n] = _adamw(w, g, ms[n], vs[n], f"adamw_{n}")
    return (loss, grad_x, *[grads[n] for n in names], *[deltas[n] for n in names],
            *[new_m[n] for n in names], *[new_v[n] for n in names])
```

```python
import functools
import math

import jax
import jax.numpy as jnp
from jax import lax
from jax.experimental import pallas as pl
from jax.experimental.pallas import tpu as pltpu

F32 = jnp.float32
BF16 = jnp.bfloat16
HEAD_DIM = 128
CHUNK = 128
ATTN_BLOCK = 128
ATTN_LANES_FWD = 2
ATTN_LANES_BWD = 2
DILATIONS = (1, 4, 16)
ROPE_DIM = HEAD_DIM // 4
ROPE_THETA = 500000.0
EPS = 1e-6
MASKED = -1e30
ADAM_LR, ADAM_B1, ADAM_B2, ADAM_EPS, ADAM_WD, ADAM_STEP = 0.001, 0.9, 0.999, 1e-08, 0.01, 10
VMEM_LIMIT_BYTES = 56 * 1024 * 1024
MESH = pl.DeviceIdType.MESH
ANY = pl.BlockSpec(memory_space=pl.ANY)
N_CHIPS = 4
N_DEV = 8


def _pick(n, cands):
    for t in cands:
        if n % t == 0:
            return t
    return n


def _params(sem):
    return pltpu.CompilerParams(dimension_semantics=sem, vmem_limit_bytes=VMEM_LIMIT_BYTES)


MM_VMEM_BUDGET = 44 * 1024 * 1024
_TILES = (2048, 1536, 1408, 1024, 512, 256, 128)


def _mm_tiles(M, N, K, out_bytes):
    best = None
    for tm in [t for t in _TILES if M % t == 0] or [M]:
        for tn in [t for t in _TILES if N % t == 0] or [N]:
            for tk in [t for t in _TILES if K % t == 0] or [K]:
                nk = K // tk
                vmem = 2 * 2 * (tm * tk + tk * tn) + 2 * out_bytes * tm * tn + 4 * tm * tn
                vmem += 4 * tm * tn if nk > 1 and out_bytes == 2 else 0
                if vmem > MM_VMEM_BUDGET:
                    continue
                key = ((M // tm) * (N // tn) * nk, -tk, -tm)
                if best is None or key < best[0]:
                    best = (key, (tm, tn, tk))
    assert best is not None, (M, N, K)
    return best[1]


def _mm(a, b, mode, out_dtype, name, layer=None, job=None):
    bshape = b.shape[1:] if layer is not None else b.shape
    if mode == "nn":
        (M, K), (K2, N) = a.shape, bshape
    elif mode == "nt":
        (M, K), (N, K2) = a.shape, bshape
    else:
        (K, M), (K2, N) = a.shape, bshape
    assert K == K2, (a.shape, b.shape, mode)
    in_place = out_dtype == F32
    tm, tn, tk = _mm_tiles(M, N, K, 4 if in_place else 2)
    nk = K // tk
    dims = {"nn": (((1,), (0,)), ((), ())), "nt": (((1,), (1,)), ((), ())), "tn": (((0,), (0,)), ((), ()))}[mode]

    n_ji = len(job["ins"]) if job else 0
    n_jo = len(job["outs"]) if job else 0
    n_acc = 1 if nk > 1 and not in_place else 0
    grid = (M // tm, N // tn, nk)

    def body(a_ref, b_ref, *rest):
        jin, o_ref, jout = rest[:n_ji], rest[n_ji], rest[n_ji + 1:n_ji + 1 + n_jo]
        scratch = rest[n_ji + 1 + n_jo:]
        acc, sems = scratch[:n_acc], scratch[n_acc:]
        i, j, k = pl.program_id(0), pl.program_id(1), pl.program_id(2)
        if job:
            @pl.when((i == 0) & (j == 0) & (k == 0))
            def _():
                job["start"](jin, jout, sems)

        def product():
            return lax.dot_general(a_ref[...].astype(BF16), b_ref[...].astype(BF16), dims, preferred_element_type=F32)

        if nk == 1:
            o_ref[...] = product().astype(o_ref.dtype)
        else:
            acc_ref = o_ref if in_place else acc[0]

            @pl.when(k == 0)
            def _():
                acc_ref[...] = jnp.zeros_like(acc_ref)

            acc_ref[...] += product()

            if not in_place:
                @pl.when(k == nk - 1)
                def _():
                    o_ref[...] = acc_ref[...].astype(o_ref.dtype)

        if job:
            @pl.when((i == grid[0] - 1) & (j == grid[1] - 1) & (k == grid[2] - 1))
            def _():
                job["finish"](jin, jout, sems)

    if mode == "tn":
        a_spec = pl.BlockSpec((tk, tm), lambda i, j, k: (k, i))
    else:
        a_spec = pl.BlockSpec((tm, tk), lambda i, j, k: (i, k))
    if mode == "nt":
        bblk, bidx = (tn, tk), (lambda i, j, k: (j, k))
    else:
        bblk, bidx = (tk, tn), (lambda i, j, k: (k, j))
    if layer is not None:
        b_spec = pl.BlockSpec((None,) + bblk, lambda i, j, k: (layer,) + bidx(i, j, k))
    else:
        b_spec = pl.BlockSpec(bblk, bidx)
    out_spec = pl.BlockSpec((tm, tn), lambda i, j, k: (i, j))
    out_shape = jax.ShapeDtypeStruct((M, N), out_dtype)
    acc_scratch = [pltpu.VMEM((tm, tn), F32)] * n_acc
    if not job:
        return pl.pallas_call(
            body, name=name, grid=grid, in_specs=[a_spec, b_spec], out_specs=out_spec, out_shape=out_shape,
            scratch_shapes=acc_scratch, compiler_params=_params(("parallel", "parallel", "arbitrary")),
        )(a, b)
    return pl.pallas_call(
        body, name=name, grid=grid, in_specs=[a_spec, b_spec] + [ANY] * n_ji,
        out_specs=(out_spec,) + (ANY,) * n_jo, out_shape=(out_shape,) + tuple(job["outs"]),
        scratch_shapes=acc_scratch + list(job["sems"]),
        compiler_params=_params(("arbitrary", "arbitrary", "arbitrary")),
    )(a, b, *job["ins"])


def _rows(S):
    return _pick(S, (256, 128, 64, 32, 16, 8))


def _row_spec(tr, width, col=0):
    return pl.BlockSpec((tr, width), lambda i: (i, col))


def _vec_spec(rows, width):
    return pl.BlockSpec((rows, width), lambda i: (0, 0))


def _rms(xv):
    return lax.rsqrt(jnp.mean(xv * xv, axis=-1, keepdims=True) + EPS)


def _norm_mod(x, y, g, w_eff, sh, name):
    S, D = x.shape
    tr = _rows(S)
    fused = y is not None

    def body(*refs):
        if fused:
            x_ref, y_ref, g_ref, w_ref, s_ref, x1_ref, h_ref = refs
            xv = x_ref[...] + g_ref[...] * y_ref[...]
            x1_ref[...] = xv
        else:
            x_ref, w_ref, s_ref, h_ref = refs
            xv = x_ref[...]
        h_ref[...] = (xv * _rms(xv) * w_ref[...] + s_ref[...]).astype(BF16)

    big, vec = _row_spec(tr, D), _vec_spec(1, D)
    if fused:
        ins, in_specs = (x, y, g, w_eff, sh), [big, big, vec, vec, vec]
        out_shape = (jax.ShapeDtypeStruct((S, D), F32), jax.ShapeDtypeStruct((S, D), BF16))
        out_specs = (big, big)
    else:
        ins, in_specs = (x, w_eff, sh), [big, vec, vec]
        out_shape = jax.ShapeDtypeStruct((S, D), BF16)
        out_specs = big
    out = pl.pallas_call(body, name=name, grid=(S // tr,), in_specs=in_specs, out_specs=out_specs,
                         out_shape=out_shape, compiler_params=_params(("parallel",)))(*ins)
    return out if fused else (None, out)


def _norm_mod_bwd(dh, x, w_eff, dres, y, g, name):
    S, D = x.shape
    tr = _rows(S)
    gated = y is not None

    def body(dh_ref, x_ref, w_ref, r_ref, *rest):
        if gated:
            y_ref, g_ref, dx_ref, dsh_ref, dw_ref, dy_ref, dg_ref = rest
        else:
            dx_ref, dsh_ref, dw_ref = rest
        xv = x_ref[...]
        dhv = dh_ref[...].astype(F32)
        r = _rms(xv)
        xn = xv * r
        dxn = dhv * w_ref[...]
        dxv = r_ref[...] + r * (dxn - xn * jnp.mean(dxn * xn, axis=-1, keepdims=True))
        dx_ref[...] = dxv

        @pl.when(pl.program_id(0) == 0)
        def _():
            dsh_ref[...] = jnp.zeros_like(dsh_ref)
            dw_ref[...] = jnp.zeros_like(dw_ref)
            if gated:
                dg_ref[...] = jnp.zeros_like(dg_ref)

        dsh_ref[...] += jnp.sum(dhv, axis=0, keepdims=True)
        dw_ref[...] += jnp.sum(dhv * xn, axis=0, keepdims=True)
        if gated:
            dy_ref[...] = (dxv * g_ref[...]).astype(BF16)
            dg_ref[...] += jnp.sum(dxv * y_ref[...], axis=0, keepdims=True)

    big, vec = _row_spec(tr, D), _vec_spec(1, D)
    f32v = jax.ShapeDtypeStruct((1, D), F32)
    outs = (jax.ShapeDtypeStruct((S, D), F32), f32v, f32v)
    if not gated:
        return pl.pallas_call(
            body, name=name, grid=(S // tr,), in_specs=[big, big, vec, big], out_specs=(big, vec, vec), out_shape=outs,
            compiler_params=_params(("arbitrary",)))(dh, x, w_eff, dres)
    return pl.pallas_call(
        body, name=name, grid=(S // tr,), in_specs=[big, big, vec, big, big, vec],
        out_specs=(big, vec, vec, big, vec), out_shape=outs + (jax.ShapeDtypeStruct((S, D), BF16), f32v),
        compiler_params=_params(("arbitrary",)))(dh, x, w_eff, dres, y, g)


def _loss_head(x, y, g, gamma, target, name):
    S, D = x.shape
    tr = _rows(S)

    def body(x_ref, y_ref, g_ref, gm_ref, t_ref, dx_ref, loss_ref, dgm_ref, dy_ref, dg_ref):
        yv = y_ref[...]
        xv = x_ref[...] + g_ref[...] * yv
        r = _rms(xv)
        xn = xv * r
        err = xn * gm_ref[...] - t_ref[...]
        dout = err * (1.0 / D)
        dxn = dout * gm_ref[...]
        dxv = r * (dxn - xn * jnp.mean(dxn * xn, axis=-1, keepdims=True))
        dx_ref[...] = dxv
        dy_ref[...] = (dxv * g_ref[...]).astype(BF16)

        @pl.when(pl.program_id(0) == 0)
        def _():
            loss_ref[...] = jnp.zeros_like(loss_ref)
            dgm_ref[...] = jnp.zeros_like(dgm_ref)
            dg_ref[...] = jnp.zeros_like(dg_ref)

        loss_ref[...] += 0.5 * jnp.sum(jnp.mean(err * err, axis=-1, keepdims=True), axis=0, keepdims=True)
        dgm_ref[...] += jnp.sum(dout * xn, axis=0, keepdims=True)
        dg_ref[...] += jnp.sum(dxv * yv, axis=0, keepdims=True)

    big, vec = _row_spec(tr, D), _vec_spec(1, D)
    f32v = jax.ShapeDtypeStruct((1, D), F32)
    return pl.pallas_call(
        body, name=name, grid=(S // tr,), in_specs=[big, big, vec, vec, big],
        out_specs=(big, _vec_spec(1, 1), vec, big, vec),
        out_shape=(jax.ShapeDtypeStruct((S, D), F32), jax.ShapeDtypeStruct((1, 1), F32), f32v,
                   jax.ShapeDtypeStruct((S, D), BF16), f32v),
        compiler_params=_params(("arbitrary",)))(x, y, g, gamma, target)


def _silu(a):
    return a * jax.nn.sigmoid(a)


def _swiglu(ab, name):
    S, F2 = ab.shape
    F = F2 // 2
    tr = _pick(S, (256, 128, 64, 32, 16, 8))

    def body(a_ref, b_ref, f_ref):
        f_ref[...] = (_silu(a_ref[...].astype(F32)) * b_ref[...].astype(F32)).astype(BF16)

    return pl.pallas_call(
        body, name=name, grid=(S // tr,), in_specs=[_row_spec(tr, F, 0), _row_spec(tr, F, 1)],
        out_specs=_row_spec(tr, F), out_shape=jax.ShapeDtypeStruct((S, F), BF16),
        compiler_params=_params(("parallel",)))(ab, ab)


def _swiglu_bwd(ab, df, name):
    S, F2 = ab.shape
    F = F2 // 2
    tr = _pick(S, (256, 128, 64, 32, 16, 8))

    def body(a_ref, b_ref, df_ref, da_ref, db_ref):
        a = a_ref[...].astype(F32)
        sg = jax.nn.sigmoid(a)
        dfv = df_ref[...].astype(F32)
        da_ref[...] = (dfv * b_ref[...].astype(F32) * (sg * (1.0 + a * (1.0 - sg)))).astype(BF16)
        db_ref[...] = (dfv * a * sg).astype(BF16)

    def body2(a_ref, b_ref, df_ref, o_ref):
        body(a_ref, b_ref, df_ref, o_ref.at[:, pl.ds(0, F)], o_ref.at[:, pl.ds(F, F)])

    return pl.pallas_call(
        body2, name=name, grid=(S // tr,),
        in_specs=[_row_spec(tr, F, 0), _row_spec(tr, F, 1), _row_spec(tr, F)],
        out_specs=_row_spec(tr, F2), out_shape=jax.ShapeDtypeStruct((S, F2), BF16),
        compiler_params=_params(("parallel",)))(ab, ab, df)


def _shift_rows(v, n):
    return pltpu.roll(v, n, 0)


def _conv_fwd(p, w8, name):
    S, D3 = p.shape
    D = D3 // 3
    tr = _rows(S)
    nb8 = tr // 8

    def body(gb_ref, gc_ref, hx_ref, gcp_ref, hxp_ref, w_ref, o_ref):
        i = pl.program_id(0)
        y = gc_ref[...].astype(F32) * hx_ref[...].astype(F32)
        yp = jnp.where(i > 0, gcp_ref[...].astype(F32) * hxp_ref[...].astype(F32), 0.0)
        w0, w1, w2 = w_ref[0:1, :], w_ref[1:2, :], w_ref[2:3, :]
        conv = w0 * _shift_rows(y, 2) + w1 * _shift_rows(y, 1) + w2 * y
        o_ref[...] = (gb_ref[...].astype(F32) * conv).astype(BF16)
        rid = lax.broadcasted_iota(jnp.int32, (8, D), 0)
        y8 = y[0:8, :]
        y1 = jnp.where(rid < 1, _shift_rows(yp, 1), _shift_rows(y8, 1))
        y2 = jnp.where(rid < 2, _shift_rows(yp, 2), _shift_rows(y8, 2))
        conv8 = w0 * y2 + w1 * y1 + w2 * y8
        o_ref[0:8, :] = (gb_ref[0:8, :].astype(F32) * conv8).astype(BF16)

    def col(c):
        return pl.BlockSpec((tr, D), lambda i: (i, c))

    def prev8(c):
        return pl.BlockSpec((8, D), lambda i: (jnp.maximum(i * nb8 - 1, 0), c))

    return pl.pallas_call(
        body, name=name, grid=(S // tr,),
        in_specs=[col(0), col(1), col(2), prev8(1), prev8(2), _vec_spec(8, D)],
        out_specs=_row_spec(tr, D), out_shape=jax.ShapeDtypeStruct((S, D), BF16),
        compiler_params=_params(("parallel",)))(p, p, p, p, p, w8)


def _conv_bwd(p, do, w8, name):
    S, D3 = p.shape
    D = D3 // 3
    tr = _rows(S)
    nb8 = tr // 8
    nt = S // tr

    def body(gb_ref, gc_ref, hx_ref, gcp_ref, hxp_ref, do_ref, gbn_ref, don_ref, w_ref, dp_ref, dw_ref):
        i = pl.program_id(0)
        gb = gb_ref[...].astype(F32)
        gc = gc_ref[...].astype(F32)
        hx = hx_ref[...].astype(F32)
        dov = do_ref[...].astype(F32)
        y = gc * hx
        yp = jnp.where(i > 0, gcp_ref[...].astype(F32) * hxp_ref[...].astype(F32), 0.0)
        dconv = dov * gb
        dcn = jnp.where(i < nt - 1, don_ref[...].astype(F32) * gbn_ref[...].astype(F32), 0.0)
        w0, w1, w2 = w_ref[0:1, :], w_ref[1:2, :], w_ref[2:3, :]
        rid = lax.broadcasted_iota(jnp.int32, (tr, D), 0)
        rid8 = lax.broadcasted_iota(jnp.int32, (8, D), 0)
        yp1 = jnp.concatenate([_shift_rows(yp, 1), jnp.zeros((tr - 8, D), F32)], axis=0)
        yp2 = jnp.concatenate([_shift_rows(yp, 2), jnp.zeros((tr - 8, D), F32)], axis=0)
        y1 = jnp.where(rid < 1, yp1, _shift_rows(y, 1))
        y2 = jnp.where(rid < 2, yp2, _shift_rows(y, 2))
        conv = w0 * y2 + w1 * y1 + w2 * y
        dn1 = jnp.concatenate([jnp.zeros((tr - 8, D), F32), _shift_rows(dcn, 7)], axis=0)
        dn2 = jnp.concatenate([jnp.zeros((tr - 8, D), F32), _shift_rows(dcn, 6)], axis=0)
        d1 = jnp.where(rid >= tr - 1, dn1, _shift_rows(dconv, tr - 1))
        d2 = jnp.where(rid >= tr - 2, dn2, _shift_rows(dconv, tr - 2))
        dy = w2 * dconv + w1 * d1 + w0 * d2
        dp_ref[:, pl.ds(0, D)] = (dov * conv).astype(BF16)
        dp_ref[:, pl.ds(D, D)] = (dy * hx).astype(BF16)
        dp_ref[:, pl.ds(2 * D, D)] = (dy * gc).astype(BF16)

        @pl.when(i == 0)
        def _():
            dw_ref[...] = jnp.zeros_like(dw_ref)

        upd = jnp.where(rid8 == 0, jnp.sum(dconv * y2, axis=0, keepdims=True),
                        jnp.where(rid8 == 1, jnp.sum(dconv * y1, axis=0, keepdims=True),
                                  jnp.where(rid8 == 2, jnp.sum(dconv * y, axis=0, keepdims=True), 0.0)))
        dw_ref[...] += upd

    def col(c):
        return pl.BlockSpec((tr, D), lambda i: (i, c))

    def prev8(c):
        return pl.BlockSpec((8, D), lambda i: (jnp.maximum(i * nb8 - 1, 0), c))

    def next8(c):
        return pl.BlockSpec((8, D), lambda i: (jnp.minimum((i + 1) * nb8, S // 8 - 1), c))

    return pl.pallas_call(
        body, name=name, grid=(nt,),
        in_specs=[col(0), col(1), col(2), prev8(1), prev8(2), col(0), next8(0), next8(0), _vec_spec(8, D)],
        out_specs=(_row_spec(tr, D3), _vec_spec(8, D)),
        out_shape=(jax.ShapeDtypeStruct((S, D3), BF16), jax.ShapeDtypeStruct((8, D), F32)),
        compiler_params=_params(("arbitrary",)))(p, p, p, p, p, do, p, do, w8)


_GELU_C = math.sqrt(2.0 / math.pi)


def _gelu(v):
    return 0.5 * v * (1.0 + jnp.tanh(_GELU_C * (v + 0.044715 * v * v * v)))


def _gelu_grad(v):
    t = jnp.tanh(_GELU_C * (v + 0.044715 * v * v * v))
    return 0.5 * (1.0 + t) + 0.5 * v * (1.0 - t * t) * _GELU_C * (1.0 + 3.0 * 0.044715 * v * v)


def _tril(w):
    r = lax.broadcasted_iota(jnp.int32, (CHUNK, CHUNK), 0)
    c = lax.broadcasted_iota(jnp.int32, (CHUNK, CHUNK), 1)
    return jnp.where(r >= c, w, 0.0)


def _sgu_fwd(z, w, bT, n_attn, name):
    S = z.shape[0]
    G = w.shape[0]
    W = G * CHUNK
    tr = _pick(S, (512, 256, 128))
    ucol = 3 * n_attn * HEAD_DIM // W

    def body(u_ref, v_ref, w_ref, b_ref, o_ref):
        for g in range(G):
            wt = _tril(w_ref[g]).astype(BF16)
            for ci in range(tr // CHUNK):
                rows, cols = pl.ds(ci * CHUNK, CHUNK), pl.ds(g * CHUNK, CHUNK)
                gv = _gelu(v_ref[rows, cols].astype(F32)).astype(BF16)
                mixed = jnp.dot(wt, gv, preferred_element_type=F32) + b_ref[:, g:g + 1]
                o_ref[rows, cols] = (_gelu(u_ref[rows, cols].astype(F32)) * mixed).astype(BF16)

    return pl.pallas_call(
        body, name=name, grid=(S // tr,),
        in_specs=[_row_spec(tr, W, ucol), _row_spec(tr, W, ucol + 1),
                  pl.BlockSpec((G, CHUNK, CHUNK), lambda i: (0, 0, 0)), _vec_spec(CHUNK, G)],
        out_specs=_row_spec(tr, W), out_shape=jax.ShapeDtypeStruct((S, W), BF16),
        compiler_params=_params(("parallel",)))(z, z, w, bT)


def _sgu_bwd(z, dcat, w, bT, n_attn, name):
    S = z.shape[0]
    G = w.shape[0]
    W = G * CHUNK
    tr = _pick(S, (512, 256, 128))
    ucol = 3 * n_attn * HEAD_DIM // W
    dcol = n_attn * HEAD_DIM // W

    def body(u_ref, v_ref, d_ref, w_ref, b_ref, o_ref, dw_ref, db_ref):
        @pl.when(pl.program_id(0) == 0)
        def _():
            dw_ref[...] = jnp.zeros_like(dw_ref)
            db_ref[...] = jnp.zeros_like(db_ref)

        lane = lax.broadcasted_iota(jnp.int32, (CHUNK, G), 1)
        for g in range(G):
            wtf = _tril(w_ref[g])
            wt = wtf.astype(BF16)
            dw_acc = jnp.zeros((CHUNK, CHUNK), F32)
            db_acc = jnp.zeros((CHUNK, 1), F32)
            for ci in range(tr // CHUNK):
                rows, cols = pl.ds(ci * CHUNK, CHUNK), pl.ds(g * CHUNK, CHUNK)
                uv = u_ref[rows, cols].astype(F32)
                vv = v_ref[rows, cols].astype(F32)
                dov = d_ref[rows, cols]
                gv = _gelu(vv).astype(BF16)
                mixed = jnp.dot(wt, gv, preferred_element_type=F32) + b_ref[:, g:g + 1]
                dmixed = dov * _gelu(uv)
                dmb = dmixed.astype(BF16)
                dgv = lax.dot_general(wt, dmb, (((0,), (0,)), ((), ())), preferred_element_type=F32)
                o_ref[rows, cols] = (dov * mixed * _gelu_grad(uv)).astype(BF16)
                o_ref[rows, pl.ds(W + g * CHUNK, CHUNK)] = (dgv * _gelu_grad(vv)).astype(BF16)
                dw_acc += lax.dot_general(dmb, gv, (((1,), (1,)), ((), ())), preferred_element_type=F32)
                db_acc += jnp.sum(dmixed, axis=1, keepdims=True)
            dw_ref[g] += _tril(dw_acc)
            db_ref[...] += jnp.where(lane == g, db_acc, 0.0)

    return pl.pallas_call(
        body, name=name, grid=(S // tr,),
        in_specs=[_row_spec(tr, W, ucol), _row_spec(tr, W, ucol + 1), _row_spec(tr, W, dcol),
                  pl.BlockSpec((G, CHUNK, CHUNK), lambda i: (0, 0, 0)), _vec_spec(CHUNK, G)],
        out_specs=(_row_spec(tr, 2 * W), pl.BlockSpec((G, CHUNK, CHUNK), lambda i: (0, 0, 0)), _vec_spec(CHUNK, G)),
        out_shape=(jax.ShapeDtypeStruct((S, 2 * W), BF16), jax.ShapeDtypeStruct((G, CHUNK, CHUNK), F32),
                   jax.ShapeDtypeStruct((CHUNK, G), F32)),
        compiler_params=_params(("arbitrary",)))(z, z, dcat, w, bT)


def _rope(v, cs, sa, sb):
    return v * cs + pltpu.roll(v, HEAD_DIM - ROPE_DIM // 2, 1) * sa + pltpu.roll(v, ROPE_DIM // 2, 1) * sb


def _rope_t(d, cs, sa, sb):
    return d * cs + pltpu.roll(d * sa, ROPE_DIM // 2, 1) + pltpu.roll(d * sb, HEAD_DIM - ROPE_DIM // 2, 1)


def _qkv_prep(z, cs, sa, sb, n_attn, name):
    S = z.shape[0]
    A = n_attn * HEAD_DIM
    tr = _rows(S)

    def body(q_ref, k_ref, v_ref, c_ref, a_ref, b_ref, qo_ref, ko_ref, vo_ref):
        cv, av, bv = c_ref[...], a_ref[...], b_ref[...]
        for h in range(n_attn):
            cols = pl.ds(h * HEAD_DIM, HEAD_DIM)
            qo_ref[:, cols] = _rope(q_ref[:, cols].astype(F32), cv, av, bv)
            ko_ref[:, cols] = _rope(k_ref[:, cols].astype(F32), cv, av, bv)
        vo_ref[...] = v_ref[...].astype(F32)

    tab = _row_spec(tr, HEAD_DIM)
    out = jax.ShapeDtypeStruct((S, A), F32)
    return pl.pallas_call(
        body, name=name, grid=(S // tr,),
        in_specs=[_row_spec(tr, A, 0), _row_spec(tr, A, 1), _row_spec(tr, A, 2), tab, tab, tab],
        out_specs=(_row_spec(tr, A),) * 3, out_shape=(out,) * 3,
        compiler_params=_params(("parallel",)))(z, z, z, cs, sa, sb)


def _dqkv_post(dq, dk, dv, cs, sa, sb, n_attn, name):
    S, A = dq.shape
    tr = _rows(S)

    def body(q_ref, k_ref, v_ref, c_ref, a_ref, b_ref, o_ref):
        cv, av, bv = c_ref[...], a_ref[...], b_ref[...]
        for h in range(n_attn):
            cols = pl.ds(h * HEAD_DIM, HEAD_DIM)
            o_ref[:, pl.ds(h * HEAD_DIM, HEAD_DIM)] = _rope_t(q_ref[:, cols], cv, av, bv).astype(BF16)
            o_ref[:, pl.ds(A + h * HEAD_DIM, HEAD_DIM)] = _rope_t(k_ref[:, cols], cv, av, bv).astype(BF16)
        o_ref[:, pl.ds(2 * A, A)] = v_ref[...].astype(BF16)

    tab = _row_spec(tr, HEAD_DIM)
    return pl.pallas_call(
        body, name=name, grid=(S // tr,),
        in_specs=[_row_spec(tr, A)] * 3 + [tab, tab, tab],
        out_specs=_row_spec(tr, 3 * A), out_shape=jax.ShapeDtypeStruct((S, 3 * A), BF16),
        compiler_params=_params(("parallel",)))(dq, dk, dv, cs, sa, sb)


ATTN_CHUNK = 64
ATTN_PAD = DILATIONS[-1] * ATTN_BLOCK


def _group(d, S):
    return 2 if (S // (d * ATTN_BLOCK)) % 2 == 0 else 1


def _work_item(d, S, it):
    G = _group(d, S)
    ngrp = S // (d * ATTN_BLOCK * G)
    r = it // ngrp
    jb = (it % ngrp) * G
    return r + d * ATTN_BLOCK * jb, ATTN_PAD + r + d * ATTN_BLOCK * (jb - 1), jb == 0


def _chunk_bias(first_row):
    al = first_row + lax.broadcasted_iota(jnp.int32, (ATTN_CHUNK, 2 * ATTN_BLOCK), 0)
    kl = lax.broadcasted_iota(jnp.int32, (ATTN_CHUNK, 2 * ATTN_BLOCK), 1)
    seen = (kl >= al) & (kl <= al + ATTN_BLOCK)
    return jnp.where(seen, 0.0, MASKED), jnp.where(seen & (kl >= ATTN_BLOCK), 0.0, MASKED)


def _chunks(G):
    return [(slice(c * ATTN_CHUNK, (c + 1) * ATTN_CHUNK), c * ATTN_CHUNK // ATTN_BLOCK, (c * ATTN_CHUNK) % ATTN_BLOCK)
            for c in range(G * ATTN_BLOCK // ATTN_CHUNK)]


_NT = (((1,), (1,)), ((), ()))
_TN = (((0,), (0,)), ((), ()))


def _attn_fwd(q, k, v, n_attn, name, job=None):
    S, A = q.shape
    scale = HEAD_DIM ** -0.5
    n_ji = len(job["ins"]) if job else 0
    n_jo = len(job["outs"]) if job else 0

    def body(q_hbm, k_hbm, v_hbm, *rest):
        jin, (o_hbm, lse_hbm), jout = rest[:n_ji], rest[n_ji:n_ji + 2], rest[n_ji + 2:n_ji + 2 + n_jo]
        qs, ks, vs, acc, ms, ls, ob, sem = rest[n_ji + 2 + n_jo:n_ji + 2 + n_jo + 8]
        jsems = rest[n_ji + 2 + n_jo + 8:]
        h = pl.program_id(0)
        if job:
            @pl.when(h == 0)
            def _():
                job["start"](jin, jout, jsems)

        cols = pl.ds(pl.multiple_of(h * HEAD_DIM, HEAD_DIM), HEAD_DIM)
        data = pl.ds(ATTN_PAD, S)
        cps = [pltpu.make_async_copy(src.at[:, cols], dst, sem.at[i])
               for i, (src, dst) in enumerate(((q_hbm, qs), (k_hbm, ks.at[data, :]), (v_hbm, vs.at[data, :])))]
        for cp in cps:
            cp.start()

        @pl.when(h == 0)
        def _():
            ks[pl.ds(0, ATTN_PAD), :] = jnp.zeros((ATTN_PAD, HEAD_DIM), F32)
            vs[pl.ds(0, ATTN_PAD), :] = jnp.zeros((ATTN_PAD, HEAD_DIM), F32)

        acc[...] = jnp.zeros_like(acc)
        ms[...] = jnp.full_like(ms, MASKED)
        ls[...] = jnp.zeros_like(ls)
        for cp in cps:
            cp.wait()
        biases = {r0: _chunk_bias(r0) for r0 in range(0, ATTN_BLOCK, ATTN_CHUNK)}
        for d in DILATIONS:
            G = _group(d, S)
            lanes = S // (ATTN_BLOCK * G * ATTN_LANES_FWD)

            def step(it, carry, d=d, G=G, lanes=lanes):
                groups, chains = [], []
                for u in range(ATTN_LANES_FWD):
                    q0, k0, first = _work_item(d, S, it + u * lanes)
                    qrows = pl.ds(q0, G * ATTN_BLOCK, stride=d)
                    krows = pl.ds(k0, (G + 1) * ATTN_BLOCK, stride=d)
                    qb, kb, vb = (t[rows, :].astype(BF16) for t, rows in ((qs, qrows), (ks, krows), (vs, krows)))
                    m_all, l_all, a_all = ms[qrows, :], ls[qrows, :], acc[qrows, :]
                    groups.append(qrows)
                    kt = kb.T
                    vb = jnp.concatenate([vb, jnp.ones(vb.shape, BF16)], axis=1)
                    for sl, qi, r0 in _chunks(G):
                        ky = slice(qi * ATTN_BLOCK, (qi + 2) * ATTN_BLOCK)
                        bias = jnp.where(first, biases[r0][1], biases[r0][0]) if qi == 0 else biases[r0][0]
                        chains.append((qb[sl], kt[:, ky], vb[ky], bias, m_all[sl], l_all[sl], a_all[sl]))
                scores = [jnp.dot(c[0], c[1], preferred_element_type=F32) for c in chains]
                m_new, probs, alphas = [], [], []
                for (_, _, _, bias, m_old, _, _), s in zip(chains, scores):
                    s = s * scale + bias
                    m_c = jnp.maximum(m_old, jnp.max(s, axis=-1, keepdims=True))
                    m_new.append(m_c)
                    alphas.append(jnp.exp(m_old - m_c))
                    probs.append(jnp.exp(s - m_c).astype(BF16))
                pv = [jnp.dot(p, c[2], preferred_element_type=F32) for c, p in zip(chains, probs)]
                a_new = [alpha * c[6] + r[:, :HEAD_DIM] for c, r, alpha in zip(chains, pv, alphas)]
                l_new = [alpha * c[5] + r[:, HEAD_DIM:HEAD_DIM + 1] for c, r, alpha in zip(chains, pv, alphas)]
                n = len(_chunks(G))
                for u, qrows in enumerate(groups):
                    ms[qrows, :] = jnp.concatenate(m_new[u * n:(u + 1) * n], axis=0)
                    ls[qrows, :] = jnp.concatenate(l_new[u * n:(u + 1) * n], axis=0)
                    acc[qrows, :] = jnp.concatenate(a_new[u * n:(u + 1) * n], axis=0)
                return carry
            lax.fori_loop(0, lanes, step, 0)
        ob[...] = (acc[...] / ls[...]).astype(BF16)
        ms[...] = ms[...] + jnp.log(ls[...])
        out = [pltpu.make_async_copy(ob, o_hbm.at[:, cols], sem.at[0]),
               pltpu.make_async_copy(ms, lse_hbm.at[h], sem.at[1])]
        for cp in out:
            cp.start()
        for cp in out:
            cp.wait()
        if job:
            @pl.when(h == n_attn - 1)
            def _():
                job["finish"](jin, jout, jsems)

    res = pl.pallas_call(
        body, name=name, grid=(n_attn,), in_specs=[ANY] * (3 + n_ji), out_specs=(ANY,) * (2 + n_jo),
        out_shape=(jax.ShapeDtypeStruct((S, A), BF16), jax.ShapeDtypeStruct((n_attn, S, 1), F32))
        + tuple(job["outs"] if job else ()),
        scratch_shapes=[pltpu.VMEM((S, HEAD_DIM), F32)] + [pltpu.VMEM((S + ATTN_PAD, HEAD_DIM), F32)] * 2
        + [pltpu.VMEM((S, HEAD_DIM), F32)] + [pltpu.VMEM((S, 1), F32)] * 2
        + [pltpu.VMEM((S, HEAD_DIM), BF16), pltpu.SemaphoreType.DMA((3,))] + list(job["sems"] if job else ()),
        compiler_params=_params(("arbitrary",)))(q, k, v, *(job["ins"] if job else ()))
    return res[0], res[1], list(res[2:])


def _attn_bwd(q, k, v, o, lse, dcat, n_attn, name):
    S, A = q.shape
    scale = HEAD_DIM ** -0.5

    def body(q_hbm, k_hbm, v_hbm, o_hbm, lse_hbm, do_hbm, dq_hbm, dk_hbm, dv_hbm,
             qs, ks, vs, dos, dqs, dks, dvs, lses, dls, ob, sem):
        h = pl.program_id(0)
        cols = pl.ds(pl.multiple_of(h * HEAD_DIM, HEAD_DIM), HEAD_DIM)
        data = pl.ds(ATTN_PAD, S)
        cps = [pltpu.make_async_copy(src.at[:, cols], dst, sem.at[i])
               for i, (src, dst) in enumerate(((q_hbm, qs), (k_hbm, ks.at[data, :]), (v_hbm, vs.at[data, :]),
                                               (do_hbm, dos), (o_hbm, ob)))]
        cps.append(pltpu.make_async_copy(lse_hbm.at[h], lses, sem.at[5]))
        for cp in cps:
            cp.start()

        @pl.when(h == 0)
        def _():
            ks[pl.ds(0, ATTN_PAD), :] = jnp.zeros((ATTN_PAD, HEAD_DIM), F32)
            vs[pl.ds(0, ATTN_PAD), :] = jnp.zeros((ATTN_PAD, HEAD_DIM), F32)

        dqs[...] = jnp.zeros_like(dqs)
        dks[...] = jnp.zeros_like(dks)
        dvs[...] = jnp.zeros_like(dvs)
        for cp in cps:
            cp.wait()
        dls[...] = jnp.sum(dos[...] * ob[...].astype(F32), axis=-1, keepdims=True)
        biases = {r0: _chunk_bias(r0) for r0 in range(0, ATTN_BLOCK, ATTN_CHUNK)}
        for d in DILATIONS:
            G = _group(d, S)
            n_lanes = 1 if d == DILATIONS[-1] else ATTN_LANES_BWD
            lanes = S // (ATTN_BLOCK * G * n_lanes)
            assert lanes % 2 == 0

            def step(it, carry, d=d, G=G, lanes=lanes, n_lanes=n_lanes):
                groups, chains = [], []
                for u in range(n_lanes):
                    q0, k0, first = _work_item(d, S, it + u * lanes)
                    qrows = pl.ds(q0, G * ATTN_BLOCK, stride=d)
                    krows = pl.ds(k0, (G + 1) * ATTN_BLOCK, stride=d)
                    qb, kb, vb, dob = (t[rows, :].astype(BF16)
                                       for t, rows in ((qs, qrows), (ks, krows), (vs, krows), (dos, qrows)))
                    lse_v, dl_v, dq_old = lses[qrows, :], dls[qrows, :], dqs[qrows, :]
                    groups.append((qrows, krows, dks[krows, :], dvs[krows, :]))
                    kt, vt = kb.T, vb.T
                    for sl, qi, r0 in _chunks(G):
                        ky = slice(qi * ATTN_BLOCK, (qi + 2) * ATTN_BLOCK)
                        bias = jnp.where(first, biases[r0][1], biases[r0][0]) if qi == 0 else biases[r0][0]
                        chains.append((qb[sl], kb[ky], vt[:, ky], dob[sl], bias, lse_v[sl], dl_v[sl], dq_old[sl], u, qi,
                                       kt[:, ky]))
                scores = [jnp.dot(c[0], c[10], preferred_element_type=F32) for c in chains]
                dprobs = [jnp.dot(c[3], c[2], preferred_element_type=F32) for c in chains]
                probs, dscores = [], []
                for c, s, dp in zip(chains, scores, dprobs):
                    p = jnp.exp(s * scale + c[4] - c[5])
                    probs.append(p.astype(BF16))
                    dscores.append((p * (dp - c[6]) * scale).astype(BF16))
                dq_new = [c[7] + jnp.dot(ds, c[1], preferred_element_type=F32) for c, ds in zip(chains, dscores)]
                dk_add = [[None] * (G + 1) for _ in groups]
                dv_add = [[None] * (G + 1) for _ in groups]
                for c, ds, p in zip(chains, dscores, probs):
                    dk_c = lax.dot_general(ds, c[0], _TN, preferred_element_type=F32)
                    dv_c = lax.dot_general(p, c[3], _TN, preferred_element_type=F32)
                    u, qi = c[8], c[9]
                    for t in range(2):
                        half = slice(t * ATTN_BLOCK, (t + 1) * ATTN_BLOCK)
                        dk_add[u][qi + t] = dk_c[half] if dk_add[u][qi + t] is None else dk_add[u][qi + t] + dk_c[half]
                        dv_add[u][qi + t] = dv_c[half] if dv_add[u][qi + t] is None else dv_add[u][qi + t] + dv_c[half]
                n = len(_chunks(G))
                for u, (qrows, krows, dk_old, dv_old) in enumerate(groups):
                    dqs[qrows, :] = jnp.concatenate(dq_new[u * n:(u + 1) * n], axis=0)
                    dks[krows, :] = dk_old + jnp.concatenate(dk_add[u], axis=0)
                    dvs[krows, :] = dv_old + jnp.concatenate(dv_add[u], axis=0)
                return carry
            lax.fori_loop(0, lanes, step, 0)
        out = [pltpu.make_async_copy(src, dst.at[:, cols], sem.at[i])
               for i, (src, dst) in enumerate(((dqs, dq_hbm), (dks.at[data, :], dk_hbm), (dvs.at[data, :], dv_hbm)))]
        for cp in out:
            cp.start()
        for cp in out:
            cp.wait()

    grad = jax.ShapeDtypeStruct((S, A), F32)
    plain, padded = pltpu.VMEM((S, HEAD_DIM), F32), pltpu.VMEM((S + ATTN_PAD, HEAD_DIM), F32)
    return pl.pallas_call(
        body, name=name, grid=(n_attn,), in_specs=[ANY] * 6, out_specs=(ANY, ANY, ANY), out_shape=(grad,) * 3,
        scratch_shapes=[plain, padded, padded, plain, plain, padded, padded] + [pltpu.VMEM((S, 1), F32)] * 2
        + [pltpu.VMEM((S, HEAD_DIM), BF16), pltpu.SemaphoreType.DMA((6,))],
        compiler_params=_params(("arbitrary",)))(q, k, v, o, lse, dcat)


def _ada_fwd(c_act, ada_w, name):
    L, D, n = ada_w.shape
    tn = _pick(n, (512, 256, 128))

    def body(c_ref, w_ref, o_ref):
        o_ref[...] = jnp.dot(c_ref[...], w_ref[...], preferred_element_type=F32)

    return pl.pallas_call(
        body, name=name, grid=(L, n // tn),
        in_specs=[pl.BlockSpec((N_DEV, D), lambda l, j: (0, 0)), pl.BlockSpec((None, D, tn), lambda l, j: (l, 0, j))],
        out_specs=pl.BlockSpec((None, N_DEV, tn), lambda l, j: (l, 0, j)),
        out_shape=jax.ShapeDtypeStruct((L, N_DEV, n), F32),
        compiler_params=_params(("parallel", "parallel")))(c_act, ada_w)


def _ada_bwd(c_act, dmod, name):
    L, _, n = dmod.shape
    D = c_act.shape[1]
    tn = _pick(n, (512, 256, 128))

    def body(c_ref, d_ref, o_ref):
        o_ref[...] = lax.dot_general(c_ref[...], d_ref[...], _TN, preferred_element_type=F32)

    return pl.pallas_call(
        body, name=name, grid=(L, n // tn),
        in_specs=[pl.BlockSpec((N_DEV, D), lambda l, j: (0, 0)), pl.BlockSpec((None, N_DEV, tn), lambda l, j: (l, 0, j))],
        out_specs=pl.BlockSpec((None, D, tn), lambda l, j: (l, 0, j)),
        out_shape=jax.ShapeDtypeStruct((L, D, n), F32),
        compiler_params=_params(("parallel", "parallel")))(c_act, dmod)


def _adamw(w, g, m, v, name, job=None):
    shape = w.shape
    C = shape[-1]
    R = w.size // C
    w2, g2, m2, v2 = (t.reshape(R, C) for t in (w, g, m, v))
    tr = _pick(R, (256, 128, 64, 32, 16, 8))
    tc = _pick(C, (2048, 1536, 1408, 1024, 512, 256, 128))
    n_ji = len(job["ins"]) if job else 0
    n_jo = len(job["outs"]) if job else 0
    grid = (R // tr, C // tc)

    def body(w_ref, g_ref, m_ref, v_ref, *rest):
        jin, (d_ref, mo_ref, vo_ref) = rest[:n_ji], rest[n_ji:n_ji + 3]
        jout, jsems = rest[n_ji + 3:n_ji + 3 + n_jo], rest[n_ji + 3 + n_jo:]
        i, j = pl.program_id(0), pl.program_id(1)
        if job:
            @pl.when((i == 0) & (j == 0))
            def _():
                job["start"](jin, jout, jsems)

        gv = g_ref[...]
        mn = ADAM_B1 * m_ref[...] + (1.0 - ADAM_B1) * gv
        vn = ADAM_B2 * v_ref[...] + (1.0 - ADAM_B2) * (gv * gv)
        m_hat = mn / (1.0 - ADAM_B1 ** ADAM_STEP)
        v_hat = vn / (1.0 - ADAM_B2 ** ADAM_STEP)
        d_ref[...] = -ADAM_LR * (m_hat / (jnp.sqrt(v_hat) + ADAM_EPS) + ADAM_WD * w_ref[...])
        mo_ref[...] = mn
        vo_ref[...] = vn
        if job:
            @pl.when((i == grid[0] - 1) & (j == grid[1] - 1))
            def _():
                job["finish"](jin, jout, jsems)

    spec = pl.BlockSpec((tr, tc), lambda i, j: (i, j))
    out = jax.ShapeDtypeStruct((R, C), F32)
    d, mn, vn, *got = pl.pallas_call(
        body, name=name, grid=grid, in_specs=[spec] * 4 + [ANY] * n_ji, out_specs=(spec,) * 3 + (ANY,) * n_jo,
        out_shape=(out,) * 3 + tuple(job["outs"] if job else ()), scratch_shapes=list(job["sems"] if job else ()),
        compiler_params=_params(("arbitrary", "arbitrary") if job else ("parallel", "parallel")),
    )(w2, g2, m2, v2, *(job["ins"] if job else ()))
    res = (d.reshape(shape), mn.reshape(shape), vn.reshape(shape))
    return res + (got,) if job else res


def _sum_leading(t, name):
    n, R, C = t.shape
    tr = _pick(R, (1024, 512, 256, 128, 64, 32, 16, 8))

    def body(t_ref, o_ref):
        acc = t_ref[0]
        for i in range(1, n):
            acc = acc + t_ref[i]
        o_ref[...] = acc

    return pl.pallas_call(
        body, name=name, grid=(R // tr,), in_specs=[pl.BlockSpec((n, tr, C), lambda i: (0, i, 0))],
        out_specs=pl.BlockSpec((tr, C), lambda i: (i, 0)), out_shape=jax.ShapeDtypeStruct((R, C), F32),
        compiler_params=_params(("parallel",)))(t)


def _coords():
    return lax.axis_index("x"), lax.axis_index("y"), lax.axis_index("c")


def _other_chips(x, y):
    return [(1 - x, y), (x, 1 - y), (1 - x, 1 - y)]


def _all_gather8(t, name):
    R, C = t.shape

    def body(x_ref, out_ref, send_sems, recv_sems, local_sem):
        x, y, c = _coords()
        me, sibling = (x, y, c), (x, y, 1 - c)
        chips = _other_chips(x, y)

        def slot(px, py, pc):
            return out_ref.at[4 * px + 2 * py + pc]

        def copy(k, block, to, src=None):
            return pltpu.make_async_remote_copy(
                src_ref=slot(*block) if src is None else src, dst_ref=slot(*block),
                send_sem=send_sems.at[k], recv_sem=recv_sems.at[k], device_id=to, device_id_type=MESH)

        mine = pltpu.make_async_copy(x_ref, slot(*me), local_sem)
        mine.start()
        first = [copy(0, me, sibling, src=x_ref)]
        first += [copy(1 + j, me, (*chip, c), src=x_ref) for j, chip in enumerate(chips)]
        for cp in first:
            cp.start()
        passed = [copy(4 + j, (*chip, c), sibling) for j, chip in enumerate(chips)]
        for j, chip in enumerate(chips):
            copy(1 + j, (*chip, c), me).wait_recv()
            passed[j].start()
        copy(0, sibling, me).wait_recv()
        for j, chip in enumerate(chips):
            copy(4 + j, (*chip, 1 - c), me).wait_recv()
        for cp in first + passed:
            cp.wait_send()
        mine.wait()

    return pl.pallas_call(
        body, name=name, out_shape=jax.ShapeDtypeStruct((N_DEV, R, C), t.dtype),
        in_specs=[pl.BlockSpec(memory_space=pltpu.VMEM)], out_specs=pl.BlockSpec(memory_space=pltpu.VMEM),
        scratch_shapes=[pltpu.SemaphoreType.DMA((7,)), pltpu.SemaphoreType.DMA((7,)), pltpu.SemaphoreType.DMA],
        compiler_params=pltpu.CompilerParams(vmem_limit_bytes=VMEM_LIMIT_BYTES))(t)


def _window(ref, r0, nr, c0, nc):
    return ref.at[pl.ds(r0, nr), pl.ds(c0, nc)]


LOCAL_CHUNKS = 4


def _gather_job(pieces, out_shape):
    n = len(pieces)

    def ctx(ins, outs, sems):
        x, y, c = _coords()
        buf = outs[0]

        def place(p, chip_idx, r0, nr):
            _, _, kind, base = pieces[p]
            r, cs = ins[p].shape[1], ins[p].shape[2]
            if kind == "row":
                return _window(buf, base + chip_idx * r + r0, nr, 0, cs)
            return _window(buf, r0, nr, base + chip_idx * cs, cs)

        def ici(p, j, chip, src, dst):
            return pltpu.make_async_remote_copy(
                src_ref=src, dst_ref=dst, send_sem=sems[0].at[3 * p + j], recv_sem=sems[1].at[3 * p + j],
                device_id=(*chip, c), device_id_type=MESH)

        def d2d(p, j, win):
            return pltpu.make_async_remote_copy(
                src_ref=win, dst_ref=win, send_sem=sems[2].at[3 * p + j], recv_sem=sems[3].at[3 * p + j],
                device_id=(x, y, 1 - c), device_id_type=MESH)

        def local(p):
            lidx, r = pieces[p][1], ins[p].shape[1]
            rc = r // (2 * LOCAL_CHUNKS)
            return [pltpu.make_async_copy(ins[p].at[lidx, pl.ds(q * rc, rc), :], place(p, 2 * x + y, q * rc, rc),
                                          sems[4].at[2 * LOCAL_CHUNKS * p + q]) for q in range(2 * LOCAL_CHUNKS)]

        return x, y, c, _other_chips(x, y), place, ici, d2d, local

    def start(ins, outs, sems):
        x, y, c, chips, place, ici, d2d, local = ctx(ins, outs, sems)
        for p in range(n):
            lidx, rh = pieces[p][1], ins[p].shape[1] // 2
            for j, chip in enumerate(chips):
                ici(p, j, chip, ins[p].at[lidx, pl.ds(c * rh, rh), :], place(p, 2 * x + y, c * rh, rh)).start()
        for p in range(n):
            for cp in local(p):
                cp.start()

    def finish(ins, outs, sems):
        x, y, c, chips, place, ici, d2d, local = ctx(ins, outs, sems)
        for p in range(n):
            rh = ins[p].shape[1] // 2
            for j, chip in enumerate(chips):
                landed = place(p, 2 * chip[0] + chip[1], c * rh, rh)
                ici(p, j, chip, landed, landed).wait_recv()
                d2d(p, j, landed).start()
        for p in range(n):
            lidx, rh = pieces[p][1], ins[p].shape[1] // 2
            for j, chip in enumerate(chips):
                theirs = place(p, 2 * chip[0] + chip[1], (1 - c) * rh, rh)
                d2d(p, j, theirs).wait_recv()
                d2d(p, j, place(p, 2 * chip[0] + chip[1], c * rh, rh)).wait_send()
                ici(p, j, chip, ins[p].at[lidx, pl.ds(c * rh, rh), :], place(p, 2 * x + y, c * rh, rh)).wait_send()
            for cp in local(p):
                cp.wait()

    return dict(
        ins=[p[0] for p in pieces], outs=[jax.ShapeDtypeStruct(out_shape, BF16)], start=start, finish=finish,
        sems=[pltpu.SemaphoreType.DMA((3 * n,))] * 4 + [pltpu.SemaphoreType.DMA((2 * LOCAL_CHUNKS * n,))])


def _scatter_job(parts, plan):
    n = len(parts)

    def shard_shape(i):
        kind, _, size = plan[i]
        R, C = parts[i].shape
        return (size, C) if kind == "row" else (R, size)

    def copies(ins, outs, sems):
        x, y, c = _coords()
        cps = []
        for i in range(n):
            kind, base, size = plan[i]
            R, C = ins[i].shape
            for j, chip in enumerate(_other_chips(x, y)):
                their = 2 * chip[0] + chip[1]
                if kind == "row":
                    src = _window(ins[i], base + their * size, size, 0, C)
                else:
                    src = _window(ins[i], 0, R, base + their * size, size)
                cps.append(pltpu.make_async_remote_copy(
                    src_ref=src, dst_ref=outs[i].at[j], send_sem=sems[0].at[3 * i + j],
                    recv_sem=sems[1].at[3 * i + j], device_id=(*chip, c), device_id_type=MESH))
        return cps

    def start(ins, outs, sems):
        for cp in copies(ins, outs, sems):
            cp.start()

    def finish(ins, outs, sems):
        for cp in copies(ins, outs, sems):
            cp.wait()

    return dict(
        ins=list(parts), outs=[jax.ShapeDtypeStruct((3,) + shard_shape(i), F32) for i in range(n)],
        start=start, finish=finish, sems=[pltpu.SemaphoreType.DMA((3 * n,))] * 2)


def _swap_job(t):
    def copy(ins, outs, sems):
        x, y, c = _coords()
        return pltpu.make_async_remote_copy(
            src_ref=ins[0].at[1 - c], dst_ref=outs[0], send_sem=sems[0].at[0], recv_sem=sems[1].at[0],
            device_id=(x, y, 1 - c), device_id_type=MESH)

    return dict(ins=[t], outs=[jax.ShapeDtypeStruct(t.shape[1:], t.dtype)],
                start=lambda ins, outs, sems: copy(ins, outs, sems).start(),
                finish=lambda ins, outs, sems: copy(ins, outs, sems).wait(),
                sems=[pltpu.SemaphoreType.DMA((1,))] * 2)


def _join_jobs(jobs):
    jobs = [j for j in jobs if j]
    if not jobs:
        return None
    if len(jobs) == 1:
        return jobs[0]

    def each(fn_name, ins, outs, sems):
        i = o = s = 0
        for j in jobs:
            ni, no, ns = len(j["ins"]), len(j["outs"]), len(j["sems"])
            j[fn_name](ins[i:i + ni], outs[o:o + no], sems[s:s + ns])
            i, o, s = i + ni, o + no, s + ns

    return dict(ins=[a for j in jobs for a in j["ins"]], outs=[a for j in jobs for a in j["outs"]],
                sems=[a for j in jobs for a in j["sems"]],
                start=lambda ins, outs, sems: each("start", ins, outs, sems),
                finish=lambda ins, outs, sems: each("finish", ins, outs, sems))


def _hosted(call, jobs):
    jobs = [j for j in jobs if j]
    if not jobs:
        return call(None), []
    out, *rest = call(_join_jobs(jobs))
    per = []
    for j in jobs:
        n = len(j["outs"])
        per.append(rest[:n])
        rest = rest[n:]
    return out, per


def _run_job(job, name):
    n_i, n_o = len(job["ins"]), len(job["outs"])

    def body(*refs):
        ins, outs, sems = refs[:n_i], refs[n_i:n_i + n_o], refs[n_i + n_o:]
        job["start"](ins, outs, sems)
        job["finish"](ins, outs, sems)

    return pl.pallas_call(
        body, name=name, in_specs=[ANY] * n_i, out_specs=tuple([ANY] * n_o), out_shape=tuple(job["outs"]),
        scratch_shapes=list(job["sems"]),
        compiler_params=pltpu.CompilerParams(vmem_limit_bytes=VMEM_LIMIT_BYTES))(*job["ins"])


def _sibling_share(both, name):
    n = len(both)

    def body(*refs):
        outs = refs[n:2 * n]
        send_sems, recv_sems = refs[2 * n:]
        x, y, c = _coords()
        cps = []
        for i in range(n):
            cp = pltpu.make_async_remote_copy(
                src_ref=outs[i].at[c], dst_ref=outs[i].at[c], send_sem=send_sems.at[i], recv_sem=recv_sems.at[i],
                device_id=(x, y, 1 - c), device_id_type=MESH)
            cp.start()
            cps.append(cp)
        for cp in cps:
            cp.wait()

    return pl.pallas_call(
        body, name=name, in_specs=[ANY] * n, out_specs=tuple([ANY] * n),
        out_shape=tuple(jax.ShapeDtypeStruct(b.shape, b.dtype) for b in both),
        input_output_aliases={i: i for i in range(n)},
        scratch_shapes=[pltpu.SemaphoreType.DMA((n,))] * 2,
        compiler_params=pltpu.CompilerParams(vmem_limit_bytes=VMEM_LIMIT_BYTES))(*both)


def _add_half(full3, recv, core, name):
    _, Rh, C = full3.shape
    tr = _pick(Rh, (256, 176, 128, 64, 32, 16, 8))
    tc = _pick(C, (2048, 1536, 1408, 1024, 512, 256, 128))

    def body(c_ref, a_ref, b_ref, o_ref):
        o_ref[...] = a_ref[...] + b_ref[...]

    return pl.pallas_call(
        body, name=name,
        grid_spec=pltpu.PrefetchScalarGridSpec(
            num_scalar_prefetch=1, grid=(Rh // tr, C // tc),
            in_specs=[pl.BlockSpec((None, tr, tc), lambda i, j, cr: (cr[0], i, j)),
                      pl.BlockSpec((tr, tc), lambda i, j, cr: (i, j))],
            out_specs=pl.BlockSpec((tr, tc), lambda i, j, cr: (i, j))),
        out_shape=jax.ShapeDtypeStruct((Rh, C), F32),
        compiler_params=_params(("parallel", "parallel")))(core, full3, recv)


def _add_scattered(part, recv, kind, base, size, core_chip, name):
    _, rs, cs = recv.shape
    tr = _pick(rs, (256, 176, 128, 64, 32, 16, 8))
    tc = _pick(cs, (2048, 1536, 1408, 1024, 512, 256, 128))
    assert base % size == 0
    if kind == "row":
        pidx = lambda i, j, cr: ((base // size + cr[1]) * (rs // tr) + i, j)
    else:
        pidx = lambda i, j, cr: (i, (base // size + cr[1]) * (cs // tc) + j)

    def body(c_ref, a_ref, r_ref, o_ref):
        o_ref[...] = ((a_ref[...] + r_ref[0]) + r_ref[1]) + r_ref[2]

    return pl.pallas_call(
        body, name=name,
        grid_spec=pltpu.PrefetchScalarGridSpec(
            num_scalar_prefetch=1, grid=(rs // tr, cs // tc),
            in_specs=[pl.BlockSpec((tr, tc), pidx), pl.BlockSpec((3, tr, tc), lambda i, j, cr: (0, i, j))],
            out_specs=pl.BlockSpec((None, tr, tc), lambda i, j, cr: (cr[0], i, j))),
        out_shape=jax.ShapeDtypeStruct((2, rs, cs), F32),
        compiler_params=_params(("parallel", "parallel")))(core_chip, part, recv)


def _rs_split(g, windows):
    R, C = g.shape
    if windows[0][0] == "row":
        size = windows[0][2]
        t = g.reshape(N_CHIPS, 2, size // 2, C).transpose(1, 0, 2, 3).reshape(2, R // 2, C)
        return t, [(k, b // 2, s // 2) for k, b, s in windows]
    return g.reshape(2, R // 2, C), list(windows)


def _rs_finish(part, windows, got, core_chip, tag):
    both = [_add_scattered(part, r, k, b, s, core_chip, f"rs_add_{tag}_{n}")
            for n, (r, (k, b, s)) in enumerate(zip(got, windows))]
    both = _sibling_share(both, f"rs_share_{tag}")
    return [t.reshape(2 * t.shape[1], t.shape[2]) for t in both]


def _rope_tables(positions, S):
    half = ROPE_DIM // 2
    inv_freq = ROPE_THETA ** (-jnp.arange(0, ROPE_DIM, 2, dtype=F32) / ROPE_DIM)
    ang = positions.reshape(S, 1).astype(F32) * inv_freq[None, :]
    cos, sin = jnp.cos(ang), jnp.sin(ang)
    zeros = jnp.zeros((S, half), F32)
    rest0 = jnp.zeros((S, HEAD_DIM - ROPE_DIM), F32)
    cs = jnp.concatenate([cos, cos, jnp.ones((S, HEAD_DIM - ROPE_DIM), F32)], axis=1)
    sa = jnp.concatenate([-sin, zeros, rest0], axis=1)
    sb = jnp.concatenate([zeros, sin, rest0], axis=1)
    return cs, sa, sb


def kernel(x, c, positions, ada_w, ada_b, norm_mix, norm_ffn, ab_w_in, sgu_w, sgu_b, ab_w_out, conv_w_in, conv_w, conv_w_out, ffn_w_gate, ffn_w_up, ffn_w_down, final_norm, loss_target, m_ada_w, m_ada_b, m_norm_mix, m_norm_ffn, m_ab_w_in, m_sgu_w, m_sgu_b, m_ab_w_out, m_conv_w_in, m_conv_w, m_conv_w_out, m_ffn_w_gate, m_ffn_w_up, m_ffn_w_down, m_final_norm, v_ada_w, v_ada_b, v_norm_mix, v_norm_ffn, v_ab_w_in, v_sgu_w, v_sgu_b, v_ab_w_out, v_conv_w_in, v_conv_w, v_conv_w_out, v_ffn_w_gate, v_ffn_w_up, v_ffn_w_down, v_final_norm):
    S, D = x.shape[1], x.shape[2]
    L = ada_w.shape[0]
    n_mix_heads = D // HEAD_DIM
    n_attn = 3 * n_mix_heads // 4
    A = n_attn * HEAD_DIM
    G = n_mix_heads - n_attn
    F = ffn_w_gate.shape[2] * N_CHIPS
    mix_in = ab_w_in.shape[2] * N_CHIPS
    xi, yi, ci = _coords()
    chip = 2 * xi + yi
    dev = 4 * xi + 2 * yi + ci
    core1 = jnp.reshape(ci, (1,)).astype(jnp.int32)
    chip1 = jnp.reshape(chip, (1,)).astype(jnp.int32)
    x2 = x.reshape(S, D)
    target = loss_target.reshape(S, D)

    n_conv = conv_w.size
    w0 = D + n_conv
    w0p = -(-w0 // 128) * 128
    pack = jnp.zeros((8, w0p), F32).at[0, :D].set(c[0]).at[0, D:w0].set(conv_w.reshape(-1))
    g0 = _all_gather8(pack, "gather_cond")
    c_all = g0[:, 0, :D]
    c_act = c_all * jax.nn.sigmoid(c_all)
    conv_full = jnp.concatenate(
        [g0[2 * j, 0, D:w0].reshape(conv_w.shape) for j in range(N_CHIPS)], axis=2)
    mod_part = _ada_fwd(c_act, ada_w, "ada_fwd")
    n_ada = ada_w.shape[2]
    g1 = _all_gather8(mod_part.reshape(L * N_DEV, n_ada), "gather_mod")
    mod_all = jnp.concatenate([g1[2 * j].reshape(L, N_DEV, n_ada) for j in range(N_CHIPS)], axis=2)
    mod = lax.dynamic_index_in_dim(mod_all, dev, axis=1, keepdims=False) + ada_b
    mods = mod.reshape(L, 6, 1, D)
    cs, sa, sb = _rope_tables(positions, S)

    bf = lambda t: t.astype(BF16)
    w_in_e, w_out_e = bf(ab_w_in), bf(ab_w_out)
    w_in_o, w_out_o = bf(conv_w_in), bf(conv_w_out)
    w_gate, w_up, w_down = bf(ffn_w_gate), bf(ffn_w_up), bf(ffn_w_down)
    def gather_jobs(l):
        i = l // 2
        first, n_in_cols = ((w_in_e, w_out_e), mix_in) if l % 2 == 0 else ((w_in_o, w_out_o), 3 * D)
        return [_gather_job([(first[0], i, "col", 0)], (D, n_in_cols)),
                _gather_job([(first[1], i, "row", 0)], (D, D)),
                _gather_job([(w_gate, l, "col", 0), (w_up, l, "col", F)], (D, 2 * F)),
                _gather_job([(w_down, l, "row", 0)], (F, D))]

    first_jobs = gather_jobs(0)
    layer_w = [[_run_job(first_jobs[0], "gather_w0_in")[0], None, None, None]]

    saved = []
    xc = x2
    pending = None
    for l in range(L):
        i = l // 2
        nxt = gather_jobs(l + 1) if l + 1 < L else [None] * 4
        if l + 1 < L:
            layer_w.append([None] * 4)

        def mm_fwd(a, b, out_dtype, name, slot, also=None):
            jobs = [nxt[slot], first_jobs[also] if also is not None else None]
            out, got = _hosted(lambda job: _mm(a, b, "nn", out_dtype, name, job=job), jobs)
            if nxt[slot] is not None:
                layer_w[l + 1][slot] = got.pop(0)[0]
            if also is not None:
                layer_w[l][also] = got.pop(0)[0]
            return out

        sh_m, sc_m, g_m, sh_f, sc_f, g_f = (mods[l, t] for t in range(6))
        weff_m = norm_mix[l][None, :] * (1.0 + sc_m)
        weff_f = norm_ffn[l][None, :] * (1.0 + sc_f)
        if pending is None:
            _, h = _norm_mod(xc, None, None, weff_m, sh_m, f"norm_mix{l}")
        else:
            xc, h = _norm_mod(xc, pending[0], pending[1], weff_m, sh_m, f"norm_mix{l}")
        z = mm_fwd(h, layer_w[l][0], BF16, f"mm_in{l}", 0, also=1 if l == 0 else None)
        st = dict(x=xc, h=h, z=z, weff_m=weff_m, weff_f=weff_f, g_m=g_m, g_f=g_f, sc_m=sc_m, sc_f=sc_f)
        if l % 2 == 0:
            q, k, v = _qkv_prep(z, cs, sa, sb, n_attn, f"qkv_prep{l}")
            o, lse, got = _attn_fwd(q, k, v, n_attn, f"attn_fwd{l}",
                                    job=_join_jobs(first_jobs[2:]) if l == 0 else None)
            if l == 0:
                layer_w[0][2], layer_w[0][3] = got
            bT = sgu_b[i].T
            so = _sgu_fwd(z, sgu_w[i], bT, n_attn, f"sgu_fwd{l}")
            cat = jnp.concatenate([o, so], axis=1)
            st.update(q=q, k=k, v=v, o=o, lse=lse, bT=bT)
        else:
            w8 = jnp.zeros((8, D), F32).at[:3].set(conv_full[i])
            cat = _conv_fwd(z, w8, f"conv_fwd{l}")
            st.update(w8=w8)
        mix = mm_fwd(cat, layer_w[l][1], F32, f"mm_out{l}", 1)
        x1, h2 = _norm_mod(xc, mix, g_m, weff_f, sh_f, f"norm_ffn{l}")
        ab = mm_fwd(h2, layer_w[l][2], BF16, f"mm_gu{l}", 2)
        f = _swiglu(ab, f"swiglu{l}")
        yv = mm_fwd(f, layer_w[l][3], F32, f"mm_down{l}", 3)
        st.update(cat=cat, mix=mix, x1=x1, h2=h2, ab=ab, f=f, y=yv)
        saved.append(st)
        xc = x1
        pending = (yv, g_f)

    dx, loss11, dfinal, dy, dg_f = _loss_head(xc, pending[0], pending[1], final_norm[None, :], target, "loss_head")
    loss = lax.psum(loss11[0, 0], ("x", "y", "c"))

    dmods = [None] * L
    dnorm_mix, dnorm_ffn = [None] * L, [None] * L
    big = {l: {} for l in range(L)}
    core_chip = jnp.concatenate([core1, chip1])
    dsgu_w, dsgu_b, dconv = [None] * (L - L // 2), [None] * (L - L // 2), [None] * (L // 2)

    def mm_bwd(a, b, mode, out_dtype, name, scatters=(), swap=None):
        jobs = [_scatter_job([p[0]] * len(p[1]), p[1]) for p in scatters]
        jobs += [_swap_job(swap[0])] if swap else []
        out, got = _hosted(lambda job: _mm(a, b, mode, out_dtype, name, job=job), jobs)
        for (part, windows, lay, keys), recv in zip(scatters, got):
            for key, red in zip(keys, _rs_finish(part, windows, recv, core_chip, f"{lay}_{keys[0]}")):
                big[lay][key] = red
        half = _add_half(swap[0], got[-1][0], core_chip, f"rs_add_half_{swap[1]}") if swap else None
        return out, half

    above = None
    for l in reversed(range(L)):
        i = l // 2
        st = saved[l]
        w_in, w_out, w_gu, w_dn = layer_w[l]
        if above is None:
            df, _ = mm_bwd(dy, w_dn, "nt", BF16, f"mm_down_dx{l}")
            dw_dn, _ = mm_bwd(st["f"], dy, "tn", F32, f"mm_down_dw{l}")
        else:
            s_out, t_in, win_in = above
            df, p_in = mm_bwd(dy, w_dn, "nt", BF16, f"mm_down_dx{l}", [s_out], (t_in, f"{l + 1}_in"))
            dw_dn, _ = mm_bwd(st["f"], dy, "tn", F32, f"mm_down_dw{l}", [(p_in, win_in, l + 1, ["in"])])
        t_dn, win_dn = _rs_split(dw_dn, [("row", 0, F // N_CHIPS)])
        dab = _swiglu_bwd(st["ab"], df, f"swiglu_bwd{l}")
        dh2, p_dn = mm_bwd(dab, w_gu, "nt", F32, f"mm_gu_dx{l}", swap=(t_dn, f"{l}_down"))
        dw_gu, _ = mm_bwd(st["h2"], dab, "tn", F32, f"mm_gu_dw{l}", [(p_dn, win_dn, l, ["down"])])
        t_gu, win_gu = _rs_split(dw_gu, [("col", 0, F // N_CHIPS), ("col", F, F // N_CHIPS)])
        dx1, dsh_f, dweff_f, dmix, dg_m = _norm_mod_bwd(dh2, st["x1"], st["weff_f"], dx, st["mix"], st["g_m"],
                                                        f"norm_ffn_bwd{l}")
        dcat, p_gu = mm_bwd(dmix, w_out, "nt", F32, f"mm_out_dx{l}", swap=(t_gu, f"{l}_gu"))
        dw_out = _mm(st["cat"], dmix, "tn", F32, f"mm_out_dw{l}")
        t_out, win_out = _rs_split(dw_out, [("row", 0, D // N_CHIPS)])
        if l % 2 == 0:
            dq, dk, dv = _attn_bwd(st["q"], st["k"], st["v"], st["o"], st["lse"], dcat, n_attn, f"attn_bwd{l}")
            dqkv = _dqkv_post(dq, dk, dv, cs, sa, sb, n_attn, f"dqkv_post{l}")
            duv, dsgu_w[i], dbT = _sgu_bwd(st["z"], dcat, sgu_w[i], st["bT"], n_attn, f"sgu_bwd{l}")
            dsgu_b[i] = dbT.T
            dz = jnp.concatenate([dqkv, duv], axis=1)
        else:
            dz, dw8 = _conv_bwd(st["z"], dcat, st["w8"], f"conv_bwd{l}")
            dconv[i] = dw8[:3]
        dh, p_out = mm_bwd(dz, w_in, "nt", F32, f"mm_in_dx{l}", [(p_gu, win_gu[:1], l, ["gate"])],
                           (t_out, f"{l}_out"))
        s_out = (p_out, win_out, l, ["out"])
        dw_in, _ = mm_bwd(st["h"], dz, "tn", F32, f"mm_in_dw{l}",
                          [(p_gu, win_gu[1:], l, ["up"])] + ([s_out] if l == 0 else []))
        t_in, win_in = _rs_split(dw_in, [("col", 0, w_in.shape[1] // N_CHIPS)])
        above = (s_out, t_in, win_in)
        dmod_f = [dsh_f, dweff_f * norm_ffn[l][None, :], dg_f]
        if l > 0:
            dx, dsh_m, dweff_m, dy, dg_f = _norm_mod_bwd(dh, st["x"], st["weff_m"], dx1, saved[l - 1]["y"],
                                                         saved[l - 1]["g_f"], f"norm_mix_bwd{l}")
        else:
            dx, dsh_m, dweff_m = _norm_mod_bwd(dh, st["x"], st["weff_m"], dx1, None, None, f"norm_mix_bwd{l}")
        dmods[l] = jnp.concatenate([dsh_m, dweff_m * norm_mix[l][None, :], dg_m] + dmod_f, axis=1)
        dnorm_mix[l] = dweff_m * (1.0 + st["sc_m"])
        dnorm_ffn[l] = dweff_f * (1.0 + st["sc_f"])
    _, t_in, win_in = above
    (recv_in,) = _run_job(_swap_job(t_in), "rs_swap_0_in")
    p_in0 = _add_half(t_in, recv_in, core_chip, "rs_add_half_0_in")
    grad_x = dx.reshape(1, S, D)

    dmod = jnp.concatenate(dmods, axis=0)
    small = [dmod.reshape(-1), jnp.concatenate(dnorm_mix, 0).reshape(-1), jnp.concatenate(dnorm_ffn, 0).reshape(-1),
             jnp.stack(dsgu_w).reshape(-1), jnp.stack(dsgu_b).reshape(-1), jnp.stack(dconv).reshape(-1), dfinal.reshape(-1)]
    sizes = [t.size for t in small]
    flat = jnp.concatenate(small)
    n_flat = flat.size
    rows = -(-n_flat // (128 * 512)) * 512
    flat = jnp.concatenate([flat, jnp.zeros((rows * 128 - n_flat,), F32)]).reshape(rows, 128)
    g2 = _all_gather8(flat, "gather_small")
    tot = _sum_leading(g2, "sum_small").reshape(-1)
    offs = [0]
    for s in sizes:
        offs.append(offs[-1] + s)
    take = lambda n, shape: tot[offs[n]:offs[n + 1]].reshape(shape)
    g_ada_b = take(0, ada_b.shape)
    g_norm_mix = take(1, norm_mix.shape)
    g_norm_ffn = take(2, norm_ffn.shape)
    g_sgu_w = take(3, sgu_w.shape)
    g_sgu_b = take(4, sgu_b.shape)
    g_conv_full = take(5, conv_full.shape)
    n_cw = conv_w.shape[2]
    g_conv_w = lax.dynamic_slice_in_dim(g_conv_full, chip * n_cw, n_cw, axis=2)
    g_final = take(6, final_norm.shape)
    dmod_all = g2[:, :, :].reshape(N_DEV, -1)[:, :offs[1]].reshape(N_DEV, L, 6 * D)
    dmod_mine = lax.dynamic_slice_in_dim(dmod_all, chip * n_ada, n_ada, axis=2).transpose(1, 0, 2)
    g_ada_w = _ada_bwd(c_act, dmod_mine, "ada_bwd")
    *ada_update, got = _adamw(ada_w, g_ada_w, m_ada_w, v_ada_w, "adamw_ada_w", job=_scatter_job([p_in0], win_in))
    (big[0]["in"],) = _rs_finish(p_in0, win_in, got, core_chip, "0_in")

    def stack(key, layers):
        return jnp.stack([big[l][key] for l in layers])

    even, odd, every = list(range(0, L, 2)), list(range(1, L, 2)), list(range(L))
    grads = dict(
        ada_w=g_ada_w, ada_b=g_ada_b, norm_mix=g_norm_mix, norm_ffn=g_norm_ffn,
        ab_w_in=stack("in", even), sgu_w=g_sgu_w, sgu_b=g_sgu_b, ab_w_out=stack("out", even),
        conv_w_in=stack("in", odd), conv_w=g_conv_w, conv_w_out=stack("out", odd),
        ffn_w_gate=stack("gate", every), ffn_w_up=stack("up", every), ffn_w_down=stack("down", every),
        final_norm=g_final)
    weights = dict(ada_w=ada_w, ada_b=ada_b, norm_mix=norm_mix, norm_ffn=norm_ffn, ab_w_in=ab_w_in, sgu_w=sgu_w,
                   sgu_b=sgu_b, ab_w_out=ab_w_out, conv_w_in=conv_w_in, conv_w=conv_w, conv_w_out=conv_w_out,
                   ffn_w_gate=ffn_w_gate, ffn_w_up=ffn_w_up, ffn_w_down=ffn_w_down, final_norm=final_norm)
    ms = dict(ada_w=m_ada_w, ada_b=m_ada_b, norm_mix=m_norm_mix, norm_ffn=m_norm_ffn, ab_w_in=m_ab_w_in, sgu_w=m_sgu_w,
              sgu_b=m_sgu_b, ab_w_out=m_ab_w_out, conv_w_in=m_conv_w_in, conv_w=m_conv_w, conv_w_out=m_conv_w_out,
              ffn_w_gate=m_ffn_w_gate, ffn_w_up=m_ffn_w_up, ffn_w_down=m_ffn_w_down, final_norm=m_final_norm)
    vs = dict(ada_w=v_ada_w, ada_b=v_ada_b, norm_mix=v_norm_mix, norm_ffn=v_norm_ffn, ab_w_in=v_ab_w_in, sgu_w=v_sgu_w,
              sgu_b=v_sgu_b, ab_w_out=v_ab_w_out, conv_w_in=v_conv_w_in, conv_w=v_conv_w, conv_w_out=v_conv_w_out,
              ffn_w_gate=v_ffn_w_gate, ffn_w_up=v_ffn_w_up, ffn_w_down=v_ffn_w_down, final_norm=v_final_norm)
    names = list(weights)
    deltas, new_m, new_v = {}, {}, {}
    for n in names:
        w, g = weights[n], grads[n]
        if n == "ada_w":
            deltas[n], new_m[n], new_v[n] = ada_update
        elif w.ndim == 1:
            d_, m_, v_ = _adamw(w[None, :], g[None, :], ms[n][None, :], vs[n][None, :], f"adamw_{n}")
            deltas[n], new_m[n], new_v[n] = d_[0], m_[0], v_[0]
        else:
            deltas[n], new_m[n], new_v[n] = _adamw(w, g, ms[n], vs[n], f"adamw_{n}")
    return (loss, grad_x, *[grads[n] for n in names], *[deltas[n] for n in names],
            *[new_m[n] for n in names], *[new_v[n] for n in names])
```
